```python
import math
import jax, jax.numpy as jnp
from jax import lax
import numpy as np

D_MODEL = 2048
BATCH = 2
SEQ = 4096
DEPTH = 1

N_META = 16
D_ATTN = D_MODEL // 2
D_LRU = D_MODEL // 2
N_HEADS = 8
HEAD_DIM = D_ATTN // (2 * N_HEADS)
V_DIM = 2 * HEAD_DIM
N_LRU_BLOCKS = 8
LRU_BLOCK = D_LRU // N_LRU_BLOCKS
CONV_WIDTH = 4
LRU_C = 8.0
N_BUCKETS = 32
MAX_DISTANCE = 128
Q_BLOCK = 128
NORM_EPS = 1e-6
SUBLN_EPS = 1e-5
D_IN = 4 * D_ATTN + 2 * D_LRU
NEG_INF = -1e30

kernel_name = "hymba_diffattn_rglru_block"


def rmsnorm(x, g, eps):
    xf = x.astype(jnp.float32)
    y = xf * lax.rsqrt(jnp.mean(xf * xf, axis=-1, keepdims=True) + eps)
    return (y * g.astype(jnp.float32)).astype(x.dtype)


def lambda_init_fn(layer_idx):
    return 0.8 - 0.6 * math.exp(-0.3 * layer_idx)


def t5_causal_bucket(dist):
    max_exact = N_BUCKETS // 2
    d = jnp.maximum(dist, 0)
    large = max_exact + (jnp.log(jnp.maximum(d, 1).astype(jnp.float32) / max_exact)
                         / math.log(MAX_DISTANCE / max_exact)
                         * (N_BUCKETS - max_exact)).astype(jnp.int32)
    large = jnp.minimum(large, N_BUCKETS - 1)
    return jnp.where(d < max_exact, d, large)


def diff_attention(q, k, v, dist_bias, lam):
    B, T = q.shape[0], q.shape[1]
    kpos = jnp.arange(T, dtype=jnp.int32)
    scale = HEAD_DIM ** -0.5
    k1, k2 = k[..., 0, :], k[..., 1, :]

    def attend(qb, qpos):
        rel = qpos[:, None] - kpos[None, :]
        bias = jnp.transpose(dist_bias[jnp.clip(rel, 0, T - 1)], (2, 0, 1))
        causal = rel >= 0

        def probs(qh, kh):
            s = jnp.einsum('blhd,bthd->bhlt', qh, kh).astype(jnp.float32) * scale + bias
            s = jnp.where(causal, s, NEG_INF)
            return jax.nn.softmax(s, axis=-1)

        p = probs(qb[..., 0, :], k1) - lam * probs(qb[..., 1, :], k2)
        return jnp.einsum('bhlt,bthe->blhe', p.astype(v.dtype), v)

    out_meta = attend(q[:, :N_META], jnp.arange(N_META, dtype=jnp.int32))
    s_real = T - N_META
    nb = s_real // Q_BLOCK
    qr = jnp.moveaxis(q[:, N_META:].reshape(B, nb, Q_BLOCK, N_HEADS, 2, HEAD_DIM), 1, 0)
    pos = (N_META + jnp.arange(s_real, dtype=jnp.int32)).reshape(nb, Q_BLOCK)
    out_real = lax.map(lambda a: attend(a[0], a[1]), (qr, pos))
    out_real = jnp.moveaxis(out_real, 0, 1).reshape(B, s_real, N_HEADS, V_DIM)
    return jnp.concatenate([out_meta, out_real], axis=1)


def rg_lru_branch(u, conv_w, conv_b, w_a, b_a, w_x, b_x, lru_lambda):
    B, T, C = u.shape
    u = lax.conv_general_dilated(u, conv_w[:, None, :], window_strides=(1,),
                                 padding=[(CONV_WIDTH - 1, 0)],
                                 dimension_numbers=('NWC', 'WIO', 'NWC'),
                                 feature_group_count=C) + conv_b
    ub = u.reshape(B, T, N_LRU_BLOCKS, LRU_BLOCK)
    r = jax.nn.sigmoid((jnp.einsum('btnc,ncd->btnd', ub, w_a).reshape(B, T, C) + b_a).astype(jnp.float32))
    i = jax.nn.sigmoid((jnp.einsum('btnc,ncd->btnd', ub, w_x).reshape(B, T, C) + b_x).astype(jnp.float32))
    log_a = -LRU_C * r * jax.nn.softplus(-lru_lambda.astype(jnp.float32))
    a = jnp.exp(log_a)
    mult = jnp.sqrt(-jnp.expm1(2.0 * log_a))
    mult = jnp.where(jnp.arange(T)[None, :, None] == 0, 1.0, mult)
    b = mult * i * u.astype(jnp.float32)

    def combine(c1, c2):
        a1, b1 = c1
        a2, b2 = c2
        return a1 * a2, a2 * b1 + b2

    _, h = lax.associative_scan(combine, (a, b), axis=1)
    return h.astype(u.dtype)


def setup_inputs(seed: int = 0) -> dict:
    key = jax.random.key(seed)
    ks = jax.random.split(key, 20)
    f32 = jnp.float32
    a0 = jax.random.uniform(ks[10], (DEPTH, D_LRU), f32, 0.9, 0.999)
    s0 = a0 ** (1.0 / LRU_C)
    return {
        "x": jax.random.normal(ks[0], (BATCH, SEQ, D_MODEL), f32),
        "meta_tokens": jax.random.normal(ks[1], (N_META, D_MODEL), f32),
        "rel_bias": 0.2 * jax.random.normal(ks[2], (N_BUCKETS, N_HEADS), f32),
        "norm_g": 1.0 + 0.02 * jax.random.normal(ks[3], (DEPTH, D_MODEL), f32),
        "w_in": jax.random.normal(ks[4], (DEPTH, D_MODEL, D_IN), f32) * D_MODEL ** -0.5,
        "conv_w": jax.random.normal(ks[5], (DEPTH, CONV_WIDTH, D_LRU), f32) * CONV_WIDTH ** -0.5,
        "conv_b": 0.01 * jax.random.normal(ks[6], (DEPTH, D_LRU), f32),
        "w_a": jax.random.normal(ks[7], (DEPTH, N_LRU_BLOCKS, LRU_BLOCK, LRU_BLOCK), f32) * LRU_BLOCK ** -0.5,
        "b_a": 0.01 * jax.random.normal(ks[8], (DEPTH, D_LRU), f32),
        "w_x": jax.random.normal(ks[9], (DEPTH, N_LRU_BLOCKS, LRU_BLOCK, LRU_BLOCK), f32) * LRU_BLOCK ** -0.5,
        "b_x": 0.01 * jax.random.normal(ks[11], (DEPTH, D_LRU), f32),
        "lru_lambda": jnp.log(s0 / (1.0 - s0)),
        "lam_q1": 0.1 * jax.random.normal(ks[12], (DEPTH, HEAD_DIM), f32),
        "lam_k1": 0.1 * jax.random.normal(ks[13], (DEPTH, HEAD_DIM), f32),
        "lam_q2": 0.1 * jax.random.normal(ks[14], (DEPTH, HEAD_DIM), f32),
        "lam_k2": 0.1 * jax.random.normal(ks[15], (DEPTH, HEAD_DIM), f32),
        "subln_g": 1.0 + 0.02 * jax.random.normal(ks[16], (DEPTH, V_DIM), f32),
        "w_out": jax.random.normal(ks[17], (DEPTH, D_ATTN + D_LRU, D_MODEL), f32) * (D_ATTN + D_LRU) ** -0.5,
        "final_g": 1.0 + 0.02 * jax.random.normal(ks[18], (D_MODEL,), f32),
    }


def reference(x, meta_tokens, rel_bias, norm_g, w_in, conv_w, conv_b, w_a, b_a, w_x, b_x,
              lru_lambda, lam_q1, lam_k1, lam_q2, lam_k2, subln_g, w_out, final_g):
    B = x.shape[0]
    meta = jnp.broadcast_to(meta_tokens[None].astype(x.dtype), (B, N_META, D_MODEL))
    x = jnp.concatenate([meta, x], axis=1)
    T = x.shape[1]
    dist_bias = rel_bias.astype(jnp.float32)[t5_causal_bucket(jnp.arange(T, dtype=jnp.int32))]
    splits = [D_ATTN, 2 * D_ATTN, 3 * D_ATTN, 4 * D_ATTN, 4 * D_ATTN + D_LRU]

    for l in range(DEPTH):
        lam_init = lambda_init_fn(l)
        h = rmsnorm(x, norm_g[l], NORM_EPS)
        proj = jnp.einsum('btd,de->bte', h, w_in[l])
        q, k, v, g_att, u, g_lru = jnp.split(proj, splits, axis=-1)
        q = q.reshape(B, T, N_HEADS, 2, HEAD_DIM)
        k = k.reshape(B, T, N_HEADS, 2, HEAD_DIM)
        v = v.reshape(B, T, N_HEADS, V_DIM)
        lam = (jnp.exp(jnp.sum(lam_q1[l].astype(jnp.float32) * lam_k1[l].astype(jnp.float32)))
               - jnp.exp(jnp.sum(lam_q2[l].astype(jnp.float32) * lam_k2[l].astype(jnp.float32)))
               + lam_init)
        att = diff_attention(q, k, v, dist_bias, lam)
        att = (rmsnorm(att, subln_g[l], SUBLN_EPS) * (1.0 - lam_init)).reshape(B, T, D_ATTN)
        att = att * jax.nn.silu(g_att)
        rec = rg_lru_branch(u, conv_w[l], conv_b[l], w_a[l], b_a[l], w_x[l], b_x[l], lru_lambda[l])
        rec = rec * jax.nn.silu(g_lru)
        mixed = jnp.concatenate([att, rec], axis=-1)
        x = x + jnp.einsum('bte,ed->btd', mixed, w_out[l])

    return rmsnorm(x, final_g, NORM_EPS)[:, N_META:]
```

```python
import functools
import math

import numpy as np
import jax
import jax.numpy as jnp
from jax import lax
from jax.experimental import pallas as pl
from jax.experimental.pallas import tpu as pltpu

D_MODEL = 2048
N_META = 16
D_ATTN = 1024
D_LRU = 1024
N_HEADS = 8
HEAD_DIM = 64
V_DIM = 128
N_LRU_BLOCKS = 8
LRU_BLOCK = 128
CONV_WIDTH = 4
LRU_C = 8.0
N_BUCKETS = 32
MAX_DISTANCE = 128
NORM_EPS = 1e-6
SUBLN_EPS = 1e-5
NEG_INF = -1e30
LAMBDA_INIT = 0.8 - 0.6 * math.exp(-0.3 * 0)

BF16 = jnp.bfloat16
F32 = jnp.float32

VMEM_LIMIT_PROJ = 48 * 1024 * 1024
VMEM_LIMIT_ATTN = 32 * 1024 * 1024
VMEM_LIMIT_LRU = 40 * 1024 * 1024
VMEM_LIMIT_OUT = 56 * 1024 * 1024

PROJ_ROWS = 512
ATTN_TQ = 256
ATTN_TK = 256
LRU_ROWS = 256
SCAN_ROWS = 128
OUT_ROWS = 256


def _bucket_thresholds():
    max_exact = N_BUCKETS // 2
    d = np.arange(0, 4 * MAX_DISTANCE, dtype=np.int64)
    val = (np.log(np.maximum(d, 1).astype(np.float64) / max_exact)
           / math.log(MAX_DISTANCE / max_exact) * (N_BUCKETS - max_exact))
    large = np.minimum(max_exact + np.floor(val + 1e-9).astype(np.int64), N_BUCKETS - 1)
    bucket = np.where(d < max_exact, d, large)
    frac = np.abs(val - np.round(val))
    interior = (d > max_exact) & (d < MAX_DISTANCE)
    assert frac[interior].min() > 1e-3
    assert (np.diff(bucket) >= 0).all() and bucket[MAX_DISTANCE] == N_BUCKETS - 1
    return tuple(int(np.argmax(bucket >= j)) for j in range(1, N_BUCKETS))


BUCKET_THRESHOLDS = _bucket_thresholds()


def _rms(x, g, eps):
    y = x * lax.rsqrt(jnp.mean(x * x, axis=-1, keepdims=True) + eps)
    return y * g


def _dot(a, b):
    return jnp.dot(a, b, preferred_element_type=F32)


def _dot_nt(a, b):
    return lax.dot_general(a, b, (((1,), (1,)), ((), ())), preferred_element_type=F32)


def _proj_kernel(x_ref, meta_ref, g_ref, w_ref, o_ref, om_ref, wb_ref, *, head_major):
    j = pl.program_id(0)
    i = pl.program_id(1)
    out_scale = jnp.where(j == 0, HEAD_DIM ** -0.5, 1.0).astype(F32) if head_major else 1.0

    @pl.when(i == 0)
    def _():
        wb_ref[...] = w_ref[...].astype(BF16)
        hm = _rms(meta_ref[...], g_ref[...], NORM_EPS).astype(BF16)
        ym = _dot(hm, wb_ref[...]) * out_scale
        if head_major:
            for h in range(N_HEADS):
                om_ref[0, h] = ym[:, h * V_DIM:(h + 1) * V_DIM].astype(om_ref.dtype)
        else:
            om_ref[0] = ym.astype(om_ref.dtype)

    hx = _rms(x_ref[...], g_ref[...], NORM_EPS).astype(BF16)
    y = _dot(hx, wb_ref[...]) * out_scale
    if head_major:
        for h in range(N_HEADS):
            o_ref[0, 0, h] = y[:, h * V_DIM:(h + 1) * V_DIM].astype(o_ref.dtype)
    else:
        o_ref[0] = y.astype(o_ref.dtype)


def _project(x2d, meta, norm_g, w_in, *, col_tile0, head_major, batch, seq):
    rows = x2d.shape[0]
    tm = PROJ_ROWS
    n_i = rows // tm
    n_ib = seq // tm
    if head_major:
        out_shape = (jax.ShapeDtypeStruct((3, batch, N_HEADS, seq, V_DIM), BF16),
                     jax.ShapeDtypeStruct((3, N_HEADS, N_META, V_DIM), BF16))
        out_specs = (pl.BlockSpec((1, 1, N_HEADS, tm, V_DIM),
                                  lambda j, i: (j, i // n_ib, 0, i % n_ib, 0)),
                     pl.BlockSpec((1, N_HEADS, N_META, V_DIM), lambda j, i: (j, 0, 0, 0)))
    else:
        out_shape = (jax.ShapeDtypeStruct((3, rows, D_ATTN), F32),
                     jax.ShapeDtypeStruct((3, N_META, D_ATTN), F32))
        out_specs = (pl.BlockSpec((1, tm, D_ATTN), lambda j, i: (j, i, 0)),
                     pl.BlockSpec((1, N_META, D_ATTN), lambda j, i: (j, 0, 0)))
    return pl.pallas_call(
        functools.partial(_proj_kernel, head_major=head_major),
        grid=(3, n_i),
        in_specs=[pl.BlockSpec((tm, D_MODEL), lambda j, i: (i, 0)),
                  pl.BlockSpec((N_META, D_MODEL), lambda j, i: (0, 0)),
                  pl.BlockSpec((1, D_MODEL), lambda j, i: (0, 0)),
                  pl.BlockSpec((D_MODEL, D_ATTN), lambda j, i: (0, j + col_tile0))],
        out_specs=out_specs,
        out_shape=out_shape,
        scratch_shapes=[pltpu.VMEM((D_MODEL, D_ATTN), BF16)],
        compiler_params=pltpu.CompilerParams(
            dimension_semantics=("arbitrary", "arbitrary"),
            vmem_limit_bytes=VMEM_LIMIT_PROJ),
        name="proj_qkv" if head_major else "proj_gates",
    )(x2d, meta, norm_g, w_in)


def _toeplitz_bias(dist, rb_ref, h, far):
    b = jnp.full(dist.shape, rb_ref[0, h] - far, F32)
    for j, thr in enumerate(BUCKET_THRESHOLDS, start=1):
        b = jnp.where(dist >= thr, rb_ref[j, h] - far, b)
    return b


def _attn_kernel(rb_ref, lq1_ref, lk1_ref, lq2_ref, lk2_ref, q_ref, k_ref, v_ref, km_ref, vm_ref,
                 sg_ref, o_ref, kall, vall, bd_ref, bs_ref, bc_ref, m_ref, l_ref, acc_ref):
    h = pl.program_id(1)
    i = pl.program_id(2)
    tq, tk = ATTN_TQ, ATTN_TK
    seq = k_ref.shape[2]

    far = rb_ref[N_BUCKETS - 1, h]

    @pl.when(i == 0)
    def _():
        kall[pl.ds(0, N_META), :] = km_ref[0]
        kall[pl.ds(N_META, seq), :] = k_ref[0, 0]
        vall[pl.ds(0, N_META), :] = vm_ref[0]
        vall[pl.ds(N_META, seq), :] = v_ref[0, 0]
        r = lax.broadcasted_iota(jnp.int32, (tq, tk), 0)
        c = lax.broadcasted_iota(jnp.int32, (tq, tk), 1)
        d_diag = N_META + r - c
        bd_ref[...] = jnp.where(d_diag >= 0, _toeplitz_bias(d_diag, rb_ref, h, far), NEG_INF)
        bs_ref[...] = _toeplitz_bias(d_diag + tk, rb_ref, h, far)
        rc = lax.broadcasted_iota(jnp.int32, (N_META, N_META), 0)
        cc = lax.broadcasted_iota(jnp.int32, (N_META, N_META), 1)
        bc_ref[...] = jnp.where(rc >= cc, _toeplitz_bias(rc - cc, rb_ref, h, far), NEG_INF)

    q = q_ref[0, 0]
    lane = lax.broadcasted_iota(jnp.int32, q.shape, 1)
    zero = jnp.zeros_like(q)
    qs = (jnp.where(lane < HEAD_DIM, q, zero), jnp.where(lane >= HEAD_DIM, q, zero))

    def masked(s, bias):
        return jnp.where(bias > 0.5 * NEG_INF, s + bias, NEG_INF)

    k_d = kall[pl.ds(pl.multiple_of(i * tk, tk), tk), :]
    v_d = vall[pl.ds(pl.multiple_of(i * tk, tk), tk), :]
    for a in range(2):
        s = masked(_dot_nt(qs[a], k_d), bd_ref[...])
        m = jnp.max(s, axis=1, keepdims=True)
        p = jnp.exp(s - m)
        m_ref[a] = m
        l_ref[a] = jnp.sum(p, axis=1, keepdims=True)
        acc_ref[a] = _dot(p.astype(BF16), v_d)

    def online(a, s, v_t, rows):
        m_old = m_ref[a, rows, :]
        m_new = jnp.maximum(m_old, jnp.max(s, axis=1, keepdims=True))
        alpha = jnp.exp(m_old - m_new)
        p = jnp.exp(s - m_new)
        l_ref[a, rows, :] = alpha * l_ref[a, rows, :] + jnp.sum(p, axis=1, keepdims=True)
        acc_ref[a, rows, :] = alpha * acc_ref[a, rows, :] + _dot(p.astype(BF16), v_t)
        m_ref[a, rows, :] = m_new

    corner = pl.ds(tq - N_META, N_META)
    k_c = kall[pl.ds(pl.multiple_of((i + 1) * tk, tk), N_META), :]
    v_c = vall[pl.ds(pl.multiple_of((i + 1) * tk, tk), N_META), :]
    for a in range(2):
        s = masked(_dot_nt(qs[a][tq - N_META:, :], k_c), bc_ref[...])
        online(a, s, v_c, corner)

    all_rows = pl.ds(0, tq)

    @pl.when(i >= 1)
    def _():
        k_s = kall[pl.ds(pl.multiple_of((i - 1) * tk, tk), tk), :]
        v_s = vall[pl.ds(pl.multiple_of((i - 1) * tk, tk), tk), :]
        for a in range(2):
            online(a, _dot_nt(qs[a], k_s) + bs_ref[...], v_s, all_rows)

    def far_body(j, carry):
        k_t = kall[pl.ds(pl.multiple_of(j * tk, tk), tk), :]
        v_t = vall[pl.ds(pl.multiple_of(j * tk, tk), tk), :]
        for a in range(2):
            online(a, _dot_nt(qs[a], k_t), v_t, all_rows)
        return carry

    lax.fori_loop(0, i - 1, far_body, 0)

    lam = (jnp.exp(jnp.sum(lq1_ref[...] * lk1_ref[...], keepdims=True))
           - jnp.exp(jnp.sum(lq2_ref[...] * lk2_ref[...], keepdims=True))
           + LAMBDA_INIT)
    out = acc_ref[0] / l_ref[0] - lam * (acc_ref[1] / l_ref[1])
    o_ref[...] = _rms(out, sg_ref[...], SUBLN_EPS) * (1.0 - LAMBDA_INIT)


def _attention(qkv, qkv_meta, rel_bias, lam_q1, lam_k1, lam_q2, lam_k2, subln_g):
    _, batch, _, seq, _ = qkv.shape
    tq = ATTN_TQ
    nq = seq // tq
    smem = pl.BlockSpec(memory_space=pltpu.SMEM)
    row64 = pl.BlockSpec((1, HEAD_DIM), lambda b, h, i: (0, 0))
    kv_spec = lambda which: pl.BlockSpec((1, 1, 1, seq, V_DIM), lambda b, h, i: (which, b, h, 0, 0))
    meta_spec = lambda which: pl.BlockSpec((1, 1, N_META, V_DIM), lambda b, h, i: (which, h, 0, 0))

    def kernel(rb, lq1, lk1, lq2, lk2, q_ref, k_ref, v_ref, km_ref, vm_ref, sg_ref, o_ref, *scratch):
        _attn_kernel(rb, lq1, lk1, lq2, lk2, q_ref.at[0], k_ref.at[0], v_ref.at[0],
                     km_ref.at[0], vm_ref.at[0], sg_ref, o_ref, *scratch)

    return pl.pallas_call(
        kernel,
        grid=(batch, N_HEADS, nq),
        in_specs=[smem, row64, row64, row64, row64,
                  pl.BlockSpec((1, 1, 1, tq, V_DIM), lambda b, h, i: (0, b, h, i, 0)),
                  kv_spec(1), kv_spec(2), meta_spec(1), meta_spec(2),
                  pl.BlockSpec((1, V_DIM), lambda b, h, i: (0, 0))],
        out_specs=pl.BlockSpec((tq, V_DIM), lambda b, h, i: (b * nq + i, h)),
        out_shape=jax.ShapeDtypeStruct((batch * seq, D_ATTN), F32),
        scratch_shapes=[pltpu.VMEM((seq + N_META, V_DIM), BF16),
                        pltpu.VMEM((seq + N_META, V_DIM), BF16),
                        pltpu.VMEM((tq, ATTN_TK), F32),
                        pltpu.VMEM((tq, ATTN_TK), F32),
                        pltpu.VMEM((N_META, N_META), F32),
                        pltpu.VMEM((2, tq, 1), F32),
                        pltpu.VMEM((2, tq, 1), F32),
                        pltpu.VMEM((2, tq, V_DIM), F32)],
        compiler_params=pltpu.CompilerParams(
            dimension_semantics=("arbitrary", "arbitrary", "arbitrary"),
            vmem_limit_bytes=VMEM_LIMIT_ATTN),
        name="diff_attention",
    )(rel_bias, lam_q1, lam_k1, lam_q2, lam_k2, qkv, qkv, qkv, qkv_meta, qkv_meta, subln_g)


def _scan_block(a, b):
    n = a.shape[0]
    row = lax.broadcasted_iota(jnp.int32, a.shape, 0)
    s = 1
    while s < n:
        keep = row >= s
        a_sh = jnp.where(keep, pltpu.roll(a, s, 0), 1.0)
        b_sh = jnp.where(keep, pltpu.roll(b, s, 0), 0.0)
        b = b + a * b_sh
        a = a * a_sh
        s *= 2
    return a, b


def _lru_kernel(u_ref, um_ref, g_ref, cw_ref, cb_ref, wa_ref, wx_ref, ba_ref, bx_ref, lam_ref,
                o_ref, ubuf, h_ref):
    c = pl.program_id(1)
    hist = 8

    x = -lam_ref[...]
    softplus = jnp.maximum(x, 0.0) + jnp.log1p(jnp.exp(-jnp.abs(x)))

    def chunk(n, first):
        cw = cw_ref[...]
        uc = (cw[3:4] * ubuf[pl.ds(hist, n), :] + cw[2:3] * ubuf[pl.ds(hist - 1, n), :]
              + cw[1:2] * ubuf[pl.ds(hist - 2, n), :] + cw[0:1] * ubuf[pl.ds(hist - 3, n), :]
              + cb_ref[...])
        ucb = uc.astype(BF16)
        outs = []
        for blk in range(N_LRU_BLOCKS):
            cols = slice(blk * LRU_BLOCK, (blk + 1) * LRU_BLOCK)
            ub = ucb[:, cols]
            r = jax.nn.sigmoid(_dot(ub, wa_ref[blk].astype(BF16)) + ba_ref[:, cols])
            gi = jax.nn.sigmoid(_dot(ub, wx_ref[blk].astype(BF16)) + bx_ref[:, cols])
            log_a = -LRU_C * r * softplus[:, cols]
            a = jnp.exp(log_a)
            mult = jnp.sqrt(jnp.tanh(-log_a) * (a * a + 1.0))
            if first:
                row = lax.broadcasted_iota(jnp.int32, mult.shape, 0)
                mult = jnp.where(row == 0, 1.0, mult)
            b = mult * gi * uc[:, cols]
            carry = h_ref[:, cols]
            pieces = []
            for s0 in range(0, n, SCAN_ROWS):
                s1 = min(s0 + SCAN_ROWS, n)
                a_cum, b_cum = _scan_block(a[s0:s1], b[s0:s1])
                hblk = b_cum + a_cum * carry
                carry = hblk[s1 - s0 - 1:s1 - s0, :]
                pieces.append(hblk)
            h_ref[:, cols] = carry
            outs.append(pieces[0] if len(pieces) == 1 else jnp.concatenate(pieces, axis=0))
        return outs

    @pl.when(c == 0)
    def _():
        ubuf[pl.ds(0, hist), :] = jnp.zeros((hist, D_LRU), F32)
        ubuf[pl.ds(hist, N_META), :] = um_ref[...]
        h_ref[...] = jnp.zeros_like(h_ref)
        chunk(N_META, True)
        ubuf[pl.ds(0, hist), :] = ubuf[pl.ds(N_META, hist), :]

    n = u_ref.shape[0]
    ubuf[pl.ds(hist, n), :] = u_ref[...]
    outs = chunk(n, False)
    g = g_ref[...]
    for blk in range(N_LRU_BLOCKS):
        cols = slice(blk * LRU_BLOCK, (blk + 1) * LRU_BLOCK)
        gb = g[:, cols]
        o_ref[:, cols] = (outs[blk] * (gb * jax.nn.sigmoid(gb))).astype(o_ref.dtype)
    ubuf[pl.ds(0, hist), :] = ubuf[pl.ds(n, hist), :]


def _lru(gates, gates_meta, conv_w, conv_b, w_a, b_a, w_x, b_x, lru_lambda, *, batch, seq):
    tc = LRU_ROWS
    nc = seq // tc
    row = lambda n: pl.BlockSpec((n, D_LRU), lambda b, c: (0, 0))
    wspec = pl.BlockSpec((N_LRU_BLOCKS, LRU_BLOCK, LRU_BLOCK), lambda b, c: (0, 0, 0))

    def kernel(u_ref, um_ref, g_ref, *rest):
        _lru_kernel(u_ref.at[0], um_ref.at[0], g_ref.at[0], *rest)

    return pl.pallas_call(
        kernel,
        grid=(batch, nc),
        in_specs=[pl.BlockSpec((1, tc, D_LRU), lambda b, c: (1, b * nc + c, 0)),
                  pl.BlockSpec((1, N_META, D_LRU), lambda b, c: (1, 0, 0)),
                  pl.BlockSpec((1, tc, D_LRU), lambda b, c: (2, b * nc + c, 0)),
                  row(CONV_WIDTH), row(1), wspec, wspec, row(1), row(1), row(1)],
        out_specs=pl.BlockSpec((tc, D_LRU), lambda b, c: (b * nc + c, 0)),
        out_shape=jax.ShapeDtypeStruct((batch * seq, D_LRU), BF16),
        scratch_shapes=[pltpu.VMEM((8 + tc, D_LRU), F32),
                        pltpu.VMEM((1, D_LRU), F32)],
        compiler_params=pltpu.CompilerParams(
            dimension_semantics=("arbitrary", "arbitrary"),
            vmem_limit_bytes=VMEM_LIMIT_LRU),
        name="conv_rglru",
    )(gates, gates_meta, gates, conv_w, conv_b, w_a, w_x, b_a, b_x, lru_lambda)


def _out_kernel(x_ref, att_ref, g_ref, rec_ref, w_ref, fg_ref, o_ref, wb_ref):
    @pl.when(pl.program_id(0) == 0)
    def _():
        wb_ref[...] = w_ref[...].astype(BF16)

    g = g_ref[0]
    att = (att_ref[...] * (g * jax.nn.sigmoid(g))).astype(BF16)
    y = _dot(att, wb_ref[pl.ds(0, D_ATTN), :]) + _dot(rec_ref[...], wb_ref[pl.ds(D_ATTN, D_LRU), :])
    o_ref[...] = _rms(x_ref[...] + y, fg_ref[...], NORM_EPS)


def _out_project(x2d, att, gates, rec, w_out, final_g):
    rows = x2d.shape[0]
    tm = OUT_ROWS
    return pl.pallas_call(
        _out_kernel,
        grid=(rows // tm,),
        in_specs=[pl.BlockSpec((tm, D_MODEL), lambda i: (i, 0)),
                  pl.BlockSpec((tm, D_ATTN), lambda i: (i, 0)),
                  pl.BlockSpec((1, tm, D_ATTN), lambda i: (0, i, 0)),
                  pl.BlockSpec((tm, D_LRU), lambda i: (i, 0)),
                  pl.BlockSpec((D_ATTN + D_LRU, D_MODEL), lambda i: (0, 0),
                               pipeline_mode=pl.Buffered(1)),
                  pl.BlockSpec((1, D_MODEL), lambda i: (0, 0))],
        out_specs=pl.BlockSpec((tm, D_MODEL), lambda i: (i, 0)),
        out_shape=jax.ShapeDtypeStruct((rows, D_MODEL), F32),
        scratch_shapes=[pltpu.VMEM((D_ATTN + D_LRU, D_MODEL), BF16)],
        compiler_params=pltpu.CompilerParams(
            dimension_semantics=("arbitrary",),
            vmem_limit_bytes=VMEM_LIMIT_OUT),
        name="out_proj_norm",
    )(x2d, att, gates, rec, w_out, final_g)


def kernel(x, meta_tokens, rel_bias, norm_g, w_in, conv_w, conv_b, w_a, b_a, w_x, b_x, lru_lambda,
           lam_q1, lam_k1, lam_q2, lam_k2, subln_g, w_out, final_g):
    batch, seq, _ = x.shape
    x2d = x.reshape(batch * seq, D_MODEL)
    qkv, qkv_meta = _project(x2d, meta_tokens, norm_g, w_in[0], col_tile0=0, head_major=True,
                             batch=batch, seq=seq)
    gates, gates_meta = _project(x2d, meta_tokens, norm_g, w_in[0], col_tile0=3, head_major=False,
                                 batch=batch, seq=seq)
    att = _attention(qkv, qkv_meta, rel_bias, lam_q1, lam_k1, lam_q2, lam_k2, subln_g)
    rec = _lru(gates, gates_meta, conv_w[0], conv_b, w_a[0], b_a, w_x[0], b_x, lru_lambda,
               batch=batch, seq=seq)
    out = _out_project(x2d, att, gates, rec, w_out[0], final_g.reshape(1, D_MODEL))
    return out.reshape(batch, seq, D_MODEL)
```

```python
import functools
import math

import numpy as np
import jax
import jax.numpy as jnp
from jax import lax
from jax.experimental import pallas as pl
from jax.experimental.pallas import tpu as pltpu

D_MODEL = 2048
N_META = 16
D_ATTN = 1024
D_LRU = 1024
N_HEADS = 8
HEAD_DIM = 64
V_DIM = 128
N_LRU_BLOCKS = 8
LRU_BLOCK = 128
CONV_WIDTH = 4
LRU_C = 8.0
N_BUCKETS = 32
MAX_DISTANCE = 128
NORM_EPS = 1e-6
SUBLN_EPS = 1e-5
NEG_INF = -1e30
LAMBDA_INIT = 0.8 - 0.6 * math.exp(-0.3 * 0)

BF16 = jnp.bfloat16
F32 = jnp.float32

VMEM_LIMIT_PROJ = 48 * 1024 * 1024
VMEM_LIMIT_ATTN = 32 * 1024 * 1024
VMEM_LIMIT_LRU = 40 * 1024 * 1024
VMEM_LIMIT_OUT = 56 * 1024 * 1024

PROJ_ROWS = 512
ATTN_TQ = 512
ATTN_TK = 512
LRU_ROWS = 256
SCAN_ROWS = 128
OUT_ROWS = 256


def _bucket_thresholds():
    max_exact = N_BUCKETS // 2
    d = np.arange(0, 4 * MAX_DISTANCE, dtype=np.int64)
    val = (np.log(np.maximum(d, 1).astype(np.float64) / max_exact)
           / math.log(MAX_DISTANCE / max_exact) * (N_BUCKETS - max_exact))
    large = np.minimum(max_exact + np.floor(val + 1e-9).astype(np.int64), N_BUCKETS - 1)
    bucket = np.where(d < max_exact, d, large)
    frac = np.abs(val - np.round(val))
    interior = (d > max_exact) & (d < MAX_DISTANCE)
    assert frac[interior].min() > 1e-3
    assert (np.diff(bucket) >= 0).all() and bucket[MAX_DISTANCE] == N_BUCKETS - 1
    return tuple(int(np.argmax(bucket >= j)) for j in range(1, N_BUCKETS))


BUCKET_THRESHOLDS = _bucket_thresholds()


def _rms(x, g, eps):
    y = x * lax.rsqrt(jnp.mean(x * x, axis=-1, keepdims=True) + eps)
    return y * g


def _dot(a, b):
    return jnp.dot(a, b, preferred_element_type=F32)


def _dot_nt(a, b):
    return lax.dot_general(a, b, (((1,), (1,)), ((), ())), preferred_element_type=F32)


def _proj_kernel(x_ref, meta_ref, g_ref, w_ref, o_ref, om_ref, wb_ref, *, head_major):
    j = pl.program_id(0)
    i = pl.program_id(1)
    out_scale = jnp.where(j == 0, HEAD_DIM ** -0.5, 1.0).astype(F32) if head_major else 1.0

    @pl.when(i == 0)
    def _():
        wb_ref[...] = w_ref[...].astype(BF16)
        hm = _rms(meta_ref[...], g_ref[...], NORM_EPS).astype(BF16)
        ym = _dot(hm, wb_ref[...]) * out_scale
        if head_major:
            for h in range(N_HEADS):
                om_ref[0, h] = ym[:, h * V_DIM:(h + 1) * V_DIM].astype(om_ref.dtype)
        else:
            om_ref[0] = ym.astype(om_ref.dtype)

    hx = _rms(x_ref[...], g_ref[...], NORM_EPS).astype(BF16)
    y = _dot(hx, wb_ref[...]) * out_scale
    if head_major:
        for h in range(N_HEADS):
            o_ref[0, 0, h] = y[:, h * V_DIM:(h + 1) * V_DIM].astype(o_ref.dtype)
    else:
        o_ref[0] = y.astype(o_ref.dtype)


def _project(x2d, meta, norm_g, w_in, *, col_tile0, head_major, batch, seq):
    rows = x2d.shape[0]
    tm = PROJ_ROWS
    n_i = rows // tm
    n_ib = seq // tm
    if head_major:
        out_shape = (jax.ShapeDtypeStruct((3, batch, N_HEADS, seq, V_DIM), BF16),
                     jax.ShapeDtypeStruct((3, N_HEADS, N_META, V_DIM), BF16))
        out_specs = (pl.BlockSpec((1, 1, N_HEADS, tm, V_DIM),
                                  lambda j, i: (j, i // n_ib, 0, i % n_ib, 0)),
                     pl.BlockSpec((1, N_HEADS, N_META, V_DIM), lambda j, i: (j, 0, 0, 0)))
    else:
        out_shape = (jax.ShapeDtypeStruct((3, rows, D_ATTN), F32),
                     jax.ShapeDtypeStruct((3, N_META, D_ATTN), F32))
        out_specs = (pl.BlockSpec((1, tm, D_ATTN), lambda j, i: (j, i, 0)),
                     pl.BlockSpec((1, N_META, D_ATTN), lambda j, i: (j, 0, 0)))
    return pl.pallas_call(
        functools.partial(_proj_kernel, head_major=head_major),
        grid=(3, n_i),
        in_specs=[pl.BlockSpec((tm, D_MODEL), lambda j, i: (i, 0)),
                  pl.BlockSpec((N_META, D_MODEL), lambda j, i: (0, 0)),
                  pl.BlockSpec((1, D_MODEL), lambda j, i: (0, 0)),
                  pl.BlockSpec((D_MODEL, D_ATTN), lambda j, i: (0, j + col_tile0))],
        out_specs=out_specs,
        out_shape=out_shape,
        scratch_shapes=[pltpu.VMEM((D_MODEL, D_ATTN), BF16)],
        compiler_params=pltpu.CompilerParams(
            dimension_semantics=("arbitrary", "arbitrary"),
            vmem_limit_bytes=VMEM_LIMIT_PROJ),
        name="proj_qkv" if head_major else "proj_gates",
    )(x2d, meta, norm_g, w_in)


def _toeplitz_bias(dist, rb_ref, h, far):
    b = jnp.full(dist.shape, rb_ref[0, h] - far, F32)
    for j, thr in enumerate(BUCKET_THRESHOLDS, start=1):
        b = jnp.where(dist >= thr, rb_ref[j, h] - far, b)
    return b


def _attn_kernel(rb_ref, lq1_ref, lk1_ref, lq2_ref, lk2_ref, q_ref, k_ref, v_ref, km_ref, vm_ref,
                 sg_ref, o_ref, kall, vall, vt_ref, bd_ref, bs_ref, m_ref, l_ref, acc_ref):
    h = pl.program_id(1)
    i = pl.program_id(2)
    tq, tk = ATTN_TQ, ATTN_TK
    seq = k_ref.shape[2]
    n_keys = seq + N_META
    n_pad = vt_ref.shape[1]

    far = rb_ref[N_BUCKETS - 1, h]

    @pl.when(i == 0)
    def _():
        kall[pl.ds(0, N_META), :] = km_ref[0]
        kall[pl.ds(N_META, seq), :] = k_ref[0, 0]
        vall[pl.ds(0, N_META), :] = vm_ref[0]
        vall[pl.ds(N_META, seq), :] = v_ref[0, 0]
        vall[pl.ds(n_keys, n_pad - n_keys), :] = jnp.zeros((n_pad - n_keys, V_DIM), BF16)

        def xpose(n, carry):
            rows = pl.ds(pl.multiple_of(n * V_DIM, V_DIM), V_DIM)
            vt_ref[:, rows] = vall[rows, :].astype(F32).T.astype(BF16)
            return carry

        lax.fori_loop(0, n_pad // V_DIM, xpose, 0)
        c = lax.broadcasted_iota(jnp.int32, (tk + N_META, tq), 0)
        r = lax.broadcasted_iota(jnp.int32, (tk + N_META, tq), 1)
        d_diag = N_META + r - c
        bd_ref[...] = jnp.where(d_diag >= 0, _toeplitz_bias(d_diag, rb_ref, h, far), NEG_INF)
        bs_ref[...] = _toeplitz_bias(d_diag[:tk] + tk, rb_ref, h, far)

    q = q_ref[0, 0]
    lane = lax.broadcasted_iota(jnp.int32, q.shape, 1)
    zero = jnp.zeros_like(q)
    qs = (jnp.where(lane < HEAD_DIM, q, zero), jnp.where(lane >= HEAD_DIM, q, zero))

    base = pl.multiple_of(i * tk, tk)
    k_d = kall[pl.ds(base, tk + N_META), :]
    v_d = vt_ref[:, pl.ds(base, tk)]
    v_c = vt_ref[:, pl.ds(base + tk, V_DIM)][:, :N_META]
    for a in range(2):
        bias = bd_ref[...]
        s = jnp.where(bias > 0.5 * NEG_INF, _dot_nt(k_d, qs[a]) + bias, NEG_INF)
        m = jnp.max(s, axis=0, keepdims=True)
        p = jnp.exp(s - m)
        m_ref[a] = m
        l_ref[a] = jnp.sum(p, axis=0, keepdims=True)
        pb = p.astype(BF16)
        acc_ref[a] = _dot(v_d, pb[:tk]) + _dot(v_c, pb[tk:])

    def online(a, s, v_t):
        m_old = m_ref[a]
        m_new = jnp.maximum(m_old, jnp.max(s, axis=0, keepdims=True))
        alpha = jnp.exp(m_old - m_new)
        p = jnp.exp(s - m_new)
        l_ref[a] = alpha * l_ref[a] + jnp.sum(p, axis=0, keepdims=True)
        acc_ref[a] = alpha * acc_ref[a] + _dot(v_t, p.astype(BF16))
        m_ref[a] = m_new

    @pl.when(i >= 1)
    def _():
        prev = pl.multiple_of((i - 1) * tk, tk)
        k_s = kall[pl.ds(prev, tk), :]
        v_s = vt_ref[:, pl.ds(prev, tk)]
        for a in range(2):
            online(a, _dot_nt(k_s, qs[a]) + bs_ref[...], v_s)

    def far_body(j, carry):
        off = pl.multiple_of(j * tk, tk)
        k_t = kall[pl.ds(off, tk), :]
        v_t = vt_ref[:, pl.ds(off, tk)]
        for a in range(2):
            online(a, _dot_nt(k_t, qs[a]), v_t)
        return carry

    lax.fori_loop(0, i - 1, far_body, 0)

    lam = (jnp.exp(jnp.sum(lq1_ref[...] * lk1_ref[...], keepdims=True))
           - jnp.exp(jnp.sum(lq2_ref[...] * lk2_ref[...], keepdims=True))
           + LAMBDA_INIT)
    out_t = acc_ref[0] / l_ref[0] - lam * (acc_ref[1] / l_ref[1])
    inv = lax.rsqrt(jnp.mean(out_t * out_t, axis=0, keepdims=True) + SUBLN_EPS)
    o_ref[...] = ((out_t * inv).T * sg_ref[...]) * (1.0 - LAMBDA_INIT)


def _attention(qkv, qkv_meta, rel_bias, lam_q1, lam_k1, lam_q2, lam_k2, subln_g):
    _, batch, _, seq, _ = qkv.shape
    tq, tk = ATTN_TQ, ATTN_TK
    nq = seq // tq
    n_pad = seq + V_DIM
    smem = pl.BlockSpec(memory_space=pltpu.SMEM)
    row64 = pl.BlockSpec((1, HEAD_DIM), lambda b, h, i: (0, 0))
    kv_spec = lambda which: pl.BlockSpec((1, 1, 1, seq, V_DIM), lambda b, h, i: (which, b, h, 0, 0))
    meta_spec = lambda which: pl.BlockSpec((1, 1, N_META, V_DIM), lambda b, h, i: (which, h, 0, 0))

    def kernel(rb, lq1, lk1, lq2, lk2, q_ref, k_ref, v_ref, km_ref, vm_ref, sg_ref, o_ref, *scratch):
        _attn_kernel(rb, lq1, lk1, lq2, lk2, q_ref.at[0], k_ref.at[0], v_ref.at[0],
                     km_ref.at[0], vm_ref.at[0], sg_ref, o_ref, *scratch)

    return pl.pallas_call(
        kernel,
        grid=(batch, N_HEADS, nq),
        in_specs=[smem, row64, row64, row64, row64,
                  pl.BlockSpec((1, 1, 1, tq, V_DIM), lambda b, h, i: (0, b, h, i, 0)),
                  kv_spec(1), kv_spec(2), meta_spec(1), meta_spec(2),
                  pl.BlockSpec((1, V_DIM), lambda b, h, i: (0, 0))],
        out_specs=pl.BlockSpec((tq, V_DIM), lambda b, h, i: (b * nq + i, h)),
        out_shape=jax.ShapeDtypeStruct((batch * seq, D_ATTN), F32),
        scratch_shapes=[pltpu.VMEM((seq + N_META, V_DIM), BF16),
                        pltpu.VMEM((n_pad, V_DIM), BF16),
                        pltpu.VMEM((V_DIM, n_pad), BF16),
                        pltpu.VMEM((tk + N_META, tq), F32),
                        pltpu.VMEM((tk, tq), F32),
                        pltpu.VMEM((2, 1, tq), F32),
                        pltpu.VMEM((2, 1, tq), F32),
                        pltpu.VMEM((2, V_DIM, tq), F32)],
        compiler_params=pltpu.CompilerParams(
            dimension_semantics=("arbitrary", "arbitrary", "arbitrary"),
            vmem_limit_bytes=VMEM_LIMIT_ATTN),
        name="diff_attention",
    )(rel_bias, lam_q1, lam_k1, lam_q2, lam_k2, qkv, qkv, qkv, qkv_meta, qkv_meta, subln_g)


def _scan_block(a, b):
    n = a.shape[0]
    row = lax.broadcasted_iota(jnp.int32, a.shape, 0)
    s = 1
    while s < n:
        keep = row >= s
        a_sh = jnp.where(keep, pltpu.roll(a, s, 0), 1.0)
        b_sh = jnp.where(keep, pltpu.roll(b, s, 0), 0.0)
        b = b + a * b_sh
        a = a * a_sh
        s *= 2
    return a, b


def _lru_kernel(u_ref, um_ref, g_ref, cw_ref, cb_ref, wa_ref, wx_ref, ba_ref, bx_ref, lam_ref,
                o_ref, ubuf, h_ref):
    c = pl.program_id(1)
    hist = 8

    x = -lam_ref[...]
    softplus = jnp.maximum(x, 0.0) + jnp.log1p(jnp.exp(-jnp.abs(x)))

    def chunk(n, first):
        cw = cw_ref[...]
        uc = (cw[3:4] * ubuf[pl.ds(hist, n), :] + cw[2:3] * ubuf[pl.ds(hist - 1, n), :]
              + cw[1:2] * ubuf[pl.ds(hist - 2, n), :] + cw[0:1] * ubuf[pl.ds(hist - 3, n), :]
              + cb_ref[...])
        ucb = uc.astype(BF16)
        outs = []
        for blk in range(N_LRU_BLOCKS):
            cols = slice(blk * LRU_BLOCK, (blk + 1) * LRU_BLOCK)
            ub = ucb[:, cols]
            r = jax.nn.sigmoid(_dot(ub, wa_ref[blk].astype(BF16)) + ba_ref[:, cols])
            gi = jax.nn.sigmoid(_dot(ub, wx_ref[blk].astype(BF16)) + bx_ref[:, cols])
            log_a = -LRU_C * r * softplus[:, cols]
            a = jnp.exp(log_a)
            mult = jnp.sqrt(jnp.tanh(-log_a) * (a * a + 1.0))
            if first:
                row = lax.broadcasted_iota(jnp.int32, mult.shape, 0)
                mult = jnp.where(row == 0, 1.0, mult)
            b = mult * gi * uc[:, cols]
            carry = h_ref[:, cols]
            pieces = []
            for s0 in range(0, n, SCAN_ROWS):
                s1 = min(s0 + SCAN_ROWS, n)
                a_cum, b_cum = _scan_block(a[s0:s1], b[s0:s1])
                hblk = b_cum + a_cum * carry
                carry = hblk[s1 - s0 - 1:s1 - s0, :]
                pieces.append(hblk)
            h_ref[:, cols] = carry
            outs.append(pieces[0] if len(pieces) == 1 else jnp.concatenate(pieces, axis=0))
        return outs

    @pl.when(c == 0)
    def _():
        ubuf[pl.ds(0, hist), :] = jnp.zeros((hist, D_LRU), F32)
        ubuf[pl.ds(hist, N_META), :] = um_ref[...]
        h_ref[...] = jnp.zeros_like(h_ref)
        chunk(N_META, True)
        ubuf[pl.ds(0, hist), :] = ubuf[pl.ds(N_META, hist), :]

    n = u_ref.shape[0]
    ubuf[pl.ds(hist, n), :] = u_ref[...]
    outs = chunk(n, False)
    g = g_ref[...]
    for blk in range(N_LRU_BLOCKS):
        cols = slice(blk * LRU_BLOCK, (blk + 1) * LRU_BLOCK)
        gb = g[:, cols]
        o_ref[:, cols] = (outs[blk] * (gb * jax.nn.sigmoid(gb))).astype(o_ref.dtype)
    ubuf[pl.ds(0, hist), :] = ubuf[pl.ds(n, hist), :]


def _lru(gates, gates_meta, conv_w, conv_b, w_a, b_a, w_x, b_x, lru_lambda, *, batch, seq):
    tc = LRU_ROWS
    nc = seq // tc
    row = lambda n: pl.BlockSpec((n, D_LRU), lambda b, c: (0, 0))
    wspec = pl.BlockSpec((N_LRU_BLOCKS, LRU_BLOCK, LRU_BLOCK), lambda b, c: (0, 0, 0))

    def kernel(u_ref, um_ref, g_ref, *rest):
        _lru_kernel(u_ref.at[0], um_ref.at[0], g_ref.at[0], *rest)

    return pl.pallas_call(
        kernel,
        grid=(batch, nc),
        in_specs=[pl.BlockSpec((1, tc, D_LRU), lambda b, c: (1, b * nc + c, 0)),
                  pl.BlockSpec((1, N_META, D_LRU), lambda b, c: (1, 0, 0)),
                  pl.BlockSpec((1, tc, D_LRU), lambda b, c: (2, b * nc + c, 0)),
                  row(CONV_WIDTH), row(1), wspec, wspec, row(1), row(1), row(1)],
        out_specs=pl.BlockSpec((tc, D_LRU), lambda b, c: (b * nc + c, 0)),
        out_shape=jax.ShapeDtypeStruct((batch * seq, D_LRU), BF16),
        scratch_shapes=[pltpu.VMEM((8 + tc, D_LRU), F32),
                        pltpu.VMEM((1, D_LRU), F32)],
        compiler_params=pltpu.CompilerParams(
            dimension_semantics=("arbitrary", "arbitrary"),
            vmem_limit_bytes=VMEM_LIMIT_LRU),
        name="conv_rglru",
    )(gates, gates_meta, gates, conv_w, conv_b, w_a, w_x, b_a, b_x, lru_lambda)


def _out_kernel(x_ref, att_ref, g_ref, rec_ref, w_ref, fg_ref, o_ref, wb_ref):
    @pl.when(pl.program_id(0) == 0)
    def _():
        wb_ref[...] = w_ref[...].astype(BF16)

    g = g_ref[0]
    att = (att_ref[...] * (g * jax.nn.sigmoid(g))).astype(BF16)
    y = _dot(att, wb_ref[pl.ds(0, D_ATTN), :]) + _dot(rec_ref[...], wb_ref[pl.ds(D_ATTN, D_LRU), :])
    o_ref[...] = _rms(x_ref[...] + y, fg_ref[...], NORM_EPS)


def _out_project(x2d, att, gates, rec, w_out, final_g):
    rows = x2d.shape[0]
    tm = OUT_ROWS
    return pl.pallas_call(
        _out_kernel,
        grid=(rows // tm,),
        in_specs=[pl.BlockSpec((tm, D_MODEL), lambda i: (i, 0)),
                  pl.BlockSpec((tm, D_ATTN), lambda i: (i, 0)),
                  pl.BlockSpec((1, tm, D_ATTN), lambda i: (0, i, 0)),
                  pl.BlockSpec((tm, D_LRU), lambda i: (i, 0)),
                  pl.BlockSpec((D_ATTN + D_LRU, D_MODEL), lambda i: (0, 0),
                               pipeline_mode=pl.Buffered(1)),
                  pl.BlockSpec((1, D_MODEL), lambda i: (0, 0))],
        out_specs=pl.BlockSpec((tm, D_MODEL), lambda i: (i, 0)),
        out_shape=jax.ShapeDtypeStruct((rows, D_MODEL), F32),
        scratch_shapes=[pltpu.VMEM((D_ATTN + D_LRU, D_MODEL), BF16)],
        compiler_params=pltpu.CompilerParams(
            dimension_semantics=("arbitrary",),
            vmem_limit_bytes=VMEM_LIMIT_OUT),
        name="out_proj_norm",
    )(x2d, att, gates, rec, w_out, final_g)


def kernel(x, meta_tokens, rel_bias, norm_g, w_in, conv_w, conv_b, w_a, b_a, w_x, b_x, lru_lambda,
           lam_q1, lam_k1, lam_q2, lam_k2, subln_g, w_out, final_g):
    batch, seq, _ = x.shape
    x2d = x.reshape(batch * seq, D_MODEL)
    qkv, qkv_meta = _project(x2d, meta_tokens, norm_g, w_in[0], col_tile0=0, head_major=True,
                             batch=batch, seq=seq)
    gates, gates_meta = _project(x2d, meta_tokens, norm_g, w_in[0], col_tile0=3, head_major=False,
                                 batch=batch, seq=seq)
    att = _attention(qkv, qkv_meta, rel_bias, lam_q1, lam_k1, lam_q2, lam_k2, subln_g)
    rec = _lru(gates, gates_meta, conv_w[0], conv_b, w_a[0], b_a, w_x[0], b_x, lru_lambda,
               batch=batch, seq=seq)
    out = _out_project(x2d, att, gates, rec, w_out[0], final_g.reshape(1, D_MODEL))
    return out.reshape(batch, seq, D_MODEL)
```

```python
import functools
import math

import numpy as np
import jax
import jax.numpy as jnp
from jax import lax
from jax.experimental import pallas as pl
from jax.experimental.pallas import tpu as pltpu

D_MODEL = 2048
N_META = 16
D_ATTN = 1024
D_LRU = 1024
N_HEADS = 8
HEAD_DIM = 64
V_DIM = 128
N_LRU_BLOCKS = 8
LRU_BLOCK = 128
CONV_WIDTH = 4
LRU_C = 8.0
N_BUCKETS = 32
MAX_DISTANCE = 128
NORM_EPS = 1e-6
SUBLN_EPS = 1e-5
NEG_INF = -1e30
LAMBDA_INIT = 0.8 - 0.6 * math.exp(-0.3 * 0)

BF16 = jnp.bfloat16
F32 = jnp.float32

VMEM_LIMIT_PROJ = 48 * 1024 * 1024
VMEM_LIMIT_ATTN = 48 * 1024 * 1024
VMEM_LIMIT_LRU = 40 * 1024 * 1024
VMEM_LIMIT_OUT = 56 * 1024 * 1024

PROJ_ROWS = 512
ATTN_TQ = 512
ATTN_TK = 512
LRU_ROWS = 256
SCAN_ROWS = 128
OUT_ROWS = 256


def _bucket_thresholds():
    max_exact = N_BUCKETS // 2
    d = np.arange(0, 4 * MAX_DISTANCE, dtype=np.int64)
    val = (np.log(np.maximum(d, 1).astype(np.float64) / max_exact)
           / math.log(MAX_DISTANCE / max_exact) * (N_BUCKETS - max_exact))
    large = np.minimum(max_exact + np.floor(val + 1e-9).astype(np.int64), N_BUCKETS - 1)
    bucket = np.where(d < max_exact, d, large)
    frac = np.abs(val - np.round(val))
    interior = (d > max_exact) & (d < MAX_DISTANCE)
    assert frac[interior].min() > 1e-3
    assert (np.diff(bucket) >= 0).all() and bucket[MAX_DISTANCE] == N_BUCKETS - 1
    return tuple(int(np.argmax(bucket >= j)) for j in range(1, N_BUCKETS))


BUCKET_THRESHOLDS = _bucket_thresholds()


def _rms(x, g, eps):
    y = x * lax.rsqrt(jnp.mean(x * x, axis=-1, keepdims=True) + eps)
    return y * g


def _dot(a, b):
    return jnp.dot(a, b, preferred_element_type=F32)


def _dot_nt(a, b):
    return lax.dot_general(a, b, (((1,), (1,)), ((), ())), preferred_element_type=F32)


def _proj_kernel(x_ref, meta_ref, g_ref, w_ref, o_ref, om_ref, wb_ref, *, head_major):
    j = pl.program_id(0)
    i = pl.program_id(1)
    out_scale = jnp.where(j == 0, HEAD_DIM ** -0.5, 1.0).astype(F32) if head_major else 1.0

    @pl.when(i == 0)
    def _():
        wb_ref[...] = w_ref[...].astype(BF16)
        hm = _rms(meta_ref[...], g_ref[...], NORM_EPS).astype(BF16)
        ym = _dot(hm, wb_ref[...]) * out_scale
        if head_major:
            for h in range(N_HEADS):
                om_ref[0, h] = ym[:, h * V_DIM:(h + 1) * V_DIM].astype(om_ref.dtype)
        else:
            om_ref[0] = ym.astype(om_ref.dtype)

    hx = _rms(x_ref[...], g_ref[...], NORM_EPS).astype(BF16)
    y = _dot(hx, wb_ref[...]) * out_scale
    if head_major:
        for h in range(N_HEADS):
            o_ref[0, 0, h] = y[:, h * V_DIM:(h + 1) * V_DIM].astype(o_ref.dtype)
    else:
        o_ref[0] = y.astype(o_ref.dtype)


def _project(x2d, meta, norm_g, w_in, *, col_tile0, head_major, batch, seq):
    rows = x2d.shape[0]
    tm = PROJ_ROWS
    n_i = rows // tm
    n_ib = seq // tm
    if head_major:
        out_shape = (jax.ShapeDtypeStruct((3, batch, N_HEADS, seq, V_DIM), BF16),
                     jax.ShapeDtypeStruct((3, N_HEADS, N_META, V_DIM), BF16))
        out_specs = (pl.BlockSpec((1, 1, N_HEADS, tm, V_DIM),
                                  lambda j, i: (j, i // n_ib, 0, i % n_ib, 0)),
                     pl.BlockSpec((1, N_HEADS, N_META, V_DIM), lambda j, i: (j, 0, 0, 0)))
    else:
        out_shape = (jax.ShapeDtypeStruct((3, rows, D_ATTN), F32),
                     jax.ShapeDtypeStruct((3, N_META, D_ATTN), F32))
        out_specs = (pl.BlockSpec((1, tm, D_ATTN), lambda j, i: (j, i, 0)),
                     pl.BlockSpec((1, N_META, D_ATTN), lambda j, i: (j, 0, 0)))
    return pl.pallas_call(
        functools.partial(_proj_kernel, head_major=head_major),
        grid=(3, n_i),
        in_specs=[pl.BlockSpec((tm, D_MODEL), lambda j, i: (i, 0)),
                  pl.BlockSpec((N_META, D_MODEL), lambda j, i: (0, 0)),
                  pl.BlockSpec((1, D_MODEL), lambda j, i: (0, 0)),
                  pl.BlockSpec((D_MODEL, D_ATTN), lambda j, i: (0, j + col_tile0))],
        out_specs=out_specs,
        out_shape=out_shape,
        scratch_shapes=[pltpu.VMEM((D_MODEL, D_ATTN), BF16)],
        compiler_params=pltpu.CompilerParams(
            dimension_semantics=("arbitrary", "arbitrary"),
            vmem_limit_bytes=VMEM_LIMIT_PROJ),
        name="proj_qkv" if head_major else "proj_gates",
    )(x2d, meta, norm_g, w_in)


def _toeplitz_bias(dist, rb_ref, h, far):
    b = jnp.full(dist.shape, rb_ref[0, h] - far, F32)
    for j, thr in enumerate(BUCKET_THRESHOLDS, start=1):
        b = jnp.where(dist >= thr, rb_ref[j, h] - far, b)
    return b


def _attn_kernel_old(rb_ref, lq1_ref, lk1_ref, lq2_ref, lk2_ref, q_ref, k_ref, v_ref, km_ref, vm_ref,
                 sg_ref, o_ref, kall, vall, vt_ref, bd_ref, bs_ref, m_ref, l_ref, acc_ref):
    h = pl.program_id(1)
    i = pl.program_id(2)
    tq, tk = ATTN_TQ, ATTN_TK
    seq = k_ref.shape[2]
    n_keys = seq + N_META
    n_pad = vt_ref.shape[1]

    far = rb_ref[N_BUCKETS - 1, h]

    @pl.when(i == 0)
    def _():
        kall[pl.ds(0, N_META), :] = km_ref[0]
        kall[pl.ds(N_META, seq), :] = k_ref[0, 0]
        vall[pl.ds(0, N_META), :] = vm_ref[0]
        vall[pl.ds(N_META, seq), :] = v_ref[0, 0]
        vall[pl.ds(n_keys, n_pad - n_keys), :] = jnp.zeros((n_pad - n_keys, V_DIM), BF16)

        def xpose(n, carry):
            rows = pl.ds(pl.multiple_of(n * V_DIM, V_DIM), V_DIM)
            vt_ref[:, rows] = vall[rows, :].astype(F32).T.astype(BF16)
            return carry

        lax.fori_loop(0, n_pad // V_DIM, xpose, 0)
        c = lax.broadcasted_iota(jnp.int32, (tk + N_META, tq), 0)
        r = lax.broadcasted_iota(jnp.int32, (tk + N_META, tq), 1)
        d_diag = N_META + r - c
        bd_ref[...] = jnp.where(d_diag >= 0, _toeplitz_bias(d_diag, rb_ref, h, far), NEG_INF)
        bs_ref[...] = _toeplitz_bias(d_diag[:tk] + tk, rb_ref, h, far)

    q = q_ref[0, 0]
    lane = lax.broadcasted_iota(jnp.int32, q.shape, 1)
    zero = jnp.zeros_like(q)
    qs = (jnp.where(lane < HEAD_DIM, q, zero), jnp.where(lane >= HEAD_DIM, q, zero))

    base = pl.multiple_of(i * tk, tk)
    k_d = kall[pl.ds(base, tk + N_META), :]
    v_d = vt_ref[:, pl.ds(base, tk)]
    v_c = vt_ref[:, pl.ds(base + tk, V_DIM)][:, :N_META]
    for a in range(2):
        bias = bd_ref[...]
        s = jnp.where(bias > 0.5 * NEG_INF, _dot_nt(k_d, qs[a]) + bias, NEG_INF)
        m = jnp.max(s, axis=0, keepdims=True)
        p = jnp.exp(s - m)
        m_ref[a] = m
        l_ref[a] = jnp.sum(p, axis=0, keepdims=True)
        pb = p.astype(BF16)
        acc_ref[a] = _dot(v_d, pb[:tk]) + _dot(v_c, pb[tk:])

    def online(a, s, v_t):
        m_old = m_ref[a]
        m_new = jnp.maximum(m_old, jnp.max(s, axis=0, keepdims=True))
        alpha = jnp.exp(m_old - m_new)
        p = jnp.exp(s - m_new)
        l_ref[a] = alpha * l_ref[a] + jnp.sum(p, axis=0, keepdims=True)
        acc_ref[a] = alpha * acc_ref[a] + _dot(v_t, p.astype(BF16))
        m_ref[a] = m_new

    @pl.when(i >= 1)
    def _():
        prev = pl.multiple_of((i - 1) * tk, tk)
        k_s = kall[pl.ds(prev, tk), :]
        v_s = vt_ref[:, pl.ds(prev, tk)]
        for a in range(2):
            online(a, _dot_nt(k_s, qs[a]) + bs_ref[...], v_s)

    def far_body(j, carry):
        off = pl.multiple_of(j * tk, tk)
        k_t = kall[pl.ds(off, tk), :]
        v_t = vt_ref[:, pl.ds(off, tk)]
        for a in range(2):
            online(a, _dot_nt(k_t, qs[a]), v_t)
        return carry

    lax.fori_loop(0, i - 1, far_body, 0)

    lam = (jnp.exp(jnp.sum(lq1_ref[...] * lk1_ref[...], keepdims=True))
           - jnp.exp(jnp.sum(lq2_ref[...] * lk2_ref[...], keepdims=True))
           + LAMBDA_INIT)
    out_t = acc_ref[0] / l_ref[0] - lam * (acc_ref[1] / l_ref[1])
    inv = lax.rsqrt(jnp.mean(out_t * out_t, axis=0, keepdims=True) + SUBLN_EPS)
    o_ref[...] = ((out_t * inv).T * sg_ref[...]) * (1.0 - LAMBDA_INIT)


def _attention_old(qkv, qkv_meta, rel_bias, lam_q1, lam_k1, lam_q2, lam_k2, subln_g):
    _, batch, _, seq, _ = qkv.shape
    tq, tk = ATTN_TQ, ATTN_TK
    nq = seq // tq
    n_pad = seq + V_DIM
    smem = pl.BlockSpec(memory_space=pltpu.SMEM)
    row64 = pl.BlockSpec((1, HEAD_DIM), lambda b, h, i: (0, 0))
    kv_spec = lambda which: pl.BlockSpec((1, 1, 1, seq, V_DIM), lambda b, h, i: (which, b, h, 0, 0))
    meta_spec = lambda which: pl.BlockSpec((1, 1, N_META, V_DIM), lambda b, h, i: (which, h, 0, 0))

    def kernel(rb, lq1, lk1, lq2, lk2, q_ref, k_ref, v_ref, km_ref, vm_ref, sg_ref, o_ref, *scratch):
        _attn_kernel(rb, lq1, lk1, lq2, lk2, q_ref.at[0], k_ref.at[0], v_ref.at[0],
                     km_ref.at[0], vm_ref.at[0], sg_ref, o_ref, *scratch)

    return pl.pallas_call(
        kernel,
        grid=(batch, N_HEADS, nq),
        in_specs=[smem, row64, row64, row64, row64,
                  pl.BlockSpec((1, 1, 1, tq, V_DIM), lambda b, h, i: (0, b, h, i, 0)),
                  kv_spec(1), kv_spec(2), meta_spec(1), meta_spec(2),
                  pl.BlockSpec((1, V_DIM), lambda b, h, i: (0, 0))],
        out_specs=pl.BlockSpec((tq, V_DIM), lambda b, h, i: (b * nq + i, h)),
        out_shape=jax.ShapeDtypeStruct((batch * seq, D_ATTN), F32),
        scratch_shapes=[pltpu.VMEM((seq + N_META, V_DIM), BF16),
                        pltpu.VMEM((n_pad, V_DIM), BF16),
                        pltpu.VMEM((V_DIM, n_pad), BF16),
                        pltpu.VMEM((tk + N_META, tq), F32),
                        pltpu.VMEM((tk, tq), F32),
                        pltpu.VMEM((2, 1, tq), F32),
                        pltpu.VMEM((2, 1, tq), F32),
                        pltpu.VMEM((2, V_DIM, tq), F32)],
        compiler_params=pltpu.CompilerParams(
            dimension_semantics=("arbitrary", "arbitrary", "arbitrary"),
            vmem_limit_bytes=VMEM_LIMIT_ATTN),
        name="diff_attention",
    )(rel_bias, lam_q1, lam_k1, lam_q2, lam_k2, qkv, qkv, qkv, qkv_meta, qkv_meta, subln_g)


def _fill_bias(ref, lead, n_rows, n_cols, d0, rb_ref, h, far):
    sub = lax.broadcasted_iota(jnp.int32, (8, V_DIM), 0)
    lane = lax.broadcasted_iota(jnp.int32, (8, V_DIM), 1)
    zeros = jnp.zeros((8, V_DIM), F32)
    masked = jnp.full((8, V_DIM), NEG_INF, F32)
    cache = {}
    for a8 in range(n_rows // 8):
        for b in range(n_cols // V_DIM):
            off = d0 + V_DIM * b - 8 * a8
            if off + V_DIM - 1 < 0:
                tile = masked
            elif off - 7 >= MAX_DISTANCE:
                tile = zeros
            else:
                if off not in cache:
                    d = off + lane - sub
                    cache[off] = jnp.where(d >= 0, _toeplitz_bias(d, rb_ref, h, far), NEG_INF)
                tile = cache[off]
            ref[(*lead, pl.ds(8 * a8, 8), pl.ds(V_DIM * b, V_DIM))] = tile


def _attn_kernel(rb_ref, lq1_ref, lk1_ref, lq2_ref, lk2_ref, q_ref, k_ref, v_ref, km_ref, vm_ref,
                 sg_ref, o_ref, kall, vall, vt_ref, bd_ref, bc_ref, sel_ref, sbuf, sc_ref, mx_ref,
                 m_ref, l_ref, acc_ref):
    h = pl.program_id(1)
    tq, tk = ATTN_TQ, ATTN_TK
    seq = k_ref.shape[2]
    nq = seq // tq
    n_keys = seq + N_META
    n_pad = vt_ref.shape[1]

    far = rb_ref[N_BUCKETS - 1, h]

    kall[pl.ds(0, N_META), :] = km_ref[0]
    kall[pl.ds(N_META, seq), :] = k_ref[0, 0]
    vall[pl.ds(0, N_META), :] = vm_ref[0]
    vall[pl.ds(N_META, seq), :] = v_ref[0, 0]
    vall[pl.ds(n_keys, n_pad - n_keys), :] = jnp.zeros((n_pad - n_keys, V_DIM), BF16)

    def xpose(n, carry):
        rows = pl.ds(pl.multiple_of(n * V_DIM, V_DIM), V_DIM)
        vt_ref[:, rows] = vall[rows, :].astype(F32).T.astype(BF16)
        return carry

    lax.fori_loop(0, n_pad // V_DIM, xpose, 0, unroll=3)

    _fill_bias(bd_ref, (), tk, tq, N_META, rb_ref, h, far)
    _fill_bias(bc_ref, (), N_META, tq, N_META - tk, rb_ref, h, far)
    _fill_bias(sel_ref, (1,), tk, tq, N_META + tk, rb_ref, h, far)
    sel_ref[0] = jnp.zeros((tk, tq), F32)

    lane = lax.broadcasted_iota(jnp.int32, (tq, V_DIM), 1)
    lam = (jnp.exp(jnp.sum(lq1_ref[...] * lk1_ref[...], keepdims=True))
           - jnp.exp(jnp.sum(lq2_ref[...] * lk2_ref[...], keepdims=True))
           + LAMBDA_INIT)

    def init_stats():
        m_ref[...] = jnp.full(m_ref.shape, NEG_INF, F32)
        l_ref[...] = jnp.zeros(l_ref.shape, F32)
        acc_ref[...] = jnp.zeros(acc_ref.shape, F32)

    def produce(q_off, k_off, kind, sel=None):
        q = q_ref[0, 0, pl.ds(pl.multiple_of(q_off, tq), tq), :]
        zero = jnp.zeros_like(q)
        k_off = pl.multiple_of(k_off, tk)
        k_t = kall[pl.ds(k_off, tk), :]
        for a in range(2):
            qa = jnp.where((lane < HEAD_DIM) if a == 0 else (lane >= HEAD_DIM), q, zero)
            s = _dot_nt(k_t, qa)
            if kind == "near":
                s = s + sel_ref[1]
            elif kind == "diag":
                s = s + bd_ref[...]
            elif kind == "select":
                s = s + sel_ref[sel]
            sbuf[a] = s
            mx = jnp.max(s, axis=0, keepdims=True)
            if kind == "diag":
                sc = _dot_nt(kall[pl.ds(k_off + tk, N_META), :], qa) + bc_ref[...]
                sc_ref[a] = sc
                mx = jnp.maximum(mx, jnp.max(sc, axis=0, keepdims=True))
            mx_ref[a] = mx

    def consume(k_off, diag=False):
        k_off = pl.multiple_of(k_off, tk)
        v_t = vt_ref[:, pl.ds(k_off, tk)]
        for a in range(2):
            m_old = m_ref[a]
            m_new = jnp.maximum(m_old, mx_ref[a])
            alpha = jnp.exp(m_old - m_new)
            p = jnp.exp(sbuf[a] - m_new)
            lsum = jnp.sum(p, axis=0, keepdims=True)
            pv = _dot(v_t, p.astype(BF16))
            if diag:
                pc = jnp.exp(sc_ref[a] - m_new)
                lsum = lsum + jnp.sum(pc, axis=0, keepdims=True)
                v_c = vt_ref[:, pl.ds(k_off + tk, V_DIM)][:, :N_META]
                pv = pv + _dot(v_c, pc.astype(BF16))
            l_ref[a] = alpha * l_ref[a] + lsum
            acc_ref[a] = alpha * acc_ref[a] + pv
            m_ref[a] = m_new

    def finalize(q_off):
        out_t = acc_ref[0] / l_ref[0] - lam * (acc_ref[1] / l_ref[1])
        inv = lax.rsqrt(jnp.mean(out_t * out_t, axis=0, keepdims=True) + SUBLN_EPS)
        o_ref[pl.ds(pl.multiple_of(q_off, tq), tq), :] = (
            ((out_t * inv).T * sg_ref[...]) * (1.0 - LAMBDA_INIT))

    init_stats()
    produce(0, 0, "diag")

    def q_body(i, carry):
        q_off = i * tq

        def far_body(t, c):
            consume(t * tk)
            produce(q_off, (t + 1) * tk, "far")
            return c

        lax.fori_loop(0, i - 2, far_body, 0)

        @pl.when(i >= 2)
        def _():
            consume((i - 2) * tk)
            produce(q_off, (i - 1) * tk, "near")

        @pl.when(i >= 1)
        def _():
            consume((i - 1) * tk)
            produce(q_off, i * tk, "diag")

        consume(i * tk, diag=True)
        finalize(q_off)
        init_stats()
        nxt = jnp.minimum(i + 1, nq - 1)
        produce(nxt * tq, 0, "select", sel=jnp.where(nxt == 1, 1, 0))
        return carry

    lax.fori_loop(0, nq, q_body, 0)


def _attention(qkv, qkv_meta, rel_bias, lam_q1, lam_k1, lam_q2, lam_k2, subln_g):
    _, batch, _, seq, _ = qkv.shape
    tq, tk = ATTN_TQ, ATTN_TK
    n_pad = seq + V_DIM
    smem = pl.BlockSpec(memory_space=pltpu.SMEM)
    row64 = pl.BlockSpec((1, HEAD_DIM), lambda b, h: (0, 0))
    qkv_spec = lambda which: pl.BlockSpec((1, 1, 1, seq, V_DIM), lambda b, h: (which, b, h, 0, 0))
    meta_spec = lambda which: pl.BlockSpec((1, 1, N_META, V_DIM), lambda b, h: (which, h, 0, 0))

    def kernel(rb, lq1, lk1, lq2, lk2, q_ref, k_ref, v_ref, km_ref, vm_ref, sg_ref, o_ref, *scratch):
        _attn_kernel(rb, lq1, lk1, lq2, lk2, q_ref.at[0], k_ref.at[0], v_ref.at[0],
                     km_ref.at[0], vm_ref.at[0], sg_ref, o_ref, *scratch)

    return pl.pallas_call(
        kernel,
        grid=(batch, N_HEADS),
        in_specs=[smem, row64, row64, row64, row64,
                  qkv_spec(0), qkv_spec(1), qkv_spec(2), meta_spec(1), meta_spec(2),
                  pl.BlockSpec((1, V_DIM), lambda b, h: (0, 0))],
        out_specs=pl.BlockSpec((seq, V_DIM), lambda b, h: (b, h)),
        out_shape=jax.ShapeDtypeStruct((batch * seq, D_ATTN), F32),
        scratch_shapes=[pltpu.VMEM((seq + N_META, V_DIM), BF16),
                        pltpu.VMEM((n_pad, V_DIM), BF16),
                        pltpu.VMEM((V_DIM, n_pad), BF16),
                        pltpu.VMEM((tk, tq), F32),
                        pltpu.VMEM((N_META, tq), F32),
                        pltpu.VMEM((2, tk, tq), F32),
                        pltpu.VMEM((2, tk, tq), F32),
                        pltpu.VMEM((2, N_META, tq), F32),
                        pltpu.VMEM((2, 1, tq), F32),
                        pltpu.VMEM((2, 1, tq), F32),
                        pltpu.VMEM((2, 1, tq), F32),
                        pltpu.VMEM((2, V_DIM, tq), F32)],
        compiler_params=pltpu.CompilerParams(
            dimension_semantics=("arbitrary", "arbitrary"),
            vmem_limit_bytes=VMEM_LIMIT_ATTN),
        name="diff_attention",
    )(rel_bias, lam_q1, lam_k1, lam_q2, lam_k2, qkv, qkv, qkv, qkv_meta, qkv_meta, subln_g)


def _scan_block(a, b):
    n = a.shape[0]
    row = lax.broadcasted_iota(jnp.int32, a.shape, 0)
    s = 1
    while s < n:
        keep = row >= s
        a_sh = jnp.where(keep, pltpu.roll(a, s, 0), 1.0)
        b_sh = jnp.where(keep, pltpu.roll(b, s, 0), 0.0)
        b = b + a * b_sh
        a = a * a_sh
        s *= 2
    return a, b


def _lru_kernel(u_ref, um_ref, g_ref, cw_ref, cb_ref, wa_ref, wx_ref, ba_ref, bx_ref, lam_ref,
                o_ref, ubuf, h_ref):
    c = pl.program_id(1)
    hist = 8

    x = -lam_ref[...]
    softplus = jnp.maximum(x, 0.0) + jnp.log1p(jnp.exp(-jnp.abs(x)))

    def chunk(n, first):
        cw = cw_ref[...]
        uc = (cw[3:4] * ubuf[pl.ds(hist, n), :] + cw[2:3] * ubuf[pl.ds(hist - 1, n), :]
              + cw[1:2] * ubuf[pl.ds(hist - 2, n), :] + cw[0:1] * ubuf[pl.ds(hist - 3, n), :]
              + cb_ref[...])
        ucb = uc.astype(BF16)
        outs = []
        for blk in range(N_LRU_BLOCKS):
            cols = slice(blk * LRU_BLOCK, (blk + 1) * LRU_BLOCK)
            ub = ucb[:, cols]
            r = jax.nn.sigmoid(_dot(ub, wa_ref[blk].astype(BF16)) + ba_ref[:, cols])
            gi = jax.nn.sigmoid(_dot(ub, wx_ref[blk].astype(BF16)) + bx_ref[:, cols])
            log_a = -LRU_C * r * softplus[:, cols]
            a = jnp.exp(log_a)
            mult = jnp.sqrt(jnp.tanh(-log_a) * (a * a + 1.0))
            if first:
                row = lax.broadcasted_iota(jnp.int32, mult.shape, 0)
                mult = jnp.where(row == 0, 1.0, mult)
            b = mult * gi * uc[:, cols]
            carry = h_ref[:, cols]
            pieces = []
            for s0 in range(0, n, SCAN_ROWS):
                s1 = min(s0 + SCAN_ROWS, n)
                a_cum, b_cum = _scan_block(a[s0:s1], b[s0:s1])
                hblk = b_cum + a_cum * carry
                carry = hblk[s1 - s0 - 1:s1 - s0, :]
                pieces.append(hblk)
            h_ref[:, cols] = carry
            outs.append(pieces[0] if len(pieces) == 1 else jnp.concatenate(pieces, axis=0))
        return outs

    @pl.when(c == 0)
    def _():
        ubuf[pl.ds(0, hist), :] = jnp.zeros((hist, D_LRU), F32)
        ubuf[pl.ds(hist, N_META), :] = um_ref[...]
        h_ref[...] = jnp.zeros_like(h_ref)
        chunk(N_META, True)
        ubuf[pl.ds(0, hist), :] = ubuf[pl.ds(N_META, hist), :]

    n = u_ref.shape[0]
    ubuf[pl.ds(hist, n), :] = u_ref[...]
    outs = chunk(n, False)
    g = g_ref[...]
    for blk in range(N_LRU_BLOCKS):
        cols = slice(blk * LRU_BLOCK, (blk + 1) * LRU_BLOCK)
        gb = g[:, cols]
        o_ref[:, cols] = (outs[blk] * (gb * jax.nn.sigmoid(gb))).astype(o_ref.dtype)
    ubuf[pl.ds(0, hist), :] = ubuf[pl.ds(n, hist), :]


def _lru(gates, gates_meta, conv_w, conv_b, w_a, b_a, w_x, b_x, lru_lambda, *, batch, seq):
    tc = LRU_ROWS
    nc = seq // tc
    row = lambda n: pl.BlockSpec((n, D_LRU), lambda b, c: (0, 0))
    wspec = pl.BlockSpec((N_LRU_BLOCKS, LRU_BLOCK, LRU_BLOCK), lambda b, c: (0, 0, 0))

    def kernel(u_ref, um_ref, g_ref, *rest):
        _lru_kernel(u_ref.at[0], um_ref.at[0], g_ref.at[0], *rest)

    return pl.pallas_call(
        kernel,
        grid=(batch, nc),
        in_specs=[pl.BlockSpec((1, tc, D_LRU), lambda b, c: (1, b * nc + c, 0)),
                  pl.BlockSpec((1, N_META, D_LRU), lambda b, c: (1, 0, 0)),
                  pl.BlockSpec((1, tc, D_LRU), lambda b, c: (2, b * nc + c, 0)),
                  row(CONV_WIDTH), row(1), wspec, wspec, row(1), row(1), row(1)],
        out_specs=pl.BlockSpec((tc, D_LRU), lambda b, c: (b * nc + c, 0)),
        out_shape=jax.ShapeDtypeStruct((batch * seq, D_LRU), BF16),
        scratch_shapes=[pltpu.VMEM((8 + tc, D_LRU), F32),
                        pltpu.VMEM((1, D_LRU), F32)],
        compiler_params=pltpu.CompilerParams(
            dimension_semantics=("arbitrary", "arbitrary"),
            vmem_limit_bytes=VMEM_LIMIT_LRU),
        name="conv_rglru",
    )(gates, gates_meta, gates, conv_w, conv_b, w_a, w_x, b_a, b_x, lru_lambda)


def _out_kernel(x_ref, att_ref, g_ref, rec_ref, w_ref, fg_ref, o_ref, wb_ref):
    @pl.when(pl.program_id(0) == 0)
    def _():
        wb_ref[...] = w_ref[...].astype(BF16)

    g = g_ref[0]
    att = (att_ref[...] * (g * jax.nn.sigmoid(g))).astype(BF16)
    y = _dot(att, wb_ref[pl.ds(0, D_ATTN), :]) + _dot(rec_ref[...], wb_ref[pl.ds(D_ATTN, D_LRU), :])
    o_ref[...] = _rms(x_ref[...] + y, fg_ref[...], NORM_EPS)


def _out_project(x2d, att, gates, rec, w_out, final_g):
    rows = x2d.shape[0]
    tm = OUT_ROWS
    return pl.pallas_call(
        _out_kernel,
        grid=(rows // tm,),
        in_specs=[pl.BlockSpec((tm, D_MODEL), lambda i: (i, 0)),
                  pl.BlockSpec((tm, D_ATTN), lambda i: (i, 0)),
                  pl.BlockSpec((1, tm, D_ATTN), lambda i: (0, i, 0)),
                  pl.BlockSpec((tm, D_LRU), lambda i: (i, 0)),
                  pl.BlockSpec((D_ATTN + D_LRU, D_MODEL), lambda i: (0, 0),
                               pipeline_mode=pl.Buffered(1)),
                  pl.BlockSpec((1, D_MODEL), lambda i: (0, 0))],
        out_specs=pl.BlockSpec((tm, D_MODEL), lambda i: (i, 0)),
        out_shape=jax.ShapeDtypeStruct((rows, D_MODEL), F32),
        scratch_shapes=[pltpu.VMEM((D_ATTN + D_LRU, D_MODEL), BF16)],
        compiler_params=pltpu.CompilerParams(
            dimension_semantics=("arbitrary",),
            vmem_limit_bytes=VMEM_LIMIT_OUT),
        name="out_proj_norm",
    )(x2d, att, gates, rec, w_out, final_g)


def kernel(x, meta_tokens, rel_bias, norm_g, w_in, conv_w, conv_b, w_a, b_a, w_x, b_x, lru_lambda,
           lam_q1, lam_k1, lam_q2, lam_k2, subln_g, w_out, final_g):
    batch, seq, _ = x.shape
    x2d = x.reshape(batch * seq, D_MODEL)
    qkv, qkv_meta = _project(x2d, meta_tokens, norm_g, w_in[0], col_tile0=0, head_major=True,
                             batch=batch, seq=seq)
    gates, gates_meta = _project(x2d, meta_tokens, norm_g, w_in[0], col_tile0=3, head_major=False,
                                 batch=batch, seq=seq)
    att = _attention(qkv, qkv_meta, rel_bias, lam_q1, lam_k1, lam_q2, lam_k2, subln_g)
    rec = _lru(gates, gates_meta, conv_w[0], conv_b, w_a[0], b_a, w_x[0], b_x, lru_lambda,
               batch=batch, seq=seq)
    out = _out_project(x2d, att, gates, rec, w_out[0], final_g.reshape(1, D_MODEL))
    return out.reshape(batch, seq, D_MODEL)
```

```python
import functools
import math

import numpy as np
import jax
import jax.numpy as jnp
from jax import lax
from jax.experimental import pallas as pl
from jax.experimental.pallas import tpu as pltpu

D_MODEL = 2048
N_META = 16
D_ATTN = 1024
D_LRU = 1024
N_HEADS = 8
HEAD_DIM = 64
V_DIM = 128
N_LRU_BLOCKS = 8
LRU_BLOCK = 128
CONV_WIDTH = 4
LRU_C = 8.0
N_BUCKETS = 32
MAX_DISTANCE = 128
NORM_EPS = 1e-6
SUBLN_EPS = 1e-5
NEG_INF = -1e30
LAMBDA_INIT = 0.8 - 0.6 * math.exp(-0.3 * 0)

BF16 = jnp.bfloat16
F32 = jnp.float32

VMEM_LIMIT_PROJ = 48 * 1024 * 1024
VMEM_LIMIT_ATTN = 48 * 1024 * 1024
VMEM_LIMIT_LRU = 40 * 1024 * 1024
VMEM_LIMIT_OUT = 56 * 1024 * 1024

PROJ_ROWS = 512
ATTN_TQ = 512
ATTN_TK = 512
SUM_ROWS = 16
LRU_ROWS = 256
SCAN_ROWS = 128
OUT_ROWS = 256


def _bucket_thresholds():
    max_exact = N_BUCKETS // 2
    d = np.arange(0, 4 * MAX_DISTANCE, dtype=np.int64)
    val = (np.log(np.maximum(d, 1).astype(np.float64) / max_exact)
           / math.log(MAX_DISTANCE / max_exact) * (N_BUCKETS - max_exact))
    large = np.minimum(max_exact + np.floor(val + 1e-9).astype(np.int64), N_BUCKETS - 1)
    bucket = np.where(d < max_exact, d, large)
    frac = np.abs(val - np.round(val))
    interior = (d > max_exact) & (d < MAX_DISTANCE)
    assert frac[interior].min() > 1e-3
    assert (np.diff(bucket) >= 0).all() and bucket[MAX_DISTANCE] == N_BUCKETS - 1
    return tuple(int(np.argmax(bucket >= j)) for j in range(1, N_BUCKETS))


BUCKET_THRESHOLDS = _bucket_thresholds()


def _rms(x, g, eps):
    y = x * lax.rsqrt(jnp.mean(x * x, axis=-1, keepdims=True) + eps)
    return y * g


def _dot(a, b):
    return jnp.dot(a, b, preferred_element_type=F32)


def _dot_nt(a, b):
    return lax.dot_general(a, b, (((1,), (1,)), ((), ())), preferred_element_type=F32)


def _proj_kernel(x_ref, meta_ref, g_ref, w_ref, o_ref, om_ref, wb_ref, *, head_major):
    j = pl.program_id(0)
    i = pl.program_id(1)
    out_scale = jnp.where(j == 0, HEAD_DIM ** -0.5, 1.0).astype(F32) if head_major else 1.0

    @pl.when(i == 0)
    def _():
        wb_ref[...] = w_ref[...].astype(BF16)
        hm = _rms(meta_ref[...], g_ref[...], NORM_EPS).astype(BF16)
        ym = _dot(hm, wb_ref[...]) * out_scale
        if head_major:
            for h in range(N_HEADS):
                om_ref[0, h] = ym[:, h * V_DIM:(h + 1) * V_DIM].astype(om_ref.dtype)
        else:
            om_ref[0] = ym.astype(om_ref.dtype)

    hx = _rms(x_ref[...], g_ref[...], NORM_EPS).astype(BF16)
    y = _dot(hx, wb_ref[...]) * out_scale
    if head_major:
        for h in range(N_HEADS):
            o_ref[0, 0, h] = y[:, h * V_DIM:(h + 1) * V_DIM].astype(o_ref.dtype)
    else:
        o_ref[0] = y.astype(o_ref.dtype)


def _project(x2d, meta, norm_g, w_in, *, col_tile0, head_major, batch, seq):
    rows = x2d.shape[0]
    tm = PROJ_ROWS
    n_i = rows // tm
    n_ib = seq // tm
    if head_major:
        out_shape = (jax.ShapeDtypeStruct((3, batch, N_HEADS, seq, V_DIM), BF16),
                     jax.ShapeDtypeStruct((3, N_HEADS, N_META, V_DIM), BF16))
        out_specs = (pl.BlockSpec((1, 1, N_HEADS, tm, V_DIM),
                                  lambda j, i: (j, i // n_ib, 0, i % n_ib, 0)),
                     pl.BlockSpec((1, N_HEADS, N_META, V_DIM), lambda j, i: (j, 0, 0, 0)))
    else:
        out_shape = (jax.ShapeDtypeStruct((3, rows, D_ATTN), F32),
                     jax.ShapeDtypeStruct((3, N_META, D_ATTN), F32))
        out_specs = (pl.BlockSpec((1, tm, D_ATTN), lambda j, i: (j, i, 0)),
                     pl.BlockSpec((1, N_META, D_ATTN), lambda j, i: (j, 0, 0)))
    return pl.pallas_call(
        functools.partial(_proj_kernel, head_major=head_major),
        grid=(3, n_i),
        in_specs=[pl.BlockSpec((tm, D_MODEL), lambda j, i: (i, 0)),
                  pl.BlockSpec((N_META, D_MODEL), lambda j, i: (0, 0)),
                  pl.BlockSpec((1, D_MODEL), lambda j, i: (0, 0)),
                  pl.BlockSpec((D_MODEL, D_ATTN), lambda j, i: (0, j + col_tile0))],
        out_specs=out_specs,
        out_shape=out_shape,
        scratch_shapes=[pltpu.VMEM((D_MODEL, D_ATTN), BF16)],
        compiler_params=pltpu.CompilerParams(
            dimension_semantics=("arbitrary", "arbitrary"),
            vmem_limit_bytes=VMEM_LIMIT_PROJ),
        name="proj_qkv" if head_major else "proj_gates",
    )(x2d, meta, norm_g, w_in)


def _toeplitz_bias(dist, rb_ref, h, far):
    b = jnp.full(dist.shape, rb_ref[0, h] - far, F32)
    for j, thr in enumerate(BUCKET_THRESHOLDS, start=1):
        b = jnp.where(dist >= thr, rb_ref[j, h] - far, b)
    return b


def _fill_bias(ref, lead, n_rows, n_cols, d0, rb_ref, h, far):
    sub = lax.broadcasted_iota(jnp.int32, (8, V_DIM), 0)
    lane = lax.broadcasted_iota(jnp.int32, (8, V_DIM), 1)
    zeros = jnp.zeros((8, V_DIM), F32)
    masked = jnp.full((8, V_DIM), NEG_INF, F32)
    cache = {}
    for a8 in range(n_rows // 8):
        for b in range(n_cols // V_DIM):
            off = d0 + V_DIM * b - 8 * a8
            if off + V_DIM - 1 < 0:
                tile = masked
            elif off - 7 >= MAX_DISTANCE:
                tile = zeros
            else:
                if off not in cache:
                    d = off + lane - sub
                    cache[off] = jnp.where(d >= 0, _toeplitz_bias(d, rb_ref, h, far), NEG_INF)
                tile = cache[off]
            ref[(*lead, pl.ds(8 * a8, 8), pl.ds(V_DIM * b, V_DIM))] = tile


def _attn_kernel(rb_ref, lq1_ref, lk1_ref, lq2_ref, lk2_ref, q_ref, k_ref, v_ref, km_ref, vm_ref,
                 g_ref, sg_ref, o_ref, kall, vall, vt_ref, qz_ref, bd_ref, bc_ref, sel_ref, sbuf,
                 sc_ref, mx_ref, m_ref, acc_ref):
    h = pl.program_id(1)
    tq, tk = ATTN_TQ, ATTN_TK
    seq = k_ref.shape[2]
    nq = seq // tq
    n_keys = seq + N_META
    n_pad = vt_ref.shape[1]

    far = rb_ref[N_BUCKETS - 1, h]

    kall[pl.ds(0, N_META), :] = km_ref[0]
    kall[pl.ds(N_META, seq), :] = k_ref[0, 0]
    vall[pl.ds(0, N_META), :] = vm_ref[0]
    vall[pl.ds(N_META, seq), :] = v_ref[0, 0]
    vall[pl.ds(n_keys, n_pad - n_keys), :] = jnp.zeros((n_pad - n_keys, V_DIM), BF16)

    def xpose(n, carry):
        rows = pl.ds(pl.multiple_of(n * V_DIM, V_DIM), V_DIM)
        vt_ref[pl.ds(0, V_DIM), rows] = vall[rows, :].astype(F32).T.astype(BF16)
        return carry

    lax.fori_loop(0, n_pad // V_DIM, xpose, 0, unroll=3)
    vt_ref[pl.ds(V_DIM, SUM_ROWS), :] = jnp.ones((SUM_ROWS, n_pad), BF16)

    q_all = q_ref[0, 0]
    q_lane = lax.broadcasted_iota(jnp.int32, q_all.shape, 1)
    qz_ref[0] = jnp.where(q_lane < HEAD_DIM, q_all, jnp.zeros_like(q_all))
    qz_ref[1] = jnp.where(q_lane >= HEAD_DIM, q_all, jnp.zeros_like(q_all))

    _fill_bias(bd_ref, (), tk, tq, N_META, rb_ref, h, far)
    _fill_bias(bc_ref, (), N_META, tq, N_META - tk, rb_ref, h, far)
    _fill_bias(sel_ref, (1,), tk, tq, N_META + tk, rb_ref, h, far)
    sel_ref[0] = jnp.zeros((tk, tq), F32)

    lam = (jnp.exp(jnp.sum(lq1_ref[...] * lk1_ref[...], keepdims=True))
           - jnp.exp(jnp.sum(lq2_ref[...] * lk2_ref[...], keepdims=True))
           + LAMBDA_INIT)

    def init_stats():
        m_ref[...] = jnp.full(m_ref.shape, NEG_INF, F32)
        acc_ref[...] = jnp.zeros(acc_ref.shape, F32)

    def produce(q_off, k_off, kind, sel=None):
        q_rows = pl.ds(pl.multiple_of(q_off, tq), tq)
        k_off = pl.multiple_of(k_off, tk)
        k_t = kall[pl.ds(k_off, tk), :]
        for a in range(2):
            qa = qz_ref[a, q_rows, :]
            s = _dot_nt(k_t, qa)
            if kind == "near":
                s = s + sel_ref[1]
            elif kind == "diag":
                s = s + bd_ref[...]
            elif kind == "select":
                s = s + sel_ref[sel]
            sbuf[a] = s
            mx = jnp.max(s, axis=0, keepdims=True)
            if kind == "diag":
                sc = _dot_nt(kall[pl.ds(k_off + tk, N_META), :], qa) + bc_ref[...]
                sc_ref[a] = sc
                mx = jnp.maximum(mx, jnp.max(sc, axis=0, keepdims=True))
            mx_ref[a] = mx

    def consume(k_off, diag=False):
        k_off = pl.multiple_of(k_off, tk)
        v_t = vt_ref[:, pl.ds(k_off, tk)]
        for a in range(2):
            m_old = m_ref[a]
            m_new = jnp.maximum(m_old, mx_ref[a])
            alpha = jnp.exp(m_old - m_new)
            p = jnp.exp(sbuf[a] - m_new)
            pv = _dot(v_t, p.astype(BF16))
            if diag:
                pc = jnp.exp(sc_ref[a] - m_new)
                v_c = vt_ref[:, pl.ds(k_off + tk, V_DIM)][:, :N_META]
                pv = pv + _dot(v_c, pc.astype(BF16))
            acc_ref[a] = alpha * acc_ref[a] + pv
            m_ref[a] = m_new

    def finalize(q_off):
        q_rows = pl.ds(pl.multiple_of(q_off, tq), tq)
        heads = [acc_ref[a, pl.ds(0, V_DIM), :] / acc_ref[a, pl.ds(V_DIM, 1), :] for a in range(2)]
        out_t = heads[0] - lam * heads[1]
        inv = lax.rsqrt(jnp.mean(out_t * out_t, axis=0, keepdims=True) + SUBLN_EPS)
        att = ((out_t * inv).T * sg_ref[...]) * (1.0 - LAMBDA_INIT)
        g = g_ref[0, q_rows, :]
        o_ref[q_rows, :] = (att * (g * jax.nn.sigmoid(g))).astype(o_ref.dtype)

    init_stats()
    produce(0, 0, "diag")

    def q_body(i, carry):
        q_off = i * tq

        def far_step(t):
            consume(t * tk)
            produce(q_off, (t + 1) * tk, "far")

        def far_pair(tt, c):
            far_step(2 * tt)
            far_step(2 * tt + 1)
            return c

        n_far = jnp.maximum(i - 2, 0)
        lax.fori_loop(0, n_far // 2, far_pair, 0)

        @pl.when(n_far % 2 == 1)
        def _():
            far_step(n_far - 1)

        @pl.when(i >= 2)
        def _():
            consume((i - 2) * tk)
            produce(q_off, (i - 1) * tk, "near")
            consume((i - 1) * tk)
            produce(q_off, i * tk, "diag")

        @pl.when(i == 1)
        def _():
            consume(0)
            produce(q_off, tk, "diag")

        consume(i * tk, diag=True)
        finalize(q_off)
        init_stats()
        nxt = jnp.minimum(i + 1, nq - 1)
        produce(nxt * tq, 0, "select", sel=jnp.where(nxt == 1, 1, 0))
        return carry

    lax.fori_loop(0, nq, q_body, 0)


def _attention(qkv, qkv_meta, gates, rel_bias, lam_q1, lam_k1, lam_q2, lam_k2, subln_g):
    _, batch, _, seq, _ = qkv.shape
    tq, tk = ATTN_TQ, ATTN_TK
    n_pad = seq + V_DIM
    smem = pl.BlockSpec(memory_space=pltpu.SMEM)
    row64 = pl.BlockSpec((1, HEAD_DIM), lambda b, h: (0, 0))
    qkv_spec = lambda which: pl.BlockSpec((1, 1, 1, seq, V_DIM), lambda b, h: (which, b, h, 0, 0))
    meta_spec = lambda which: pl.BlockSpec((1, 1, N_META, V_DIM), lambda b, h: (which, h, 0, 0))

    def kernel(rb, lq1, lk1, lq2, lk2, q_ref, k_ref, v_ref, km_ref, vm_ref, *rest):
        _attn_kernel(rb, lq1, lk1, lq2, lk2, q_ref.at[0], k_ref.at[0], v_ref.at[0],
                     km_ref.at[0], vm_ref.at[0], *rest)

    return pl.pallas_call(
        kernel,
        grid=(batch, N_HEADS),
        in_specs=[smem, row64, row64, row64, row64,
                  qkv_spec(0), qkv_spec(1), qkv_spec(2), meta_spec(1), meta_spec(2),
                  pl.BlockSpec((1, seq, V_DIM), lambda b, h: (0, b, h)),
                  pl.BlockSpec((1, V_DIM), lambda b, h: (0, 0))],
        out_specs=pl.BlockSpec((seq, V_DIM), lambda b, h: (b, h)),
        out_shape=jax.ShapeDtypeStruct((batch * seq, D_ATTN), BF16),
        scratch_shapes=[pltpu.VMEM((seq + N_META, V_DIM), BF16),
                        pltpu.VMEM((n_pad, V_DIM), BF16),
                        pltpu.VMEM((V_DIM + SUM_ROWS, n_pad), BF16),
                        pltpu.VMEM((2, seq, V_DIM), BF16),
                        pltpu.VMEM((tk, tq), F32),
                        pltpu.VMEM((N_META, tq), F32),
                        pltpu.VMEM((2, tk, tq), F32),
                        pltpu.VMEM((2, tk, tq), F32),
                        pltpu.VMEM((2, N_META, tq), F32),
                        pltpu.VMEM((2, 1, tq), F32),
                        pltpu.VMEM((2, 1, tq), F32),
                        pltpu.VMEM((2, V_DIM + SUM_ROWS, tq), F32)],
        compiler_params=pltpu.CompilerParams(
            dimension_semantics=("arbitrary", "arbitrary"),
            vmem_limit_bytes=VMEM_LIMIT_ATTN),
        name="diff_attention",
    )(rel_bias, lam_q1, lam_k1, lam_q2, lam_k2, qkv, qkv, qkv, qkv_meta, qkv_meta, gates, subln_g)


def _scan_block(a, b):
    n = a.shape[0]
    row = lax.broadcasted_iota(jnp.int32, a.shape, 0)
    s = 1
    while s < n:
        keep = row >= s
        a_sh = jnp.where(keep, pltpu.roll(a, s, 0), 1.0)
        b_sh = jnp.where(keep, pltpu.roll(b, s, 0), 0.0)
        b = b + a * b_sh
        a = a * a_sh
        s *= 2
    return a, b


def _lru_kernel(u_ref, um_ref, g_ref, cw_ref, cb_ref, wa_ref, wx_ref, ba_ref, bx_ref, lam_ref,
                o_ref, ubuf, h_ref):
    c = pl.program_id(1)
    hist = 8

    x = -lam_ref[...]
    softplus = jnp.maximum(x, 0.0) + jnp.log1p(jnp.exp(-jnp.abs(x)))

    def chunk(n, first):
        cw = cw_ref[...]
        uc = (cw[3:4] * ubuf[pl.ds(hist, n), :] + cw[2:3] * ubuf[pl.ds(hist - 1, n), :]
              + cw[1:2] * ubuf[pl.ds(hist - 2, n), :] + cw[0:1] * ubuf[pl.ds(hist - 3, n), :]
              + cb_ref[...])
        ucb = uc.astype(BF16)
        outs = []
        for blk in range(N_LRU_BLOCKS):
            cols = slice(blk * LRU_BLOCK, (blk + 1) * LRU_BLOCK)
            ub = ucb[:, cols]
            r = jax.nn.sigmoid(_dot(ub, wa_ref[blk].astype(BF16)) + ba_ref[:, cols])
            gi = jax.nn.sigmoid(_dot(ub, wx_ref[blk].astype(BF16)) + bx_ref[:, cols])
            log_a = -LRU_C * r * softplus[:, cols]
            a = jnp.exp(log_a)
            mult = jnp.sqrt(jnp.tanh(-log_a) * (a * a + 1.0))
            if first:
                row = lax.broadcasted_iota(jnp.int32, mult.shape, 0)
                mult = jnp.where(row == 0, 1.0, mult)
            b = mult * gi * uc[:, cols]
            carry = h_ref[:, cols]
            pieces = []
            for s0 in range(0, n, SCAN_ROWS):
                s1 = min(s0 + SCAN_ROWS, n)
                a_cum, b_cum = _scan_block(a[s0:s1], b[s0:s1])
                hblk = b_cum + a_cum * carry
                carry = hblk[s1 - s0 - 1:s1 - s0, :]
                pieces.append(hblk)
            h_ref[:, cols] = carry
            outs.append(pieces[0] if len(pieces) == 1 else jnp.concatenate(pieces, axis=0))
        return outs

    @pl.when(c == 0)
    def _():
        ubuf[pl.ds(0, hist), :] = jnp.zeros((hist, D_LRU), F32)
        ubuf[pl.ds(hist, N_META), :] = um_ref[...]
        h_ref[...] = jnp.zeros_like(h_ref)
        chunk(N_META, True)
        ubuf[pl.ds(0, hist), :] = ubuf[pl.ds(N_META, hist), :]

    n = u_ref.shape[0]
    ubuf[pl.ds(hist, n), :] = u_ref[...]
    outs = chunk(n, False)
    g = g_ref[...]
    for blk in range(N_LRU_BLOCKS):
        cols = slice(blk * LRU_BLOCK, (blk + 1) * LRU_BLOCK)
        gb = g[:, cols]
        o_ref[:, cols] = (outs[blk] * (gb * jax.nn.sigmoid(gb))).astype(o_ref.dtype)
    ubuf[pl.ds(0, hist), :] = ubuf[pl.ds(n, hist), :]


def _lru(gates, gates_meta, conv_w, conv_b, w_a, b_a, w_x, b_x, lru_lambda, *, batch, seq):
    tc = LRU_ROWS
    nc = seq // tc
    row = lambda n: pl.BlockSpec((n, D_LRU), lambda b, c: (0, 0))
    wspec = pl.BlockSpec((N_LRU_BLOCKS, LRU_BLOCK, LRU_BLOCK), lambda b, c: (0, 0, 0))

    def kernel(u_ref, um_ref, g_ref, *rest):
        _lru_kernel(u_ref.at[0], um_ref.at[0], g_ref.at[0], *rest)

    return pl.pallas_call(
        kernel,
        grid=(batch, nc),
        in_specs=[pl.BlockSpec((1, tc, D_LRU), lambda b, c: (1, b * nc + c, 0)),
                  pl.BlockSpec((1, N_META, D_LRU), lambda b, c: (1, 0, 0)),
                  pl.BlockSpec((1, tc, D_LRU), lambda b, c: (2, b * nc + c, 0)),
                  row(CONV_WIDTH), row(1), wspec, wspec, row(1), row(1), row(1)],
        out_specs=pl.BlockSpec((tc, D_LRU), lambda b, c: (b * nc + c, 0)),
        out_shape=jax.ShapeDtypeStruct((batch * seq, D_LRU), BF16),
        scratch_shapes=[pltpu.VMEM((8 + tc, D_LRU), F32),
                        pltpu.VMEM((1, D_LRU), F32)],
        compiler_params=pltpu.CompilerParams(
            dimension_semantics=("arbitrary", "arbitrary"),
            vmem_limit_bytes=VMEM_LIMIT_LRU),
        name="conv_rglru",
    )(gates, gates_meta, gates, conv_w, conv_b, w_a, w_x, b_a, b_x, lru_lambda)


def _out_kernel(x_ref, att_ref, rec_ref, w_ref, fg_ref, o_ref, wb_ref):
    @pl.when(pl.program_id(0) == 0)
    def _():
        wb_ref[...] = w_ref[...].astype(BF16)

    y = (_dot(att_ref[...], wb_ref[pl.ds(0, D_ATTN), :])
         + _dot(rec_ref[...], wb_ref[pl.ds(D_ATTN, D_LRU), :]))
    o_ref[...] = _rms(x_ref[...] + y, fg_ref[...], NORM_EPS)


def _out_project(x2d, att, rec, w_out, final_g):
    rows = x2d.shape[0]
    tm = OUT_ROWS
    return pl.pallas_call(
        _out_kernel,
        grid=(rows // tm,),
        in_specs=[pl.BlockSpec((tm, D_MODEL), lambda i: (i, 0)),
                  pl.BlockSpec((tm, D_ATTN), lambda i: (i, 0)),
                  pl.BlockSpec((tm, D_LRU), lambda i: (i, 0)),
                  pl.BlockSpec((D_ATTN + D_LRU, D_MODEL), lambda i: (0, 0),
                               pipeline_mode=pl.Buffered(1)),
                  pl.BlockSpec((1, D_MODEL), lambda i: (0, 0))],
        out_specs=pl.BlockSpec((tm, D_MODEL), lambda i: (i, 0)),
        out_shape=jax.ShapeDtypeStruct((rows, D_MODEL), F32),
        scratch_shapes=[pltpu.VMEM((D_ATTN + D_LRU, D_MODEL), BF16)],
        compiler_params=pltpu.CompilerParams(
            dimension_semantics=("arbitrary",),
            vmem_limit_bytes=VMEM_LIMIT_OUT),
        name="out_proj_norm",
    )(x2d, att, rec, w_out, final_g)


def kernel(x, meta_tokens, rel_bias, norm_g, w_in, conv_w, conv_b, w_a, b_a, w_x, b_x, lru_lambda,
           lam_q1, lam_k1, lam_q2, lam_k2, subln_g, w_out, final_g):
    batch, seq, _ = x.shape
    x2d = x.reshape(batch * seq, D_MODEL)
    qkv, qkv_meta = _project(x2d, meta_tokens, norm_g, w_in[0], col_tile0=0, head_major=True,
                             batch=batch, seq=seq)
    gates, gates_meta = _project(x2d, meta_tokens, norm_g, w_in[0], col_tile0=3, head_major=False,
                                 batch=batch, seq=seq)
    att = _attention(qkv, qkv_meta, gates, rel_bias, lam_q1, lam_k1, lam_q2, lam_k2, subln_g)
    rec = _lru(gates, gates_meta, conv_w[0], conv_b, w_a[0], b_a, w_x[0], b_x, lru_lambda,
               batch=batch, seq=seq)
    out = _out_project(x2d, att, rec, w_out[0], final_g.reshape(1, D_MODEL))
    return out.reshape(batch, seq, D_MODEL)
```

```python
import functools
import math

import numpy as np
import jax
import jax.numpy as jnp
from jax import lax
from jax.experimental import pallas as pl
from jax.experimental.pallas import tpu as pltpu

D_MODEL = 2048
N_META = 16
D_ATTN = 1024
D_LRU = 1024
N_HEADS = 8
HEAD_DIM = 64
V_DIM = 128
N_LRU_BLOCKS = 8
LRU_BLOCK = 128
CONV_WIDTH = 4
LRU_C = 8.0
N_BUCKETS = 32
MAX_DISTANCE = 128
NORM_EPS = 1e-6
SUBLN_EPS = 1e-5
NEG_INF = -1e30
LAMBDA_INIT = 0.8 - 0.6 * math.exp(-0.3 * 0)

BF16 = jnp.bfloat16
F32 = jnp.float32

VMEM_LIMIT_PROJ = 48 * 1024 * 1024
VMEM_LIMIT_ATTN = 48 * 1024 * 1024
VMEM_LIMIT_LRU = 40 * 1024 * 1024
VMEM_LIMIT_OUT = 56 * 1024 * 1024

PROJ_ROWS = 512
ATTN_TQ = 512
ATTN_TK = 512
SUM_ROWS = 16
LRU_ROWS = 256
SCAN_ROWS = 128
OUT_ROWS = 256
SEG_PITCH = 40


def _bucket_thresholds():
    max_exact = N_BUCKETS // 2
    d = np.arange(0, 4 * MAX_DISTANCE, dtype=np.int64)
    val = (np.log(np.maximum(d, 1).astype(np.float64) / max_exact)
           / math.log(MAX_DISTANCE / max_exact) * (N_BUCKETS - max_exact))
    large = np.minimum(max_exact + np.floor(val + 1e-9).astype(np.int64), N_BUCKETS - 1)
    bucket = np.where(d < max_exact, d, large)
    frac = np.abs(val - np.round(val))
    interior = (d > max_exact) & (d < MAX_DISTANCE)
    assert frac[interior].min() > 1e-3
    assert (np.diff(bucket) >= 0).all() and bucket[MAX_DISTANCE] == N_BUCKETS - 1
    return tuple(int(np.argmax(bucket >= j)) for j in range(1, N_BUCKETS))


BUCKET_THRESHOLDS = _bucket_thresholds()


def _rms(x, g, eps):
    y = x * lax.rsqrt(jnp.mean(x * x, axis=-1, keepdims=True) + eps)
    return y * g


def _dot(a, b):
    return jnp.dot(a, b, preferred_element_type=F32)


def _dot_nt(a, b):
    return lax.dot_general(a, b, (((1,), (1,)), ((), ())), preferred_element_type=F32)


def _proj_kernel(x_ref, meta_ref, g_ref, w_ref, o_ref, om_ref, wb_ref, *, head_major):
    j = pl.program_id(0)
    i = pl.program_id(1)
    out_scale = jnp.where(j == 0, HEAD_DIM ** -0.5, 1.0).astype(F32) if head_major else 1.0

    @pl.when(i == 0)
    def _():
        wb_ref[...] = w_ref[...].astype(BF16)
        hm = _rms(meta_ref[...], g_ref[...], NORM_EPS).astype(BF16)
        ym = _dot(hm, wb_ref[...]) * out_scale
        if head_major:
            for h in range(N_HEADS):
                om_ref[0, h] = ym[:, h * V_DIM:(h + 1) * V_DIM].astype(om_ref.dtype)
        else:
            om_ref[0] = ym.astype(om_ref.dtype)

    hx = _rms(x_ref[...], g_ref[...], NORM_EPS).astype(BF16)
    y = _dot(hx, wb_ref[...]) * out_scale
    if head_major:
        for h in range(N_HEADS):
            o_ref[0, 0, h] = y[:, h * V_DIM:(h + 1) * V_DIM].astype(o_ref.dtype)
    else:
        o_ref[0] = y.astype(o_ref.dtype)


def _project(x2d, meta, norm_g, w_in, *, col_tile0, head_major, batch, seq):
    rows = x2d.shape[0]
    tm = PROJ_ROWS
    n_i = rows // tm
    n_ib = seq // tm
    if head_major:
        out_shape = (jax.ShapeDtypeStruct((3, batch, N_HEADS, seq, V_DIM), BF16),
                     jax.ShapeDtypeStruct((3, N_HEADS, N_META, V_DIM), BF16))
        out_specs = (pl.BlockSpec((1, 1, N_HEADS, tm, V_DIM),
                                  lambda j, i: (j, i // n_ib, 0, i % n_ib, 0)),
                     pl.BlockSpec((1, N_HEADS, N_META, V_DIM), lambda j, i: (j, 0, 0, 0)))
    else:
        out_shape = (jax.ShapeDtypeStruct((3, rows, D_ATTN), F32),
                     jax.ShapeDtypeStruct((3, N_META, D_ATTN), F32))
        out_specs = (pl.BlockSpec((1, tm, D_ATTN), lambda j, i: (j, i, 0)),
                     pl.BlockSpec((1, N_META, D_ATTN), lambda j, i: (j, 0, 0)))
    return pl.pallas_call(
        functools.partial(_proj_kernel, head_major=head_major),
        grid=(3, n_i),
        in_specs=[pl.BlockSpec((tm, D_MODEL), lambda j, i: (i, 0)),
                  pl.BlockSpec((N_META, D_MODEL), lambda j, i: (0, 0)),
                  pl.BlockSpec((1, D_MODEL), lambda j, i: (0, 0)),
                  pl.BlockSpec((D_MODEL, D_ATTN), lambda j, i: (0, j + col_tile0))],
        out_specs=out_specs,
        out_shape=out_shape,
        scratch_shapes=[pltpu.VMEM((D_MODEL, D_ATTN), BF16)],
        compiler_params=pltpu.CompilerParams(
            dimension_semantics=("arbitrary", "arbitrary"),
            vmem_limit_bytes=VMEM_LIMIT_PROJ),
        name="proj_qkv" if head_major else "proj_gates",
    )(x2d, meta, norm_g, w_in)


def _toeplitz_bias(dist, rb_ref, h, far):
    b = jnp.full(dist.shape, rb_ref[0, h] - far, F32)
    for j, thr in enumerate(BUCKET_THRESHOLDS, start=1):
        b = jnp.where(dist >= thr, rb_ref[j, h] - far, b)
    return b


def _fill_bias(ref, lead, n_rows, n_cols, d0, rb_ref, h, far):
    sub = lax.broadcasted_iota(jnp.int32, (8, V_DIM), 0)
    lane = lax.broadcasted_iota(jnp.int32, (8, V_DIM), 1)
    zeros = jnp.zeros((8, V_DIM), F32)
    masked = jnp.full((8, V_DIM), NEG_INF, F32)
    cache = {}
    for a8 in range(n_rows // 8):
        for b in range(n_cols // V_DIM):
            off = d0 + V_DIM * b - 8 * a8
            if off + V_DIM - 1 < 0:
                tile = masked
            elif off - 7 >= MAX_DISTANCE:
                tile = zeros
            else:
                if off not in cache:
                    d = off + lane - sub
                    cache[off] = jnp.where(d >= 0, _toeplitz_bias(d, rb_ref, h, far), NEG_INF)
                tile = cache[off]
            ref[(*lead, pl.ds(8 * a8, 8), pl.ds(V_DIM * b, V_DIM))] = tile


def _attn_kernel(rb_ref, lq1_ref, lk1_ref, lq2_ref, lk2_ref, q_ref, k_ref, v_ref, km_ref, vm_ref,
                 g_ref, sg_ref, o_ref, kall, vall, vt_ref, qz_ref, bd_ref, bc_ref, sel_ref, sbuf,
                 sc_ref, mx_ref, m_ref, acc_ref):
    h = pl.program_id(1)
    tq, tk = ATTN_TQ, ATTN_TK
    seq = k_ref.shape[2]
    nq = seq // tq
    n_keys = seq + N_META
    n_pad = vt_ref.shape[1]

    far = rb_ref[N_BUCKETS - 1, h]

    kall[pl.ds(0, N_META), :] = km_ref[0]
    kall[pl.ds(N_META, seq), :] = k_ref[0, 0]
    vall[pl.ds(0, N_META), :] = vm_ref[0]
    vall[pl.ds(N_META, seq), :] = v_ref[0, 0]
    vall[pl.ds(n_keys, n_pad - n_keys), :] = jnp.zeros((n_pad - n_keys, V_DIM), BF16)

    def xpose(n, carry):
        rows = pl.ds(pl.multiple_of(n * V_DIM, V_DIM), V_DIM)
        vt_ref[pl.ds(0, V_DIM), rows] = vall[rows, :].astype(F32).T.astype(BF16)
        return carry

    lax.fori_loop(0, n_pad // V_DIM, xpose, 0, unroll=3)
    vt_ref[pl.ds(V_DIM, SUM_ROWS), :] = jnp.ones((SUM_ROWS, n_pad), BF16)

    q_all = q_ref[0, 0]
    q_lane = lax.broadcasted_iota(jnp.int32, q_all.shape, 1)
    qz_ref[0] = jnp.where(q_lane < HEAD_DIM, q_all, jnp.zeros_like(q_all))
    qz_ref[1] = jnp.where(q_lane >= HEAD_DIM, q_all, jnp.zeros_like(q_all))

    _fill_bias(bd_ref, (), tk, tq, N_META, rb_ref, h, far)
    _fill_bias(bc_ref, (), N_META, tq, N_META - tk, rb_ref, h, far)
    _fill_bias(sel_ref, (1,), tk, tq, N_META + tk, rb_ref, h, far)
    sel_ref[0] = jnp.zeros((tk, tq), F32)

    lam = (jnp.exp(jnp.sum(lq1_ref[...] * lk1_ref[...], keepdims=True))
           - jnp.exp(jnp.sum(lq2_ref[...] * lk2_ref[...], keepdims=True))
           + LAMBDA_INIT)

    def init_stats():
        m_ref[...] = jnp.full(m_ref.shape, NEG_INF, F32)
        acc_ref[...] = jnp.zeros(acc_ref.shape, F32)

    def produce(q_off, k_off, kind, sel=None):
        q_rows = pl.ds(pl.multiple_of(q_off, tq), tq)
        k_off = pl.multiple_of(k_off, tk)
        k_t = kall[pl.ds(k_off, tk), :]
        for a in range(2):
            qa = qz_ref[a, q_rows, :]
            s = _dot_nt(k_t, qa)
            if kind == "near":
                s = s + sel_ref[1]
            elif kind == "diag":
                s = s + bd_ref[...]
            elif kind == "select":
                s = s + sel_ref[sel]
            sbuf[a] = s
            mx = jnp.max(s, axis=0, keepdims=True)
            if kind == "diag":
                sc = _dot_nt(kall[pl.ds(k_off + tk, N_META), :], qa) + bc_ref[...]
                sc_ref[a] = sc
                mx = jnp.maximum(mx, jnp.max(sc, axis=0, keepdims=True))
            mx_ref[a] = mx

    def consume(k_off, diag=False):
        k_off = pl.multiple_of(k_off, tk)
        v_t = vt_ref[:, pl.ds(k_off, tk)]
        for a in range(2):
            m_old = m_ref[a]
            m_new = jnp.maximum(m_old, mx_ref[a])
            alpha = jnp.exp(m_old - m_new)
            p = jnp.exp(sbuf[a] - m_new)
            pv = _dot(v_t, p.astype(BF16))
            if diag:
                pc = jnp.exp(sc_ref[a] - m_new)
                v_c = vt_ref[:, pl.ds(k_off + tk, V_DIM)][:, :N_META]
                pv = pv + _dot(v_c, pc.astype(BF16))
            acc_ref[a] = alpha * acc_ref[a] + pv
            m_ref[a] = m_new

    def finalize(q_off):
        q_rows = pl.ds(pl.multiple_of(q_off, tq), tq)
        heads = [acc_ref[a, pl.ds(0, V_DIM), :] / acc_ref[a, pl.ds(V_DIM, 1), :] for a in range(2)]
        out_t = heads[0] - lam * heads[1]
        inv = lax.rsqrt(jnp.mean(out_t * out_t, axis=0, keepdims=True) + SUBLN_EPS)
        att = ((out_t * inv).T * sg_ref[...]) * (1.0 - LAMBDA_INIT)
        g = g_ref[0, q_rows, :]
        o_ref[q_rows, :] = (att * (g * jax.nn.sigmoid(g))).astype(o_ref.dtype)

    init_stats()
    produce(0, 0, "diag")

    def q_body(i, carry):
        q_off = i * tq

        def far_step(t):
            consume(t * tk)
            produce(q_off, (t + 1) * tk, "far")

        def far_pair(tt, c):
            far_step(2 * tt)
            far_step(2 * tt + 1)
            return c

        n_far = jnp.maximum(i - 2, 0)
        lax.fori_loop(0, n_far // 2, far_pair, 0)

        @pl.when(n_far % 2 == 1)
        def _():
            far_step(n_far - 1)

        @pl.when(i >= 2)
        def _():
            consume((i - 2) * tk)
            produce(q_off, (i - 1) * tk, "near")
            consume((i - 1) * tk)
            produce(q_off, i * tk, "diag")

        @pl.when(i == 1)
        def _():
            consume(0)
            produce(q_off, tk, "diag")

        consume(i * tk, diag=True)
        finalize(q_off)
        init_stats()
        nxt = jnp.minimum(i + 1, nq - 1)
        produce(nxt * tq, 0, "select", sel=jnp.where(nxt == 1, 1, 0))
        return carry

    lax.fori_loop(0, nq, q_body, 0)


def _attention(qkv, qkv_meta, gates, rel_bias, lam_q1, lam_k1, lam_q2, lam_k2, subln_g):
    _, batch, _, seq, _ = qkv.shape
    tq, tk = ATTN_TQ, ATTN_TK
    n_pad = seq + V_DIM
    smem = pl.BlockSpec(memory_space=pltpu.SMEM)
    row64 = pl.BlockSpec((1, HEAD_DIM), lambda b, h: (0, 0))
    qkv_spec = lambda which: pl.BlockSpec((1, 1, 1, seq, V_DIM), lambda b, h: (which, b, h, 0, 0))
    meta_spec = lambda which: pl.BlockSpec((1, 1, N_META, V_DIM), lambda b, h: (which, h, 0, 0))

    def kernel(rb, lq1, lk1, lq2, lk2, q_ref, k_ref, v_ref, km_ref, vm_ref, *rest):
        _attn_kernel(rb, lq1, lk1, lq2, lk2, q_ref.at[0], k_ref.at[0], v_ref.at[0],
                     km_ref.at[0], vm_ref.at[0], *rest)

    return pl.pallas_call(
        kernel,
        grid=(batch, N_HEADS),
        in_specs=[smem, row64, row64, row64, row64,
                  qkv_spec(0), qkv_spec(1), qkv_spec(2), meta_spec(1), meta_spec(2),
                  pl.BlockSpec((1, seq, V_DIM), lambda b, h: (0, b, h)),
                  pl.BlockSpec((1, V_DIM), lambda b, h: (0, 0))],
        out_specs=pl.BlockSpec((seq, V_DIM), lambda b, h: (b, h)),
        out_shape=jax.ShapeDtypeStruct((batch * seq, D_ATTN), BF16),
        scratch_shapes=[pltpu.VMEM((seq + N_META, V_DIM), BF16),
                        pltpu.VMEM((n_pad, V_DIM), BF16),
                        pltpu.VMEM((V_DIM + SUM_ROWS, n_pad), BF16),
                        pltpu.VMEM((2, seq, V_DIM), BF16),
                        pltpu.VMEM((tk, tq), F32),
                        pltpu.VMEM((N_META, tq), F32),
                        pltpu.VMEM((2, tk, tq), F32),
                        pltpu.VMEM((2, tk, tq), F32),
                        pltpu.VMEM((2, N_META, tq), F32),
                        pltpu.VMEM((2, 1, tq), F32),
                        pltpu.VMEM((2, 1, tq), F32),
                        pltpu.VMEM((2, V_DIM + SUM_ROWS, tq), F32)],
        compiler_params=pltpu.CompilerParams(
            dimension_semantics=("arbitrary", "arbitrary"),
            vmem_limit_bytes=VMEM_LIMIT_ATTN),
        name="diff_attention",
    )(rel_bias, lam_q1, lam_k1, lam_q2, lam_k2, qkv, qkv, qkv, qkv_meta, qkv_meta, gates, subln_g)


def _scan_block(a, b):
    n = a.shape[0]
    row = lax.broadcasted_iota(jnp.int32, a.shape, 0)
    s = 1
    while s < n:
        keep = row >= s
        a_sh = jnp.where(keep, pltpu.roll(a, s, 0), 1.0)
        b_sh = jnp.where(keep, pltpu.roll(b, s, 0), 0.0)
        b = b + a * b_sh
        a = a * a_sh
        s *= 2
    return a, b


def _lru_kernel(u_ref, um_ref, g_ref, cw_ref, cb_ref, wa_ref, wx_ref, ba_ref, bx_ref, lam_ref,
                o_ref, ubuf, h_ref):
    c = pl.program_id(1)
    hist = 8

    x = -lam_ref[...]
    softplus = jnp.maximum(x, 0.0) + jnp.log1p(jnp.exp(-jnp.abs(x)))

    def chunk(n, first):
        cw = cw_ref[...]
        uc = (cw[3:4] * ubuf[pl.ds(hist, n), :] + cw[2:3] * ubuf[pl.ds(hist - 1, n), :]
              + cw[1:2] * ubuf[pl.ds(hist - 2, n), :] + cw[0:1] * ubuf[pl.ds(hist - 3, n), :]
              + cb_ref[...])
        ucb = uc.astype(BF16)
        outs = []
        for blk in range(N_LRU_BLOCKS):
            cols = slice(blk * LRU_BLOCK, (blk + 1) * LRU_BLOCK)
            ub = ucb[:, cols]
            r = jax.nn.sigmoid(_dot(ub, wa_ref[blk].astype(BF16)) + ba_ref[:, cols])
            gi = jax.nn.sigmoid(_dot(ub, wx_ref[blk].astype(BF16)) + bx_ref[:, cols])
            log_a = -LRU_C * r * softplus[:, cols]
            a = jnp.exp(log_a)
            mult = jnp.sqrt(jnp.tanh(-log_a) * (a * a + 1.0))
            if first:
                row = lax.broadcasted_iota(jnp.int32, mult.shape, 0)
                mult = jnp.where(row == 0, 1.0, mult)
            b = mult * gi * uc[:, cols]
            carry = h_ref[:, cols]
            pieces = []
            for s0 in range(0, n, SCAN_ROWS):
                s1 = min(s0 + SCAN_ROWS, n)
                a_cum, b_cum = _scan_block(a[s0:s1], b[s0:s1])
                hblk = b_cum + a_cum * carry
                carry = hblk[s1 - s0 - 1:s1 - s0, :]
                pieces.append(hblk)
            h_ref[:, cols] = carry
            outs.append(pieces[0] if len(pieces) == 1 else jnp.concatenate(pieces, axis=0))
        return outs

    @pl.when(c == 0)
    def _():
        ubuf[pl.ds(0, hist), :] = jnp.zeros((hist, D_LRU), F32)
        ubuf[pl.ds(hist, N_META), :] = um_ref[...]
        h_ref[...] = jnp.zeros_like(h_ref)
        chunk(N_META, True)
        ubuf[pl.ds(0, hist), :] = ubuf[pl.ds(N_META, hist), :]

    n = u_ref.shape[0]
    ubuf[pl.ds(hist, n), :] = u_ref[...]
    outs = chunk(n, False)
    g = g_ref[...]
    for blk in range(N_LRU_BLOCKS):
        cols = slice(blk * LRU_BLOCK, (blk + 1) * LRU_BLOCK)
        gb = g[:, cols]
        o_ref[:, cols] = (outs[blk] * (gb * jax.nn.sigmoid(gb))).astype(o_ref.dtype)
    ubuf[pl.ds(0, hist), :] = ubuf[pl.ds(n, hist), :]


def _lru(gates, gates_meta, conv_w, conv_b, w_a, b_a, w_x, b_x, lru_lambda, *, batch, seq):
    tc = LRU_ROWS
    nc = seq // tc
    row = lambda n: pl.BlockSpec((n, D_LRU), lambda b, c: (0, 0))
    wspec = pl.BlockSpec((N_LRU_BLOCKS, LRU_BLOCK, LRU_BLOCK), lambda b, c: (0, 0, 0))

    def kernel(u_ref, um_ref, g_ref, *rest):
        _lru_kernel(u_ref.at[0], um_ref.at[0], g_ref.at[0], *rest)

    return pl.pallas_call(
        kernel,
        grid=(batch, nc),
        in_specs=[pl.BlockSpec((1, tc, D_LRU), lambda b, c: (1, b * nc + c, 0)),
                  pl.BlockSpec((1, N_META, D_LRU), lambda b, c: (1, 0, 0)),
                  pl.BlockSpec((1, tc, D_LRU), lambda b, c: (2, b * nc + c, 0)),
                  row(CONV_WIDTH), row(1), wspec, wspec, row(1), row(1), row(1)],
        out_specs=pl.BlockSpec((tc, D_LRU), lambda b, c: (b * nc + c, 0)),
        out_shape=jax.ShapeDtypeStruct((batch * seq, D_LRU), BF16),
        scratch_shapes=[pltpu.VMEM((8 + tc, D_LRU), F32),
                        pltpu.VMEM((1, D_LRU), F32)],
        compiler_params=pltpu.CompilerParams(
            dimension_semantics=("arbitrary", "arbitrary"),
            vmem_limit_bytes=VMEM_LIMIT_LRU),
        name="conv_rglru",
    )(gates, gates_meta, gates, conv_w, conv_b, w_a, w_x, b_a, b_x, lru_lambda)


def _out_kernel(x_ref, att_ref, rec_ref, w_ref, fg_ref, o_ref, wb_ref):
    @pl.when(pl.program_id(0) == 0)
    def _():
        wb_ref[...] = w_ref[...].astype(BF16)

    y = (_dot(att_ref[...], wb_ref[pl.ds(0, D_ATTN), :])
         + _dot(rec_ref[...], wb_ref[pl.ds(D_ATTN, D_LRU), :]))
    o_ref[...] = _rms(x_ref[...] + y, fg_ref[...], NORM_EPS)


def _out_project(x2d, att, rec, w_out, final_g):
    rows = x2d.shape[0]
    tm = OUT_ROWS
    return pl.pallas_call(
        _out_kernel,
        grid=(rows // tm,),
        in_specs=[pl.BlockSpec((tm, D_MODEL), lambda i: (i, 0)),
                  pl.BlockSpec((tm, D_ATTN), lambda i: (i, 0)),
                  pl.BlockSpec((tm, D_LRU), lambda i: (i, 0)),
                  pl.BlockSpec((D_ATTN + D_LRU, D_MODEL), lambda i: (0, 0),
                               pipeline_mode=pl.Buffered(1)),
                  pl.BlockSpec((1, D_MODEL), lambda i: (0, 0))],
        out_specs=pl.BlockSpec((tm, D_MODEL), lambda i: (i, 0)),
        out_shape=jax.ShapeDtypeStruct((rows, D_MODEL), F32),
        scratch_shapes=[pltpu.VMEM((D_ATTN + D_LRU, D_MODEL), BF16)],
        compiler_params=pltpu.CompilerParams(
            dimension_semantics=("arbitrary",),
            vmem_limit_bytes=VMEM_LIMIT_OUT),
        name="out_proj_norm",
    )(x2d, att, rec, w_out, final_g)


def _tail_kernel(x_ref, att_ref, u_ref, um_ref, g_ref, cw_ref, cb_ref, wa_ref, wx_ref, ba_ref, bx_ref,
                 lam_ref, w_ref, fg_ref, o_ref, wb_ref, wg_ref, ubuf, h_ref, a_s, b_s, rec_s, rec_prev,
                 *, chunks_per_batch):
    s = pl.program_id(0)
    n = u_ref.shape[0]
    hist = 8
    seg = n // 8

    x = -lam_ref[...]
    softplus = jnp.maximum(x, 0.0) + jnp.log1p(jnp.exp(-jnp.abs(x)))

    @pl.when(s == 0)
    def _():
        wb_ref[...] = w_ref[...].astype(BF16)
        for blk in range(N_LRU_BLOCKS):
            wg_ref[blk, :, pl.ds(0, LRU_BLOCK)] = wa_ref[blk].astype(BF16)
            wg_ref[blk, :, pl.ds(LRU_BLOCK, LRU_BLOCK)] = wx_ref[blk].astype(BF16)
        rec_s[...] = jnp.zeros_like(rec_s)

    def gates(rows, cols, blk, first):
        cw = cw_ref[:, cols]
        uc = (cw[3:4] * ubuf[pl.ds(hist, rows), cols] + cw[2:3] * ubuf[pl.ds(hist - 1, rows), cols]
              + cw[1:2] * ubuf[pl.ds(hist - 2, rows), cols] + cw[0:1] * ubuf[pl.ds(hist - 3, rows), cols]
              + cb_ref[:, cols])
        pre = _dot(uc.astype(BF16), wg_ref[blk])
        r = jax.nn.sigmoid(pre[:, :LRU_BLOCK] + ba_ref[:, cols])
        gi = jax.nn.sigmoid(pre[:, LRU_BLOCK:] + bx_ref[:, cols])
        log_a = -LRU_C * r * softplus[:, cols]
        a = jnp.exp(log_a)
        mult = jnp.sqrt(jnp.tanh(-log_a) * (a * a + 1.0))
        if first:
            row = lax.broadcasted_iota(jnp.int32, mult.shape, 0)
            mult = jnp.where(row == 0, 1.0, mult)
        return a, mult * gi * uc

    @pl.when(s % chunks_per_batch == 0)
    def _():
        ubuf[pl.ds(0, hist), :] = jnp.zeros((hist, D_LRU), F32)
        ubuf[pl.ds(hist, N_META), :] = um_ref[...]
        for blk in range(N_LRU_BLOCKS):
            cols = slice(blk * LRU_BLOCK, (blk + 1) * LRU_BLOCK)
            a, b = gates(N_META, cols, blk, True)
            _, hm = _scan_block(a, b)
            h_ref[:, cols] = hm[N_META - 1:N_META, :]
        ubuf[pl.ds(0, hist), :] = ubuf[pl.ds(N_META, hist), :]

    rec_prev[...] = rec_s[...]
    ubuf[pl.ds(hist, n), :] = u_ref[...]
    sub = lax.broadcasted_iota(jnp.int32, (8, LRU_BLOCK), 0)
    n_out = D_MODEL // N_LRU_BLOCKS
    ssq = jnp.zeros((n, LRU_BLOCK), F32)
    for blk in range(N_LRU_BLOCKS):
        ocols = slice(blk * n_out, (blk + 1) * n_out)
        z = (x_ref[:, ocols] + _dot(att_ref[...], wb_ref[pl.ds(0, D_ATTN), ocols])
             + _dot(rec_prev[...], wb_ref[pl.ds(D_ATTN, D_LRU), ocols]))
        o_ref[:, ocols] = z
        for part in range(n_out // LRU_BLOCK):
            zp = z[:, part * LRU_BLOCK:(part + 1) * LRU_BLOCK]
            ssq = ssq + zp * zp

        cols = slice(blk * LRU_BLOCK, (blk + 1) * LRU_BLOCK)
        a, b = gates(n, cols, blk, False)
        for k in range(8):
            a_s[blk, pl.ds(SEG_PITCH * k, seg), :] = a[seg * k:seg * (k + 1)]
            b_s[blk, pl.ds(SEG_PITCH * k, seg), :] = b[seg * k:seg * (k + 1)]
        step = lambda j: pl.ds(j, 8, stride=SEG_PITCH)
        a_run = a_s[blk, step(0), :]
        h_run = b_s[blk, step(0), :]
        for j in range(1, seg):
            aj = a_s[blk, step(j), :]
            h_run = aj * h_run + b_s[blk, step(j), :]
            a_run = aj * a_run
        a_cum, h_cum = _scan_block(a_run, h_run)
        carry = h_ref[:, cols]
        seg_end = h_cum + a_cum * carry
        h_ref[:, cols] = seg_end[7:8, :]
        h_run = jnp.where(sub == 0, carry, pltpu.roll(seg_end, 1, 0))
        for j in range(seg):
            h_run = a_s[blk, step(j), :] * h_run + b_s[blk, step(j), :]
            b_s[blk, step(j), :] = h_run
        g = g_ref[:, cols]
        h = jnp.concatenate([b_s[blk, pl.ds(SEG_PITCH * k, seg), :] for k in range(8)], axis=0)
        rec_s[:, cols] = (h * (g * jax.nn.sigmoid(g))).astype(BF16)
    ubuf[pl.ds(0, hist), :] = ubuf[pl.ds(n, hist), :]
    inv = lax.rsqrt(jnp.sum(ssq, axis=1, keepdims=True) * (1.0 / D_MODEL) + NORM_EPS)
    o_ref[...] = (o_ref[...] * inv) * fg_ref[...]


def _tail(x2d, att, gates, gates_meta, conv_w, conv_b, w_a, b_a, w_x, b_x, lru_lambda, w_out,
          final_g, *, seq):
    rows = x2d.shape[0]
    tm = OUT_ROWS
    n_tiles = rows // tm
    prev = lambda s: (jnp.maximum(s - 1, 0), 0)
    cur = lambda which: (lambda s: (which, jnp.minimum(s, n_tiles - 1), 0))
    row = lambda n: pl.BlockSpec((n, D_LRU), lambda s: (0, 0))
    wspec = pl.BlockSpec((N_LRU_BLOCKS, LRU_BLOCK, LRU_BLOCK), lambda s: (0, 0, 0))

    def kernel(x_ref, att_ref, u_ref, um_ref, g_ref, *rest):
        _tail_kernel(x_ref, att_ref, u_ref.at[0], um_ref.at[0], g_ref.at[0], *rest,
                     chunks_per_batch=seq // tm)

    return pl.pallas_call(
        kernel,
        grid=(n_tiles + 1,),
        in_specs=[pl.BlockSpec((tm, D_MODEL), prev),
                  pl.BlockSpec((tm, D_ATTN), prev),
                  pl.BlockSpec((1, tm, D_LRU), cur(1)),
                  pl.BlockSpec((1, N_META, D_LRU), lambda s: (1, 0, 0)),
                  pl.BlockSpec((1, tm, D_LRU), cur(2)),
                  row(CONV_WIDTH), row(1), wspec, wspec, row(1), row(1), row(1),
                  pl.BlockSpec((D_ATTN + D_LRU, D_MODEL), lambda s: (0, 0),
                               pipeline_mode=pl.Buffered(1)),
                  pl.BlockSpec((1, D_MODEL), lambda s: (0, 0))],
        out_specs=pl.BlockSpec((tm, D_MODEL), prev),
        out_shape=jax.ShapeDtypeStruct((rows, D_MODEL), F32),
        scratch_shapes=[pltpu.VMEM((D_ATTN + D_LRU, D_MODEL), BF16),
                        pltpu.VMEM((N_LRU_BLOCKS, LRU_BLOCK, 2 * LRU_BLOCK), BF16),
                        pltpu.VMEM((8 + tm, D_LRU), F32),
                        pltpu.VMEM((1, D_LRU), F32),
                        pltpu.VMEM((N_LRU_BLOCKS, 8 * SEG_PITCH, LRU_BLOCK), F32),
                        pltpu.VMEM((N_LRU_BLOCKS, 8 * SEG_PITCH, LRU_BLOCK), F32),
                        pltpu.VMEM((tm, D_LRU), BF16),
                        pltpu.VMEM((tm, D_LRU), BF16)],
        compiler_params=pltpu.CompilerParams(
            dimension_semantics=("arbitrary",),
            vmem_limit_bytes=VMEM_LIMIT_OUT),
        name="rglru_out_proj",
    )(x2d, att, gates, gates_meta, gates, conv_w, conv_b, w_a, w_x, b_a, b_x, lru_lambda, w_out,
      final_g)


def kernel(x, meta_tokens, rel_bias, norm_g, w_in, conv_w, conv_b, w_a, b_a, w_x, b_x, lru_lambda,
           lam_q1, lam_k1, lam_q2, lam_k2, subln_g, w_out, final_g):
    batch, seq, _ = x.shape
    x2d = x.reshape(batch * seq, D_MODEL)
    qkv, qkv_meta = _project(x2d, meta_tokens, norm_g, w_in[0], col_tile0=0, head_major=True,
                             batch=batch, seq=seq)
    gates, gates_meta = _project(x2d, meta_tokens, norm_g, w_in[0], col_tile0=3, head_major=False,
                                 batch=batch, seq=seq)
    att = _attention(qkv, qkv_meta, gates, rel_bias, lam_q1, lam_k1, lam_q2, lam_k2, subln_g)
    out = _tail(x2d, att, gates, gates_meta, conv_w[0], conv_b, w_a[0], b_a, w_x[0], b_x, lru_lambda,
                w_out[0], final_g.reshape(1, D_MODEL), seq=seq)
    return out.reshape(batch, seq, D_MODEL)
```

```python
import functools
import math

import numpy as np
import jax
import jax.numpy as jnp
from jax import lax
from jax.experimental import pallas as pl
from jax.experimental.pallas import tpu as pltpu

D_MODEL = 2048
N_META = 16
D_ATTN = 1024
D_LRU = 1024
N_HEADS = 8
HEAD_DIM = 64
V_DIM = 128
N_LRU_BLOCKS = 8
LRU_BLOCK = 128
CONV_WIDTH = 4
LRU_C = 8.0
N_BUCKETS = 32
MAX_DISTANCE = 128
NORM_EPS = 1e-6
SUBLN_EPS = 1e-5
NEG_INF = -1e30
LAMBDA_INIT = 0.8 - 0.6 * math.exp(-0.3 * 0)
LOG2E = math.log2(math.e)

BF16 = jnp.bfloat16
F32 = jnp.float32

VMEM_LIMIT_PROJ = 48 * 1024 * 1024
VMEM_LIMIT_ATTN = 48 * 1024 * 1024
VMEM_LIMIT_OUT = 56 * 1024 * 1024

PROJ_ROWS = 512
ATTN_TQ = 512
ATTN_TK = 512
SUM_ROWS = 16
OUT_ROWS = 256
SEG_PITCH = 40


def _bucket_thresholds():
    max_exact = N_BUCKETS // 2
    d = np.arange(0, 4 * MAX_DISTANCE, dtype=np.int64)
    val = (np.log(np.maximum(d, 1).astype(np.float64) / max_exact)
           / math.log(MAX_DISTANCE / max_exact) * (N_BUCKETS - max_exact))
    large = np.minimum(max_exact + np.floor(val + 1e-9).astype(np.int64), N_BUCKETS - 1)
    bucket = np.where(d < max_exact, d, large)
    frac = np.abs(val - np.round(val))
    interior = (d > max_exact) & (d < MAX_DISTANCE)
    assert frac[interior].min() > 1e-3
    assert (np.diff(bucket) >= 0).all() and bucket[MAX_DISTANCE] == N_BUCKETS - 1
    return tuple(int(np.argmax(bucket >= j)) for j in range(1, N_BUCKETS))


BUCKET_THRESHOLDS = _bucket_thresholds()


def _rms(x, g, eps):
    y = x * lax.rsqrt(jnp.mean(x * x, axis=-1, keepdims=True) + eps)
    return y * g


def _dot(a, b):
    return jnp.dot(a, b, preferred_element_type=F32)


def _dot_nt(a, b):
    return lax.dot_general(a, b, (((1,), (1,)), ((), ())), preferred_element_type=F32)


def _proj_kernel(x_ref, meta_ref, g_ref, w_ref, o_ref, om_ref, wb_ref, *, head_major):
    j = pl.program_id(0)
    i = pl.program_id(1)
    out_scale = jnp.where(j == 0, HEAD_DIM ** -0.5 * LOG2E, 1.0).astype(F32) if head_major else 1.0

    @pl.when(i == 0)
    def _():
        wb_ref[...] = w_ref[...].astype(BF16)
        hm = _rms(meta_ref[...], g_ref[...], NORM_EPS).astype(BF16)
        ym = _dot(hm, wb_ref[...]) * out_scale
        if head_major:
            for h in range(N_HEADS):
                om_ref[0, h] = ym[:, h * V_DIM:(h + 1) * V_DIM].astype(om_ref.dtype)
        else:
            om_ref[0] = ym.astype(om_ref.dtype)

    hx = _rms(x_ref[...], g_ref[...], NORM_EPS).astype(BF16)
    y = _dot(hx, wb_ref[...]) * out_scale
    if head_major:
        for h in range(N_HEADS):
            o_ref[0, 0, h] = y[:, h * V_DIM:(h + 1) * V_DIM].astype(o_ref.dtype)
    else:
        o_ref[0] = y.astype(o_ref.dtype)


def _project(x2d, meta, norm_g, w_in, *, col_tile0, head_major, batch, seq):
    rows = x2d.shape[0]
    tm = PROJ_ROWS
    n_i = rows // tm
    n_ib = seq // tm
    if head_major:
        out_shape = (jax.ShapeDtypeStruct((3, batch, N_HEADS, seq, V_DIM), BF16),
                     jax.ShapeDtypeStruct((3, N_HEADS, N_META, V_DIM), BF16))
        out_specs = (pl.BlockSpec((1, 1, N_HEADS, tm, V_DIM),
                                  lambda j, i: (j, i // n_ib, 0, i % n_ib, 0)),
                     pl.BlockSpec((1, N_HEADS, N_META, V_DIM), lambda j, i: (j, 0, 0, 0)))
    else:
        out_shape = (jax.ShapeDtypeStruct((3, rows, D_ATTN), F32),
                     jax.ShapeDtypeStruct((3, N_META, D_ATTN), F32))
        out_specs = (pl.BlockSpec((1, tm, D_ATTN), lambda j, i: (j, i, 0)),
                     pl.BlockSpec((1, N_META, D_ATTN), lambda j, i: (j, 0, 0)))
    return pl.pallas_call(
        functools.partial(_proj_kernel, head_major=head_major),
        grid=(3, n_i),
        in_specs=[pl.BlockSpec((tm, D_MODEL), lambda j, i: (i, 0)),
                  pl.BlockSpec((N_META, D_MODEL), lambda j, i: (0, 0)),
                  pl.BlockSpec((1, D_MODEL), lambda j, i: (0, 0)),
                  pl.BlockSpec((D_MODEL, D_ATTN), lambda j, i: (0, j + col_tile0))],
        out_specs=out_specs,
        out_shape=out_shape,
        scratch_shapes=[pltpu.VMEM((D_MODEL, D_ATTN), BF16)],
        compiler_params=pltpu.CompilerParams(
            dimension_semantics=("arbitrary", "arbitrary"),
            vmem_limit_bytes=VMEM_LIMIT_PROJ),
        name="proj_qkv" if head_major else "proj_gates",
    )(x2d, meta, norm_g, w_in)


def _toeplitz_bias(dist, rb_ref, h, far):
    b = jnp.full(dist.shape, (rb_ref[0, h] - far) * LOG2E, F32)
    for j, thr in enumerate(BUCKET_THRESHOLDS, start=1):
        b = jnp.where(dist >= thr, (rb_ref[j, h] - far) * LOG2E, b)
    return b


def _fill_bias(ref, lead, n_rows, n_cols, d0, rb_ref, h, far):
    sub = lax.broadcasted_iota(jnp.int32, (8, V_DIM), 0)
    lane = lax.broadcasted_iota(jnp.int32, (8, V_DIM), 1)
    zeros = jnp.zeros((8, V_DIM), F32)
    masked = jnp.full((8, V_DIM), NEG_INF, F32)
    cache = {}
    for a8 in range(n_rows // 8):
        for b in range(n_cols // V_DIM):
            off = d0 + V_DIM * b - 8 * a8
            if off + V_DIM - 1 < 0:
                tile = masked
            elif off - 7 >= MAX_DISTANCE:
                tile = zeros
            else:
                if off not in cache:
                    d = off + lane - sub
                    cache[off] = jnp.where(d >= 0, _toeplitz_bias(d, rb_ref, h, far), NEG_INF)
                tile = cache[off]
            ref[(*lead, pl.ds(8 * a8, 8), pl.ds(V_DIM * b, V_DIM))] = tile


def _attn_kernel(rb_ref, lq1_ref, lk1_ref, lq2_ref, lk2_ref, q_ref, k_ref, v_ref, km_ref, vm_ref,
                 g_ref, sg_ref, o_ref, kall, vall, vt_ref, qz_ref, bd_ref, bc_ref, sel_ref, sbuf,
                 sc_ref, mx_ref, m_ref, acc_ref):
    h = pl.program_id(1)
    tq, tk = ATTN_TQ, ATTN_TK
    seq = k_ref.shape[2]
    nq = seq // tq
    n_keys = seq + N_META
    n_pad = vt_ref.shape[1]

    far = rb_ref[N_BUCKETS - 1, h]

    kall[pl.ds(0, N_META), :] = km_ref[0]
    kall[pl.ds(N_META, seq), :] = k_ref[0, 0]
    vall[pl.ds(0, N_META), :] = vm_ref[0]
    vall[pl.ds(N_META, seq), :] = v_ref[0, 0]
    vall[pl.ds(n_keys, n_pad - n_keys), :] = jnp.zeros((n_pad - n_keys, V_DIM), BF16)

    def xpose(n, carry):
        rows = pl.ds(pl.multiple_of(n * V_DIM, V_DIM), V_DIM)
        vt_ref[pl.ds(0, V_DIM), rows] = vall[rows, :].astype(F32).T.astype(BF16)
        return carry

    lax.fori_loop(0, n_pad // V_DIM, xpose, 0, unroll=3)
    vt_ref[pl.ds(V_DIM, SUM_ROWS), :] = jnp.ones((SUM_ROWS, n_pad), BF16)

    q_all = q_ref[0, 0]
    q_lane = lax.broadcasted_iota(jnp.int32, q_all.shape, 1)
    qz_ref[0] = jnp.where(q_lane < HEAD_DIM, q_all, jnp.zeros_like(q_all))
    qz_ref[1] = jnp.where(q_lane >= HEAD_DIM, q_all, jnp.zeros_like(q_all))

    _fill_bias(bd_ref, (), tk, tq, N_META, rb_ref, h, far)
    _fill_bias(bc_ref, (), N_META, tq, N_META - tk, rb_ref, h, far)
    _fill_bias(sel_ref, (1,), tk, tq, N_META + tk, rb_ref, h, far)
    sel_ref[0] = jnp.zeros((tk, tq), F32)

    lam = (jnp.exp(jnp.sum(lq1_ref[...] * lk1_ref[...], keepdims=True))
           - jnp.exp(jnp.sum(lq2_ref[...] * lk2_ref[...], keepdims=True))
           + LAMBDA_INIT)

    def init_stats(st):
        m_ref[st] = jnp.full(m_ref.shape[1:], NEG_INF, F32)
        for a in range(2):
            acc_ref[st, a, pl.ds(0, V_DIM), :] = jnp.zeros((V_DIM, tq), F32)
            acc_ref[st, a, pl.ds(V_DIM, SUM_ROWS), :] = jnp.ones((SUM_ROWS, tq), F32)

    def produce(buf, q_off, k_off, kind, sel=None):
        q_rows = pl.ds(pl.multiple_of(q_off, tq), tq)
        k_off = pl.multiple_of(k_off, tk)
        k_t = kall[pl.ds(k_off, tk), :]
        for a in range(2):
            qa = qz_ref[a, q_rows, :]
            s = _dot_nt(k_t, qa)
            if kind == "near":
                s = s + sel_ref[1]
            elif kind == "diag":
                s = s + bd_ref[...]
            elif kind == "select":
                s = s + sel_ref[sel]
            sbuf[buf, a] = s
            mx = jnp.max(s, axis=0, keepdims=True)
            if kind == "diag":
                sc = _dot_nt(kall[pl.ds(k_off + tk, N_META), :], qa) + bc_ref[...]
                sc_ref[buf, a] = sc
                mx = jnp.maximum(mx, jnp.max(sc, axis=0, keepdims=True))
            mx_ref[buf, a] = mx

    def consume(buf, st, k_off, diag=False):
        k_off = pl.multiple_of(k_off, tk)
        v_t = vt_ref[:, pl.ds(k_off, tk)]
        for a in range(2):
            m_old = m_ref[st, a]
            m_new = jnp.maximum(m_old, mx_ref[buf, a])
            alpha = jnp.exp2(m_old - m_new)
            p = jnp.exp2(sbuf[buf, a] - m_new)
            pv = _dot(v_t, p.astype(BF16))
            if diag:
                pc = jnp.exp2(sc_ref[buf, a] - m_new)
                v_c = vt_ref[:, pl.ds(k_off + tk, V_DIM)][:, :N_META]
                pv = pv + _dot(v_c, pc.astype(BF16))
            acc_ref[st, a] = alpha * acc_ref[st, a] + pv
            m_ref[st, a] = m_new

    def finalize(st, q_off):
        q_rows = pl.ds(pl.multiple_of(q_off, tq), tq)
        heads = [acc_ref[st, a, pl.ds(0, V_DIM), :] / acc_ref[st, a, pl.ds(V_DIM, 1), :]
                 for a in range(2)]
        out_t = heads[0] - lam * heads[1]
        inv = lax.rsqrt(jnp.mean(out_t * out_t, axis=0, keepdims=True) + SUBLN_EPS)
        att = ((out_t * inv).T * sg_ref[...]) * (1.0 - LAMBDA_INIT)
        g = g_ref[0, q_rows, :]
        o_ref[q_rows, :] = (att * (g * jax.nn.sigmoid(g))).astype(o_ref.dtype)

    init_stats(0)
    init_stats(1)
    produce(0, 0, 0, "diag")

    def q_tile(i, cur):
        oth = 1 - cur
        q_off = i * tq

        def far_pair(tt, c):
            t = 2 * tt
            produce(oth, q_off, (t + 1) * tk, "far")
            consume(cur, cur, t * tk)
            produce(cur, q_off, (t + 2) * tk, "far")
            consume(oth, cur, (t + 1) * tk)
            return c

        n_far = jnp.maximum(i - 2, 0)
        lax.fori_loop(0, n_far // 2, far_pair, 0)

        @pl.when(n_far % 2 == 1)
        def _():
            consume(cur, cur, (n_far - 1) * tk)
            produce(cur, q_off, n_far * tk, "far")

        @pl.when(i >= 2)
        def _():
            produce(oth, q_off, (i - 1) * tk, "near")
            consume(cur, cur, (i - 2) * tk)
            produce(cur, q_off, i * tk, "diag")
            consume(oth, cur, (i - 1) * tk)

        @pl.when(i == 1)
        def _():
            consume(cur, cur, 0)
            produce(cur, q_off, tk, "diag")

        finalize(oth, jnp.maximum(i - 1, 0) * tq)
        init_stats(oth)
        nxt = jnp.minimum(i + 1, nq - 1)
        produce(oth, nxt * tq, 0, "select", sel=jnp.where(nxt == 1, 1, 0))
        consume(cur, cur, i * tk, diag=True)

    def q_pair(ii, carry):
        q_tile(2 * ii, 0)
        q_tile(2 * ii + 1, 1)
        return carry

    lax.fori_loop(0, nq // 2, q_pair, 0)
    finalize((nq - 1) % 2, (nq - 1) * tq)


def _attention(qkv, qkv_meta, gates, rel_bias, lam_q1, lam_k1, lam_q2, lam_k2, subln_g):
    _, batch, _, seq, _ = qkv.shape
    tq, tk = ATTN_TQ, ATTN_TK
    n_pad = seq + V_DIM
    smem = pl.BlockSpec(memory_space=pltpu.SMEM)
    row64 = pl.BlockSpec((1, HEAD_DIM), lambda b, h: (0, 0))
    qkv_spec = lambda which: pl.BlockSpec((1, 1, 1, seq, V_DIM), lambda b, h: (which, b, h, 0, 0))
    meta_spec = lambda which: pl.BlockSpec((1, 1, N_META, V_DIM), lambda b, h: (which, h, 0, 0))

    def kernel(rb, lq1, lk1, lq2, lk2, q_ref, k_ref, v_ref, km_ref, vm_ref, *rest):
        _attn_kernel(rb, lq1, lk1, lq2, lk2, q_ref.at[0], k_ref.at[0], v_ref.at[0],
                     km_ref.at[0], vm_ref.at[0], *rest)

    return pl.pallas_call(
        kernel,
        grid=(batch, N_HEADS),
        in_specs=[smem, row64, row64, row64, row64,
                  qkv_spec(0), qkv_spec(1), qkv_spec(2), meta_spec(1), meta_spec(2),
                  pl.BlockSpec((1, seq, V_DIM), lambda b, h: (0, b, h)),
                  pl.BlockSpec((1, V_DIM), lambda b, h: (0, 0))],
        out_specs=pl.BlockSpec((seq, V_DIM), lambda b, h: (b, h)),
        out_shape=jax.ShapeDtypeStruct((batch * seq, D_ATTN), BF16),
        scratch_shapes=[pltpu.VMEM((seq + N_META, V_DIM), BF16),
                        pltpu.VMEM((n_pad, V_DIM), BF16),
                        pltpu.VMEM((V_DIM + SUM_ROWS, n_pad), BF16),
                        pltpu.VMEM((2, seq, V_DIM), BF16),
                        pltpu.VMEM((tk, tq), F32),
                        pltpu.VMEM((N_META, tq), F32),
                        pltpu.VMEM((2, tk, tq), F32),
                        pltpu.VMEM((2, 2, tk, tq), F32),
                        pltpu.VMEM((2, 2, N_META, tq), F32),
                        pltpu.VMEM((2, 2, 1, tq), F32),
                        pltpu.VMEM((2, 2, 1, tq), F32),
                        pltpu.VMEM((2, 2, V_DIM + SUM_ROWS, tq), F32)],
        compiler_params=pltpu.CompilerParams(
            dimension_semantics=("arbitrary", "arbitrary"),
            vmem_limit_bytes=VMEM_LIMIT_ATTN),
        name="diff_attention",
    )(rel_bias, lam_q1, lam_k1, lam_q2, lam_k2, qkv, qkv, qkv, qkv_meta, qkv_meta, gates, subln_g)


def _scan_block(a, b):
    n = a.shape[0]
    row = lax.broadcasted_iota(jnp.int32, a.shape, 0)
    s = 1
    while s < n:
        keep = row >= s
        a_sh = jnp.where(keep, pltpu.roll(a, s, 0), 1.0)
        b_sh = jnp.where(keep, pltpu.roll(b, s, 0), 0.0)
        b = b + a * b_sh
        a = a * a_sh
        s *= 2
    return a, b


def _tail_kernel(x_ref, att_ref, u_ref, um_ref, g_ref, cw_ref, cb_ref, wa_ref, wx_ref, ba_ref, bx_ref,
                 lam_ref, w_ref, fg_ref, o_ref, wb_ref, wg_ref, ubuf, h_ref, a_s, b_s, rec_s, rec_prev,
                 *, chunks_per_batch):
    s = pl.program_id(0)
    n = u_ref.shape[0]
    hist = 8
    seg = n // 8

    x = -lam_ref[...]
    softplus = jnp.maximum(x, 0.0) + jnp.log1p(jnp.exp(-jnp.abs(x)))

    @pl.when(s == 0)
    def _():
        wb_ref[...] = w_ref[...].astype(BF16)
        for blk in range(N_LRU_BLOCKS):
            wg_ref[blk, :, pl.ds(0, LRU_BLOCK)] = wa_ref[blk].astype(BF16)
            wg_ref[blk, :, pl.ds(LRU_BLOCK, LRU_BLOCK)] = wx_ref[blk].astype(BF16)
        rec_s[...] = jnp.zeros_like(rec_s)

    def gates(rows, cols, blk, first):
        cw = cw_ref[:, cols]
        uc = (cw[3:4] * ubuf[pl.ds(hist, rows), cols] + cw[2:3] * ubuf[pl.ds(hist - 1, rows), cols]
              + cw[1:2] * ubuf[pl.ds(hist - 2, rows), cols] + cw[0:1] * ubuf[pl.ds(hist - 3, rows), cols]
              + cb_ref[:, cols])
        pre = _dot(uc.astype(BF16), wg_ref[blk])
        r = jax.nn.sigmoid(pre[:, :LRU_BLOCK] + ba_ref[:, cols])
        gi = jax.nn.sigmoid(pre[:, LRU_BLOCK:] + bx_ref[:, cols])
        log_a = -LRU_C * r * softplus[:, cols]
        a = jnp.exp(log_a)
        mult = jnp.sqrt(jnp.tanh(-log_a) * (a * a + 1.0))
        if first:
            row = lax.broadcasted_iota(jnp.int32, mult.shape, 0)
            mult = jnp.where(row == 0, 1.0, mult)
        return a, mult * gi * uc

    @pl.when(s % chunks_per_batch == 0)
    def _():
        ubuf[pl.ds(0, hist), :] = jnp.zeros((hist, D_LRU), F32)
        ubuf[pl.ds(hist, N_META), :] = um_ref[...]
        for blk in range(N_LRU_BLOCKS):
            cols = slice(blk * LRU_BLOCK, (blk + 1) * LRU_BLOCK)
            a, b = gates(N_META, cols, blk, True)
            _, hm = _scan_block(a, b)
            h_ref[:, cols] = hm[N_META - 1:N_META, :]
        ubuf[pl.ds(0, hist), :] = ubuf[pl.ds(N_META, hist), :]

    rec_prev[...] = rec_s[...]
    ubuf[pl.ds(hist, n), :] = u_ref[...]
    sub = lax.broadcasted_iota(jnp.int32, (8, LRU_BLOCK), 0)
    n_out = D_MODEL // N_LRU_BLOCKS
    ssq = jnp.zeros((n, LRU_BLOCK), F32)
    for blk in range(N_LRU_BLOCKS):
        ocols = slice(blk * n_out, (blk + 1) * n_out)
        z = (x_ref[:, ocols] + _dot(att_ref[...], wb_ref[pl.ds(0, D_ATTN), ocols])
             + _dot(rec_prev[...], wb_ref[pl.ds(D_ATTN, D_LRU), ocols]))
        o_ref[:, ocols] = z
        for part in range(n_out // LRU_BLOCK):
            zp = z[:, part * LRU_BLOCK:(part + 1) * LRU_BLOCK]
            ssq = ssq + zp * zp

        cols = slice(blk * LRU_BLOCK, (blk + 1) * LRU_BLOCK)
        a, b = gates(n, cols, blk, False)
        for k in range(8):
            a_s[blk, pl.ds(SEG_PITCH * k, seg), :] = a[seg * k:seg * (k + 1)]
            b_s[blk, pl.ds(SEG_PITCH * k, seg), :] = b[seg * k:seg * (k + 1)]
        step = lambda j: pl.ds(j, 8, stride=SEG_PITCH)
        a_run = a_s[blk, step(0), :]
        h_run = b_s[blk, step(0), :]
        for j in range(1, seg):
            aj = a_s[blk, step(j), :]
            h_run = aj * h_run + b_s[blk, step(j), :]
            a_run = aj * a_run
        a_cum, h_cum = _scan_block(a_run, h_run)
        carry = h_ref[:, cols]
        seg_end = h_cum + a_cum * carry
        h_ref[:, cols] = seg_end[7:8, :]
        h_run = jnp.where(sub == 0, carry, pltpu.roll(seg_end, 1, 0))
        for j in range(seg):
            h_run = a_s[blk, step(j), :] * h_run + b_s[blk, step(j), :]
            b_s[blk, step(j), :] = h_run
        g = g_ref[:, cols]
        h = jnp.concatenate([b_s[blk, pl.ds(SEG_PITCH * k, seg), :] for k in range(8)], axis=0)
        rec_s[:, cols] = (h * (g * jax.nn.sigmoid(g))).astype(BF16)
    ubuf[pl.ds(0, hist), :] = ubuf[pl.ds(n, hist), :]
    inv = lax.rsqrt(jnp.sum(ssq, axis=1, keepdims=True) * (1.0 / D_MODEL) + NORM_EPS)
    o_ref[...] = (o_ref[...] * inv) * fg_ref[...]


def _tail(x2d, att, gates, gates_meta, conv_w, conv_b, w_a, b_a, w_x, b_x, lru_lambda, w_out,
          final_g, *, seq):
    rows = x2d.shape[0]
    tm = OUT_ROWS
    n_tiles = rows // tm
    prev = lambda s: (jnp.maximum(s - 1, 0), 0)
    cur = lambda which: (lambda s: (which, jnp.minimum(s, n_tiles - 1), 0))
    row = lambda n: pl.BlockSpec((n, D_LRU), lambda s: (0, 0))
    wspec = pl.BlockSpec((N_LRU_BLOCKS, LRU_BLOCK, LRU_BLOCK), lambda s: (0, 0, 0))

    def kernel(x_ref, att_ref, u_ref, um_ref, g_ref, *rest):
        _tail_kernel(x_ref, att_ref, u_ref.at[0], um_ref.at[0], g_ref.at[0], *rest,
                     chunks_per_batch=seq // tm)

    return pl.pallas_call(
        kernel,
        grid=(n_tiles + 1,),
        in_specs=[pl.BlockSpec((tm, D_MODEL), prev),
                  pl.BlockSpec((tm, D_ATTN), prev),
                  pl.BlockSpec((1, tm, D_LRU), cur(1)),
                  pl.BlockSpec((1, N_META, D_LRU), lambda s: (1, 0, 0)),
                  pl.BlockSpec((1, tm, D_LRU), cur(2)),
                  row(CONV_WIDTH), row(1), wspec, wspec, row(1), row(1), row(1),
                  pl.BlockSpec((D_ATTN + D_LRU, D_MODEL), lambda s: (0, 0),
                               pipeline_mode=pl.Buffered(1)),
                  pl.BlockSpec((1, D_MODEL), lambda s: (0, 0))],
        out_specs=pl.BlockSpec((tm, D_MODEL), prev),
        out_shape=jax.ShapeDtypeStruct((rows, D_MODEL), F32),
        scratch_shapes=[pltpu.VMEM((D_ATTN + D_LRU, D_MODEL), BF16),
                        pltpu.VMEM((N_LRU_BLOCKS, LRU_BLOCK, 2 * LRU_BLOCK), BF16),
                        pltpu.VMEM((8 + tm, D_LRU), F32),
                        pltpu.VMEM((1, D_LRU), F32),
                        pltpu.VMEM((N_LRU_BLOCKS, 8 * SEG_PITCH, LRU_BLOCK), F32),
                        pltpu.VMEM((N_LRU_BLOCKS, 8 * SEG_PITCH, LRU_BLOCK), F32),
                        pltpu.VMEM((tm, D_LRU), BF16),
                        pltpu.VMEM((tm, D_LRU), BF16)],
        compiler_params=pltpu.CompilerParams(
            dimension_semantics=("arbitrary",),
            vmem_limit_bytes=VMEM_LIMIT_OUT),
        name="rglru_out_proj",
    )(x2d, att, gates, gates_meta, gates, conv_w, conv_b, w_a, w_x, b_a, b_x, lru_lambda, w_out,
      final_g)


def kernel(x, meta_tokens, rel_bias, norm_g, w_in, conv_w, conv_b, w_a, b_a, w_x, b_x, lru_lambda,
           lam_q1, lam_k1, lam_q2, lam_k2, subln_g, w_out, final_g):
    batch, seq, _ = x.shape
    x2d = x.reshape(batch * seq, D_MODEL)
    qkv, qkv_meta = _project(x2d, meta_tokens, norm_g, w_in[0], col_tile0=0, head_major=True,
                             batch=batch, seq=seq)
    gates, gates_meta = _project(x2d, meta_tokens, norm_g, w_in[0], col_tile0=3, head_major=False,
                                 batch=batch, seq=seq)
    att = _attention(qkv, qkv_meta, gates, rel_bias, lam_q1, lam_k1, lam_q2, lam_k2, subln_g)
    out = _tail(x2d, att, gates, gates_meta, conv_w[0], conv_b, w_a[0], b_a, w_x[0], b_x, lru_lambda,
                w_out[0], final_g.reshape(1, D_MODEL), seq=seq)
    return out.reshape(batch, seq, D_MODEL)
```

```python
import functools
import math

import numpy as np
import jax
import jax.numpy as jnp
from jax import lax
from jax.experimental import pallas as pl
from jax.experimental.pallas import tpu as pltpu

D_MODEL = 2048
N_META = 16
D_ATTN = 1024
D_LRU = 1024
N_HEADS = 8
HEAD_DIM = 64
V_DIM = 128
N_LRU_BLOCKS = 8
LRU_BLOCK = 128
CONV_WIDTH = 4
LRU_C = 8.0
N_BUCKETS = 32
MAX_DISTANCE = 128
NORM_EPS = 1e-6
SUBLN_EPS = 1e-5
NEG_INF = -1e30
LAMBDA_INIT = 0.8 - 0.6 * math.exp(-0.3 * 0)
LOG2E = math.log2(math.e)

BF16 = jnp.bfloat16
F32 = jnp.float32

VMEM_LIMIT_PROJ = 48 * 1024 * 1024
VMEM_LIMIT_ATTN = 48 * 1024 * 1024
VMEM_LIMIT_OUT = 56 * 1024 * 1024

PROJ_ROWS = 512
ATTN_TQ = 512
ATTN_TK = 512
SUM_ROWS = 16
OUT_ROWS = 256
SEG_PITCH = 40


def _bucket_thresholds():
    max_exact = N_BUCKETS // 2
    d = np.arange(0, 4 * MAX_DISTANCE, dtype=np.int64)
    val = (np.log(np.maximum(d, 1).astype(np.float64) / max_exact)
           / math.log(MAX_DISTANCE / max_exact) * (N_BUCKETS - max_exact))
    large = np.minimum(max_exact + np.floor(val + 1e-9).astype(np.int64), N_BUCKETS - 1)
    bucket = np.where(d < max_exact, d, large)
    frac = np.abs(val - np.round(val))
    interior = (d > max_exact) & (d < MAX_DISTANCE)
    assert frac[interior].min() > 1e-3
    assert (np.diff(bucket) >= 0).all() and bucket[MAX_DISTANCE] == N_BUCKETS - 1
    return tuple(int(np.argmax(bucket >= j)) for j in range(1, N_BUCKETS))


BUCKET_THRESHOLDS = _bucket_thresholds()


def _rms(x, g, eps):
    y = x * lax.rsqrt(jnp.mean(x * x, axis=-1, keepdims=True) + eps)
    return y * g


def _dot(a, b):
    return jnp.dot(a, b, preferred_element_type=F32)


def _dot_nt(a, b):
    return lax.dot_general(a, b, (((1,), (1,)), ((), ())), preferred_element_type=F32)


def _proj_kernel(x_ref, meta_ref, g_ref, w_ref, o_ref, om_ref, wb_ref, *, head_major):
    j = pl.program_id(0)
    i = pl.program_id(1)
    out_scale = jnp.where(j == 0, HEAD_DIM ** -0.5 * LOG2E, 1.0).astype(F32) if head_major else 1.0

    @pl.when(i == 0)
    def _():
        wb_ref[...] = w_ref[...].astype(BF16)
        hm = _rms(meta_ref[...], g_ref[...], NORM_EPS).astype(BF16)
        ym = _dot(hm, wb_ref[...]) * out_scale
        if head_major:
            for h in range(N_HEADS):
                om_ref[0, h] = ym[:, h * V_DIM:(h + 1) * V_DIM].astype(om_ref.dtype)
        else:
            om_ref[0] = ym.astype(om_ref.dtype)

    hx = _rms(x_ref[...], g_ref[...], NORM_EPS).astype(BF16)
    y = _dot(hx, wb_ref[...]) * out_scale
    if head_major:
        for h in range(N_HEADS):
            o_ref[0, 0, h] = y[:, h * V_DIM:(h + 1) * V_DIM].astype(o_ref.dtype)
    else:
        o_ref[0] = y.astype(o_ref.dtype)


def _project(x2d, meta, norm_g, w_in, *, col_tile0, head_major, batch, seq):
    rows = x2d.shape[0]
    tm = PROJ_ROWS
    n_i = rows // tm
    n_ib = seq // tm
    if head_major:
        out_shape = (jax.ShapeDtypeStruct((3, batch, N_HEADS, seq, V_DIM), BF16),
                     jax.ShapeDtypeStruct((3, N_HEADS, N_META, V_DIM), BF16))
        out_specs = (pl.BlockSpec((1, 1, N_HEADS, tm, V_DIM),
                                  lambda j, i: (j, i // n_ib, 0, i % n_ib, 0)),
                     pl.BlockSpec((1, N_HEADS, N_META, V_DIM), lambda j, i: (j, 0, 0, 0)))
    else:
        out_shape = (jax.ShapeDtypeStruct((3, rows, D_ATTN), F32),
                     jax.ShapeDtypeStruct((3, N_META, D_ATTN), F32))
        out_specs = (pl.BlockSpec((1, tm, D_ATTN), lambda j, i: (j, i, 0)),
                     pl.BlockSpec((1, N_META, D_ATTN), lambda j, i: (j, 0, 0)))
    return pl.pallas_call(
        functools.partial(_proj_kernel, head_major=head_major),
        grid=(3, n_i),
        in_specs=[pl.BlockSpec((tm, D_MODEL), lambda j, i: (i, 0)),
                  pl.BlockSpec((N_META, D_MODEL), lambda j, i: (0, 0)),
                  pl.BlockSpec((1, D_MODEL), lambda j, i: (0, 0)),
                  pl.BlockSpec((D_MODEL, D_ATTN), lambda j, i: (0, j + col_tile0))],
        out_specs=out_specs,
        out_shape=out_shape,
        scratch_shapes=[pltpu.VMEM((D_MODEL, D_ATTN), BF16)],
        compiler_params=pltpu.CompilerParams(
            dimension_semantics=("arbitrary", "arbitrary"),
            vmem_limit_bytes=VMEM_LIMIT_PROJ),
        name="proj_qkv" if head_major else "proj_gates",
    )(x2d, meta, norm_g, w_in)


def _toeplitz_bias(dist, rb_ref, h, far):
    b = jnp.full(dist.shape, (rb_ref[0, h] - far) * LOG2E, F32)
    for j, thr in enumerate(BUCKET_THRESHOLDS, start=1):
        b = jnp.where(dist >= thr, (rb_ref[j, h] - far) * LOG2E, b)
    return b


def _fill_bias(ref, lead, n_rows, n_cols, d0, rb_ref, h, far):
    sub = lax.broadcasted_iota(jnp.int32, (8, V_DIM), 0)
    lane = lax.broadcasted_iota(jnp.int32, (8, V_DIM), 1)
    zeros = jnp.zeros((8, V_DIM), F32)
    masked = jnp.full((8, V_DIM), NEG_INF, F32)
    cache = {}
    for a8 in range(n_rows // 8):
        for b in range(n_cols // V_DIM):
            off = d0 + V_DIM * b - 8 * a8
            if off + V_DIM - 1 < 0:
                tile = masked
            elif off - 7 >= MAX_DISTANCE:
                tile = zeros
            else:
                if off not in cache:
                    d = off + lane - sub
                    cache[off] = jnp.where(d >= 0, _toeplitz_bias(d, rb_ref, h, far), NEG_INF)
                tile = cache[off]
            ref[(*lead, pl.ds(8 * a8, 8), pl.ds(V_DIM * b, V_DIM))] = tile


def _attn_kernel(rb_ref, lq1_ref, lk1_ref, lq2_ref, lk2_ref, q_ref, k_ref, v_ref, km_ref, vm_ref,
                 g_ref, sg_ref, o_ref, kall, vall, vt_ref, qz_ref, bd_ref, bc_ref, sel_ref, sbuf,
                 sc_ref, mx_ref, m_ref, acc_ref):
    h = pl.program_id(0)
    tq, tk = ATTN_TQ, ATTN_TK
    seq = k_ref.shape[2]
    nq = seq // tq
    n_keys = seq + N_META
    n_pad = vt_ref.shape[1]

    far = rb_ref[N_BUCKETS - 1, h]

    kall[pl.ds(0, N_META), :] = km_ref[0]
    kall[pl.ds(N_META, seq), :] = k_ref[0, 0]
    vall[pl.ds(0, N_META), :] = vm_ref[0]
    vall[pl.ds(N_META, seq), :] = v_ref[0, 0]
    vall[pl.ds(n_keys, n_pad - n_keys), :] = jnp.zeros((n_pad - n_keys, V_DIM), BF16)

    def xpose(n, carry):
        rows = pl.ds(pl.multiple_of(n * V_DIM, V_DIM), V_DIM)
        vt_ref[pl.ds(0, V_DIM), rows] = vall[rows, :].astype(F32).T.astype(BF16)
        return carry

    lax.fori_loop(0, n_pad // V_DIM, xpose, 0, unroll=3)
    vt_ref[pl.ds(V_DIM, SUM_ROWS), :] = jnp.ones((SUM_ROWS, n_pad), BF16)

    q_all = q_ref[0, 0]
    q_lane = lax.broadcasted_iota(jnp.int32, q_all.shape, 1)
    qz_ref[0] = jnp.where(q_lane < HEAD_DIM, q_all, jnp.zeros_like(q_all))
    qz_ref[1] = jnp.where(q_lane >= HEAD_DIM, q_all, jnp.zeros_like(q_all))

    @pl.when(pl.program_id(1) == 0)
    def _():
        _fill_bias(bd_ref, (), tk, tq, N_META, rb_ref, h, far)
        _fill_bias(bc_ref, (), N_META, tq, N_META - tk, rb_ref, h, far)
        _fill_bias(sel_ref, (1,), tk, tq, N_META + tk, rb_ref, h, far)
        sel_ref[0] = jnp.zeros((tk, tq), F32)

    lam = (jnp.exp(jnp.sum(lq1_ref[...] * lk1_ref[...], keepdims=True))
           - jnp.exp(jnp.sum(lq2_ref[...] * lk2_ref[...], keepdims=True))
           + LAMBDA_INIT)

    def init_stats(st):
        m_ref[st] = jnp.full(m_ref.shape[1:], NEG_INF, F32)
        for a in range(2):
            acc_ref[st, a, pl.ds(0, V_DIM), :] = jnp.zeros((V_DIM, tq), F32)
            acc_ref[st, a, pl.ds(V_DIM, SUM_ROWS), :] = jnp.ones((SUM_ROWS, tq), F32)

    def produce(buf, q_off, k_off, kind, sel=None):
        q_rows = pl.ds(pl.multiple_of(q_off, tq), tq)
        k_off = pl.multiple_of(k_off, tk)
        k_t = kall[pl.ds(k_off, tk), :]
        for a in range(2):
            qa = qz_ref[a, q_rows, :]
            s = _dot_nt(k_t, qa)
            if kind == "near":
                s = s + sel_ref[1]
            elif kind == "diag":
                s = s + bd_ref[...]
            elif kind == "select":
                s = s + sel_ref[sel]
            sbuf[buf, a] = s
            mx = jnp.max(s, axis=0, keepdims=True)
            if kind == "diag":
                sc = _dot_nt(kall[pl.ds(k_off + tk, N_META), :], qa) + bc_ref[...]
                sc_ref[buf, a] = sc
                mx = jnp.maximum(mx, jnp.max(sc, axis=0, keepdims=True))
            mx_ref[buf, a] = mx

    def consume(buf, st, k_off, diag=False):
        k_off = pl.multiple_of(k_off, tk)
        v_t = vt_ref[:, pl.ds(k_off, tk)]
        for a in range(2):
            m_old = m_ref[st, a]
            m_new = jnp.maximum(m_old, mx_ref[buf, a])
            alpha = jnp.exp2(m_old - m_new)
            p = jnp.exp2(sbuf[buf, a] - m_new)
            pv = _dot(v_t, p.astype(BF16))
            if diag:
                pc = jnp.exp2(sc_ref[buf, a] - m_new)
                v_c = vt_ref[:, pl.ds(k_off + tk, V_DIM)][:, :N_META]
                pv = pv + _dot(v_c, pc.astype(BF16))
            acc_ref[st, a] = alpha * acc_ref[st, a] + pv
            m_ref[st, a] = m_new

    def finalize(st, q_off):
        q_rows = pl.ds(pl.multiple_of(q_off, tq), tq)
        heads = [acc_ref[st, a, pl.ds(0, V_DIM), :] / acc_ref[st, a, pl.ds(V_DIM, 1), :]
                 for a in range(2)]
        out_t = heads[0] - lam * heads[1]
        inv = lax.rsqrt(jnp.mean(out_t * out_t, axis=0, keepdims=True) + SUBLN_EPS)
        att = ((out_t * inv).T * sg_ref[...]) * (1.0 - LAMBDA_INIT)
        g = g_ref[0, q_rows, :]
        o_ref[q_rows, :] = (att * (g * jax.nn.sigmoid(g))).astype(o_ref.dtype)

    init_stats(0)
    init_stats(1)
    produce(0, 0, 0, "diag")

    def q_tile(i, cur, odd):
        oth = 1 - cur
        q_off = i * tq

        def far_pair(tt, c):
            t = 2 * tt
            produce(oth, q_off, (t + 1) * tk, "far")
            consume(cur, cur, t * tk)
            produce(cur, q_off, (t + 2) * tk, "far")
            consume(oth, cur, (t + 1) * tk)
            return c

        n_far = jnp.maximum(i - 2, 0)
        lax.fori_loop(0, n_far // 2, far_pair, 0)

        def near_diag():
            produce(oth, q_off, (i - 1) * tk, "near")
            consume(cur, cur, (i - 2) * tk)
            produce(cur, q_off, i * tk, "diag")
            consume(oth, cur, (i - 1) * tk)

        def last():
            finalize(oth, jnp.maximum(i - 1, 0) * tq)
            init_stats(oth)
            nxt = jnp.minimum(i + 1, nq - 1)
            produce(oth, nxt * tq, 0, "select", sel=jnp.where(nxt == 1, 1, 0))
            consume(cur, cur, i * tk, diag=True)

        if odd:
            @pl.when(i >= 3)
            def _():
                consume(cur, cur, (n_far - 1) * tk)
                produce(cur, q_off, n_far * tk, "far")
                near_diag()
                last()

            @pl.when(i == 1)
            def _():
                consume(cur, cur, 0)
                produce(cur, q_off, tk, "diag")
                last()
        else:
            @pl.when(i >= 2)
            def _():
                near_diag()
                last()

            @pl.when(i == 0)
            def _():
                last()

    def q_pair(ii, carry):
        q_tile(2 * ii, 0, False)
        q_tile(2 * ii + 1, 1, True)
        return carry

    lax.fori_loop(0, nq // 2, q_pair, 0)
    finalize((nq - 1) % 2, (nq - 1) * tq)


def _attention(qkv, qkv_meta, gates, rel_bias, lam_q1, lam_k1, lam_q2, lam_k2, subln_g):
    _, batch, _, seq, _ = qkv.shape
    tq, tk = ATTN_TQ, ATTN_TK
    n_pad = seq + V_DIM
    smem = pl.BlockSpec(memory_space=pltpu.SMEM)
    row64 = pl.BlockSpec((1, HEAD_DIM), lambda h, b: (0, 0))
    qkv_spec = lambda which: pl.BlockSpec((1, 1, 1, seq, V_DIM), lambda h, b: (which, b, h, 0, 0))
    meta_spec = lambda which: pl.BlockSpec((1, 1, N_META, V_DIM), lambda h, b: (which, h, 0, 0))

    def kernel(rb, lq1, lk1, lq2, lk2, q_ref, k_ref, v_ref, km_ref, vm_ref, *rest):
        _attn_kernel(rb, lq1, lk1, lq2, lk2, q_ref.at[0], k_ref.at[0], v_ref.at[0],
                     km_ref.at[0], vm_ref.at[0], *rest)

    return pl.pallas_call(
        kernel,
        grid=(N_HEADS, batch),
        in_specs=[smem, row64, row64, row64, row64,
                  qkv_spec(0), qkv_spec(1), qkv_spec(2), meta_spec(1), meta_spec(2),
                  pl.BlockSpec((1, seq, V_DIM), lambda h, b: (0, b, h)),
                  pl.BlockSpec((1, V_DIM), lambda h, b: (0, 0))],
        out_specs=pl.BlockSpec((seq, V_DIM), lambda h, b: (b, h)),
        out_shape=jax.ShapeDtypeStruct((batch * seq, D_ATTN), BF16),
        scratch_shapes=[pltpu.VMEM((seq + N_META, V_DIM), BF16),
                        pltpu.VMEM((n_pad, V_DIM), BF16),
                        pltpu.VMEM((V_DIM + SUM_ROWS, n_pad), BF16),
                        pltpu.VMEM((2, seq, V_DIM), BF16),
                        pltpu.VMEM((tk, tq), F32),
                        pltpu.VMEM((N_META, tq), F32),
                        pltpu.VMEM((2, tk, tq), F32),
                        pltpu.VMEM((2, 2, tk, tq), F32),
                        pltpu.VMEM((2, 2, N_META, tq), F32),
                        pltpu.VMEM((2, 2, 1, tq), F32),
                        pltpu.VMEM((2, 2, 1, tq), F32),
                        pltpu.VMEM((2, 2, V_DIM + SUM_ROWS, tq), F32)],
        compiler_params=pltpu.CompilerParams(
            dimension_semantics=("arbitrary", "arbitrary"),
            vmem_limit_bytes=VMEM_LIMIT_ATTN),
        name="diff_attention",
    )(rel_bias, lam_q1, lam_k1, lam_q2, lam_k2, qkv, qkv, qkv, qkv_meta, qkv_meta, gates, subln_g)


def _scan_block(a, b):
    n = a.shape[0]
    row = lax.broadcasted_iota(jnp.int32, a.shape, 0)
    s = 1
    while s < n:
        keep = row >= s
        a_sh = jnp.where(keep, pltpu.roll(a, s, 0), 1.0)
        b_sh = jnp.where(keep, pltpu.roll(b, s, 0), 0.0)
        b = b + a * b_sh
        a = a * a_sh
        s *= 2
    return a, b


def _tail_kernel(x_ref, att_ref, u_ref, um_ref, g_ref, cw_ref, cb_ref, wa_ref, wx_ref, ba_ref, bx_ref,
                 lam_ref, w_ref, fg_ref, o_ref, wb_ref, wg_ref, ubuf, h_ref, a_s, b_s, rec_s, rec_prev,
                 *, chunks_per_batch):
    s = pl.program_id(0)
    n = u_ref.shape[0]
    hist = 8
    seg = n // 8

    x = -lam_ref[...]
    softplus = jnp.maximum(x, 0.0) + jnp.log1p(jnp.exp(-jnp.abs(x)))

    @pl.when(s == 0)
    def _():
        wb_ref[...] = w_ref[...].astype(BF16)
        for blk in range(N_LRU_BLOCKS):
            wg_ref[blk, :, pl.ds(0, LRU_BLOCK)] = wa_ref[blk].astype(BF16)
            wg_ref[blk, :, pl.ds(LRU_BLOCK, LRU_BLOCK)] = wx_ref[blk].astype(BF16)
        rec_s[...] = jnp.zeros_like(rec_s)

    def gates(rows, cols, blk, first):
        cw = cw_ref[:, cols]
        uc = (cw[3:4] * ubuf[pl.ds(hist, rows), cols] + cw[2:3] * ubuf[pl.ds(hist - 1, rows), cols]
              + cw[1:2] * ubuf[pl.ds(hist - 2, rows), cols] + cw[0:1] * ubuf[pl.ds(hist - 3, rows), cols]
              + cb_ref[:, cols])
        pre = _dot(uc.astype(BF16), wg_ref[blk])
        r = jax.nn.sigmoid(pre[:, :LRU_BLOCK] + ba_ref[:, cols])
        gi = jax.nn.sigmoid(pre[:, LRU_BLOCK:] + bx_ref[:, cols])
        log_a = -LRU_C * r * softplus[:, cols]
        a = jnp.exp(log_a)
        mult = jnp.sqrt(jnp.tanh(-log_a) * (a * a + 1.0))
        if first:
            row = lax.broadcasted_iota(jnp.int32, mult.shape, 0)
            mult = jnp.where(row == 0, 1.0, mult)
        return a, mult * gi * uc

    @pl.when(s % chunks_per_batch == 0)
    def _():
        ubuf[pl.ds(0, hist), :] = jnp.zeros((hist, D_LRU), F32)
        ubuf[pl.ds(hist, N_META), :] = um_ref[...]
        for blk in range(N_LRU_BLOCKS):
            cols = slice(blk * LRU_BLOCK, (blk + 1) * LRU_BLOCK)
            a, b = gates(N_META, cols, blk, True)
            _, hm = _scan_block(a, b)
            h_ref[:, cols] = hm[N_META - 1:N_META, :]
        ubuf[pl.ds(0, hist), :] = ubuf[pl.ds(N_META, hist), :]

    rec_prev[...] = rec_s[...]
    ubuf[pl.ds(hist, n), :] = u_ref[...]
    sub = lax.broadcasted_iota(jnp.int32, (8, LRU_BLOCK), 0)
    n_out = D_MODEL // N_LRU_BLOCKS
    ssq = jnp.zeros((n, LRU_BLOCK), F32)
    for blk in range(N_LRU_BLOCKS):
        ocols = slice(blk * n_out, (blk + 1) * n_out)
        z = (x_ref[:, ocols] + _dot(att_ref[...], wb_ref[pl.ds(0, D_ATTN), ocols])
             + _dot(rec_prev[...], wb_ref[pl.ds(D_ATTN, D_LRU), ocols]))
        o_ref[:, ocols] = z
        for part in range(n_out // LRU_BLOCK):
            zp = z[:, part * LRU_BLOCK:(part + 1) * LRU_BLOCK]
            ssq = ssq + zp * zp

        cols = slice(blk * LRU_BLOCK, (blk + 1) * LRU_BLOCK)
        a, b = gates(n, cols, blk, False)
        for k in range(8):
            a_s[blk, pl.ds(SEG_PITCH * k, seg), :] = a[seg * k:seg * (k + 1)]
            b_s[blk, pl.ds(SEG_PITCH * k, seg), :] = b[seg * k:seg * (k + 1)]
        step = lambda j: pl.ds(j, 8, stride=SEG_PITCH)
        a_run = a_s[blk, step(0), :]
        h_run = b_s[blk, step(0), :]
        for j in range(1, seg):
            aj = a_s[blk, step(j), :]
            h_run = aj * h_run + b_s[blk, step(j), :]
            a_run = aj * a_run
        a_cum, h_cum = _scan_block(a_run, h_run)
        carry = h_ref[:, cols]
        seg_end = h_cum + a_cum * carry
        h_ref[:, cols] = seg_end[7:8, :]
        h_run = jnp.where(sub == 0, carry, pltpu.roll(seg_end, 1, 0))
        for j in range(seg):
            h_run = a_s[blk, step(j), :] * h_run + b_s[blk, step(j), :]
            b_s[blk, step(j), :] = h_run
        g = g_ref[:, cols]
        h = jnp.concatenate([b_s[blk, pl.ds(SEG_PITCH * k, seg), :] for k in range(8)], axis=0)
        rec_s[:, cols] = (h * (g * jax.nn.sigmoid(g))).astype(BF16)
    ubuf[pl.ds(0, hist), :] = ubuf[pl.ds(n, hist), :]
    inv = lax.rsqrt(jnp.sum(ssq, axis=1, keepdims=True) * (1.0 / D_MODEL) + NORM_EPS)
    o_ref[...] = (o_ref[...] * inv) * fg_ref[...]


def _tail(x2d, att, gates, gates_meta, conv_w, conv_b, w_a, b_a, w_x, b_x, lru_lambda, w_out,
          final_g, *, seq):
    rows = x2d.shape[0]
    tm = OUT_ROWS
    n_tiles = rows // tm
    prev = lambda s: (jnp.maximum(s - 1, 0), 0)
    cur = lambda which: (lambda s: (which, jnp.minimum(s, n_tiles - 1), 0))
    row = lambda n: pl.BlockSpec((n, D_LRU), lambda s: (0, 0))
    wspec = pl.BlockSpec((N_LRU_BLOCKS, LRU_BLOCK, LRU_BLOCK), lambda s: (0, 0, 0))

    def kernel(x_ref, att_ref, u_ref, um_ref, g_ref, *rest):
        _tail_kernel(x_ref, att_ref, u_ref.at[0], um_ref.at[0], g_ref.at[0], *rest,
                     chunks_per_batch=seq // tm)

    return pl.pallas_call(
        kernel,
        grid=(n_tiles + 1,),
        in_specs=[pl.BlockSpec((tm, D_MODEL), prev),
                  pl.BlockSpec((tm, D_ATTN), prev),
                  pl.BlockSpec((1, tm, D_LRU), cur(1)),
                  pl.BlockSpec((1, N_META, D_LRU), lambda s: (1, 0, 0)),
                  pl.BlockSpec((1, tm, D_LRU), cur(2)),
                  row(CONV_WIDTH), row(1), wspec, wspec, row(1), row(1), row(1),
                  pl.BlockSpec((D_ATTN + D_LRU, D_MODEL), lambda s: (0, 0),
                               pipeline_mode=pl.Buffered(1)),
                  pl.BlockSpec((1, D_MODEL), lambda s: (0, 0))],
        out_specs=pl.BlockSpec((tm, D_MODEL), prev),
        out_shape=jax.ShapeDtypeStruct((rows, D_MODEL), F32),
        scratch_shapes=[pltpu.VMEM((D_ATTN + D_LRU, D_MODEL), BF16),
                        pltpu.VMEM((N_LRU_BLOCKS, LRU_BLOCK, 2 * LRU_BLOCK), BF16),
                        pltpu.VMEM((8 + tm, D_LRU), F32),
                        pltpu.VMEM((1, D_LRU), F32),
                        pltpu.VMEM((N_LRU_BLOCKS, 8 * SEG_PITCH, LRU_BLOCK), F32),
                        pltpu.VMEM((N_LRU_BLOCKS, 8 * SEG_PITCH, LRU_BLOCK), F32),
                        pltpu.VMEM((tm, D_LRU), BF16),
                        pltpu.VMEM((tm, D_LRU), BF16)],
        compiler_params=pltpu.CompilerParams(
            dimension_semantics=("arbitrary",),
            vmem_limit_bytes=VMEM_LIMIT_OUT),
        name="rglru_out_proj",
    )(x2d, att, gates, gates_meta, gates, conv_w, conv_b, w_a, w_x, b_a, b_x, lru_lambda, w_out,
      final_g)


def kernel(x, meta_tokens, rel_bias, norm_g, w_in, conv_w, conv_b, w_a, b_a, w_x, b_x, lru_lambda,
           lam_q1, lam_k1, lam_q2, lam_k2, subln_g, w_out, final_g):
    batch, seq, _ = x.shape
    x2d = x.reshape(batch * seq, D_MODEL)
    qkv, qkv_meta = _project(x2d, meta_tokens, norm_g, w_in[0], col_tile0=0, head_major=True,
                             batch=batch, seq=seq)
    gates, gates_meta = _project(x2d, meta_tokens, norm_g, w_in[0], col_tile0=3, head_major=False,
                                 batch=batch, seq=seq)
    att = _attention(qkv, qkv_meta, gates, rel_bias, lam_q1, lam_k1, lam_q2, lam_k2, subln_g)
    out = _tail(x2d, att, gates, gates_meta, conv_w[0], conv_b, w_a[0], b_a, w_x[0], b_x, lru_lambda,
                w_out[0], final_g.reshape(1, D_MODEL), seq=seq)
    return out.reshape(batch, seq, D_MODEL)
```

```python
import functools
import math

import numpy as np
import jax
import jax.numpy as jnp
from jax import lax
from jax.experimental import pallas as pl
from jax.experimental.pallas import tpu as pltpu

D_MODEL = 2048
N_META = 16
D_ATTN = 1024
D_LRU = 1024
N_HEADS = 8
HEAD_DIM = 64
V_DIM = 128
N_LRU_BLOCKS = 8
LRU_BLOCK = 128
CONV_WIDTH = 4
LRU_C = 8.0
N_BUCKETS = 32
MAX_DISTANCE = 128
NORM_EPS = 1e-6
SUBLN_EPS = 1e-5
NEG_INF = -1e30
LAMBDA_INIT = 0.8 - 0.6 * math.exp(-0.3 * 0)
LOG2E = math.log2(math.e)

BF16 = jnp.bfloat16
F32 = jnp.float32

VMEM_LIMIT_PROJ = 56 * 1024 * 1024
VMEM_LIMIT_ATTN = 48 * 1024 * 1024
VMEM_LIMIT_OUT = 56 * 1024 * 1024

PROJ_ROWS = 1024
ATTN_TQ = 512
ATTN_TK = 512
SUM_ROWS = 16
OUT_ROWS = 256
SEG_PITCH = 40


def _bucket_thresholds():
    max_exact = N_BUCKETS // 2
    d = np.arange(0, 4 * MAX_DISTANCE, dtype=np.int64)
    val = (np.log(np.maximum(d, 1).astype(np.float64) / max_exact)
           / math.log(MAX_DISTANCE / max_exact) * (N_BUCKETS - max_exact))
    large = np.minimum(max_exact + np.floor(val + 1e-9).astype(np.int64), N_BUCKETS - 1)
    bucket = np.where(d < max_exact, d, large)
    frac = np.abs(val - np.round(val))
    interior = (d > max_exact) & (d < MAX_DISTANCE)
    assert frac[interior].min() > 1e-3
    assert (np.diff(bucket) >= 0).all() and bucket[MAX_DISTANCE] == N_BUCKETS - 1
    return tuple(int(np.argmax(bucket >= j)) for j in range(1, N_BUCKETS))


BUCKET_THRESHOLDS = _bucket_thresholds()


def _rms(x, g, eps):
    y = x * lax.rsqrt(jnp.mean(x * x, axis=-1, keepdims=True) + eps)
    return y * g


def _dot(a, b):
    return jnp.dot(a, b, preferred_element_type=F32)


def _dot_nt(a, b):
    return lax.dot_general(a, b, (((1,), (1,)), ((), ())), preferred_element_type=F32)


def _proj_kernel(x_ref, meta_ref, g_ref, w_ref, o_ref, om_ref, wb_ref, *, head_major):
    j = pl.program_id(0)
    i = pl.program_id(1)
    out_scale = jnp.where(j == 0, HEAD_DIM ** -0.5 * LOG2E, 1.0).astype(F32) if head_major else 1.0

    @pl.when(i == 0)
    def _():
        wb_ref[...] = w_ref[...].astype(BF16)
        hm = _rms(meta_ref[...], g_ref[...], NORM_EPS).astype(BF16)
        ym = _dot(hm, wb_ref[...]) * out_scale
        if head_major:
            for h in range(N_HEADS):
                om_ref[0, h] = ym[:, h * V_DIM:(h + 1) * V_DIM].astype(om_ref.dtype)
        else:
            om_ref[0] = ym.astype(om_ref.dtype)

    hx = _rms(x_ref[...], g_ref[...], NORM_EPS).astype(BF16)
    y = _dot(hx, wb_ref[...]) * out_scale
    if head_major:
        for h in range(N_HEADS):
            o_ref[0, 0, h] = y[:, h * V_DIM:(h + 1) * V_DIM].astype(o_ref.dtype)
    else:
        o_ref[0] = y.astype(o_ref.dtype)


def _project(x2d, meta, norm_g, w_in, *, col_tile0, head_major, batch, seq):
    rows = x2d.shape[0]
    tm = PROJ_ROWS
    n_i = rows // tm
    n_ib = seq // tm
    if head_major:
        out_shape = (jax.ShapeDtypeStruct((3, batch, N_HEADS, seq, V_DIM), BF16),
                     jax.ShapeDtypeStruct((3, N_HEADS, N_META, V_DIM), BF16))
        out_specs = (pl.BlockSpec((1, 1, N_HEADS, tm, V_DIM),
                                  lambda j, i: (j, i // n_ib, 0, i % n_ib, 0)),
                     pl.BlockSpec((1, N_HEADS, N_META, V_DIM), lambda j, i: (j, 0, 0, 0)))
    else:
        out_shape = (jax.ShapeDtypeStruct((3, rows, D_ATTN), F32),
                     jax.ShapeDtypeStruct((3, N_META, D_ATTN), F32))
        out_specs = (pl.BlockSpec((1, tm, D_ATTN), lambda j, i: (j, i, 0)),
                     pl.BlockSpec((1, N_META, D_ATTN), lambda j, i: (j, 0, 0)))
    return pl.pallas_call(
        functools.partial(_proj_kernel, head_major=head_major),
        grid=(3, n_i),
        in_specs=[pl.BlockSpec((tm, D_MODEL), lambda j, i: (i, 0)),
                  pl.BlockSpec((N_META, D_MODEL), lambda j, i: (0, 0)),
                  pl.BlockSpec((1, D_MODEL), lambda j, i: (0, 0)),
                  pl.BlockSpec((D_MODEL, D_ATTN), lambda j, i: (0, j + col_tile0))],
        out_specs=out_specs,
        out_shape=out_shape,
        scratch_shapes=[pltpu.VMEM((D_MODEL, D_ATTN), BF16)],
        compiler_params=pltpu.CompilerParams(
            dimension_semantics=("arbitrary", "arbitrary"),
            vmem_limit_bytes=VMEM_LIMIT_PROJ),
        name="proj_qkv" if head_major else "proj_gates",
    )(x2d, meta, norm_g, w_in)


def _toeplitz_bias(dist, rb_ref, h, far):
    b = jnp.full(dist.shape, (rb_ref[0, h] - far) * LOG2E, F32)
    for j, thr in enumerate(BUCKET_THRESHOLDS, start=1):
        b = jnp.where(dist >= thr, (rb_ref[j, h] - far) * LOG2E, b)
    return b


def _fill_bias(ref, lead, n_rows, n_cols, d0, rb_ref, h, far):
    sub = lax.broadcasted_iota(jnp.int32, (8, V_DIM), 0)
    lane = lax.broadcasted_iota(jnp.int32, (8, V_DIM), 1)
    zeros = jnp.zeros((8, V_DIM), F32)
    masked = jnp.full((8, V_DIM), NEG_INF, F32)
    cache = {}
    for a8 in range(n_rows // 8):
        for b in range(n_cols // V_DIM):
            off = d0 + V_DIM * b - 8 * a8
            if off + V_DIM - 1 < 0:
                tile = masked
            elif off - 7 >= MAX_DISTANCE:
                tile = zeros
            else:
                if off not in cache:
                    d = off + lane - sub
                    cache[off] = jnp.where(d >= 0, _toeplitz_bias(d, rb_ref, h, far), NEG_INF)
                tile = cache[off]
            ref[(*lead, pl.ds(8 * a8, 8), pl.ds(V_DIM * b, V_DIM))] = tile


def _attn_kernel(rb_ref, lq1_ref, lk1_ref, lq2_ref, lk2_ref, q_ref, k_ref, v_ref, km_ref, vm_ref,
                 g_ref, sg_ref, o_ref, kall, vall, vt_ref, qz_ref, bd_ref, bc_ref, sel_ref, sbuf,
                 sc_ref, mx_ref, m_ref, acc_ref):
    h = pl.program_id(0)
    tq, tk = ATTN_TQ, ATTN_TK
    seq = k_ref.shape[2]
    nq = seq // tq
    n_keys = seq + N_META
    n_pad = vt_ref.shape[1]

    far = rb_ref[N_BUCKETS - 1, h]

    kall[pl.ds(0, N_META), :] = km_ref[0]
    kall[pl.ds(N_META, seq), :] = k_ref[0, 0]
    vall[pl.ds(0, N_META), :] = vm_ref[0]
    vall[pl.ds(N_META, seq), :] = v_ref[0, 0]
    vall[pl.ds(n_keys, n_pad - n_keys), :] = jnp.zeros((n_pad - n_keys, V_DIM), BF16)

    def xpose(n, carry):
        rows = pl.ds(pl.multiple_of(n * V_DIM, V_DIM), V_DIM)
        vt_ref[pl.ds(0, V_DIM), rows] = vall[rows, :].astype(F32).T.astype(BF16)
        return carry

    lax.fori_loop(0, n_pad // V_DIM, xpose, 0, unroll=3)
    vt_ref[pl.ds(V_DIM, SUM_ROWS), :] = jnp.ones((SUM_ROWS, n_pad), BF16)

    q_all = q_ref[0, 0]
    q_lane = lax.broadcasted_iota(jnp.int32, q_all.shape, 1)
    qz_ref[0] = jnp.where(q_lane < HEAD_DIM, q_all, jnp.zeros_like(q_all))
    qz_ref[1] = jnp.where(q_lane >= HEAD_DIM, q_all, jnp.zeros_like(q_all))

    @pl.when(pl.program_id(1) == 0)
    def _():
        _fill_bias(bd_ref, (), tk, tq, N_META, rb_ref, h, far)
        _fill_bias(bc_ref, (), N_META, tq, N_META - tk, rb_ref, h, far)
        _fill_bias(sel_ref, (1,), tk, tq, N_META + tk, rb_ref, h, far)
        sel_ref[0] = jnp.zeros((tk, tq), F32)

    lam = (jnp.exp(jnp.sum(lq1_ref[...] * lk1_ref[...], keepdims=True))
           - jnp.exp(jnp.sum(lq2_ref[...] * lk2_ref[...], keepdims=True))
           + LAMBDA_INIT)

    def init_stats(st):
        m_ref[st] = jnp.full(m_ref.shape[1:], NEG_INF, F32)
        for a in range(2):
            acc_ref[st, a, pl.ds(0, V_DIM), :] = jnp.zeros((V_DIM, tq), F32)
            acc_ref[st, a, pl.ds(V_DIM, SUM_ROWS), :] = jnp.ones((SUM_ROWS, tq), F32)

    def produce(buf, q_off, k_off, kind, sel=None):
        q_rows = pl.ds(pl.multiple_of(q_off, tq), tq)
        k_off = pl.multiple_of(k_off, tk)
        k_t = kall[pl.ds(k_off, tk), :]
        for a in range(2):
            qa = qz_ref[a, q_rows, :]
            s = _dot_nt(k_t, qa)
            if kind == "near":
                s = s + sel_ref[1]
            elif kind == "diag":
                s = s + bd_ref[...]
            elif kind == "select":
                s = s + sel_ref[sel]
            sbuf[buf, a] = s
            mx = jnp.max(s, axis=0, keepdims=True)
            if kind == "diag":
                sc = _dot_nt(kall[pl.ds(k_off + tk, N_META), :], qa) + bc_ref[...]
                sc_ref[buf, a] = sc
                mx = jnp.maximum(mx, jnp.max(sc, axis=0, keepdims=True))
            mx_ref[buf, a] = mx

    def consume(buf, st, k_off, diag=False):
        k_off = pl.multiple_of(k_off, tk)
        v_t = vt_ref[:, pl.ds(k_off, tk)]
        for a in range(2):
            m_old = m_ref[st, a]
            m_new = jnp.maximum(m_old, mx_ref[buf, a])
            alpha = jnp.exp2(m_old - m_new)
            p = jnp.exp2(sbuf[buf, a] - m_new)
            pv = _dot(v_t, p.astype(BF16))
            if diag:
                pc = jnp.exp2(sc_ref[buf, a] - m_new)
                v_c = vt_ref[:, pl.ds(k_off + tk, V_DIM)][:, :N_META]
                pv = pv + _dot(v_c, pc.astype(BF16))
            acc_ref[st, a] = alpha * acc_ref[st, a] + pv
            m_ref[st, a] = m_new

    def finalize(st, q_off):
        q_rows = pl.ds(pl.multiple_of(q_off, tq), tq)
        heads = [acc_ref[st, a, pl.ds(0, V_DIM), :] / acc_ref[st, a, pl.ds(V_DIM, 1), :]
                 for a in range(2)]
        out_t = heads[0] - lam * heads[1]
        inv = lax.rsqrt(jnp.mean(out_t * out_t, axis=0, keepdims=True) + SUBLN_EPS)
        att = ((out_t * inv).T * sg_ref[...]) * (1.0 - LAMBDA_INIT)
        g = g_ref[0, q_rows, :]
        o_ref[q_rows, :] = (att * (g * jax.nn.sigmoid(g))).astype(o_ref.dtype)

    init_stats(0)
    init_stats(1)
    produce(0, 0, 0, "diag")

    def q_tile(i, cur, odd):
        oth = 1 - cur
        q_off = i * tq

        def far_pair(tt, c):
            t = 2 * tt
            produce(oth, q_off, (t + 1) * tk, "far")
            consume(cur, cur, t * tk)
            produce(cur, q_off, (t + 2) * tk, "far")
            consume(oth, cur, (t + 1) * tk)
            return c

        n_far = jnp.maximum(i - 2, 0)
        lax.fori_loop(0, n_far // 2, far_pair, 0)

        def near_diag():
            produce(oth, q_off, (i - 1) * tk, "near")
            consume(cur, cur, (i - 2) * tk)
            produce(cur, q_off, i * tk, "diag")
            consume(oth, cur, (i - 1) * tk)

        def last():
            finalize(oth, jnp.maximum(i - 1, 0) * tq)
            init_stats(oth)
            nxt = jnp.minimum(i + 1, nq - 1)
            produce(oth, nxt * tq, 0, "select", sel=jnp.where(nxt == 1, 1, 0))
            consume(cur, cur, i * tk, diag=True)

        if odd:
            @pl.when(i >= 3)
            def _():
                consume(cur, cur, (n_far - 1) * tk)
                produce(cur, q_off, n_far * tk, "far")
                near_diag()
                last()

            @pl.when(i == 1)
            def _():
                consume(cur, cur, 0)
                produce(cur, q_off, tk, "diag")
                last()
        else:
            @pl.when(i >= 2)
            def _():
                near_diag()
                last()

            @pl.when(i == 0)
            def _():
                last()

    def q_pair(ii, carry):
        q_tile(2 * ii, 0, False)
        q_tile(2 * ii + 1, 1, True)
        return carry

    lax.fori_loop(0, nq // 2, q_pair, 0)
    finalize((nq - 1) % 2, (nq - 1) * tq)


def _attention(qkv, qkv_meta, gates, rel_bias, lam_q1, lam_k1, lam_q2, lam_k2, subln_g):
    _, batch, _, seq, _ = qkv.shape
    tq, tk = ATTN_TQ, ATTN_TK
    n_pad = seq + V_DIM
    smem = pl.BlockSpec(memory_space=pltpu.SMEM)
    row64 = pl.BlockSpec((1, HEAD_DIM), lambda h, b: (0, 0))
    qkv_spec = lambda which: pl.BlockSpec((1, 1, 1, seq, V_DIM), lambda h, b: (which, b, h, 0, 0))
    meta_spec = lambda which: pl.BlockSpec((1, 1, N_META, V_DIM), lambda h, b: (which, h, 0, 0))

    def kernel(rb, lq1, lk1, lq2, lk2, q_ref, k_ref, v_ref, km_ref, vm_ref, *rest):
        _attn_kernel(rb, lq1, lk1, lq2, lk2, q_ref.at[0], k_ref.at[0], v_ref.at[0],
                     km_ref.at[0], vm_ref.at[0], *rest)

    return pl.pallas_call(
        kernel,
        grid=(N_HEADS, batch),
        in_specs=[smem, row64, row64, row64, row64,
                  qkv_spec(0), qkv_spec(1), qkv_spec(2), meta_spec(1), meta_spec(2),
                  pl.BlockSpec((1, seq, V_DIM), lambda h, b: (0, b, h)),
                  pl.BlockSpec((1, V_DIM), lambda h, b: (0, 0))],
        out_specs=pl.BlockSpec((seq, V_DIM), lambda h, b: (b, h)),
        out_shape=jax.ShapeDtypeStruct((batch * seq, D_ATTN), BF16),
        scratch_shapes=[pltpu.VMEM((seq + N_META, V_DIM), BF16),
                        pltpu.VMEM((n_pad, V_DIM), BF16),
                        pltpu.VMEM((V_DIM + SUM_ROWS, n_pad), BF16),
                        pltpu.VMEM((2, seq, V_DIM), BF16),
                        pltpu.VMEM((tk, tq), F32),
                        pltpu.VMEM((N_META, tq), F32),
                        pltpu.VMEM((2, tk, tq), F32),
                        pltpu.VMEM((2, 2, tk, tq), F32),
                        pltpu.VMEM((2, 2, N_META, tq), F32),
                        pltpu.VMEM((2, 2, 1, tq), F32),
                        pltpu.VMEM((2, 2, 1, tq), F32),
                        pltpu.VMEM((2, 2, V_DIM + SUM_ROWS, tq), F32)],
        compiler_params=pltpu.CompilerParams(
            dimension_semantics=("arbitrary", "arbitrary"),
            vmem_limit_bytes=VMEM_LIMIT_ATTN),
        name="diff_attention",
    )(rel_bias, lam_q1, lam_k1, lam_q2, lam_k2, qkv, qkv, qkv, qkv_meta, qkv_meta, gates, subln_g)


def _scan_block(a, b):
    n = a.shape[0]
    row = lax.broadcasted_iota(jnp.int32, a.shape, 0)
    s = 1
    while s < n:
        keep = row >= s
        a_sh = jnp.where(keep, pltpu.roll(a, s, 0), 1.0)
        b_sh = jnp.where(keep, pltpu.roll(b, s, 0), 0.0)
        b = b + a * b_sh
        a = a * a_sh
        s *= 2
    return a, b


def _tail_kernel(x_ref, att_ref, u_ref, um_ref, g_ref, cw_ref, cb_ref, wa_ref, wx_ref, ba_ref, bx_ref,
                 lam_ref, w_ref, fg_ref, o_ref, wb_ref, wg_ref, ubuf, h_ref, a_s, b_s, rec_s, rec_prev,
                 *, chunks_per_batch):
    s = pl.program_id(0)
    n = u_ref.shape[0]
    hist = 8
    seg = n // 8

    x = -lam_ref[...]
    softplus = jnp.maximum(x, 0.0) + jnp.log1p(jnp.exp(-jnp.abs(x)))

    @pl.when(s == 0)
    def _():
        wb_ref[...] = w_ref[...].astype(BF16)
        for blk in range(N_LRU_BLOCKS):
            wg_ref[blk, :, pl.ds(0, LRU_BLOCK)] = wa_ref[blk].astype(BF16)
            wg_ref[blk, :, pl.ds(LRU_BLOCK, LRU_BLOCK)] = wx_ref[blk].astype(BF16)
        rec_s[...] = jnp.zeros_like(rec_s)

    def gates(rows, cols, blk, first):
        cw = cw_ref[:, cols]
        uc = (cw[3:4] * ubuf[pl.ds(hist, rows), cols] + cw[2:3] * ubuf[pl.ds(hist - 1, rows), cols]
              + cw[1:2] * ubuf[pl.ds(hist - 2, rows), cols] + cw[0:1] * ubuf[pl.ds(hist - 3, rows), cols]
              + cb_ref[:, cols])
        pre = _dot(uc.astype(BF16), wg_ref[blk])
        r = jax.nn.sigmoid(pre[:, :LRU_BLOCK] + ba_ref[:, cols])
        gi = jax.nn.sigmoid(pre[:, LRU_BLOCK:] + bx_ref[:, cols])
        log_a = -LRU_C * r * softplus[:, cols]
        a = jnp.exp(log_a)
        mult = jnp.sqrt(jnp.tanh(-log_a) * (a * a + 1.0))
        if first:
            row = lax.broadcasted_iota(jnp.int32, mult.shape, 0)
            mult = jnp.where(row == 0, 1.0, mult)
        return a, mult * gi * uc

    @pl.when(s % chunks_per_batch == 0)
    def _():
        ubuf[pl.ds(0, hist), :] = jnp.zeros((hist, D_LRU), F32)
        ubuf[pl.ds(hist, N_META), :] = um_ref[...]
        for blk in range(N_LRU_BLOCKS):
            cols = slice(blk * LRU_BLOCK, (blk + 1) * LRU_BLOCK)
            a, b = gates(N_META, cols, blk, True)
            _, hm = _scan_block(a, b)
            h_ref[:, cols] = hm[N_META - 1:N_META, :]
        ubuf[pl.ds(0, hist), :] = ubuf[pl.ds(N_META, hist), :]

    rec_prev[...] = rec_s[...]
    ubuf[pl.ds(hist, n), :] = u_ref[...]
    sub = lax.broadcasted_iota(jnp.int32, (8, LRU_BLOCK), 0)
    n_out = D_MODEL // N_LRU_BLOCKS
    ssq = jnp.zeros((n, LRU_BLOCK), F32)
    for blk in range(N_LRU_BLOCKS):
        ocols = slice(blk * n_out, (blk + 1) * n_out)
        z = (x_ref[:, ocols] + _dot(att_ref[...], wb_ref[pl.ds(0, D_ATTN), ocols])
             + _dot(rec_prev[...], wb_ref[pl.ds(D_ATTN, D_LRU), ocols]))
        o_ref[:, ocols] = z
        for part in range(n_out // LRU_BLOCK):
            zp = z[:, part * LRU_BLOCK:(part + 1) * LRU_BLOCK]
            ssq = ssq + zp * zp

        cols = slice(blk * LRU_BLOCK, (blk + 1) * LRU_BLOCK)
        a, b = gates(n, cols, blk, False)
        for k in range(8):
            a_s[blk, pl.ds(SEG_PITCH * k, seg), :] = a[seg * k:seg * (k + 1)]
            b_s[blk, pl.ds(SEG_PITCH * k, seg), :] = b[seg * k:seg * (k + 1)]
        step = lambda j: pl.ds(j, 8, stride=SEG_PITCH)
        a_run = a_s[blk, step(0), :]
        h_run = b_s[blk, step(0), :]
        for j in range(1, seg):
            aj = a_s[blk, step(j), :]
            h_run = aj * h_run + b_s[blk, step(j), :]
            a_run = aj * a_run
        a_cum, h_cum = _scan_block(a_run, h_run)
        carry = h_ref[:, cols]
        seg_end = h_cum + a_cum * carry
        h_ref[:, cols] = seg_end[7:8, :]
        h_run = jnp.where(sub == 0, carry, pltpu.roll(seg_end, 1, 0))
        for j in range(seg):
            h_run = a_s[blk, step(j), :] * h_run + b_s[blk, step(j), :]
            b_s[blk, step(j), :] = h_run
        g = g_ref[:, cols]
        h = jnp.concatenate([b_s[blk, pl.ds(SEG_PITCH * k, seg), :] for k in range(8)], axis=0)
        rec_s[:, cols] = (h * (g * jax.nn.sigmoid(g))).astype(BF16)
    ubuf[pl.ds(0, hist), :] = ubuf[pl.ds(n, hist), :]
    inv = lax.rsqrt(jnp.sum(ssq, axis=1, keepdims=True) * (1.0 / D_MODEL) + NORM_EPS)
    o_ref[...] = (o_ref[...] * inv) * fg_ref[...]


def _tail(x2d, att, gates, gates_meta, conv_w, conv_b, w_a, b_a, w_x, b_x, lru_lambda, w_out,
          final_g, *, seq):
    rows = x2d.shape[0]
    tm = OUT_ROWS
    n_tiles = rows // tm
    prev = lambda s: (jnp.maximum(s - 1, 0), 0)
    cur = lambda which: (lambda s: (which, jnp.minimum(s, n_tiles - 1), 0))
    row = lambda n: pl.BlockSpec((n, D_LRU), lambda s: (0, 0))
    wspec = pl.BlockSpec((N_LRU_BLOCKS, LRU_BLOCK, LRU_BLOCK), lambda s: (0, 0, 0))

    def kernel(x_ref, att_ref, u_ref, um_ref, g_ref, *rest):
        _tail_kernel(x_ref, att_ref, u_ref.at[0], um_ref.at[0], g_ref.at[0], *rest,
                     chunks_per_batch=seq // tm)

    return pl.pallas_call(
        kernel,
        grid=(n_tiles + 1,),
        in_specs=[pl.BlockSpec((tm, D_MODEL), prev),
                  pl.BlockSpec((tm, D_ATTN), prev),
                  pl.BlockSpec((1, tm, D_LRU), cur(1)),
                  pl.BlockSpec((1, N_META, D_LRU), lambda s: (1, 0, 0)),
                  pl.BlockSpec((1, tm, D_LRU), cur(2)),
                  row(CONV_WIDTH), row(1), wspec, wspec, row(1), row(1), row(1),
                  pl.BlockSpec((D_ATTN + D_LRU, D_MODEL), lambda s: (0, 0),
                               pipeline_mode=pl.Buffered(1)),
                  pl.BlockSpec((1, D_MODEL), lambda s: (0, 0))],
        out_specs=pl.BlockSpec((tm, D_MODEL), prev),
        out_shape=jax.ShapeDtypeStruct((rows, D_MODEL), F32),
        scratch_shapes=[pltpu.VMEM((D_ATTN + D_LRU, D_MODEL), BF16),
                        pltpu.VMEM((N_LRU_BLOCKS, LRU_BLOCK, 2 * LRU_BLOCK), BF16),
                        pltpu.VMEM((8 + tm, D_LRU), F32),
                        pltpu.VMEM((1, D_LRU), F32),
                        pltpu.VMEM((N_LRU_BLOCKS, 8 * SEG_PITCH, LRU_BLOCK), F32),
                        pltpu.VMEM((N_LRU_BLOCKS, 8 * SEG_PITCH, LRU_BLOCK), F32),
                        pltpu.VMEM((tm, D_LRU), BF16),
                        pltpu.VMEM((tm, D_LRU), BF16)],
        compiler_params=pltpu.CompilerParams(
            dimension_semantics=("arbitrary",),
            vmem_limit_bytes=VMEM_LIMIT_OUT),
        name="rglru_out_proj",
    )(x2d, att, gates, gates_meta, gates, conv_w, conv_b, w_a, w_x, b_a, b_x, lru_lambda, w_out,
      final_g)


def kernel(x, meta_tokens, rel_bias, norm_g, w_in, conv_w, conv_b, w_a, b_a, w_x, b_x, lru_lambda,
           lam_q1, lam_k1, lam_q2, lam_k2, subln_g, w_out, final_g):
    batch, seq, _ = x.shape
    x2d = x.reshape(batch * seq, D_MODEL)
    qkv, qkv_meta = _project(x2d, meta_tokens, norm_g, w_in[0], col_tile0=0, head_major=True,
                             batch=batch, seq=seq)
    gates, gates_meta = _project(x2d, meta_tokens, norm_g, w_in[0], col_tile0=3, head_major=False,
                                 batch=batch, seq=seq)
    att = _attention(qkv, qkv_meta, gates, rel_bias, lam_q1, lam_k1, lam_q2, lam_k2, subln_g)
    out = _tail(x2d, att, gates, gates_meta, conv_w[0], conv_b, w_a[0], b_a, w_x[0], b_x, lru_lambda,
                w_out[0], final_g.reshape(1, D_MODEL), seq=seq)
    return out.reshape(batch, seq, D_MODEL)
```

```python
import functools
import math

import numpy as np
import jax
import jax.numpy as jnp
from jax import lax
from jax.experimental import pallas as pl
from jax.experimental.pallas import tpu as pltpu

D_MODEL = 2048
N_META = 16
D_ATTN = 1024
D_LRU = 1024
N_HEADS = 8
HEAD_DIM = 64
V_DIM = 128
N_LRU_BLOCKS = 8
LRU_BLOCK = 128
CONV_WIDTH = 4
LRU_C = 8.0
N_BUCKETS = 32
MAX_DISTANCE = 128
NORM_EPS = 1e-6
SUBLN_EPS = 1e-5
NEG_INF = -1e30
LAMBDA_INIT = 0.8 - 0.6 * math.exp(-0.3 * 0)
LOG2E = math.log2(math.e)

BF16 = jnp.bfloat16
F32 = jnp.float32

VMEM_LIMIT_PROJ = 56 * 1024 * 1024
VMEM_LIMIT_ATTN = 48 * 1024 * 1024
VMEM_LIMIT_OUT = 56 * 1024 * 1024

PROJ_ROWS = 1024
ATTN_TQ = 512
ATTN_TK = 512
SUM_ROWS = 16
OUT_ROWS = 256
SEG_PITCH = 40


def _bucket_thresholds():
    max_exact = N_BUCKETS // 2
    d = np.arange(0, 4 * MAX_DISTANCE, dtype=np.int64)
    val = (np.log(np.maximum(d, 1).astype(np.float64) / max_exact)
           / math.log(MAX_DISTANCE / max_exact) * (N_BUCKETS - max_exact))
    large = np.minimum(max_exact + np.floor(val + 1e-9).astype(np.int64), N_BUCKETS - 1)
    bucket = np.where(d < max_exact, d, large)
    frac = np.abs(val - np.round(val))
    interior = (d > max_exact) & (d < MAX_DISTANCE)
    assert frac[interior].min() > 1e-3
    assert (np.diff(bucket) >= 0).all() and bucket[MAX_DISTANCE] == N_BUCKETS - 1
    return tuple(int(np.argmax(bucket >= j)) for j in range(1, N_BUCKETS))


BUCKET_THRESHOLDS = _bucket_thresholds()


def _rms(x, g, eps):
    y = x * lax.rsqrt(jnp.mean(x * x, axis=-1, keepdims=True) + eps)
    return y * g


def _dot(a, b):
    return jnp.dot(a, b, preferred_element_type=F32)


def _dot_nt(a, b):
    return lax.dot_general(a, b, (((1,), (1,)), ((), ())), preferred_element_type=F32)


def _proj_kernel(x_ref, meta_ref, g_ref, w_ref, o_ref, om_ref, wb_ref, *, mode):
    j = pl.program_id(0)
    i = pl.program_id(1)
    out_scale = jnp.where(j == 0, HEAD_DIM ** -0.5 * LOG2E, 1.0).astype(F32) if mode == "heads" else 1.0
    head = lambda y, h: y[:, h * V_DIM:(h + 1) * V_DIM]

    @pl.when(i == 0)
    def _():
        wb_ref[...] = w_ref[...].astype(BF16)
        hm = _rms(meta_ref[...], g_ref[...], NORM_EPS).astype(BF16)
        ym = _dot(hm, wb_ref[...]) * out_scale
        if mode == "flat":
            om_ref[0] = ym
        else:
            for h in range(N_HEADS):
                om_ref[0, h] = head(ym, h).astype(BF16)

    hx = _rms(x_ref[...], g_ref[...], NORM_EPS).astype(BF16)
    y = _dot(hx, wb_ref[...]) * out_scale
    if mode == "flat":
        o_ref[0] = y
    elif mode == "heads":
        for h in range(N_HEADS):
            o_ref[0, 0, h] = head(y, h).astype(BF16)
    else:
        rows = y.shape[0]
        for h in range(N_HEADS):
            o_ref[0, h, pl.ds(0, V_DIM), :] = head(y, h).T.astype(BF16)
            o_ref[0, h, pl.ds(V_DIM, SUM_ROWS), :] = jnp.ones((SUM_ROWS, rows), BF16)


def _project(x2d, meta, norm_g, w_in, *, col_tile0, n_slabs, mode, batch, seq):
    rows = x2d.shape[0]
    tm = PROJ_ROWS
    n_i = rows // tm
    n_ib = seq // tm
    if mode == "heads":
        out_shape = (jax.ShapeDtypeStruct((n_slabs, batch, N_HEADS, seq, V_DIM), BF16),
                     jax.ShapeDtypeStruct((n_slabs, N_HEADS, N_META, V_DIM), BF16))
        out_specs = (pl.BlockSpec((1, 1, N_HEADS, tm, V_DIM),
                                  lambda j, i: (j, i // n_ib, 0, i % n_ib, 0)),
                     pl.BlockSpec((1, N_HEADS, N_META, V_DIM), lambda j, i: (j, 0, 0, 0)))
    elif mode == "vt":
        out_shape = (jax.ShapeDtypeStruct((batch, N_HEADS, V_DIM + SUM_ROWS, seq), BF16),
                     jax.ShapeDtypeStruct((n_slabs, N_HEADS, N_META, V_DIM), BF16))
        out_specs = (pl.BlockSpec((1, N_HEADS, V_DIM + SUM_ROWS, tm),
                                  lambda j, i: (i // n_ib, 0, 0, i % n_ib)),
                     pl.BlockSpec((1, N_HEADS, N_META, V_DIM), lambda j, i: (j, 0, 0, 0)))
    else:
        out_shape = (jax.ShapeDtypeStruct((n_slabs, rows, D_ATTN), F32),
                     jax.ShapeDtypeStruct((n_slabs, N_META, D_ATTN), F32))
        out_specs = (pl.BlockSpec((1, tm, D_ATTN), lambda j, i: (j, i, 0)),
                     pl.BlockSpec((1, N_META, D_ATTN), lambda j, i: (j, 0, 0)))
    return pl.pallas_call(
        functools.partial(_proj_kernel, mode=mode),
        grid=(n_slabs, n_i),
        in_specs=[pl.BlockSpec((tm, D_MODEL), lambda j, i: (i, 0)),
                  pl.BlockSpec((N_META, D_MODEL), lambda j, i: (0, 0)),
                  pl.BlockSpec((1, D_MODEL), lambda j, i: (0, 0)),
                  pl.BlockSpec((D_MODEL, D_ATTN), lambda j, i: (0, j + col_tile0))],
        out_specs=out_specs,
        out_shape=out_shape,
        scratch_shapes=[pltpu.VMEM((D_MODEL, D_ATTN), BF16)],
        compiler_params=pltpu.CompilerParams(
            dimension_semantics=("arbitrary", "arbitrary"),
            vmem_limit_bytes=VMEM_LIMIT_PROJ),
        name={"heads": "proj_qk", "vt": "proj_v", "flat": "proj_gates"}[mode],
    )(x2d, meta, norm_g, w_in)


def _toeplitz_bias(dist, rb_ref, h, far):
    b = jnp.full(dist.shape, (rb_ref[0, h] - far) * LOG2E, F32)
    for j, thr in enumerate(BUCKET_THRESHOLDS, start=1):
        b = jnp.where(dist >= thr, (rb_ref[j, h] - far) * LOG2E, b)
    return b


def _fill_bias(ref, lead, n_rows, n_cols, d0, rb_ref, h, far):
    sub = lax.broadcasted_iota(jnp.int32, (8, V_DIM), 0)
    lane = lax.broadcasted_iota(jnp.int32, (8, V_DIM), 1)
    zeros = jnp.zeros((8, V_DIM), F32)
    masked = jnp.full((8, V_DIM), NEG_INF, F32)
    cache = {}
    for a8 in range(n_rows // 8):
        for b in range(n_cols // V_DIM):
            off = d0 + V_DIM * b - 8 * a8
            if off + V_DIM - 1 < 0:
                tile = masked
            elif off - 7 >= MAX_DISTANCE:
                tile = zeros
            else:
                if off not in cache:
                    d = off + lane - sub
                    cache[off] = jnp.where(d >= 0, _toeplitz_bias(d, rb_ref, h, far), NEG_INF)
                tile = cache[off]
            ref[(*lead, pl.ds(8 * a8, 8), pl.ds(V_DIM * b, V_DIM))] = tile


def _attn_kernel(rb_ref, lq1_ref, lk1_ref, lq2_ref, lk2_ref, q_ref, k_ref, vt_ref, km_ref, vm_ref,
                 g_ref, sg_ref, o_ref, vmt_ref, bd_ref, bm_ref, sel_ref, sbuf, sc_ref, mx_ref,
                 m_ref, acc_ref):
    h = pl.program_id(0)
    tq, tk = ATTN_TQ, ATTN_TK
    seq = k_ref.shape[2]
    nq = seq // tq

    far = rb_ref[N_BUCKETS - 1, h]

    @pl.when(pl.program_id(1) == 0)
    def _():
        _fill_bias(bd_ref, (), tk, tq, 0, rb_ref, h, far)
        _fill_bias(sel_ref, (1,), tk, tq, tk, rb_ref, h, far)
        sel_ref[0] = jnp.zeros((tk, tq), F32)
        _fill_bias(bm_ref, (1,), N_META, tq, N_META, rb_ref, h, far)
        bm_ref[0] = jnp.zeros((N_META, tq), F32)
        padded = jnp.concatenate(
            [vm_ref[0].astype(F32), jnp.zeros((V_DIM - N_META, V_DIM), F32)], axis=0)
        vmt_ref[pl.ds(0, V_DIM), :] = padded.T.astype(BF16)
        vmt_ref[pl.ds(V_DIM, SUM_ROWS), :] = jnp.ones((SUM_ROWS, V_DIM), BF16)

    lane = lax.broadcasted_iota(jnp.int32, (tq, V_DIM), 1)

    lam = (jnp.exp(jnp.sum(lq1_ref[...] * lk1_ref[...], keepdims=True))
           - jnp.exp(jnp.sum(lq2_ref[...] * lk2_ref[...], keepdims=True))
           + LAMBDA_INIT)

    def init_stats(st):
        m_ref[st] = jnp.full(m_ref.shape[1:], NEG_INF, F32)
        for a in range(2):
            acc_ref[st, a, pl.ds(0, V_DIM), :] = jnp.zeros((V_DIM, tq), F32)
            acc_ref[st, a, pl.ds(V_DIM, SUM_ROWS), :] = jnp.ones((SUM_ROWS, tq), F32)

    def produce(buf, q_off, k_off, kind, sel=None):
        q = q_ref[0, 0, pl.ds(pl.multiple_of(q_off, tq), tq), :]
        zero = jnp.zeros_like(q)
        k_off = pl.multiple_of(k_off, tk)
        k_t = k_ref[0, 0, pl.ds(k_off, tk), :]
        for a in range(2):
            qa = jnp.where((lane < HEAD_DIM) if a == 0 else (lane >= HEAD_DIM), q, zero)
            s = _dot_nt(k_t, qa)
            if kind == "near":
                s = s + sel_ref[1]
            elif kind == "diag":
                s = s + bd_ref[...]
            elif kind == "select":
                s = s + sel_ref[sel]
            sbuf[buf, a] = s
            mx = jnp.max(s, axis=0, keepdims=True)
            if kind == "diag":
                sc = _dot_nt(km_ref[0], qa) + bm_ref[jnp.where(k_off == 0, 1, 0)]
                sc_ref[buf, a] = sc
                mx = jnp.maximum(mx, jnp.max(sc, axis=0, keepdims=True))
            mx_ref[buf, a] = mx

    def consume(buf, st, k_off, diag=False):
        k_off = pl.multiple_of(k_off, tk)
        v_t = vt_ref[0, 0, :, pl.ds(k_off, tk)]
        for a in range(2):
            m_old = m_ref[st, a]
            m_new = jnp.maximum(m_old, mx_ref[buf, a])
            alpha = jnp.exp2(m_old - m_new)
            p = jnp.exp2(sbuf[buf, a] - m_new)
            pv = _dot(v_t, p.astype(BF16))
            if diag:
                pc = jnp.exp2(sc_ref[buf, a] - m_new)
                v_c = vmt_ref[...][:, :N_META]
                pv = pv + _dot(v_c, pc.astype(BF16))
            acc_ref[st, a] = alpha * acc_ref[st, a] + pv
            m_ref[st, a] = m_new

    def finalize(st, q_off):
        q_rows = pl.ds(pl.multiple_of(q_off, tq), tq)
        heads = [acc_ref[st, a, pl.ds(0, V_DIM), :] / acc_ref[st, a, pl.ds(V_DIM, 1), :]
                 for a in range(2)]
        out_t = heads[0] - lam * heads[1]
        inv = lax.rsqrt(jnp.mean(out_t * out_t, axis=0, keepdims=True) + SUBLN_EPS)
        att = ((out_t * inv).T * sg_ref[...]) * (1.0 - LAMBDA_INIT)
        g = g_ref[0, q_rows, :]
        o_ref[q_rows, :] = (att * (g * jax.nn.sigmoid(g))).astype(o_ref.dtype)

    init_stats(0)
    init_stats(1)
    produce(0, 0, 0, "diag")

    def q_tile(i, cur, odd):
        oth = 1 - cur
        q_off = i * tq

        def far_pair(tt, c):
            t = 2 * tt
            produce(oth, q_off, (t + 1) * tk, "far")
            consume(cur, cur, t * tk)
            produce(cur, q_off, (t + 2) * tk, "far")
            consume(oth, cur, (t + 1) * tk)
            return c

        n_far = jnp.maximum(i - 2, 0)
        lax.fori_loop(0, n_far // 2, far_pair, 0)

        def near_diag():
            produce(oth, q_off, (i - 1) * tk, "near")
            consume(cur, cur, (i - 2) * tk)
            produce(cur, q_off, i * tk, "diag")
            consume(oth, cur, (i - 1) * tk)

        def last():
            finalize(oth, jnp.maximum(i - 1, 0) * tq)
            init_stats(oth)
            nxt = jnp.minimum(i + 1, nq - 1)
            produce(oth, nxt * tq, 0, "select", sel=jnp.where(nxt == 1, 1, 0))
            consume(cur, cur, i * tk, diag=True)

        if odd:
            @pl.when(i >= 3)
            def _():
                consume(cur, cur, (n_far - 1) * tk)
                produce(cur, q_off, n_far * tk, "far")
                near_diag()
                last()

            @pl.when(i == 1)
            def _():
                consume(cur, cur, 0)
                produce(cur, q_off, tk, "diag")
                last()
        else:
            @pl.when(i >= 2)
            def _():
                near_diag()
                last()

            @pl.when(i == 0)
            def _():
                last()

    def q_pair(ii, carry):
        q_tile(2 * ii, 0, False)
        q_tile(2 * ii + 1, 1, True)
        return carry

    lax.fori_loop(0, nq // 2, q_pair, 0)
    finalize((nq - 1) % 2, (nq - 1) * tq)


def _attention(qk, qk_meta, vt, v_meta, gates, rel_bias, lam_q1, lam_k1, lam_q2, lam_k2, subln_g):
    _, batch, _, seq, _ = qk.shape
    tq, tk = ATTN_TQ, ATTN_TK
    smem = pl.BlockSpec(memory_space=pltpu.SMEM)
    row64 = pl.BlockSpec((1, HEAD_DIM), lambda h, b: (0, 0))
    qk_spec = lambda which: pl.BlockSpec((1, 1, 1, seq, V_DIM), lambda h, b: (which, b, h, 0, 0))
    meta_spec = lambda which: pl.BlockSpec((1, 1, N_META, V_DIM), lambda h, b: (which, h, 0, 0))

    def kernel(rb, lq1, lk1, lq2, lk2, q_ref, k_ref, vt_ref, km_ref, vm_ref, *rest):
        _attn_kernel(rb, lq1, lk1, lq2, lk2, q_ref.at[0], k_ref.at[0], vt_ref,
                     km_ref.at[0], vm_ref.at[0], *rest)

    return pl.pallas_call(
        kernel,
        grid=(N_HEADS, batch),
        in_specs=[smem, row64, row64, row64, row64,
                  qk_spec(0), qk_spec(1),
                  pl.BlockSpec((1, 1, V_DIM + SUM_ROWS, seq), lambda h, b: (b, h, 0, 0)),
                  meta_spec(1), meta_spec(0),
                  pl.BlockSpec((1, seq, V_DIM), lambda h, b: (0, b, h)),
                  pl.BlockSpec((1, V_DIM), lambda h, b: (0, 0))],
        out_specs=pl.BlockSpec((seq, V_DIM), lambda h, b: (b, h)),
        out_shape=jax.ShapeDtypeStruct((batch * seq, D_ATTN), BF16),
        scratch_shapes=[pltpu.VMEM((V_DIM + SUM_ROWS, V_DIM), BF16),
                        pltpu.VMEM((tk, tq), F32),
                        pltpu.VMEM((2, N_META, tq), F32),
                        pltpu.VMEM((2, tk, tq), F32),
                        pltpu.VMEM((2, 2, tk, tq), F32),
                        pltpu.VMEM((2, 2, N_META, tq), F32),
                        pltpu.VMEM((2, 2, 1, tq), F32),
                        pltpu.VMEM((2, 2, 1, tq), F32),
                        pltpu.VMEM((2, 2, V_DIM + SUM_ROWS, tq), F32)],
        compiler_params=pltpu.CompilerParams(
            dimension_semantics=("arbitrary", "arbitrary"),
            vmem_limit_bytes=VMEM_LIMIT_ATTN),
        name="diff_attention",
    )(rel_bias, lam_q1, lam_k1, lam_q2, lam_k2, qk, qk, vt, qk_meta, v_meta, gates, subln_g)


def _scan_block(a, b):
    n = a.shape[0]
    row = lax.broadcasted_iota(jnp.int32, a.shape, 0)
    s = 1
    while s < n:
        keep = row >= s
        a_sh = jnp.where(keep, pltpu.roll(a, s, 0), 1.0)
        b_sh = jnp.where(keep, pltpu.roll(b, s, 0), 0.0)
        b = b + a * b_sh
        a = a * a_sh
        s *= 2
    return a, b


def _tail_kernel(x_ref, att_ref, u_ref, um_ref, g_ref, cw_ref, cb_ref, wa_ref, wx_ref, ba_ref, bx_ref,
                 lam_ref, w_ref, fg_ref, o_ref, wb_ref, wg_ref, ubuf, h_ref, a_s, b_s, rec_s, rec_prev,
                 *, chunks_per_batch):
    s = pl.program_id(0)
    n = u_ref.shape[0]
    hist = 8
    seg = n // 8

    x = -lam_ref[...]
    softplus = jnp.maximum(x, 0.0) + jnp.log1p(jnp.exp(-jnp.abs(x)))

    @pl.when(s == 0)
    def _():
        wb_ref[...] = w_ref[...].astype(BF16)
        for blk in range(N_LRU_BLOCKS):
            wg_ref[blk, :, pl.ds(0, LRU_BLOCK)] = wa_ref[blk].astype(BF16)
            wg_ref[blk, :, pl.ds(LRU_BLOCK, LRU_BLOCK)] = wx_ref[blk].astype(BF16)
        rec_s[...] = jnp.zeros_like(rec_s)

    def gates(rows, cols, blk, first):
        cw = cw_ref[:, cols]
        uc = (cw[3:4] * ubuf[pl.ds(hist, rows), cols] + cw[2:3] * ubuf[pl.ds(hist - 1, rows), cols]
              + cw[1:2] * ubuf[pl.ds(hist - 2, rows), cols] + cw[0:1] * ubuf[pl.ds(hist - 3, rows), cols]
              + cb_ref[:, cols])
        pre = _dot(uc.astype(BF16), wg_ref[blk])
        r = jax.nn.sigmoid(pre[:, :LRU_BLOCK] + ba_ref[:, cols])
        gi = jax.nn.sigmoid(pre[:, LRU_BLOCK:] + bx_ref[:, cols])
        log_a = -LRU_C * r * softplus[:, cols]
        a = jnp.exp(log_a)
        mult = jnp.sqrt(jnp.tanh(-log_a) * (a * a + 1.0))
        if first:
            row = lax.broadcasted_iota(jnp.int32, mult.shape, 0)
            mult = jnp.where(row == 0, 1.0, mult)
        return a, mult * gi * uc

    @pl.when(s % chunks_per_batch == 0)
    def _():
        ubuf[pl.ds(0, hist), :] = jnp.zeros((hist, D_LRU), F32)
        ubuf[pl.ds(hist, N_META), :] = um_ref[...]
        for blk in range(N_LRU_BLOCKS):
            cols = slice(blk * LRU_BLOCK, (blk + 1) * LRU_BLOCK)
            a, b = gates(N_META, cols, blk, True)
            _, hm = _scan_block(a, b)
            h_ref[:, cols] = hm[N_META - 1:N_META, :]
        ubuf[pl.ds(0, hist), :] = ubuf[pl.ds(N_META, hist), :]

    rec_prev[...] = rec_s[...]
    ubuf[pl.ds(hist, n), :] = u_ref[...]
    sub = lax.broadcasted_iota(jnp.int32, (8, LRU_BLOCK), 0)
    n_out = D_MODEL // N_LRU_BLOCKS
    ssq = jnp.zeros((n, LRU_BLOCK), F32)
    for blk in range(N_LRU_BLOCKS):
        ocols = slice(blk * n_out, (blk + 1) * n_out)
        z = (x_ref[:, ocols] + _dot(att_ref[...], wb_ref[pl.ds(0, D_ATTN), ocols])
             + _dot(rec_prev[...], wb_ref[pl.ds(D_ATTN, D_LRU), ocols]))
        o_ref[:, ocols] = z
        for part in range(n_out // LRU_BLOCK):
            zp = z[:, part * LRU_BLOCK:(part + 1) * LRU_BLOCK]
            ssq = ssq + zp * zp

        cols = slice(blk * LRU_BLOCK, (blk + 1) * LRU_BLOCK)
        a, b = gates(n, cols, blk, False)
        for k in range(8):
            a_s[blk, pl.ds(SEG_PITCH * k, seg), :] = a[seg * k:seg * (k + 1)]
            b_s[blk, pl.ds(SEG_PITCH * k, seg), :] = b[seg * k:seg * (k + 1)]
        step = lambda j: pl.ds(j, 8, stride=SEG_PITCH)
        a_run = a_s[blk, step(0), :]
        h_run = b_s[blk, step(0), :]
        for j in range(1, seg):
            aj = a_s[blk, step(j), :]
            h_run = aj * h_run + b_s[blk, step(j), :]
            a_run = aj * a_run
        a_cum, h_cum = _scan_block(a_run, h_run)
        carry = h_ref[:, cols]
        seg_end = h_cum + a_cum * carry
        h_ref[:, cols] = seg_end[7:8, :]
        h_run = jnp.where(sub == 0, carry, pltpu.roll(seg_end, 1, 0))
        for j in range(seg):
            h_run = a_s[blk, step(j), :] * h_run + b_s[blk, step(j), :]
            b_s[blk, step(j), :] = h_run
        g = g_ref[:, cols]
        h = jnp.concatenate([b_s[blk, pl.ds(SEG_PITCH * k, seg), :] for k in range(8)], axis=0)
        rec_s[:, cols] = (h * (g * jax.nn.sigmoid(g))).astype(BF16)
    ubuf[pl.ds(0, hist), :] = ubuf[pl.ds(n, hist), :]
    inv = lax.rsqrt(jnp.sum(ssq, axis=1, keepdims=True) * (1.0 / D_MODEL) + NORM_EPS)
    o_ref[...] = (o_ref[...] * inv) * fg_ref[...]


def _tail(x2d, att, gates, gates_meta, conv_w, conv_b, w_a, b_a, w_x, b_x, lru_lambda, w_out,
          final_g, *, seq):
    rows = x2d.shape[0]
    tm = OUT_ROWS
    n_tiles = rows // tm
    prev = lambda s: (jnp.maximum(s - 1, 0), 0)
    cur = lambda which: (lambda s: (which, jnp.minimum(s, n_tiles - 1), 0))
    row = lambda n: pl.BlockSpec((n, D_LRU), lambda s: (0, 0))
    wspec = pl.BlockSpec((N_LRU_BLOCKS, LRU_BLOCK, LRU_BLOCK), lambda s: (0, 0, 0))

    def kernel(x_ref, att_ref, u_ref, um_ref, g_ref, *rest):
        _tail_kernel(x_ref, att_ref, u_ref.at[0], um_ref.at[0], g_ref.at[0], *rest,
                     chunks_per_batch=seq // tm)

    return pl.pallas_call(
        kernel,
        grid=(n_tiles + 1,),
        in_specs=[pl.BlockSpec((tm, D_MODEL), prev),
                  pl.BlockSpec((tm, D_ATTN), prev),
                  pl.BlockSpec((1, tm, D_LRU), cur(1)),
                  pl.BlockSpec((1, N_META, D_LRU), lambda s: (1, 0, 0)),
                  pl.BlockSpec((1, tm, D_LRU), cur(2)),
                  row(CONV_WIDTH), row(1), wspec, wspec, row(1), row(1), row(1),
                  pl.BlockSpec((D_ATTN + D_LRU, D_MODEL), lambda s: (0, 0),
                               pipeline_mode=pl.Buffered(1)),
                  pl.BlockSpec((1, D_MODEL), lambda s: (0, 0))],
        out_specs=pl.BlockSpec((tm, D_MODEL), prev),
        out_shape=jax.ShapeDtypeStruct((rows, D_MODEL), F32),
        scratch_shapes=[pltpu.VMEM((D_ATTN + D_LRU, D_MODEL), BF16),
                        pltpu.VMEM((N_LRU_BLOCKS, LRU_BLOCK, 2 * LRU_BLOCK), BF16),
                        pltpu.VMEM((8 + tm, D_LRU), F32),
                        pltpu.VMEM((1, D_LRU), F32),
                        pltpu.VMEM((N_LRU_BLOCKS, 8 * SEG_PITCH, LRU_BLOCK), F32),
                        pltpu.VMEM((N_LRU_BLOCKS, 8 * SEG_PITCH, LRU_BLOCK), F32),
                        pltpu.VMEM((tm, D_LRU), BF16),
                        pltpu.VMEM((tm, D_LRU), BF16)],
        compiler_params=pltpu.CompilerParams(
            dimension_semantics=("arbitrary",),
            vmem_limit_bytes=VMEM_LIMIT_OUT),
        name="rglru_out_proj",
    )(x2d, att, gates, gates_meta, gates, conv_w, conv_b, w_a, w_x, b_a, b_x, lru_lambda, w_out,
      final_g)


def kernel(x, meta_tokens, rel_bias, norm_g, w_in, conv_w, conv_b, w_a, b_a, w_x, b_x, lru_lambda,
           lam_q1, lam_k1, lam_q2, lam_k2, subln_g, w_out, final_g):
    batch, seq, _ = x.shape
    x2d = x.reshape(batch * seq, D_MODEL)
    project = functools.partial(_project, x2d, meta_tokens, norm_g, w_in[0], batch=batch, seq=seq)
    qk, qk_meta = project(col_tile0=0, n_slabs=2, mode="heads")
    vt, v_meta = project(col_tile0=2, n_slabs=1, mode="vt")
    gates, gates_meta = project(col_tile0=3, n_slabs=3, mode="flat")
    att = _attention(qk, qk_meta, vt, v_meta, gates, rel_bias, lam_q1, lam_k1, lam_q2, lam_k2,
                     subln_g)
    out = _tail(x2d, att, gates, gates_meta, conv_w[0], conv_b, w_a[0], b_a, w_x[0], b_x, lru_lambda,
                w_out[0], final_g.reshape(1, D_MODEL), seq=seq)
    return out.reshape(batch, seq, D_MODEL)
```

```python
import functools
import math

import numpy as np
import jax
import jax.numpy as jnp
from jax import lax
from jax.experimental import pallas as pl
from jax.experimental.pallas import tpu as pltpu

D_MODEL = 2048
N_META = 16
D_ATTN = 1024
D_LRU = 1024
N_HEADS = 8
HEAD_DIM = 64
V_DIM = 128
N_LRU_BLOCKS = 8
LRU_BLOCK = 128
CONV_WIDTH = 4
LRU_C = 8.0
N_BUCKETS = 32
MAX_DISTANCE = 128
NORM_EPS = 1e-6
SUBLN_EPS = 1e-5
NEG_INF = -1e30
LAMBDA_INIT = 0.8 - 0.6 * math.exp(-0.3 * 0)
LOG2E = math.log2(math.e)

BF16 = jnp.bfloat16
F32 = jnp.float32

VMEM_LIMIT_PROJ = 56 * 1024 * 1024
VMEM_LIMIT_ATTN = 48 * 1024 * 1024
VMEM_LIMIT_OUT = 56 * 1024 * 1024

PROJ_ROWS = 1024
ATTN_TQ = 512
ATTN_TK = 512
SUM_ROWS = 16
OUT_ROWS = 256
SEG_PITCH = 40


def _bucket_thresholds():
    max_exact = N_BUCKETS // 2
    d = np.arange(0, 4 * MAX_DISTANCE, dtype=np.int64)
    val = (np.log(np.maximum(d, 1).astype(np.float64) / max_exact)
           / math.log(MAX_DISTANCE / max_exact) * (N_BUCKETS - max_exact))
    large = np.minimum(max_exact + np.floor(val + 1e-9).astype(np.int64), N_BUCKETS - 1)
    bucket = np.where(d < max_exact, d, large)
    frac = np.abs(val - np.round(val))
    interior = (d > max_exact) & (d < MAX_DISTANCE)
    assert frac[interior].min() > 1e-3
    assert (np.diff(bucket) >= 0).all() and bucket[MAX_DISTANCE] == N_BUCKETS - 1
    return tuple(int(np.argmax(bucket >= j)) for j in range(1, N_BUCKETS))


BUCKET_THRESHOLDS = _bucket_thresholds()


def _rms(x, g, eps):
    y = x * lax.rsqrt(jnp.mean(x * x, axis=-1, keepdims=True) + eps)
    return y * g


def _dot(a, b):
    return jnp.dot(a, b, preferred_element_type=F32)


def _dot_nt(a, b):
    return lax.dot_general(a, b, (((1,), (1,)), ((), ())), preferred_element_type=F32)


def _proj_kernel(x_ref, meta_ref, g_ref, w_ref, o_ref, om_ref, wb_ref, *, mode):
    j = pl.program_id(0)
    i = pl.program_id(1)
    out_scale = jnp.where(j == 0, HEAD_DIM ** -0.5 * LOG2E, 1.0).astype(F32) if mode == "heads" else 1.0
    head = lambda y, h: y[:, h * V_DIM:(h + 1) * V_DIM]

    @pl.when(i == 0)
    def _():
        wb_ref[...] = w_ref[...].astype(BF16)
        hm = _rms(meta_ref[...], g_ref[...], NORM_EPS).astype(BF16)
        ym = _dot(hm, wb_ref[...]) * out_scale
        if mode == "flat":
            om_ref[0] = ym
        else:
            for h in range(N_HEADS):
                om_ref[0, h] = head(ym, h).astype(BF16)

    hx = _rms(x_ref[...], g_ref[...], NORM_EPS).astype(BF16)
    y = _dot(hx, wb_ref[...]) * out_scale
    if mode == "flat":
        o_ref[0] = y
    elif mode == "heads":
        for h in range(N_HEADS):
            o_ref[0, 0, h] = head(y, h).astype(BF16)
    else:
        rows = y.shape[0]
        for h in range(N_HEADS):
            o_ref[0, h, pl.ds(0, V_DIM), :] = head(y, h).T.astype(BF16)
            o_ref[0, h, pl.ds(V_DIM, SUM_ROWS), :] = jnp.ones((SUM_ROWS, rows), BF16)


def _project(x2d, meta, norm_g, w_in, *, col_tile0, n_slabs, mode, batch, seq):
    rows = x2d.shape[0]
    tm = PROJ_ROWS
    n_i = rows // tm
    n_ib = seq // tm
    if mode == "heads":
        out_shape = (jax.ShapeDtypeStruct((n_slabs, batch, N_HEADS, seq, V_DIM), BF16),
                     jax.ShapeDtypeStruct((n_slabs, N_HEADS, N_META, V_DIM), BF16))
        out_specs = (pl.BlockSpec((1, 1, N_HEADS, tm, V_DIM),
                                  lambda j, i: (j, i // n_ib, 0, i % n_ib, 0)),
                     pl.BlockSpec((1, N_HEADS, N_META, V_DIM), lambda j, i: (j, 0, 0, 0)))
    elif mode == "vt":
        out_shape = (jax.ShapeDtypeStruct((batch, N_HEADS, V_DIM + SUM_ROWS, seq), BF16),
                     jax.ShapeDtypeStruct((n_slabs, N_HEADS, N_META, V_DIM), BF16))
        out_specs = (pl.BlockSpec((1, N_HEADS, V_DIM + SUM_ROWS, tm),
                                  lambda j, i: (i // n_ib, 0, 0, i % n_ib)),
                     pl.BlockSpec((1, N_HEADS, N_META, V_DIM), lambda j, i: (j, 0, 0, 0)))
    else:
        out_shape = (jax.ShapeDtypeStruct((n_slabs, rows, D_ATTN), F32),
                     jax.ShapeDtypeStruct((n_slabs, N_META, D_ATTN), F32))
        out_specs = (pl.BlockSpec((1, tm, D_ATTN), lambda j, i: (j, i, 0)),
                     pl.BlockSpec((1, N_META, D_ATTN), lambda j, i: (j, 0, 0)))
    return pl.pallas_call(
        functools.partial(_proj_kernel, mode=mode),
        grid=(n_slabs, n_i),
        in_specs=[pl.BlockSpec((tm, D_MODEL), lambda j, i: (i, 0)),
                  pl.BlockSpec((N_META, D_MODEL), lambda j, i: (0, 0)),
                  pl.BlockSpec((1, D_MODEL), lambda j, i: (0, 0)),
                  pl.BlockSpec((D_MODEL, D_ATTN), lambda j, i: (0, j + col_tile0))],
        out_specs=out_specs,
        out_shape=out_shape,
        scratch_shapes=[pltpu.VMEM((D_MODEL, D_ATTN), BF16)],
        compiler_params=pltpu.CompilerParams(
            dimension_semantics=("arbitrary", "arbitrary"),
            vmem_limit_bytes=VMEM_LIMIT_PROJ),
        name={"heads": "proj_qk", "vt": "proj_v", "flat": "proj_gates"}[mode],
    )(x2d, meta, norm_g, w_in)


def _toeplitz_bias(dist, rb_ref, h, far):
    b = jnp.full(dist.shape, (rb_ref[0, h] - far) * LOG2E, F32)
    for j, thr in enumerate(BUCKET_THRESHOLDS, start=1):
        b = jnp.where(dist >= thr, (rb_ref[j, h] - far) * LOG2E, b)
    return b


def _fill_bias(ref, lead, n_rows, n_cols, d0, rb_ref, h, far):
    sub = lax.broadcasted_iota(jnp.int32, (8, V_DIM), 0)
    lane = lax.broadcasted_iota(jnp.int32, (8, V_DIM), 1)
    zeros = jnp.zeros((8, V_DIM), F32)
    masked = jnp.full((8, V_DIM), NEG_INF, F32)
    cache = {}
    for a8 in range(n_rows // 8):
        for b in range(n_cols // V_DIM):
            off = d0 + V_DIM * b - 8 * a8
            if off + V_DIM - 1 < 0:
                tile = masked
            elif off - 7 >= MAX_DISTANCE:
                tile = zeros
            else:
                if off not in cache:
                    d = off + lane - sub
                    cache[off] = jnp.where(d >= 0, _toeplitz_bias(d, rb_ref, h, far), NEG_INF)
                tile = cache[off]
            ref[(*lead, pl.ds(8 * a8, 8), pl.ds(V_DIM * b, V_DIM))] = tile


def _attn_kernel(rb_ref, lq1_ref, lk1_ref, lq2_ref, lk2_ref, q_ref, k_ref, vt_ref, km_ref, vm_ref,
                 g_ref, sg_ref, o_ref, vmt_ref, bd_ref, bm_ref, sel_ref, sbuf, sc_ref, mx_ref,
                 m_ref, acc_ref):
    h = pl.program_id(0)
    tq, tk = ATTN_TQ, ATTN_TK
    seq = k_ref.shape[2]
    nq = seq // tq

    far = rb_ref[N_BUCKETS - 1, h]

    @pl.when(pl.program_id(1) == 0)
    def _():
        _fill_bias(bd_ref, (), tk, tq, 0, rb_ref, h, far)
        _fill_bias(sel_ref, (1,), tk, tq, tk, rb_ref, h, far)
        sel_ref[0] = jnp.zeros((tk, tq), F32)
        _fill_bias(bm_ref, (1,), N_META, tq, N_META, rb_ref, h, far)
        bm_ref[0] = jnp.zeros((N_META, tq), F32)
        padded = jnp.concatenate(
            [vm_ref[0].astype(F32), jnp.zeros((V_DIM - N_META, V_DIM), F32)], axis=0)
        vmt_ref[pl.ds(0, V_DIM), :] = padded.T.astype(BF16)
        vmt_ref[pl.ds(V_DIM, SUM_ROWS), :] = jnp.ones((SUM_ROWS, V_DIM), BF16)

    lane = lax.broadcasted_iota(jnp.int32, (tq, V_DIM), 1)

    lam = (jnp.exp(jnp.sum(lq1_ref[...] * lk1_ref[...], keepdims=True))
           - jnp.exp(jnp.sum(lq2_ref[...] * lk2_ref[...], keepdims=True))
           + LAMBDA_INIT)

    def init_stats(st):
        m_ref[st] = jnp.full(m_ref.shape[1:], NEG_INF, F32)
        for a in range(2):
            acc_ref[st, a, pl.ds(0, V_DIM), :] = jnp.zeros((V_DIM, tq), F32)
            acc_ref[st, a, pl.ds(V_DIM, SUM_ROWS), :] = jnp.ones((SUM_ROWS, tq), F32)

    hk, hq = tk // 2, tq // 2

    def produce(buf, q_off, k_off, kind, sel=None):
        q = q_ref[0, 0, pl.ds(pl.multiple_of(q_off, tq), tq), :]
        zero = jnp.zeros_like(q)
        k_off = pl.multiple_of(k_off, tk)
        k_t = k_ref[0, 0, pl.ds(k_off, tk), :]
        for a in range(2):
            qa = jnp.where((lane < HEAD_DIM) if a == 0 else (lane >= HEAD_DIM), q, zero)
            if kind == "diag":
                top = _dot_nt(k_t[:hk], qa) + bd_ref[pl.ds(0, hk), :]
                low = _dot_nt(k_t[hk:], qa[hq:]) + bd_ref[pl.ds(hk, hk), pl.ds(hq, hq)]
                sc = _dot_nt(km_ref[0], qa) + bm_ref[jnp.where(k_off == 0, 1, 0)]
                sbuf[buf, a, pl.ds(0, hk), :] = top
                sbuf[buf, a, pl.ds(hk, hk), pl.ds(hq, hq)] = low
                sc_ref[buf, a] = sc
                mx = jnp.maximum(jnp.max(top, axis=0, keepdims=True),
                                 jnp.max(sc, axis=0, keepdims=True))
                mx_low = jnp.max(low, axis=0, keepdims=True)
                mx_ref[buf, a] = jnp.concatenate(
                    [mx[:, :hq], jnp.maximum(mx[:, hq:], mx_low)], axis=1)
                continue
            s = _dot_nt(k_t, qa)
            if kind == "near":
                s = s + sel_ref[1]
            elif kind == "select":
                s = s + sel_ref[sel]
            sbuf[buf, a] = s
            mx_ref[buf, a] = jnp.max(s, axis=0, keepdims=True)

    def consume(buf, st, k_off, diag=False):
        k_off = pl.multiple_of(k_off, tk)
        v_t = vt_ref[0, 0, :, pl.ds(k_off, tk)]
        for a in range(2):
            m_old = m_ref[st, a]
            m_new = jnp.maximum(m_old, mx_ref[buf, a])
            alpha = jnp.exp2(m_old - m_new)
            if diag:
                top = jnp.exp2(sbuf[buf, a, pl.ds(0, hk), :] - m_new)
                low = jnp.exp2(sbuf[buf, a, pl.ds(hk, hk), pl.ds(hq, hq)] - m_new[:, hq:])
                pc = jnp.exp2(sc_ref[buf, a] - m_new)
                pv = (_dot(v_t[:, :hk], top.astype(BF16))
                      + _dot(vmt_ref[...][:, :N_META], pc.astype(BF16)))
                pv_low = _dot(v_t[:, hk:], low.astype(BF16))
                pv = jnp.concatenate([pv[:, :hq], pv[:, hq:] + pv_low], axis=1)
            else:
                p = jnp.exp2(sbuf[buf, a] - m_new)
                pv = _dot(v_t, p.astype(BF16))
            acc_ref[st, a] = alpha * acc_ref[st, a] + pv
            m_ref[st, a] = m_new

    def finalize(st, q_off):
        q_rows = pl.ds(pl.multiple_of(q_off, tq), tq)
        heads = [acc_ref[st, a, pl.ds(0, V_DIM), :] / acc_ref[st, a, pl.ds(V_DIM, 1), :]
                 for a in range(2)]
        out_t = heads[0] - lam * heads[1]
        inv = lax.rsqrt(jnp.mean(out_t * out_t, axis=0, keepdims=True) + SUBLN_EPS)
        att = ((out_t * inv).T * sg_ref[...]) * (1.0 - LAMBDA_INIT)
        g = g_ref[0, q_rows, :]
        o_ref[q_rows, :] = (att * (g * jax.nn.sigmoid(g))).astype(o_ref.dtype)

    init_stats(0)
    init_stats(1)
    produce(0, 0, 0, "diag")

    def q_tile(i, cur, odd):
        oth = 1 - cur
        q_off = i * tq

        def far_pair(tt, c):
            t = 2 * tt
            produce(oth, q_off, (t + 1) * tk, "far")
            consume(cur, cur, t * tk)
            produce(cur, q_off, (t + 2) * tk, "far")
            consume(oth, cur, (t + 1) * tk)
            return c

        n_far = jnp.maximum(i - 2, 0)
        lax.fori_loop(0, n_far // 2, far_pair, 0)

        def near_diag():
            produce(oth, q_off, (i - 1) * tk, "near")
            consume(cur, cur, (i - 2) * tk)
            produce(cur, q_off, i * tk, "diag")
            consume(oth, cur, (i - 1) * tk)

        def last():
            finalize(oth, jnp.maximum(i - 1, 0) * tq)
            init_stats(oth)
            nxt = jnp.minimum(i + 1, nq - 1)
            produce(oth, nxt * tq, 0, "select", sel=jnp.where(nxt == 1, 1, 0))
            consume(cur, cur, i * tk, diag=True)

        if odd:
            @pl.when(i >= 3)
            def _():
                consume(cur, cur, (n_far - 1) * tk)
                produce(cur, q_off, n_far * tk, "far")
                near_diag()
                last()

            @pl.when(i == 1)
            def _():
                consume(cur, cur, 0)
                produce(cur, q_off, tk, "diag")
                last()
        else:
            @pl.when(i >= 2)
            def _():
                near_diag()
                last()

            @pl.when(i == 0)
            def _():
                last()

    def q_pair(ii, carry):
        q_tile(2 * ii, 0, False)
        q_tile(2 * ii + 1, 1, True)
        return carry

    lax.fori_loop(0, nq // 2, q_pair, 0)
    finalize((nq - 1) % 2, (nq - 1) * tq)


def _attention(qk, qk_meta, vt, v_meta, gates, rel_bias, lam_q1, lam_k1, lam_q2, lam_k2, subln_g):
    _, batch, _, seq, _ = qk.shape
    tq, tk = ATTN_TQ, ATTN_TK
    smem = pl.BlockSpec(memory_space=pltpu.SMEM)
    row64 = pl.BlockSpec((1, HEAD_DIM), lambda h, b: (0, 0))
    qk_spec = lambda which: pl.BlockSpec((1, 1, 1, seq, V_DIM), lambda h, b: (which, b, h, 0, 0))
    meta_spec = lambda which: pl.BlockSpec((1, 1, N_META, V_DIM), lambda h, b: (which, h, 0, 0))

    def kernel(rb, lq1, lk1, lq2, lk2, q_ref, k_ref, vt_ref, km_ref, vm_ref, *rest):
        _attn_kernel(rb, lq1, lk1, lq2, lk2, q_ref.at[0], k_ref.at[0], vt_ref,
                     km_ref.at[0], vm_ref.at[0], *rest)

    return pl.pallas_call(
        kernel,
        grid=(N_HEADS, batch),
        in_specs=[smem, row64, row64, row64, row64,
                  qk_spec(0), qk_spec(1),
                  pl.BlockSpec((1, 1, V_DIM + SUM_ROWS, seq), lambda h, b: (b, h, 0, 0)),
                  meta_spec(1), meta_spec(0),
                  pl.BlockSpec((1, seq, V_DIM), lambda h, b: (0, b, h)),
                  pl.BlockSpec((1, V_DIM), lambda h, b: (0, 0))],
        out_specs=pl.BlockSpec((seq, V_DIM), lambda h, b: (b, h)),
        out_shape=jax.ShapeDtypeStruct((batch * seq, D_ATTN), BF16),
        scratch_shapes=[pltpu.VMEM((V_DIM + SUM_ROWS, V_DIM), BF16),
                        pltpu.VMEM((tk, tq), F32),
                        pltpu.VMEM((2, N_META, tq), F32),
                        pltpu.VMEM((2, tk, tq), F32),
                        pltpu.VMEM((2, 2, tk, tq), F32),
                        pltpu.VMEM((2, 2, N_META, tq), F32),
                        pltpu.VMEM((2, 2, 1, tq), F32),
                        pltpu.VMEM((2, 2, 1, tq), F32),
                        pltpu.VMEM((2, 2, V_DIM + SUM_ROWS, tq), F32)],
        compiler_params=pltpu.CompilerParams(
            dimension_semantics=("arbitrary", "arbitrary"),
            vmem_limit_bytes=VMEM_LIMIT_ATTN),
        name="diff_attention",
    )(rel_bias, lam_q1, lam_k1, lam_q2, lam_k2, qk, qk, vt, qk_meta, v_meta, gates, subln_g)


def _scan_block(a, b):
    n = a.shape[0]
    row = lax.broadcasted_iota(jnp.int32, a.shape, 0)
    s = 1
    while s < n:
        keep = row >= s
        a_sh = jnp.where(keep, pltpu.roll(a, s, 0), 1.0)
        b_sh = jnp.where(keep, pltpu.roll(b, s, 0), 0.0)
        b = b + a * b_sh
        a = a * a_sh
        s *= 2
    return a, b


def _tail_kernel(x_ref, att_ref, u_ref, um_ref, g_ref, cw_ref, cb_ref, wa_ref, wx_ref, ba_ref, bx_ref,
                 lam_ref, w_ref, fg_ref, o_ref, wb_ref, wg_ref, ubuf, h_ref, a_s, b_s, rec_s, rec_prev,
                 *, chunks_per_batch):
    s = pl.program_id(0)
    n = u_ref.shape[0]
    hist = 8
    seg = n // 8

    x = -lam_ref[...]
    softplus = jnp.maximum(x, 0.0) + jnp.log1p(jnp.exp(-jnp.abs(x)))

    @pl.when(s == 0)
    def _():
        wb_ref[...] = w_ref[...].astype(BF16)
        for blk in range(N_LRU_BLOCKS):
            wg_ref[blk, :, pl.ds(0, LRU_BLOCK)] = wa_ref[blk].astype(BF16)
            wg_ref[blk, :, pl.ds(LRU_BLOCK, LRU_BLOCK)] = wx_ref[blk].astype(BF16)
        rec_s[...] = jnp.zeros_like(rec_s)

    def gate_inputs(rows, cols, blk):
        cw = cw_ref[:, cols]
        uc = (cw[3:4] * ubuf[pl.ds(hist, rows), cols] + cw[2:3] * ubuf[pl.ds(hist - 1, rows), cols]
              + cw[1:2] * ubuf[pl.ds(hist - 2, rows), cols] + cw[0:1] * ubuf[pl.ds(hist - 3, rows), cols]
              + cb_ref[:, cols])
        return uc, _dot(uc.astype(BF16), wg_ref[blk])

    def gates(uc, pre, cols, first):
        r = jax.nn.sigmoid(pre[:, :LRU_BLOCK] + ba_ref[:, cols])
        gi = jax.nn.sigmoid(pre[:, LRU_BLOCK:] + bx_ref[:, cols])
        log_a = -LRU_C * r * softplus[:, cols]
        a = jnp.exp(log_a)
        v = jnp.tanh(-log_a) * (a * a + 1.0)
        mult = jnp.where(v > 0.0, v * lax.rsqrt(v), 0.0)
        if first:
            row = lax.broadcasted_iota(jnp.int32, mult.shape, 0)
            mult = jnp.where(row == 0, 1.0, mult)
        return a, mult * gi * uc

    @pl.when(s % chunks_per_batch == 0)
    def _():
        ubuf[pl.ds(0, hist), :] = jnp.zeros((hist, D_LRU), F32)
        ubuf[pl.ds(hist, N_META), :] = um_ref[...]
        for blk in range(N_LRU_BLOCKS):
            cols = slice(blk * LRU_BLOCK, (blk + 1) * LRU_BLOCK)
            a, b = gates(*gate_inputs(N_META, cols, blk), cols, True)
            _, hm = _scan_block(a, b)
            h_ref[:, cols] = hm[N_META - 1:N_META, :]
        ubuf[pl.ds(0, hist), :] = ubuf[pl.ds(N_META, hist), :]

    rec_prev[...] = rec_s[...]
    ubuf[pl.ds(hist, n), :] = u_ref[...]
    sub = lax.broadcasted_iota(jnp.int32, (8, LRU_BLOCK), 0)
    n_out = D_MODEL // N_LRU_BLOCKS
    ssq = jnp.zeros((n, LRU_BLOCK), F32)
    for blk in range(N_LRU_BLOCKS):
        cols = slice(blk * LRU_BLOCK, (blk + 1) * LRU_BLOCK)
        ocols = slice(blk * n_out, (blk + 1) * n_out)
        z = (x_ref[:, ocols] + _dot(att_ref[...], wb_ref[pl.ds(0, D_ATTN), ocols])
             + _dot(rec_prev[...], wb_ref[pl.ds(D_ATTN, D_LRU), ocols]))
        o_ref[:, ocols] = z
        for part in range(n_out // LRU_BLOCK):
            zp = z[:, part * LRU_BLOCK:(part + 1) * LRU_BLOCK]
            ssq = ssq + zp * zp

        a, b = gates(*gate_inputs(n, cols, blk), cols, False)
        for k in range(8):
            a_s[blk, pl.ds(SEG_PITCH * k, seg), :] = a[seg * k:seg * (k + 1)]
            b_s[blk, pl.ds(SEG_PITCH * k, seg), :] = b[seg * k:seg * (k + 1)]
        step = lambda j: pl.ds(j, 8, stride=SEG_PITCH)
        a_run = a_s[blk, step(0), :]
        h_run = b_s[blk, step(0), :]
        for j in range(1, seg):
            aj = a_s[blk, step(j), :]
            h_run = aj * h_run + b_s[blk, step(j), :]
            a_run = aj * a_run
        a_cum, h_cum = _scan_block(a_run, h_run)
        carry = h_ref[:, cols]
        seg_end = h_cum + a_cum * carry
        h_ref[:, cols] = seg_end[7:8, :]
        h_run = jnp.where(sub == 0, carry, pltpu.roll(seg_end, 1, 0))
        for j in range(seg):
            h_run = a_s[blk, step(j), :] * h_run + b_s[blk, step(j), :]
            b_s[blk, step(j), :] = h_run
        g = g_ref[:, cols]
        h = jnp.concatenate([b_s[blk, pl.ds(SEG_PITCH * k, seg), :] for k in range(8)], axis=0)
        rec_s[:, cols] = (h * (g * jax.nn.sigmoid(g))).astype(BF16)
    ubuf[pl.ds(0, hist), :] = ubuf[pl.ds(n, hist), :]
    inv = lax.rsqrt(jnp.sum(ssq, axis=1, keepdims=True) * (1.0 / D_MODEL) + NORM_EPS)
    o_ref[...] = (o_ref[...] * inv) * fg_ref[...]


def _tail(x2d, att, gates, gates_meta, conv_w, conv_b, w_a, b_a, w_x, b_x, lru_lambda, w_out,
          final_g, *, seq):
    rows = x2d.shape[0]
    tm = OUT_ROWS
    n_tiles = rows // tm
    prev = lambda s: (jnp.maximum(s - 1, 0), 0)
    cur = lambda which: (lambda s: (which, jnp.minimum(s, n_tiles - 1), 0))
    row = lambda n: pl.BlockSpec((n, D_LRU), lambda s: (0, 0))
    wspec = pl.BlockSpec((N_LRU_BLOCKS, LRU_BLOCK, LRU_BLOCK), lambda s: (0, 0, 0))

    def kernel(x_ref, att_ref, u_ref, um_ref, g_ref, *rest):
        _tail_kernel(x_ref, att_ref, u_ref.at[0], um_ref.at[0], g_ref.at[0], *rest,
                     chunks_per_batch=seq // tm)

    return pl.pallas_call(
        kernel,
        grid=(n_tiles + 1,),
        in_specs=[pl.BlockSpec((tm, D_MODEL), prev),
                  pl.BlockSpec((tm, D_ATTN), prev),
                  pl.BlockSpec((1, tm, D_LRU), cur(1)),
                  pl.BlockSpec((1, N_META, D_LRU), lambda s: (1, 0, 0)),
                  pl.BlockSpec((1, tm, D_LRU), cur(2)),
                  row(CONV_WIDTH), row(1), wspec, wspec, row(1), row(1), row(1),
                  pl.BlockSpec((D_ATTN + D_LRU, D_MODEL), lambda s: (0, 0),
                               pipeline_mode=pl.Buffered(1)),
                  pl.BlockSpec((1, D_MODEL), lambda s: (0, 0))],
        out_specs=pl.BlockSpec((tm, D_MODEL), prev),
        out_shape=jax.ShapeDtypeStruct((rows, D_MODEL), F32),
        scratch_shapes=[pltpu.VMEM((D_ATTN + D_LRU, D_MODEL), BF16),
                        pltpu.VMEM((N_LRU_BLOCKS, LRU_BLOCK, 2 * LRU_BLOCK), BF16),
                        pltpu.VMEM((8 + tm, D_LRU), F32),
                        pltpu.VMEM((1, D_LRU), F32),
                        pltpu.VMEM((N_LRU_BLOCKS, 8 * SEG_PITCH, LRU_BLOCK), F32),
                        pltpu.VMEM((N_LRU_BLOCKS, 8 * SEG_PITCH, LRU_BLOCK), F32),
                        pltpu.VMEM((tm, D_LRU), BF16),
                        pltpu.VMEM((tm, D_LRU), BF16)],
        compiler_params=pltpu.CompilerParams(
            dimension_semantics=("arbitrary",),
            vmem_limit_bytes=VMEM_LIMIT_OUT),
        name="rglru_out_proj",
    )(x2d, att, gates, gates_meta, gates, conv_w, conv_b, w_a, w_x, b_a, b_x, lru_lambda, w_out,
      final_g)


def kernel(x, meta_tokens, rel_bias, norm_g, w_in, conv_w, conv_b, w_a, b_a, w_x, b_x, lru_lambda,
           lam_q1, lam_k1, lam_q2, lam_k2, subln_g, w_out, final_g):
    batch, seq, _ = x.shape
    x2d = x.reshape(batch * seq, D_MODEL)
    project = functools.partial(_project, x2d, meta_tokens, norm_g, w_in[0], batch=batch, seq=seq)
    qk, qk_meta = project(col_tile0=0, n_slabs=2, mode="heads")
    vt, v_meta = project(col_tile0=2, n_slabs=1, mode="vt")
    gates, gates_meta = project(col_tile0=3, n_slabs=3, mode="flat")
    att = _attention(qk, qk_meta, vt, v_meta, gates, rel_bias, lam_q1, lam_k1, lam_q2, lam_k2,
                     subln_g)
    out = _tail(x2d, att, gates, gates_meta, conv_w[0], conv_b, w_a[0], b_a, w_x[0], b_x, lru_lambda,
                w_out[0], final_g.reshape(1, D_MODEL), seq=seq)
    return out.reshape(batch, seq, D_MODEL)
```

```python
import functools
import math

import numpy as np
import jax
import jax.numpy as jnp
from jax import lax
from jax.experimental import pallas as pl
from jax.experimental.pallas import tpu as pltpu

D_MODEL = 2048
N_META = 16
D_ATTN = 1024
D_LRU = 1024
N_HEADS = 8
HEAD_DIM = 64
V_DIM = 128
N_LRU_BLOCKS = 8
LRU_BLOCK = 128
CONV_WIDTH = 4
LRU_C = 8.0
N_BUCKETS = 32
MAX_DISTANCE = 128
NORM_EPS = 1e-6
SUBLN_EPS = 1e-5
NEG_INF = -1e30
LAMBDA_INIT = 0.8 - 0.6 * math.exp(-0.3 * 0)
LOG2E = math.log2(math.e)

BF16 = jnp.bfloat16
F32 = jnp.float32

VMEM_LIMIT_PROJ = 56 * 1024 * 1024
VMEM_LIMIT_ATTN = 48 * 1024 * 1024
VMEM_LIMIT_OUT = 56 * 1024 * 1024

PROJ_ROWS = 1024
ATTN_TQ = 512
ATTN_TK = 512
SUM_ROWS = 16
OUT_ROWS = 256
SEG_PITCH = 40


def _bucket_thresholds():
    max_exact = N_BUCKETS // 2
    d = np.arange(0, 4 * MAX_DISTANCE, dtype=np.int64)
    val = (np.log(np.maximum(d, 1).astype(np.float64) / max_exact)
           / math.log(MAX_DISTANCE / max_exact) * (N_BUCKETS - max_exact))
    large = np.minimum(max_exact + np.floor(val + 1e-9).astype(np.int64), N_BUCKETS - 1)
    bucket = np.where(d < max_exact, d, large)
    frac = np.abs(val - np.round(val))
    interior = (d > max_exact) & (d < MAX_DISTANCE)
    assert frac[interior].min() > 1e-3
    assert (np.diff(bucket) >= 0).all() and bucket[MAX_DISTANCE] == N_BUCKETS - 1
    return tuple(int(np.argmax(bucket >= j)) for j in range(1, N_BUCKETS))


BUCKET_THRESHOLDS = _bucket_thresholds()


def _rms(x, g, eps):
    y = x * lax.rsqrt(jnp.mean(x * x, axis=-1, keepdims=True) + eps)
    return y * g


def _dot(a, b):
    return jnp.dot(a, b, preferred_element_type=F32)


def _dot_nt(a, b):
    return lax.dot_general(a, b, (((1,), (1,)), ((), ())), preferred_element_type=F32)


def _proj_kernel(*refs, mode):
    if mode == "vt":
        x_ref, meta_ref, g_ref, w_ref, o_ref, om_ref, h_ref, hm_ref, wb_ref = refs
    else:
        h_ref, hm_ref, w_ref, o_ref, om_ref, wb_ref = refs
    j = pl.program_id(0)
    i = pl.program_id(1)
    out_scale = jnp.where(j == 0, HEAD_DIM ** -0.5 * LOG2E, 1.0).astype(F32) if mode == "heads" else 1.0
    head = lambda y, h: y[:, h * V_DIM:(h + 1) * V_DIM]

    @pl.when(i == 0)
    def _():
        wb_ref[...] = w_ref[...].astype(BF16)
        if mode == "vt":
            hm_ref[...] = _rms(meta_ref[...], g_ref[...], NORM_EPS).astype(BF16)
        ym = _dot(hm_ref[...], wb_ref[...]) * out_scale
        if mode == "flat":
            om_ref[0] = ym
        else:
            for h in range(N_HEADS):
                om_ref[0, h] = head(ym, h).astype(BF16)

    if mode == "vt":
        h_ref[...] = _rms(x_ref[...], g_ref[...], NORM_EPS).astype(BF16)
    y = _dot(h_ref[...], wb_ref[...]) * out_scale
    if mode == "flat":
        o_ref[0] = y
    elif mode == "heads":
        for h in range(N_HEADS):
            o_ref[0, 0, h] = head(y, h).astype(BF16)
    else:
        rows = y.shape[0]
        for h in range(N_HEADS):
            o_ref[0, h, pl.ds(0, V_DIM), :] = head(y, h).T.astype(BF16)
            o_ref[0, h, pl.ds(V_DIM, SUM_ROWS), :] = jnp.ones((SUM_ROWS, rows), BF16)


def _project(inputs, w_in, *, col_tile0, n_slabs, mode, batch, seq):
    rows = inputs[0].shape[0]
    tm = PROJ_ROWS
    n_i = rows // tm
    n_ib = seq // tm
    row_tile = pl.BlockSpec((tm, D_MODEL), lambda j, i: (i, 0))
    meta_rows = pl.BlockSpec((N_META, D_MODEL), lambda j, i: (0, 0))
    if mode == "vt":
        in_specs = [row_tile, meta_rows, pl.BlockSpec((1, D_MODEL), lambda j, i: (0, 0))]
    else:
        in_specs = [row_tile, meta_rows]
    in_specs.append(pl.BlockSpec((D_MODEL, D_ATTN), lambda j, i: (0, j + col_tile0)))
    if mode == "heads":
        out_shape = (jax.ShapeDtypeStruct((n_slabs, batch, N_HEADS, seq, V_DIM), BF16),
                     jax.ShapeDtypeStruct((n_slabs, N_HEADS, N_META, V_DIM), BF16))
        out_specs = (pl.BlockSpec((1, 1, N_HEADS, tm, V_DIM),
                                  lambda j, i: (j, i // n_ib, 0, i % n_ib, 0)),
                     pl.BlockSpec((1, N_HEADS, N_META, V_DIM), lambda j, i: (j, 0, 0, 0)))
    elif mode == "vt":
        assert n_slabs == 1
        out_shape = (jax.ShapeDtypeStruct((batch, N_HEADS, V_DIM + SUM_ROWS, seq), BF16),
                     jax.ShapeDtypeStruct((n_slabs, N_HEADS, N_META, V_DIM), BF16),
                     jax.ShapeDtypeStruct((rows, D_MODEL), BF16),
                     jax.ShapeDtypeStruct((N_META, D_MODEL), BF16))
        out_specs = (pl.BlockSpec((1, N_HEADS, V_DIM + SUM_ROWS, tm),
                                  lambda j, i: (i // n_ib, 0, 0, i % n_ib)),
                     pl.BlockSpec((1, N_HEADS, N_META, V_DIM), lambda j, i: (j, 0, 0, 0)),
                     row_tile, meta_rows)
    else:
        out_shape = (jax.ShapeDtypeStruct((n_slabs, rows, D_ATTN), F32),
                     jax.ShapeDtypeStruct((n_slabs, N_META, D_ATTN), F32))
        out_specs = (pl.BlockSpec((1, tm, D_ATTN), lambda j, i: (j, i, 0)),
                     pl.BlockSpec((1, N_META, D_ATTN), lambda j, i: (j, 0, 0)))
    return pl.pallas_call(
        functools.partial(_proj_kernel, mode=mode),
        grid=(n_slabs, n_i),
        in_specs=in_specs,
        out_specs=out_specs,
        out_shape=out_shape,
        scratch_shapes=[pltpu.VMEM((D_MODEL, D_ATTN), BF16)],
        compiler_params=pltpu.CompilerParams(
            dimension_semantics=("arbitrary", "arbitrary"),
            vmem_limit_bytes=VMEM_LIMIT_PROJ),
        name={"heads": "proj_qk", "vt": "proj_v", "flat": "proj_gates"}[mode],
    )(*inputs, w_in)


def _toeplitz_bias(dist, rb_ref, h, far):
    b = jnp.full(dist.shape, (rb_ref[0, h] - far) * LOG2E, F32)
    for j, thr in enumerate(BUCKET_THRESHOLDS, start=1):
        b = jnp.where(dist >= thr, (rb_ref[j, h] - far) * LOG2E, b)
    return b


def _fill_bias(ref, lead, n_rows, n_cols, d0, rb_ref, h, far):
    sub = lax.broadcasted_iota(jnp.int32, (8, V_DIM), 0)
    lane = lax.broadcasted_iota(jnp.int32, (8, V_DIM), 1)
    zeros = jnp.zeros((8, V_DIM), F32)
    masked = jnp.full((8, V_DIM), NEG_INF, F32)
    cache = {}
    for a8 in range(n_rows // 8):
        for b in range(n_cols // V_DIM):
            off = d0 + V_DIM * b - 8 * a8
            if off + V_DIM - 1 < 0:
                tile = masked
            elif off - 7 >= MAX_DISTANCE:
                tile = zeros
            else:
                if off not in cache:
                    d = off + lane - sub
                    cache[off] = jnp.where(d >= 0, _toeplitz_bias(d, rb_ref, h, far), NEG_INF)
                tile = cache[off]
            ref[(*lead, pl.ds(8 * a8, 8), pl.ds(V_DIM * b, V_DIM))] = tile


def _attn_kernel(rb_ref, lq1_ref, lk1_ref, lq2_ref, lk2_ref, q_ref, k_ref, vt_ref, km_ref, vm_ref,
                 g_ref, sg_ref, o_ref, vmt_ref, bd_ref, bm_ref, sel_ref, sbuf, sc_ref, mx_ref,
                 m_ref, acc_ref):
    h = pl.program_id(0)
    tq, tk = ATTN_TQ, ATTN_TK
    seq = k_ref.shape[2]
    nq = seq // tq

    far = rb_ref[N_BUCKETS - 1, h]

    @pl.when(pl.program_id(1) == 0)
    def _():
        _fill_bias(bd_ref, (), tk, tq, 0, rb_ref, h, far)
        _fill_bias(sel_ref, (1,), tk, tq, tk, rb_ref, h, far)
        sel_ref[0] = jnp.zeros((tk, tq), F32)
        _fill_bias(bm_ref, (1,), N_META, tq, N_META, rb_ref, h, far)
        bm_ref[0] = jnp.zeros((N_META, tq), F32)
        padded = jnp.concatenate(
            [vm_ref[0].astype(F32), jnp.zeros((V_DIM - N_META, V_DIM), F32)], axis=0)
        vmt_ref[pl.ds(0, V_DIM), :] = padded.T.astype(BF16)
        vmt_ref[pl.ds(V_DIM, SUM_ROWS), :] = jnp.ones((SUM_ROWS, V_DIM), BF16)

    lane = lax.broadcasted_iota(jnp.int32, (tq, V_DIM), 1)

    lam = (jnp.exp(jnp.sum(lq1_ref[...] * lk1_ref[...], keepdims=True))
           - jnp.exp(jnp.sum(lq2_ref[...] * lk2_ref[...], keepdims=True))
           + LAMBDA_INIT)

    def init_stats(st):
        m_ref[st] = jnp.full(m_ref.shape[1:], NEG_INF, F32)
        for a in range(2):
            acc_ref[st, a, pl.ds(0, V_DIM), :] = jnp.zeros((V_DIM, tq), F32)
            acc_ref[st, a, pl.ds(V_DIM, SUM_ROWS), :] = jnp.ones((SUM_ROWS, tq), F32)

    hk, hq = tk // 2, tq // 2

    def produce(buf, q_off, k_off, kind, sel=None):
        q = q_ref[0, 0, pl.ds(pl.multiple_of(q_off, tq), tq), :]
        zero = jnp.zeros_like(q)
        k_off = pl.multiple_of(k_off, tk)
        k_t = k_ref[0, 0, pl.ds(k_off, tk), :]
        for a in range(2):
            qa = jnp.where((lane < HEAD_DIM) if a == 0 else (lane >= HEAD_DIM), q, zero)
            if kind == "diag":
                top = _dot_nt(k_t[:hk], qa) + bd_ref[pl.ds(0, hk), :]
                low = _dot_nt(k_t[hk:], qa[hq:]) + bd_ref[pl.ds(hk, hk), pl.ds(hq, hq)]
                sc = _dot_nt(km_ref[0], qa) + bm_ref[jnp.where(k_off == 0, 1, 0)]
                sbuf[buf, a, pl.ds(0, hk), :] = top
                sbuf[buf, a, pl.ds(hk, hk), pl.ds(hq, hq)] = low
                sc_ref[buf, a] = sc
                mx = jnp.maximum(jnp.max(top, axis=0, keepdims=True),
                                 jnp.max(sc, axis=0, keepdims=True))
                mx_low = jnp.max(low, axis=0, keepdims=True)
                mx_ref[buf, a] = jnp.concatenate(
                    [mx[:, :hq], jnp.maximum(mx[:, hq:], mx_low)], axis=1)
                continue
            s = _dot_nt(k_t, qa)
            if kind == "near":
                s = s + sel_ref[1]
            elif kind == "select":
                s = s + sel_ref[sel]
            sbuf[buf, a] = s
            mx_ref[buf, a] = jnp.max(s, axis=0, keepdims=True)

    def consume(buf, st, k_off, diag=False):
        k_off = pl.multiple_of(k_off, tk)
        v_t = vt_ref[0, 0, :, pl.ds(k_off, tk)]
        for a in range(2):
            m_old = m_ref[st, a]
            m_new = jnp.maximum(m_old, mx_ref[buf, a])
            alpha = jnp.exp2(m_old - m_new)
            if diag:
                top = jnp.exp2(sbuf[buf, a, pl.ds(0, hk), :] - m_new)
                low = jnp.exp2(sbuf[buf, a, pl.ds(hk, hk), pl.ds(hq, hq)] - m_new[:, hq:])
                pc = jnp.exp2(sc_ref[buf, a] - m_new)
                pv = (_dot(v_t[:, :hk], top.astype(BF16))
                      + _dot(vmt_ref[...][:, :N_META], pc.astype(BF16)))
                pv_low = _dot(v_t[:, hk:], low.astype(BF16))
                pv = jnp.concatenate([pv[:, :hq], pv[:, hq:] + pv_low], axis=1)
            else:
                p = jnp.exp2(sbuf[buf, a] - m_new)
                pv = _dot(v_t, p.astype(BF16))
            acc_ref[st, a] = alpha * acc_ref[st, a] + pv
            m_ref[st, a] = m_new

    def finalize(st, q_off):
        q_rows = pl.ds(pl.multiple_of(q_off, tq), tq)
        heads = [acc_ref[st, a, pl.ds(0, V_DIM), :] / acc_ref[st, a, pl.ds(V_DIM, 1), :]
                 for a in range(2)]
        out_t = heads[0] - lam * heads[1]
        inv = lax.rsqrt(jnp.mean(out_t * out_t, axis=0, keepdims=True) + SUBLN_EPS)
        att = ((out_t * inv).T * sg_ref[...]) * (1.0 - LAMBDA_INIT)
        g = g_ref[0, q_rows, :]
        o_ref[q_rows, :] = (att * (g * jax.nn.sigmoid(g))).astype(o_ref.dtype)

    init_stats(0)
    init_stats(1)
    produce(0, 0, 0, "diag")

    def q_tile(i, cur, odd):
        oth = 1 - cur
        q_off = i * tq

        def far_pair(tt, c):
            t = 2 * tt
            produce(oth, q_off, (t + 1) * tk, "far")
            consume(cur, cur, t * tk)
            produce(cur, q_off, (t + 2) * tk, "far")
            consume(oth, cur, (t + 1) * tk)
            return c

        n_far = jnp.maximum(i - 2, 0)
        lax.fori_loop(0, n_far // 2, far_pair, 0)

        def near_diag():
            produce(oth, q_off, (i - 1) * tk, "near")
            consume(cur, cur, (i - 2) * tk)
            produce(cur, q_off, i * tk, "diag")
            consume(oth, cur, (i - 1) * tk)

        def last():
            finalize(oth, jnp.maximum(i - 1, 0) * tq)
            init_stats(oth)
            nxt = jnp.minimum(i + 1, nq - 1)
            produce(oth, nxt * tq, 0, "select", sel=jnp.where(nxt == 1, 1, 0))
            consume(cur, cur, i * tk, diag=True)

        if odd:
            @pl.when(i >= 3)
            def _():
                consume(cur, cur, (n_far - 1) * tk)
                produce(cur, q_off, n_far * tk, "far")
                near_diag()
                last()

            @pl.when(i == 1)
            def _():
                consume(cur, cur, 0)
                produce(cur, q_off, tk, "diag")
                last()
        else:
            @pl.when(i >= 2)
            def _():
                near_diag()
                last()

            @pl.when(i == 0)
            def _():
                last()

    def q_pair(ii, carry):
        q_tile(2 * ii, 0, False)
        q_tile(2 * ii + 1, 1, True)
        return carry

    lax.fori_loop(0, nq // 2, q_pair, 0)
    finalize((nq - 1) % 2, (nq - 1) * tq)


def _attention(qk, qk_meta, vt, v_meta, gates, rel_bias, lam_q1, lam_k1, lam_q2, lam_k2, subln_g):
    _, batch, _, seq, _ = qk.shape
    tq, tk = ATTN_TQ, ATTN_TK
    smem = pl.BlockSpec(memory_space=pltpu.SMEM)
    row64 = pl.BlockSpec((1, HEAD_DIM), lambda h, b: (0, 0))
    qk_spec = lambda which: pl.BlockSpec((1, 1, 1, seq, V_DIM), lambda h, b: (which, b, h, 0, 0))
    meta_spec = lambda which: pl.BlockSpec((1, 1, N_META, V_DIM), lambda h, b: (which, h, 0, 0))

    def kernel(rb, lq1, lk1, lq2, lk2, q_ref, k_ref, vt_ref, km_ref, vm_ref, *rest):
        _attn_kernel(rb, lq1, lk1, lq2, lk2, q_ref.at[0], k_ref.at[0], vt_ref,
                     km_ref.at[0], vm_ref.at[0], *rest)

    return pl.pallas_call(
        kernel,
        grid=(N_HEADS, batch),
        in_specs=[smem, row64, row64, row64, row64,
                  qk_spec(0), qk_spec(1),
                  pl.BlockSpec((1, 1, V_DIM + SUM_ROWS, seq), lambda h, b: (b, h, 0, 0)),
                  meta_spec(1), meta_spec(0),
                  pl.BlockSpec((1, seq, V_DIM), lambda h, b: (0, b, h)),
                  pl.BlockSpec((1, V_DIM), lambda h, b: (0, 0))],
        out_specs=pl.BlockSpec((seq, V_DIM), lambda h, b: (b, h)),
        out_shape=jax.ShapeDtypeStruct((batch * seq, D_ATTN), BF16),
        scratch_shapes=[pltpu.VMEM((V_DIM + SUM_ROWS, V_DIM), BF16),
                        pltpu.VMEM((tk, tq), F32),
                        pltpu.VMEM((2, N_META, tq), F32),
                        pltpu.VMEM((2, tk, tq), F32),
                        pltpu.VMEM((2, 2, tk, tq), F32),
                        pltpu.VMEM((2, 2, N_META, tq), F32),
                        pltpu.VMEM((2, 2, 1, tq), F32),
                        pltpu.VMEM((2, 2, 1, tq), F32),
                        pltpu.VMEM((2, 2, V_DIM + SUM_ROWS, tq), F32)],
        compiler_params=pltpu.CompilerParams(
            dimension_semantics=("arbitrary", "arbitrary"),
            vmem_limit_bytes=VMEM_LIMIT_ATTN),
        name="diff_attention",
    )(rel_bias, lam_q1, lam_k1, lam_q2, lam_k2, qk, qk, vt, qk_meta, v_meta, gates, subln_g)


def _scan_block(a, b):
    n = a.shape[0]
    row = lax.broadcasted_iota(jnp.int32, a.shape, 0)
    s = 1
    while s < n:
        keep = row >= s
        a_sh = jnp.where(keep, pltpu.roll(a, s, 0), 1.0)
        b_sh = jnp.where(keep, pltpu.roll(b, s, 0), 0.0)
        b = b + a * b_sh
        a = a * a_sh
        s *= 2
    return a, b


def _tail_kernel(x_ref, att_ref, u_ref, um_ref, g_ref, cw_ref, cb_ref, wa_ref, wx_ref, ba_ref, bx_ref,
                 lam_ref, w_ref, fg_ref, o_ref, wb_ref, wg_ref, ubuf, h_ref, a_s, b_s, rec_s, rec_prev,
                 *, chunks_per_batch):
    s = pl.program_id(0)
    n = u_ref.shape[0]
    hist = 8
    seg = n // 8

    x = -lam_ref[...]
    softplus = jnp.maximum(x, 0.0) + jnp.log1p(jnp.exp(-jnp.abs(x)))

    @pl.when(s == 0)
    def _():
        wb_ref[...] = w_ref[...].astype(BF16)
        for blk in range(N_LRU_BLOCKS):
            wg_ref[blk, :, pl.ds(0, LRU_BLOCK)] = wa_ref[blk].astype(BF16)
            wg_ref[blk, :, pl.ds(LRU_BLOCK, LRU_BLOCK)] = wx_ref[blk].astype(BF16)
        rec_s[...] = jnp.zeros_like(rec_s)

    def gate_inputs(rows, cols, blk):
        cw = cw_ref[:, cols]
        uc = (cw[3:4] * ubuf[pl.ds(hist, rows), cols] + cw[2:3] * ubuf[pl.ds(hist - 1, rows), cols]
              + cw[1:2] * ubuf[pl.ds(hist - 2, rows), cols] + cw[0:1] * ubuf[pl.ds(hist - 3, rows), cols]
              + cb_ref[:, cols])
        return uc, _dot(uc.astype(BF16), wg_ref[blk])

    def gates(uc, pre, cols, first):
        r = jax.nn.sigmoid(pre[:, :LRU_BLOCK] + ba_ref[:, cols])
        gi = jax.nn.sigmoid(pre[:, LRU_BLOCK:] + bx_ref[:, cols])
        log_a = -LRU_C * r * softplus[:, cols]
        a = jnp.exp(log_a)
        v = jnp.tanh(-log_a) * (a * a + 1.0)
        mult = jnp.where(v > 0.0, v * lax.rsqrt(v), 0.0)
        if first:
            row = lax.broadcasted_iota(jnp.int32, mult.shape, 0)
            mult = jnp.where(row == 0, 1.0, mult)
        return a, mult * gi * uc

    @pl.when(s % chunks_per_batch == 0)
    def _():
        ubuf[pl.ds(0, hist), :] = jnp.zeros((hist, D_LRU), F32)
        ubuf[pl.ds(hist, N_META), :] = um_ref[...]
        for blk in range(N_LRU_BLOCKS):
            cols = slice(blk * LRU_BLOCK, (blk + 1) * LRU_BLOCK)
            a, b = gates(*gate_inputs(N_META, cols, blk), cols, True)
            _, hm = _scan_block(a, b)
            h_ref[:, cols] = hm[N_META - 1:N_META, :]
        ubuf[pl.ds(0, hist), :] = ubuf[pl.ds(N_META, hist), :]

    rec_prev[...] = rec_s[...]
    ubuf[pl.ds(hist, n), :] = u_ref[...]
    sub = lax.broadcasted_iota(jnp.int32, (8, LRU_BLOCK), 0)
    n_out = D_MODEL // N_LRU_BLOCKS
    ssq = jnp.zeros((n, LRU_BLOCK), F32)
    for blk in range(N_LRU_BLOCKS):
        cols = slice(blk * LRU_BLOCK, (blk + 1) * LRU_BLOCK)
        ocols = slice(blk * n_out, (blk + 1) * n_out)
        z = (x_ref[:, ocols] + _dot(att_ref[...], wb_ref[pl.ds(0, D_ATTN), ocols])
             + _dot(rec_prev[...], wb_ref[pl.ds(D_ATTN, D_LRU), ocols]))
        o_ref[:, ocols] = z
        for part in range(n_out // LRU_BLOCK):
            zp = z[:, part * LRU_BLOCK:(part + 1) * LRU_BLOCK]
            ssq = ssq + zp * zp

        a, b = gates(*gate_inputs(n, cols, blk), cols, False)
        for k in range(8):
            a_s[blk, pl.ds(SEG_PITCH * k, seg), :] = a[seg * k:seg * (k + 1)]
            b_s[blk, pl.ds(SEG_PITCH * k, seg), :] = b[seg * k:seg * (k + 1)]
        step = lambda j: pl.ds(j, 8, stride=SEG_PITCH)
        a_run = a_s[blk, step(0), :]
        h_run = b_s[blk, step(0), :]
        for j in range(1, seg):
            aj = a_s[blk, step(j), :]
            h_run = aj * h_run + b_s[blk, step(j), :]
            a_run = aj * a_run
        a_cum, h_cum = _scan_block(a_run, h_run)
        carry = h_ref[:, cols]
        seg_end = h_cum + a_cum * carry
        h_ref[:, cols] = seg_end[7:8, :]
        h_run = jnp.where(sub == 0, carry, pltpu.roll(seg_end, 1, 0))
        for j in range(seg):
            h_run = a_s[blk, step(j), :] * h_run + b_s[blk, step(j), :]
            b_s[blk, step(j), :] = h_run
        g = g_ref[:, cols]
        h = jnp.concatenate([b_s[blk, pl.ds(SEG_PITCH * k, seg), :] for k in range(8)], axis=0)
        rec_s[:, cols] = (h * (g * jax.nn.sigmoid(g))).astype(BF16)
    ubuf[pl.ds(0, hist), :] = ubuf[pl.ds(n, hist), :]
    inv = lax.rsqrt(jnp.sum(ssq, axis=1, keepdims=True) * (1.0 / D_MODEL) + NORM_EPS)
    o_ref[...] = (o_ref[...] * inv) * fg_ref[...]


def _tail(x2d, att, gates, gates_meta, conv_w, conv_b, w_a, b_a, w_x, b_x, lru_lambda, w_out,
          final_g, *, seq):
    rows = x2d.shape[0]
    tm = OUT_ROWS
    n_tiles = rows // tm
    prev = lambda s: (jnp.maximum(s - 1, 0), 0)
    cur = lambda which: (lambda s: (which, jnp.minimum(s, n_tiles - 1), 0))
    row = lambda n: pl.BlockSpec((n, D_LRU), lambda s: (0, 0))
    wspec = pl.BlockSpec((N_LRU_BLOCKS, LRU_BLOCK, LRU_BLOCK), lambda s: (0, 0, 0))

    def kernel(x_ref, att_ref, u_ref, um_ref, g_ref, *rest):
        _tail_kernel(x_ref, att_ref, u_ref.at[0], um_ref.at[0], g_ref.at[0], *rest,
                     chunks_per_batch=seq // tm)

    return pl.pallas_call(
        kernel,
        grid=(n_tiles + 1,),
        in_specs=[pl.BlockSpec((tm, D_MODEL), prev),
                  pl.BlockSpec((tm, D_ATTN), prev),
                  pl.BlockSpec((1, tm, D_LRU), cur(1)),
                  pl.BlockSpec((1, N_META, D_LRU), lambda s: (1, 0, 0)),
                  pl.BlockSpec((1, tm, D_LRU), cur(2)),
                  row(CONV_WIDTH), row(1), wspec, wspec, row(1), row(1), row(1),
                  pl.BlockSpec((D_ATTN + D_LRU, D_MODEL), lambda s: (0, 0),
                               pipeline_mode=pl.Buffered(1)),
                  pl.BlockSpec((1, D_MODEL), lambda s: (0, 0))],
        out_specs=pl.BlockSpec((tm, D_MODEL), prev),
        out_shape=jax.ShapeDtypeStruct((rows, D_MODEL), F32),
        scratch_shapes=[pltpu.VMEM((D_ATTN + D_LRU, D_MODEL), BF16),
                        pltpu.VMEM((N_LRU_BLOCKS, LRU_BLOCK, 2 * LRU_BLOCK), BF16),
                        pltpu.VMEM((8 + tm, D_LRU), F32),
                        pltpu.VMEM((1, D_LRU), F32),
                        pltpu.VMEM((N_LRU_BLOCKS, 8 * SEG_PITCH, LRU_BLOCK), F32),
                        pltpu.VMEM((N_LRU_BLOCKS, 8 * SEG_PITCH, LRU_BLOCK), F32),
                        pltpu.VMEM((tm, D_LRU), BF16),
                        pltpu.VMEM((tm, D_LRU), BF16)],
        compiler_params=pltpu.CompilerParams(
            dimension_semantics=("arbitrary",),
            vmem_limit_bytes=VMEM_LIMIT_OUT),
        name="rglru_out_proj",
    )(x2d, att, gates, gates_meta, gates, conv_w, conv_b, w_a, w_x, b_a, b_x, lru_lambda, w_out,
      final_g)


def kernel(x, meta_tokens, rel_bias, norm_g, w_in, conv_w, conv_b, w_a, b_a, w_x, b_x, lru_lambda,
           lam_q1, lam_k1, lam_q2, lam_k2, subln_g, w_out, final_g):
    batch, seq, _ = x.shape
    x2d = x.reshape(batch * seq, D_MODEL)
    project = functools.partial(_project, w_in=w_in[0], batch=batch, seq=seq)
    vt, v_meta, h2d, h_meta = project((x2d, meta_tokens, norm_g), col_tile0=2, n_slabs=1, mode="vt")
    qk, qk_meta = project((h2d, h_meta), col_tile0=0, n_slabs=2, mode="heads")
    gates, gates_meta = project((h2d, h_meta), col_tile0=3, n_slabs=3, mode="flat")
    att = _attention(qk, qk_meta, vt, v_meta, gates, rel_bias, lam_q1, lam_k1, lam_q2, lam_k2,
                     subln_g)
    out = _tail(x2d, att, gates, gates_meta, conv_w[0], conv_b, w_a[0], b_a, w_x[0], b_x, lru_lambda,
                w_out[0], final_g.reshape(1, D_MODEL), seq=seq)
    return out.reshape(batch, seq, D_MODEL)
```

```python
import functools
import math

import numpy as np
import jax
import jax.numpy as jnp
from jax import lax
from jax.experimental import pallas as pl
from jax.experimental.pallas import tpu as pltpu

D_MODEL = 2048
N_META = 16
D_ATTN = 1024
D_LRU = 1024
N_HEADS = 8
HEAD_DIM = 64
V_DIM = 128
N_LRU_BLOCKS = 8
LRU_BLOCK = 128
CONV_WIDTH = 4
LRU_C = 8.0
N_BUCKETS = 32
MAX_DISTANCE = 128
NORM_EPS = 1e-6
SUBLN_EPS = 1e-5
NEG_INF = -1e30
LAMBDA_INIT = 0.8 - 0.6 * math.exp(-0.3 * 0)
LOG2E = math.log2(math.e)

BF16 = jnp.bfloat16
F32 = jnp.float32

VMEM_LIMIT_PROJ = 56 * 1024 * 1024
VMEM_LIMIT_ATTN = 48 * 1024 * 1024
VMEM_LIMIT_OUT = 56 * 1024 * 1024

PROJ_ROWS = 1024
ATTN_TQ = 512
ATTN_TK = 512
SUM_ROWS = 16
OUT_ROWS = 256
SEG_PITCH = 40


def _bucket_thresholds():
    max_exact = N_BUCKETS // 2
    d = np.arange(0, 4 * MAX_DISTANCE, dtype=np.int64)
    val = (np.log(np.maximum(d, 1).astype(np.float64) / max_exact)
           / math.log(MAX_DISTANCE / max_exact) * (N_BUCKETS - max_exact))
    large = np.minimum(max_exact + np.floor(val + 1e-9).astype(np.int64), N_BUCKETS - 1)
    bucket = np.where(d < max_exact, d, large)
    frac = np.abs(val - np.round(val))
    interior = (d > max_exact) & (d < MAX_DISTANCE)
    assert frac[interior].min() > 1e-3
    assert (np.diff(bucket) >= 0).all() and bucket[MAX_DISTANCE] == N_BUCKETS - 1
    return tuple(int(np.argmax(bucket >= j)) for j in range(1, N_BUCKETS))


BUCKET_THRESHOLDS = _bucket_thresholds()


def _rms(x, g, eps):
    y = x * lax.rsqrt(jnp.mean(x * x, axis=-1, keepdims=True) + eps)
    return y * g


def _dot(a, b):
    return jnp.dot(a, b, preferred_element_type=F32)


def _dot_nt(a, b):
    return lax.dot_general(a, b, (((1,), (1,)), ((), ())), preferred_element_type=F32)


def _proj_kernel(*refs, mode):
    if mode == "vt":
        x_ref, meta_ref, g_ref, w_ref, o_ref, om_ref, h_ref, hm_ref, wb_ref = refs
    else:
        h_ref, hm_ref, w_ref, o_ref, om_ref, wb_ref = refs
    j = pl.program_id(0)
    i = pl.program_id(1)
    out_scale = jnp.where(j == 0, HEAD_DIM ** -0.5 * LOG2E, 1.0).astype(F32) if mode == "heads" else 1.0
    head = lambda y, h: y[:, h * V_DIM:(h + 1) * V_DIM]

    @pl.when(i == 0)
    def _():
        wb_ref[...] = w_ref[...].astype(BF16)
        if mode == "vt":
            hm_ref[...] = _rms(meta_ref[...], g_ref[...], NORM_EPS).astype(BF16)
        ym = _dot(hm_ref[...], wb_ref[...]) * out_scale
        if mode == "flat":
            om_ref[0] = ym
        else:
            for h in range(N_HEADS):
                om_ref[0, h] = head(ym, h).astype(BF16)

    if mode == "vt":
        h_ref[...] = _rms(x_ref[...], g_ref[...], NORM_EPS).astype(BF16)
    y = _dot(h_ref[...], wb_ref[...]) * out_scale
    if mode == "flat":
        o_ref[0] = y
    elif mode == "heads":
        for h in range(N_HEADS):
            o_ref[0, 0, h] = head(y, h).astype(BF16)
    else:
        rows = y.shape[0]
        for h in range(N_HEADS):
            o_ref[0, h, pl.ds(0, V_DIM), :] = head(y, h).T.astype(BF16)
            o_ref[0, h, pl.ds(V_DIM, SUM_ROWS), :] = jnp.ones((SUM_ROWS, rows), BF16)


def _project(inputs, w_in, *, col_tile0, n_slabs, mode, batch, seq):
    rows = inputs[0].shape[0]
    tm = PROJ_ROWS
    n_i = rows // tm
    n_ib = seq // tm
    row_tile = pl.BlockSpec((tm, D_MODEL), lambda j, i: (i, 0))
    meta_rows = pl.BlockSpec((N_META, D_MODEL), lambda j, i: (0, 0))
    if mode == "vt":
        in_specs = [row_tile, meta_rows, pl.BlockSpec((1, D_MODEL), lambda j, i: (0, 0))]
    else:
        in_specs = [row_tile, meta_rows]
    in_specs.append(pl.BlockSpec((D_MODEL, D_ATTN), lambda j, i: (0, j + col_tile0)))
    if mode == "heads":
        out_shape = (jax.ShapeDtypeStruct((n_slabs, batch, N_HEADS, seq, V_DIM), BF16),
                     jax.ShapeDtypeStruct((n_slabs, N_HEADS, N_META, V_DIM), BF16))
        out_specs = (pl.BlockSpec((1, 1, N_HEADS, tm, V_DIM),
                                  lambda j, i: (j, i // n_ib, 0, i % n_ib, 0)),
                     pl.BlockSpec((1, N_HEADS, N_META, V_DIM), lambda j, i: (j, 0, 0, 0)))
    elif mode == "vt":
        assert n_slabs == 1
        out_shape = (jax.ShapeDtypeStruct((batch, N_HEADS, V_DIM + SUM_ROWS, seq), BF16),
                     jax.ShapeDtypeStruct((n_slabs, N_HEADS, N_META, V_DIM), BF16),
                     jax.ShapeDtypeStruct((rows, D_MODEL), BF16),
                     jax.ShapeDtypeStruct((N_META, D_MODEL), BF16))
        out_specs = (pl.BlockSpec((1, N_HEADS, V_DIM + SUM_ROWS, tm),
                                  lambda j, i: (i // n_ib, 0, 0, i % n_ib)),
                     pl.BlockSpec((1, N_HEADS, N_META, V_DIM), lambda j, i: (j, 0, 0, 0)),
                     row_tile, meta_rows)
    else:
        out_shape = (jax.ShapeDtypeStruct((n_slabs, rows, D_ATTN), F32),
                     jax.ShapeDtypeStruct((n_slabs, N_META, D_ATTN), F32))
        out_specs = (pl.BlockSpec((1, tm, D_ATTN), lambda j, i: (j, i, 0)),
                     pl.BlockSpec((1, N_META, D_ATTN), lambda j, i: (j, 0, 0)))
    return pl.pallas_call(
        functools.partial(_proj_kernel, mode=mode),
        grid=(n_slabs, n_i),
        in_specs=in_specs,
        out_specs=out_specs,
        out_shape=out_shape,
        scratch_shapes=[pltpu.VMEM((D_MODEL, D_ATTN), BF16)],
        compiler_params=pltpu.CompilerParams(
            dimension_semantics=("arbitrary", "arbitrary"),
            vmem_limit_bytes=VMEM_LIMIT_PROJ),
        name={"heads": "proj_qk", "vt": "proj_v", "flat": "proj_gates"}[mode],
    )(*inputs, w_in)


def _toeplitz_bias(dist, rb_ref, h, far):
    b = jnp.full(dist.shape, (rb_ref[0, h] - far) * LOG2E, F32)
    for j, thr in enumerate(BUCKET_THRESHOLDS, start=1):
        b = jnp.where(dist >= thr, (rb_ref[j, h] - far) * LOG2E, b)
    return b


def _fill_bias(ref, lead, n_rows, n_cols, d0, rb_ref, h, far):
    sub = lax.broadcasted_iota(jnp.int32, (8, V_DIM), 0)
    lane = lax.broadcasted_iota(jnp.int32, (8, V_DIM), 1)
    zeros = jnp.zeros((8, V_DIM), F32)
    masked = jnp.full((8, V_DIM), NEG_INF, F32)
    cache = {}
    for a8 in range(n_rows // 8):
        for b in range(n_cols // V_DIM):
            off = d0 + V_DIM * b - 8 * a8
            if off + V_DIM - 1 < 0:
                tile = masked
            elif off - 7 >= MAX_DISTANCE:
                tile = zeros
            else:
                if off not in cache:
                    d = off + lane - sub
                    cache[off] = jnp.where(d >= 0, _toeplitz_bias(d, rb_ref, h, far), NEG_INF)
                tile = cache[off]
            ref[(*lead, pl.ds(8 * a8, 8), pl.ds(V_DIM * b, V_DIM))] = tile


def _attn_kernel(rb_ref, lq1_ref, lk1_ref, lq2_ref, lk2_ref, q_ref, k_ref, vt_ref, km_ref, vm_ref,
                 g_ref, sg_ref, o_ref, vmt_ref, bd_ref, bm_ref, bn_ref, sbuf, sc_ref, mx_ref,
                 m_ref, acc_ref):
    h = pl.program_id(0)
    tq, tk = ATTN_TQ, ATTN_TK
    seq = k_ref.shape[2]
    nq = seq // tq

    far = rb_ref[N_BUCKETS - 1, h]

    @pl.when(pl.program_id(1) == 0)
    def _():
        _fill_bias(bd_ref, (), tk, tq, 0, rb_ref, h, far)
        _fill_bias(bn_ref, (), MAX_DISTANCE, MAX_DISTANCE, MAX_DISTANCE, rb_ref, h, far)
        _fill_bias(bm_ref, (1,), N_META, tq, N_META, rb_ref, h, far)
        bm_ref[0] = jnp.zeros((N_META, tq), F32)
        padded = jnp.concatenate(
            [vm_ref[0].astype(F32), jnp.zeros((V_DIM - N_META, V_DIM), F32)], axis=0)
        vmt_ref[pl.ds(0, V_DIM), :] = padded.T.astype(BF16)
        vmt_ref[pl.ds(V_DIM, SUM_ROWS), :] = jnp.ones((SUM_ROWS, V_DIM), BF16)

    lane = lax.broadcasted_iota(jnp.int32, (tq, V_DIM), 1)

    lam = (jnp.exp(jnp.sum(lq1_ref[...] * lk1_ref[...], keepdims=True))
           - jnp.exp(jnp.sum(lq2_ref[...] * lk2_ref[...], keepdims=True))
           + LAMBDA_INIT)

    def init_stats(st):
        m_ref[st] = jnp.full(m_ref.shape[1:], NEG_INF, F32)
        for a in range(2):
            acc_ref[st, a, pl.ds(0, V_DIM), :] = jnp.zeros((V_DIM, tq), F32)
            acc_ref[st, a, pl.ds(V_DIM, SUM_ROWS), :] = jnp.ones((SUM_ROWS, tq), F32)

    hk, hq = tk // 2, tq // 2

    def produce(buf, q_off, k_off, kind):
        q = q_ref[0, 0, pl.ds(pl.multiple_of(q_off, tq), tq), :]
        zero = jnp.zeros_like(q)
        k_off = pl.multiple_of(k_off, tk)
        k_t = k_ref[0, 0, pl.ds(k_off, tk), :]
        for a in range(2):
            qa = jnp.where((lane < HEAD_DIM) if a == 0 else (lane >= HEAD_DIM), q, zero)
            if kind == "diag":
                top = _dot_nt(k_t[:hk], qa) + bd_ref[pl.ds(0, hk), :]
                low = _dot_nt(k_t[hk:], qa[hq:]) + bd_ref[pl.ds(hk, hk), pl.ds(hq, hq)]
                sc = _dot_nt(km_ref[0], qa) + bm_ref[jnp.where(k_off == 0, 1, 0)]
                sbuf[buf, a, pl.ds(0, hk), :] = top
                sbuf[buf, a, pl.ds(hk, hk), pl.ds(hq, hq)] = low
                sc_ref[buf, a] = sc
                mx = jnp.maximum(jnp.max(top, axis=0, keepdims=True),
                                 jnp.max(sc, axis=0, keepdims=True))
                mx_low = jnp.max(low, axis=0, keepdims=True)
                mx_ref[buf, a] = jnp.concatenate(
                    [mx[:, :hq], jnp.maximum(mx[:, hq:], mx_low)], axis=1)
                continue
            s = _dot_nt(k_t, qa)
            if kind == "near":
                band = tk - MAX_DISTANCE
                corner = s[band:, :MAX_DISTANCE] + bn_ref[...]
                sbuf[buf, a, pl.ds(0, band), :] = s[:band]
                sbuf[buf, a, pl.ds(band, MAX_DISTANCE), pl.ds(0, MAX_DISTANCE)] = corner
                sbuf[buf, a, pl.ds(band, MAX_DISTANCE), pl.ds(MAX_DISTANCE, tq - MAX_DISTANCE)] = (
                    s[band:, MAX_DISTANCE:])
                mx = jnp.max(s[:band], axis=0, keepdims=True)
                mx_ref[buf, a] = jnp.concatenate(
                    [jnp.maximum(mx[:, :MAX_DISTANCE], jnp.max(corner, axis=0, keepdims=True)),
                     jnp.maximum(mx[:, MAX_DISTANCE:],
                                 jnp.max(s[band:, MAX_DISTANCE:], axis=0, keepdims=True))], axis=1)
                continue
            sbuf[buf, a] = s
            mx_ref[buf, a] = jnp.max(s, axis=0, keepdims=True)

    def consume(buf, st, k_off, diag=False):
        k_off = pl.multiple_of(k_off, tk)
        v_t = vt_ref[0, 0, :, pl.ds(k_off, tk)]
        for a in range(2):
            m_old = m_ref[st, a]
            m_new = jnp.maximum(m_old, mx_ref[buf, a])
            alpha = jnp.exp2(m_old - m_new)
            if diag:
                top = jnp.exp2(sbuf[buf, a, pl.ds(0, hk), :] - m_new)
                low = jnp.exp2(sbuf[buf, a, pl.ds(hk, hk), pl.ds(hq, hq)] - m_new[:, hq:])
                pc = jnp.exp2(sc_ref[buf, a] - m_new)
                pv = (_dot(v_t[:, :hk], top.astype(BF16))
                      + _dot(vmt_ref[...][:, :N_META], pc.astype(BF16)))
                pv_low = _dot(v_t[:, hk:], low.astype(BF16))
                pv = jnp.concatenate([pv[:, :hq], pv[:, hq:] + pv_low], axis=1)
            else:
                p = jnp.exp2(sbuf[buf, a] - m_new)
                pv = _dot(v_t, p.astype(BF16))
            acc_ref[st, a] = alpha * acc_ref[st, a] + pv
            m_ref[st, a] = m_new

    def finalize(st, q_off):
        q_rows = pl.ds(pl.multiple_of(q_off, tq), tq)
        heads = [acc_ref[st, a, pl.ds(0, V_DIM), :] / acc_ref[st, a, pl.ds(V_DIM, 1), :]
                 for a in range(2)]
        out_t = heads[0] - lam * heads[1]
        inv = lax.rsqrt(jnp.mean(out_t * out_t, axis=0, keepdims=True) + SUBLN_EPS)
        att = ((out_t * inv).T * sg_ref[...]) * (1.0 - LAMBDA_INIT)
        g = g_ref[0, q_rows, :]
        o_ref[q_rows, :] = (att * (g * jax.nn.sigmoid(g))).astype(o_ref.dtype)

    init_stats(0)
    init_stats(1)
    produce(0, 0, 0, "diag")

    def q_tile(i, cur, odd):
        oth = 1 - cur
        q_off = i * tq

        def far_pair(tt, c):
            t = 2 * tt
            produce(oth, q_off, (t + 1) * tk, "far")
            consume(cur, cur, t * tk)
            produce(cur, q_off, (t + 2) * tk, "far")
            consume(oth, cur, (t + 1) * tk)
            return c

        n_far = jnp.maximum(i - 2, 0)
        lax.fori_loop(0, n_far // 2, far_pair, 0)

        def near_diag():
            produce(oth, q_off, (i - 1) * tk, "near")
            consume(cur, cur, (i - 2) * tk)
            produce(cur, q_off, i * tk, "diag")
            consume(oth, cur, (i - 1) * tk)

        def last(next_kind):
            finalize(oth, jnp.maximum(i - 1, 0) * tq)
            init_stats(oth)
            produce(oth, jnp.minimum(i + 1, nq - 1) * tq, 0, next_kind)
            consume(cur, cur, i * tk, diag=True)

        if odd:
            @pl.when(i >= 3)
            def _():
                consume(cur, cur, (n_far - 1) * tk)
                produce(cur, q_off, n_far * tk, "far")
                near_diag()
                last("far")

            @pl.when(i == 1)
            def _():
                consume(cur, cur, 0)
                produce(cur, q_off, tk, "diag")
                last("far")
        else:
            @pl.when(i >= 2)
            def _():
                near_diag()
                last("far")

            @pl.when(i == 0)
            def _():
                last("near")

    def q_pair(ii, carry):
        q_tile(2 * ii, 0, False)
        q_tile(2 * ii + 1, 1, True)
        return carry

    lax.fori_loop(0, nq // 2, q_pair, 0)
    finalize((nq - 1) % 2, (nq - 1) * tq)


def _attention(qk, qk_meta, vt, v_meta, gates, rel_bias, lam_q1, lam_k1, lam_q2, lam_k2, subln_g):
    _, batch, _, seq, _ = qk.shape
    tq, tk = ATTN_TQ, ATTN_TK
    smem = pl.BlockSpec(memory_space=pltpu.SMEM)
    row64 = pl.BlockSpec((1, HEAD_DIM), lambda h, b: (0, 0))
    qk_spec = lambda which: pl.BlockSpec((1, 1, 1, seq, V_DIM), lambda h, b: (which, b, h, 0, 0))
    meta_spec = lambda which: pl.BlockSpec((1, 1, N_META, V_DIM), lambda h, b: (which, h, 0, 0))

    def kernel(rb, lq1, lk1, lq2, lk2, q_ref, k_ref, vt_ref, km_ref, vm_ref, *rest):
        _attn_kernel(rb, lq1, lk1, lq2, lk2, q_ref.at[0], k_ref.at[0], vt_ref,
                     km_ref.at[0], vm_ref.at[0], *rest)

    return pl.pallas_call(
        kernel,
        grid=(N_HEADS, batch),
        in_specs=[smem, row64, row64, row64, row64,
                  qk_spec(0), qk_spec(1),
                  pl.BlockSpec((1, 1, V_DIM + SUM_ROWS, seq), lambda h, b: (b, h, 0, 0)),
                  meta_spec(1), meta_spec(0),
                  pl.BlockSpec((1, seq, V_DIM), lambda h, b: (0, b, h)),
                  pl.BlockSpec((1, V_DIM), lambda h, b: (0, 0))],
        out_specs=pl.BlockSpec((seq, V_DIM), lambda h, b: (b, h)),
        out_shape=jax.ShapeDtypeStruct((batch * seq, D_ATTN), BF16),
        scratch_shapes=[pltpu.VMEM((V_DIM + SUM_ROWS, V_DIM), BF16),
                        pltpu.VMEM((tk, tq), F32),
                        pltpu.VMEM((2, N_META, tq), F32),
                        pltpu.VMEM((MAX_DISTANCE, MAX_DISTANCE), F32),
                        pltpu.VMEM((2, 2, tk, tq), F32),
                        pltpu.VMEM((2, 2, N_META, tq), F32),
                        pltpu.VMEM((2, 2, 1, tq), F32),
                        pltpu.VMEM((2, 2, 1, tq), F32),
                        pltpu.VMEM((2, 2, V_DIM + SUM_ROWS, tq), F32)],
        compiler_params=pltpu.CompilerParams(
            dimension_semantics=("arbitrary", "arbitrary"),
            vmem_limit_bytes=VMEM_LIMIT_ATTN),
        name="diff_attention",
    )(rel_bias, lam_q1, lam_k1, lam_q2, lam_k2, qk, qk, vt, qk_meta, v_meta, gates, subln_g)


def _scan_block(a, b):
    n = a.shape[0]
    row = lax.broadcasted_iota(jnp.int32, a.shape, 0)
    s = 1
    while s < n:
        keep = row >= s
        a_sh = jnp.where(keep, pltpu.roll(a, s, 0), 1.0)
        b_sh = jnp.where(keep, pltpu.roll(b, s, 0), 0.0)
        b = b + a * b_sh
        a = a * a_sh
        s *= 2
    return a, b


def _tail_kernel(x_ref, att_ref, u_ref, um_ref, g_ref, cw_ref, cb_ref, wa_ref, wx_ref, ba_ref, bx_ref,
                 lam_ref, w_ref, fg_ref, o_ref, wb_ref, wg_ref, ubuf, h_ref, a_s, b_s, rec_s, rec_prev,
                 *, chunks_per_batch):
    s = pl.program_id(0)
    n = u_ref.shape[0]
    hist = 8
    seg = n // 8

    x = -lam_ref[...]
    softplus = jnp.maximum(x, 0.0) + jnp.log1p(jnp.exp(-jnp.abs(x)))

    @pl.when(s == 0)
    def _():
        wb_ref[...] = w_ref[...].astype(BF16)
        for blk in range(N_LRU_BLOCKS):
            wg_ref[blk, :, pl.ds(0, LRU_BLOCK)] = wa_ref[blk].astype(BF16)
            wg_ref[blk, :, pl.ds(LRU_BLOCK, LRU_BLOCK)] = wx_ref[blk].astype(BF16)
        rec_s[...] = jnp.zeros_like(rec_s)

    def gate_inputs(rows, cols, blk):
        cw = cw_ref[:, cols]
        uc = (cw[3:4] * ubuf[pl.ds(hist, rows), cols] + cw[2:3] * ubuf[pl.ds(hist - 1, rows), cols]
              + cw[1:2] * ubuf[pl.ds(hist - 2, rows), cols] + cw[0:1] * ubuf[pl.ds(hist - 3, rows), cols]
              + cb_ref[:, cols])
        return uc, _dot(uc.astype(BF16), wg_ref[blk])

    def gates(uc, pre, cols, first):
        r = jax.nn.sigmoid(pre[:, :LRU_BLOCK] + ba_ref[:, cols])
        gi = jax.nn.sigmoid(pre[:, LRU_BLOCK:] + bx_ref[:, cols])
        log_a = -LRU_C * r * softplus[:, cols]
        a = jnp.exp(log_a)
        v = jnp.tanh(-log_a) * (a * a + 1.0)
        mult = jnp.where(v > 0.0, v * lax.rsqrt(v), 0.0)
        if first:
            row = lax.broadcasted_iota(jnp.int32, mult.shape, 0)
            mult = jnp.where(row == 0, 1.0, mult)
        return a, mult * gi * uc

    @pl.when(s % chunks_per_batch == 0)
    def _():
        ubuf[pl.ds(0, hist), :] = jnp.zeros((hist, D_LRU), F32)
        ubuf[pl.ds(hist, N_META), :] = um_ref[...]
        for blk in range(N_LRU_BLOCKS):
            cols = slice(blk * LRU_BLOCK, (blk + 1) * LRU_BLOCK)
            a, b = gates(*gate_inputs(N_META, cols, blk), cols, True)
            _, hm = _scan_block(a, b)
            h_ref[:, cols] = hm[N_META - 1:N_META, :]
        ubuf[pl.ds(0, hist), :] = ubuf[pl.ds(N_META, hist), :]

    rec_prev[...] = rec_s[...]
    ubuf[pl.ds(hist, n), :] = u_ref[...]
    sub = lax.broadcasted_iota(jnp.int32, (8, LRU_BLOCK), 0)
    n_out = D_MODEL // N_LRU_BLOCKS
    ssq = jnp.zeros((n, LRU_BLOCK), F32)
    for blk in range(N_LRU_BLOCKS):
        cols = slice(blk * LRU_BLOCK, (blk + 1) * LRU_BLOCK)
        ocols = slice(blk * n_out, (blk + 1) * n_out)
        z = (x_ref[:, ocols] + _dot(att_ref[...], wb_ref[pl.ds(0, D_ATTN), ocols])
             + _dot(rec_prev[...], wb_ref[pl.ds(D_ATTN, D_LRU), ocols]))
        o_ref[:, ocols] = z
        for part in range(n_out // LRU_BLOCK):
            zp = z[:, part * LRU_BLOCK:(part + 1) * LRU_BLOCK]
            ssq = ssq + zp * zp

        a, b = gates(*gate_inputs(n, cols, blk), cols, False)
        for k in range(8):
            a_s[blk, pl.ds(SEG_PITCH * k, seg), :] = a[seg * k:seg * (k + 1)]
            b_s[blk, pl.ds(SEG_PITCH * k, seg), :] = b[seg * k:seg * (k + 1)]
        step = lambda j: pl.ds(j, 8, stride=SEG_PITCH)
        a_run = a_s[blk, step(0), :]
        h_run = b_s[blk, step(0), :]
        for j in range(1, seg):
            aj = a_s[blk, step(j), :]
            h_run = aj * h_run + b_s[blk, step(j), :]
            a_run = aj * a_run
        a_cum, h_cum = _scan_block(a_run, h_run)
        carry = h_ref[:, cols]
        seg_end = h_cum + a_cum * carry
        h_ref[:, cols] = seg_end[7:8, :]
        h_run = jnp.where(sub == 0, carry, pltpu.roll(seg_end, 1, 0))
        for j in range(seg):
            h_run = a_s[blk, step(j), :] * h_run + b_s[blk, step(j), :]
            b_s[blk, step(j), :] = h_run
        g = g_ref[:, cols]
        h = jnp.concatenate([b_s[blk, pl.ds(SEG_PITCH * k, seg), :] for k in range(8)], axis=0)
        rec_s[:, cols] = (h * (g * jax.nn.sigmoid(g))).astype(BF16)
    ubuf[pl.ds(0, hist), :] = ubuf[pl.ds(n, hist), :]
    inv = lax.rsqrt(jnp.sum(ssq, axis=1, keepdims=True) * (1.0 / D_MODEL) + NORM_EPS)
    o_ref[...] = (o_ref[...] * inv) * fg_ref[...]


def _tail(x2d, att, gates, gates_meta, conv_w, conv_b, w_a, b_a, w_x, b_x, lru_lambda, w_out,
          final_g, *, seq):
    rows = x2d.shape[0]
    tm = OUT_ROWS
    n_tiles = rows // tm
    prev = lambda s: (jnp.maximum(s - 1, 0), 0)
    cur = lambda which: (lambda s: (which, jnp.minimum(s, n_tiles - 1), 0))
    row = lambda n: pl.BlockSpec((n, D_LRU), lambda s: (0, 0))
    wspec = pl.BlockSpec((N_LRU_BLOCKS, LRU_BLOCK, LRU_BLOCK), lambda s: (0, 0, 0))

    def kernel(x_ref, att_ref, u_ref, um_ref, g_ref, *rest):
        _tail_kernel(x_ref, att_ref, u_ref.at[0], um_ref.at[0], g_ref.at[0], *rest,
                     chunks_per_batch=seq // tm)

    return pl.pallas_call(
        kernel,
        grid=(n_tiles + 1,),
        in_specs=[pl.BlockSpec((tm, D_MODEL), prev),
                  pl.BlockSpec((tm, D_ATTN), prev),
                  pl.BlockSpec((1, tm, D_LRU), cur(1)),
                  pl.BlockSpec((1, N_META, D_LRU), lambda s: (1, 0, 0)),
                  pl.BlockSpec((1, tm, D_LRU), cur(2)),
                  row(CONV_WIDTH), row(1), wspec, wspec, row(1), row(1), row(1),
                  pl.BlockSpec((D_ATTN + D_LRU, D_MODEL), lambda s: (0, 0),
                               pipeline_mode=pl.Buffered(1)),
                  pl.BlockSpec((1, D_MODEL), lambda s: (0, 0))],
        out_specs=pl.BlockSpec((tm, D_MODEL), prev),
        out_shape=jax.ShapeDtypeStruct((rows, D_MODEL), F32),
        scratch_shapes=[pltpu.VMEM((D_ATTN + D_LRU, D_MODEL), BF16),
                        pltpu.VMEM((N_LRU_BLOCKS, LRU_BLOCK, 2 * LRU_BLOCK), BF16),
                        pltpu.VMEM((8 + tm, D_LRU), F32),
                        pltpu.VMEM((1, D_LRU), F32),
                        pltpu.VMEM((N_LRU_BLOCKS, 8 * SEG_PITCH, LRU_BLOCK), F32),
                        pltpu.VMEM((N_LRU_BLOCKS, 8 * SEG_PITCH, LRU_BLOCK), F32),
                        pltpu.VMEM((tm, D_LRU), BF16),
                        pltpu.VMEM((tm, D_LRU), BF16)],
        compiler_params=pltpu.CompilerParams(
            dimension_semantics=("arbitrary",),
            vmem_limit_bytes=VMEM_LIMIT_OUT),
        name="rglru_out_proj",
    )(x2d, att, gates, gates_meta, gates, conv_w, conv_b, w_a, w_x, b_a, b_x, lru_lambda, w_out,
      final_g)


def kernel(x, meta_tokens, rel_bias, norm_g, w_in, conv_w, conv_b, w_a, b_a, w_x, b_x, lru_lambda,
           lam_q1, lam_k1, lam_q2, lam_k2, subln_g, w_out, final_g):
    batch, seq, _ = x.shape
    x2d = x.reshape(batch * seq, D_MODEL)
    project = functools.partial(_project, w_in=w_in[0], batch=batch, seq=seq)
    vt, v_meta, h2d, h_meta = project((x2d, meta_tokens, norm_g), col_tile0=2, n_slabs=1, mode="vt")
    qk, qk_meta = project((h2d, h_meta), col_tile0=0, n_slabs=2, mode="heads")
    gates, gates_meta = project((h2d, h_meta), col_tile0=3, n_slabs=3, mode="flat")
    att = _attention(qk, qk_meta, vt, v_meta, gates, rel_bias, lam_q1, lam_k1, lam_q2, lam_k2,
                     subln_g)
    out = _tail(x2d, att, gates, gates_meta, conv_w[0], conv_b, w_a[0], b_a, w_x[0], b_x, lru_lambda,
                w_out[0], final_g.reshape(1, D_MODEL), seq=seq)
    return out.reshape(batch, seq, D_MODEL)
```

```python
import functools
import math

import numpy as np
import jax
import jax.numpy as jnp
from jax import lax
from jax.experimental import pallas as pl
from jax.experimental.pallas import tpu as pltpu

D_MODEL = 2048
N_META = 16
D_ATTN = 1024
D_LRU = 1024
N_HEADS = 8
HEAD_DIM = 64
V_DIM = 128
N_LRU_BLOCKS = 8
LRU_BLOCK = 128
CONV_WIDTH = 4
LRU_C = 8.0
N_BUCKETS = 32
MAX_DISTANCE = 128
NORM_EPS = 1e-6
SUBLN_EPS = 1e-5
NEG_INF = -1e30
LAMBDA_INIT = 0.8 - 0.6 * math.exp(-0.3 * 0)
LOG2E = math.log2(math.e)

BF16 = jnp.bfloat16
F32 = jnp.float32

VMEM_LIMIT_PROJ = 56 * 1024 * 1024
VMEM_LIMIT_ATTN = 48 * 1024 * 1024
VMEM_LIMIT_OUT = 56 * 1024 * 1024

PROJ_ROWS = 1024
ATTN_TQ = 512
ATTN_TK = 512
SUM_ROWS = 16
OUT_ROWS = 256
SEG_PITCH = 40


def _bucket_thresholds():
    max_exact = N_BUCKETS // 2
    d = np.arange(0, 4 * MAX_DISTANCE, dtype=np.int64)
    val = (np.log(np.maximum(d, 1).astype(np.float64) / max_exact)
           / math.log(MAX_DISTANCE / max_exact) * (N_BUCKETS - max_exact))
    large = np.minimum(max_exact + np.floor(val + 1e-9).astype(np.int64), N_BUCKETS - 1)
    bucket = np.where(d < max_exact, d, large)
    frac = np.abs(val - np.round(val))
    interior = (d > max_exact) & (d < MAX_DISTANCE)
    assert frac[interior].min() > 1e-3
    assert (np.diff(bucket) >= 0).all() and bucket[MAX_DISTANCE] == N_BUCKETS - 1
    return tuple(int(np.argmax(bucket >= j)) for j in range(1, N_BUCKETS))


BUCKET_THRESHOLDS = _bucket_thresholds()


def _rms(x, g, eps):
    y = x * lax.rsqrt(jnp.mean(x * x, axis=-1, keepdims=True) + eps)
    return y * g


def _dot(a, b):
    return jnp.dot(a, b, preferred_element_type=F32)


def _dot_nt(a, b):
    return lax.dot_general(a, b, (((1,), (1,)), ((), ())), preferred_element_type=F32)


def _proj_kernel(*refs, mode):
    if mode == "vt":
        x_ref, meta_ref, g_ref, w_ref, o_ref, om_ref, h_ref, hm_ref, wb_ref = refs
    else:
        h_ref, hm_ref, w_ref, o_ref, om_ref, wb_ref = refs
    j = pl.program_id(0)
    i = pl.program_id(1)
    out_scale = jnp.where(j == 0, HEAD_DIM ** -0.5 * LOG2E, 1.0).astype(F32) if mode == "heads" else 1.0
    head = lambda y, h: y[:, h * V_DIM:(h + 1) * V_DIM]

    @pl.when(i == 0)
    def _():
        wb_ref[...] = w_ref[...].astype(BF16)
        if mode == "vt":
            hm_ref[...] = _rms(meta_ref[...], g_ref[...], NORM_EPS).astype(BF16)
        ym = _dot(hm_ref[...], wb_ref[...]) * out_scale
        if mode == "flat":
            om_ref[0] = ym
        else:
            for h in range(N_HEADS):
                om_ref[0, h] = head(ym, h).astype(BF16)

    if mode == "vt":
        h_ref[...] = _rms(x_ref[...], g_ref[...], NORM_EPS).astype(BF16)
    y = _dot(h_ref[...], wb_ref[...]) * out_scale
    if mode == "flat":
        o_ref[0] = y
    elif mode == "heads":
        for h in range(N_HEADS):
            o_ref[0, 0, h] = head(y, h).astype(BF16)
    else:
        rows = y.shape[0]
        for h in range(N_HEADS):
            o_ref[0, h, pl.ds(0, V_DIM), :] = head(y, h).T.astype(BF16)
            o_ref[0, h, pl.ds(V_DIM, SUM_ROWS), :] = jnp.ones((SUM_ROWS, rows), BF16)


def _project(inputs, w_in, *, col_tile0, n_slabs, mode, batch, seq):
    rows = inputs[0].shape[0]
    tm = PROJ_ROWS
    n_i = rows // tm
    n_ib = seq // tm
    row_tile = pl.BlockSpec((tm, D_MODEL), lambda j, i: (i, 0))
    meta_rows = pl.BlockSpec((N_META, D_MODEL), lambda j, i: (0, 0))
    if mode == "vt":
        in_specs = [row_tile, meta_rows, pl.BlockSpec((1, D_MODEL), lambda j, i: (0, 0))]
    else:
        in_specs = [row_tile, meta_rows]
    in_specs.append(pl.BlockSpec((D_MODEL, D_ATTN), lambda j, i: (0, j + col_tile0)))
    if mode == "heads":
        out_shape = (jax.ShapeDtypeStruct((n_slabs, batch, N_HEADS, seq, V_DIM), BF16),
                     jax.ShapeDtypeStruct((n_slabs, N_HEADS, N_META, V_DIM), BF16))
        out_specs = (pl.BlockSpec((1, 1, N_HEADS, tm, V_DIM),
                                  lambda j, i: (j, i // n_ib, 0, i % n_ib, 0)),
                     pl.BlockSpec((1, N_HEADS, N_META, V_DIM), lambda j, i: (j, 0, 0, 0)))
    elif mode == "vt":
        assert n_slabs == 1
        out_shape = (jax.ShapeDtypeStruct((batch, N_HEADS, V_DIM + SUM_ROWS, seq), BF16),
                     jax.ShapeDtypeStruct((n_slabs, N_HEADS, N_META, V_DIM), BF16),
                     jax.ShapeDtypeStruct((rows, D_MODEL), BF16),
                     jax.ShapeDtypeStruct((N_META, D_MODEL), BF16))
        out_specs = (pl.BlockSpec((1, N_HEADS, V_DIM + SUM_ROWS, tm),
                                  lambda j, i: (i // n_ib, 0, 0, i % n_ib)),
                     pl.BlockSpec((1, N_HEADS, N_META, V_DIM), lambda j, i: (j, 0, 0, 0)),
                     row_tile, meta_rows)
    else:
        out_shape = (jax.ShapeDtypeStruct((n_slabs, rows, D_ATTN), F32),
                     jax.ShapeDtypeStruct((n_slabs, N_META, D_ATTN), F32))
        out_specs = (pl.BlockSpec((1, tm, D_ATTN), lambda j, i: (j, i, 0)),
                     pl.BlockSpec((1, N_META, D_ATTN), lambda j, i: (j, 0, 0)))
    return pl.pallas_call(
        functools.partial(_proj_kernel, mode=mode),
        grid=(n_slabs, n_i),
        in_specs=in_specs,
        out_specs=out_specs,
        out_shape=out_shape,
        scratch_shapes=[pltpu.VMEM((D_MODEL, D_ATTN), BF16)],
        compiler_params=pltpu.CompilerParams(
            dimension_semantics=("arbitrary", "arbitrary"),
            vmem_limit_bytes=VMEM_LIMIT_PROJ),
        name={"heads": "proj_qk", "vt": "proj_v", "flat": "proj_gates"}[mode],
    )(*inputs, w_in)


def _toeplitz_bias(dist, rb_ref, h, far):
    b = jnp.full(dist.shape, (rb_ref[0, h] - far) * LOG2E, F32)
    for j, thr in enumerate(BUCKET_THRESHOLDS, start=1):
        b = jnp.where(dist >= thr, (rb_ref[j, h] - far) * LOG2E, b)
    return b


def _fill_bias(ref, lead, n_rows, n_cols, d0, rb_ref, h, far):
    sub = lax.broadcasted_iota(jnp.int32, (8, V_DIM), 0)
    lane = lax.broadcasted_iota(jnp.int32, (8, V_DIM), 1)
    zeros = jnp.zeros((8, V_DIM), F32)
    masked = jnp.full((8, V_DIM), NEG_INF, F32)
    cache = {}
    for a8 in range(n_rows // 8):
        for b in range(n_cols // V_DIM):
            off = d0 + V_DIM * b - 8 * a8
            if off + V_DIM - 1 < 0:
                tile = masked
            elif off - 7 >= MAX_DISTANCE:
                tile = zeros
            else:
                if off not in cache:
                    d = off + lane - sub
                    cache[off] = jnp.where(d >= 0, _toeplitz_bias(d, rb_ref, h, far), NEG_INF)
                tile = cache[off]
            ref[(*lead, pl.ds(8 * a8, 8), pl.ds(V_DIM * b, V_DIM))] = tile


def _attn_kernel(rb_ref, lq1_ref, lk1_ref, lq2_ref, lk2_ref, q_ref, k_ref, vt_ref, km_ref, vm_ref,
                 g_ref, sg_ref, o_ref, vmt_ref, bd_ref, bm_ref, bn_ref, sbuf, sc_ref, mx_ref,
                 m_ref, acc_ref):
    h = pl.program_id(0)
    tq, tk = ATTN_TQ, ATTN_TK
    seq = k_ref.shape[2]
    nq = seq // tq

    far = rb_ref[N_BUCKETS - 1, h]

    @pl.when(pl.program_id(1) == 0)
    def _():
        _fill_bias(bd_ref, (), tk, tq, 0, rb_ref, h, far)
        _fill_bias(bn_ref, (), MAX_DISTANCE, MAX_DISTANCE, MAX_DISTANCE, rb_ref, h, far)
        _fill_bias(bm_ref, (1,), N_META, tq, N_META, rb_ref, h, far)
        bm_ref[0] = jnp.zeros((N_META, tq), F32)
        padded = jnp.concatenate(
            [vm_ref[0].astype(F32), jnp.zeros((V_DIM - N_META, V_DIM), F32)], axis=0)
        vmt_ref[pl.ds(0, V_DIM), :] = padded.T.astype(BF16)
        vmt_ref[pl.ds(V_DIM, SUM_ROWS), :] = jnp.ones((SUM_ROWS, V_DIM), BF16)

    lane = lax.broadcasted_iota(jnp.int32, (tq, V_DIM), 1)

    lam = (jnp.exp(jnp.sum(lq1_ref[...] * lk1_ref[...], keepdims=True))
           - jnp.exp(jnp.sum(lq2_ref[...] * lk2_ref[...], keepdims=True))
           + LAMBDA_INIT)

    def init_stats(st):
        m_ref[st] = jnp.full(m_ref.shape[1:], NEG_INF, F32)
        for a in range(2):
            acc_ref[st, a, pl.ds(0, V_DIM), :] = jnp.zeros((V_DIM, tq), F32)
            acc_ref[st, a, pl.ds(V_DIM, SUM_ROWS), :] = jnp.ones((SUM_ROWS, tq), F32)

    hk, hq = tk // 2, tq // 2

    def produce(buf, q_off, k_off, kind):
        q = q_ref[0, 0, pl.ds(pl.multiple_of(q_off, tq), tq), :]
        zero = jnp.zeros_like(q)
        k_off = pl.multiple_of(k_off, tk)
        k_t = k_ref[0, 0, pl.ds(k_off, tk), :]
        for a in range(2):
            qa = jnp.where((lane < HEAD_DIM) if a == 0 else (lane >= HEAD_DIM), q, zero)
            if kind == "diag":
                top = _dot_nt(k_t[:hk], qa) + bd_ref[pl.ds(0, hk), :]
                low = _dot_nt(k_t[hk:], qa[hq:]) + bd_ref[pl.ds(hk, hk), pl.ds(hq, hq)]
                sc = _dot_nt(km_ref[0], qa) + bm_ref[jnp.where(k_off == 0, 1, 0)]
                sbuf[buf, a, pl.ds(0, hk), :] = top
                sbuf[buf, a, pl.ds(hk, hk), pl.ds(hq, hq)] = low
                sc_ref[buf, a] = sc
                mx = jnp.maximum(jnp.max(top, axis=0, keepdims=True),
                                 jnp.max(sc, axis=0, keepdims=True))
                mx_low = jnp.max(low, axis=0, keepdims=True)
                mx_ref[buf, a] = jnp.concatenate(
                    [mx[:, :hq], jnp.maximum(mx[:, hq:], mx_low)], axis=1)
                continue
            s = _dot_nt(k_t, qa)
            if kind == "near":
                band = tk - MAX_DISTANCE
                corner = s[band:, :MAX_DISTANCE] + bn_ref[...]
                sbuf[buf, a, pl.ds(0, band), :] = s[:band]
                sbuf[buf, a, pl.ds(band, MAX_DISTANCE), pl.ds(0, MAX_DISTANCE)] = corner
                sbuf[buf, a, pl.ds(band, MAX_DISTANCE), pl.ds(MAX_DISTANCE, tq - MAX_DISTANCE)] = (
                    s[band:, MAX_DISTANCE:])
                mx = jnp.max(s[:band], axis=0, keepdims=True)
                mx_ref[buf, a] = jnp.concatenate(
                    [jnp.maximum(mx[:, :MAX_DISTANCE], jnp.max(corner, axis=0, keepdims=True)),
                     jnp.maximum(mx[:, MAX_DISTANCE:],
                                 jnp.max(s[band:, MAX_DISTANCE:], axis=0, keepdims=True))], axis=1)
                continue
            sbuf[buf, a] = s
            mx_ref[buf, a] = jnp.max(s, axis=0, keepdims=True)

    def consume(buf, st, k_off, diag=False):
        k_off = pl.multiple_of(k_off, tk)
        v_t = vt_ref[0, 0, :, pl.ds(k_off, tk)]
        for a in range(2):
            m_old = m_ref[st, a]
            m_new = jnp.maximum(m_old, mx_ref[buf, a])
            alpha = jnp.exp2(m_old - m_new)
            if diag:
                top = jnp.exp2(sbuf[buf, a, pl.ds(0, hk), :] - m_new)
                low = jnp.exp2(sbuf[buf, a, pl.ds(hk, hk), pl.ds(hq, hq)] - m_new[:, hq:])
                pc = jnp.exp2(sc_ref[buf, a] - m_new)
                pv = (_dot(v_t[:, :hk], top.astype(BF16))
                      + _dot(vmt_ref[...][:, :N_META], pc.astype(BF16)))
                pv_low = _dot(v_t[:, hk:], low.astype(BF16))
                pv = jnp.concatenate([pv[:, :hq], pv[:, hq:] + pv_low], axis=1)
            else:
                p = jnp.exp2(sbuf[buf, a] - m_new)
                pv = _dot(v_t, p.astype(BF16))
            acc_ref[st, a] = alpha * acc_ref[st, a] + pv
            m_ref[st, a] = m_new

    def finalize(st, q_off):
        q_rows = pl.ds(pl.multiple_of(q_off, tq), tq)
        heads = [acc_ref[st, a, pl.ds(0, V_DIM), :] / acc_ref[st, a, pl.ds(V_DIM, 1), :]
                 for a in range(2)]
        out_t = heads[0] - lam * heads[1]
        inv = lax.rsqrt(jnp.mean(out_t * out_t, axis=0, keepdims=True) + SUBLN_EPS)
        att = ((out_t * inv).T * sg_ref[...]) * (1.0 - LAMBDA_INIT)
        g = g_ref[0, q_rows, :]
        o_ref[q_rows, :] = (att * (g * jax.nn.sigmoid(g))).astype(o_ref.dtype)

    init_stats(0)
    init_stats(1)
    produce(0, 0, 0, "diag")

    def q_tile(i, cur, odd):
        oth = 1 - cur
        q_off = i * tq

        def far_pair(tt, c):
            t = 2 * tt
            produce(oth, q_off, (t + 1) * tk, "far")
            consume(cur, cur, t * tk)
            produce(cur, q_off, (t + 2) * tk, "far")
            consume(oth, cur, (t + 1) * tk)
            return c

        n_far = jnp.maximum(i - 2, 0)
        lax.fori_loop(0, n_far // 2, far_pair, 0)

        def near_diag():
            produce(oth, q_off, (i - 1) * tk, "near")
            consume(cur, cur, (i - 2) * tk)
            produce(cur, q_off, i * tk, "diag")
            consume(oth, cur, (i - 1) * tk)

        def last(next_kind):
            finalize(oth, jnp.maximum(i - 1, 0) * tq)
            init_stats(oth)
            produce(oth, jnp.minimum(i + 1, nq - 1) * tq, 0, next_kind)
            consume(cur, cur, i * tk, diag=True)

        if odd:
            @pl.when(i >= 3)
            def _():
                consume(cur, cur, (n_far - 1) * tk)
                produce(cur, q_off, n_far * tk, "far")
                near_diag()
                last("far")

            @pl.when(i == 1)
            def _():
                consume(cur, cur, 0)
                produce(cur, q_off, tk, "diag")
                last("far")
        else:
            @pl.when(i >= 2)
            def _():
                near_diag()
                last("far")

            @pl.when(i == 0)
            def _():
                last("near")

    def q_pair(ii, carry):
        q_tile(2 * ii, 0, False)
        q_tile(2 * ii + 1, 1, True)
        return carry

    lax.fori_loop(0, nq // 2, q_pair, 0)
    finalize((nq - 1) % 2, (nq - 1) * tq)


def _attention(qk, qk_meta, vt, v_meta, gates, rel_bias, lam_q1, lam_k1, lam_q2, lam_k2, subln_g):
    _, batch, _, seq, _ = qk.shape
    tq, tk = ATTN_TQ, ATTN_TK
    smem = pl.BlockSpec(memory_space=pltpu.SMEM)
    row64 = pl.BlockSpec((1, HEAD_DIM), lambda h, b: (0, 0))
    qk_spec = lambda which: pl.BlockSpec((1, 1, 1, seq, V_DIM), lambda h, b: (which, b, h, 0, 0))
    meta_spec = lambda which: pl.BlockSpec((1, 1, N_META, V_DIM), lambda h, b: (which, h, 0, 0))

    def kernel(rb, lq1, lk1, lq2, lk2, q_ref, k_ref, vt_ref, km_ref, vm_ref, *rest):
        _attn_kernel(rb, lq1, lk1, lq2, lk2, q_ref.at[0], k_ref.at[0], vt_ref,
                     km_ref.at[0], vm_ref.at[0], *rest)

    return pl.pallas_call(
        kernel,
        grid=(N_HEADS, batch),
        in_specs=[smem, row64, row64, row64, row64,
                  qk_spec(0), qk_spec(1),
                  pl.BlockSpec((1, 1, V_DIM + SUM_ROWS, seq), lambda h, b: (b, h, 0, 0)),
                  meta_spec(1), meta_spec(0),
                  pl.BlockSpec((1, seq, V_DIM), lambda h, b: (0, b, h)),
                  pl.BlockSpec((1, V_DIM), lambda h, b: (0, 0))],
        out_specs=pl.BlockSpec((seq, V_DIM), lambda h, b: (b, h)),
        out_shape=jax.ShapeDtypeStruct((batch * seq, D_ATTN), BF16),
        scratch_shapes=[pltpu.VMEM((V_DIM + SUM_ROWS, V_DIM), BF16),
                        pltpu.VMEM((tk, tq), F32),
                        pltpu.VMEM((2, N_META, tq), F32),
                        pltpu.VMEM((MAX_DISTANCE, MAX_DISTANCE), F32),
                        pltpu.VMEM((2, 2, tk, tq), F32),
                        pltpu.VMEM((2, 2, N_META, tq), F32),
                        pltpu.VMEM((2, 2, 1, tq), F32),
                        pltpu.VMEM((2, 2, 1, tq), F32),
                        pltpu.VMEM((2, 2, V_DIM + SUM_ROWS, tq), F32)],
        compiler_params=pltpu.CompilerParams(
            dimension_semantics=("arbitrary", "arbitrary"),
            vmem_limit_bytes=VMEM_LIMIT_ATTN),
        name="diff_attention",
    )(rel_bias, lam_q1, lam_k1, lam_q2, lam_k2, qk, qk, vt, qk_meta, v_meta, gates, subln_g)


def _scan_block(a, b):
    n = a.shape[0]
    row = lax.broadcasted_iota(jnp.int32, a.shape, 0)
    s = 1
    while s < n:
        keep = row >= s
        a_sh = jnp.where(keep, pltpu.roll(a, s, 0), 1.0)
        b_sh = jnp.where(keep, pltpu.roll(b, s, 0), 0.0)
        b = b + a * b_sh
        a = a * a_sh
        s *= 2
    return a, b


def _tail_kernel(x_ref, att_ref, u_ref, um_ref, g_ref, cw_ref, cb_ref, wa_ref, wx_ref, ba_ref, bx_ref,
                 lam_ref, w_ref, fg_ref, o_ref, wb_ref, wg_ref, ubuf, tail_ref, h_ref, u_p, h_p, rec_s,
                 rec_prev, *, chunks_per_batch):
    s = pl.program_id(0)
    n = u_ref.shape[0]
    hist = 8
    seg = n // 8

    x = -lam_ref[...]
    softplus = jnp.maximum(x, 0.0) + jnp.log1p(jnp.exp(-jnp.abs(x)))

    @pl.when(s == 0)
    def _():
        wb_ref[...] = w_ref[...].astype(BF16)
        for blk in range(N_LRU_BLOCKS):
            wg_ref[blk, :, pl.ds(0, LRU_BLOCK)] = wa_ref[blk].astype(BF16)
            wg_ref[blk, :, pl.ds(LRU_BLOCK, LRU_BLOCK)] = wx_ref[blk].astype(BF16)
        rec_s[...] = jnp.zeros_like(rec_s)

    def gate_inputs(rows, cols, blk):
        cw = cw_ref[:, cols]
        uc = (cw[3:4] * ubuf[pl.ds(hist, rows), cols] + cw[2:3] * ubuf[pl.ds(hist - 1, rows), cols]
              + cw[1:2] * ubuf[pl.ds(hist - 2, rows), cols] + cw[0:1] * ubuf[pl.ds(hist - 3, rows), cols]
              + cb_ref[:, cols])
        return uc, _dot(uc.astype(BF16), wg_ref[blk])

    def gates(uc, pre, cols, first):
        r = jax.nn.sigmoid(pre[:, :LRU_BLOCK] + ba_ref[:, cols])
        gi = jax.nn.sigmoid(pre[:, LRU_BLOCK:] + bx_ref[:, cols])
        log_a = -LRU_C * r * softplus[:, cols]
        a = jnp.exp(log_a)
        v = jnp.tanh(-log_a) * (a * a + 1.0)
        mult = jnp.where(v > 0.0, v * lax.rsqrt(v), 0.0)
        if first:
            row = lax.broadcasted_iota(jnp.int32, mult.shape, 0)
            mult = jnp.where(row == 0, 1.0, mult)
        return a, mult * gi * uc

    @pl.when(s % chunks_per_batch == 0)
    def _():
        ubuf[pl.ds(0, hist), :] = jnp.zeros((hist, D_LRU), F32)
        ubuf[pl.ds(hist, N_META), :] = um_ref[...]
        for blk in range(N_LRU_BLOCKS):
            cols = slice(blk * LRU_BLOCK, (blk + 1) * LRU_BLOCK)
            a, b = gates(*gate_inputs(N_META, cols, blk), cols, True)
            _, hm = _scan_block(a, b)
            h_ref[:, cols] = hm[N_META - 1:N_META, :]
        tail_ref[...] = um_ref[pl.ds(N_META - hist, hist), :]

    rec_prev[...] = rec_s[...]
    sub =lax.broadcasted_iota(jnp.int32, (8, LRU_BLOCK), 0)
    n_out = D_MODEL // N_LRU_BLOCKS
    ssq = jnp.zeros((n, LRU_BLOCK), F32)
    step = lambda j: pl.ds(j, 8, stride=SEG_PITCH)
    for blk in range(N_LRU_BLOCKS):
        cols = slice(blk * LRU_BLOCK, (blk + 1) * LRU_BLOCK)
        ocols = slice(blk * n_out, (blk + 1) * n_out)
        z = (x_ref[:, ocols] + _dot(att_ref[...], wb_ref[pl.ds(0, D_ATTN), ocols])
             + _dot(rec_prev[...], wb_ref[pl.ds(D_ATTN, D_LRU), ocols]))
        o_ref[:, ocols] = z
        for part in range(n_out // LRU_BLOCK):
            zp = z[:, part * LRU_BLOCK:(part + 1) * LRU_BLOCK]
            ssq = ssq + zp * zp

        for k in range(8):
            u_p[blk, pl.ds(SEG_PITCH * k, seg), :] = u_ref[pl.ds(seg * k, seg), cols]
        us = [u_p[blk, step(j), :] for j in range(seg)]
        prev = tail_ref[:, cols]

        def before(back):
            return jnp.where(sub == 0, prev[hist - back:hist - back + 1, :],
                             pltpu.roll(us[seg - back], 1, 0))

        older = {-back: before(back) for back in range(1, CONV_WIDTH)}
        u_at = lambda j: us[j] if j >= 0 else older[j]
        cw = cw_ref[:, cols]
        cb = cb_ref[:, cols]
        uc = jnp.concatenate(
            [cw[3:4] * u_at(j) + cw[2:3] * u_at(j - 1) + cw[1:2] * u_at(j - 2)
             + cw[0:1] * u_at(j - 3) + cb for j in range(seg)], axis=0)
        a, b = gates(uc, _dot(uc.astype(BF16), wg_ref[blk]), cols, False)
        a_j = lambda j: a[8 * j:8 * j + 8]
        b_j = lambda j: b[8 * j:8 * j + 8]
        a_run, h_run = a_j(0), b_j(0)
        for j in range(1, seg):
            h_run = a_j(j) * h_run + b_j(j)
            a_run = a_j(j) * a_run
        a_cum, h_cum = _scan_block(a_run, h_run)
        carry = h_ref[:, cols]
        seg_end = h_cum + a_cum * carry
        h_ref[:, cols] = seg_end[7:8, :]
        h_run = jnp.where(sub == 0, carry, pltpu.roll(seg_end, 1, 0))
        for j in range(seg):
            h_run = a_j(j) * h_run + b_j(j)
            h_p[blk, step(j), :] = h_run
        g = g_ref[:, cols]
        h = jnp.concatenate([h_p[blk, pl.ds(SEG_PITCH * k, seg), :] for k in range(8)], axis=0)
        rec_s[:, cols] = (h * (g * jax.nn.sigmoid(g))).astype(BF16)
    tail_ref[...] = u_ref[pl.ds(n - hist, hist), :]
    inv = lax.rsqrt(jnp.sum(ssq, axis=1, keepdims=True) * (1.0 / D_MODEL) + NORM_EPS)
    o_ref[...] = (o_ref[...] * inv) * fg_ref[...]


def _tail(x2d, att, gates, gates_meta, conv_w, conv_b, w_a, b_a, w_x, b_x, lru_lambda, w_out,
          final_g, *, seq):
    rows = x2d.shape[0]
    tm = OUT_ROWS
    n_tiles = rows // tm
    prev = lambda s: (jnp.maximum(s - 1, 0), 0)
    cur = lambda which: (lambda s: (which, jnp.minimum(s, n_tiles - 1), 0))
    row = lambda n: pl.BlockSpec((n, D_LRU), lambda s: (0, 0))
    wspec = pl.BlockSpec((N_LRU_BLOCKS, LRU_BLOCK, LRU_BLOCK), lambda s: (0, 0, 0))

    def kernel(x_ref, att_ref, u_ref, um_ref, g_ref, *rest):
        _tail_kernel(x_ref, att_ref, u_ref.at[0], um_ref.at[0], g_ref.at[0], *rest,
                     chunks_per_batch=seq // tm)

    return pl.pallas_call(
        kernel,
        grid=(n_tiles + 1,),
        in_specs=[pl.BlockSpec((tm, D_MODEL), prev),
                  pl.BlockSpec((tm, D_ATTN), prev),
                  pl.BlockSpec((1, tm, D_LRU), cur(1)),
                  pl.BlockSpec((1, N_META, D_LRU), lambda s: (1, 0, 0)),
                  pl.BlockSpec((1, tm, D_LRU), cur(2)),
                  row(CONV_WIDTH), row(1), wspec, wspec, row(1), row(1), row(1),
                  pl.BlockSpec((D_ATTN + D_LRU, D_MODEL), lambda s: (0, 0),
                               pipeline_mode=pl.Buffered(1)),
                  pl.BlockSpec((1, D_MODEL), lambda s: (0, 0))],
        out_specs=pl.BlockSpec((tm, D_MODEL), prev),
        out_shape=jax.ShapeDtypeStruct((rows, D_MODEL), F32),
        scratch_shapes=[pltpu.VMEM((D_ATTN + D_LRU, D_MODEL), BF16),
                        pltpu.VMEM((N_LRU_BLOCKS, LRU_BLOCK, 2 * LRU_BLOCK), BF16),
                        pltpu.VMEM((8 + N_META, D_LRU), F32),
                        pltpu.VMEM((8, D_LRU), F32),
                        pltpu.VMEM((1, D_LRU), F32),
                        pltpu.VMEM((N_LRU_BLOCKS, 8 * SEG_PITCH, LRU_BLOCK), F32),
                        pltpu.VMEM((N_LRU_BLOCKS, 8 * SEG_PITCH, LRU_BLOCK), F32),
                        pltpu.VMEM((tm, D_LRU), BF16),
                        pltpu.VMEM((tm, D_LRU), BF16)],
        compiler_params=pltpu.CompilerParams(
            dimension_semantics=("arbitrary",),
            vmem_limit_bytes=VMEM_LIMIT_OUT),
        name="rglru_out_proj",
    )(x2d, att, gates, gates_meta, gates, conv_w, conv_b, w_a, w_x, b_a, b_x, lru_lambda, w_out,
      final_g)


def kernel(x, meta_tokens, rel_bias, norm_g, w_in, conv_w, conv_b, w_a, b_a, w_x, b_x, lru_lambda,
           lam_q1, lam_k1, lam_q2, lam_k2, subln_g, w_out, final_g):
    batch, seq, _ = x.shape
    x2d = x.reshape(batch * seq, D_MODEL)
    project = functools.partial(_project, w_in=w_in[0], batch=batch, seq=seq)
    vt, v_meta, h2d, h_meta = project((x2d, meta_tokens, norm_g), col_tile0=2, n_slabs=1, mode="vt")
    qk, qk_meta = project((h2d, h_meta), col_tile0=0, n_slabs=2, mode="heads")
    gates, gates_meta = project((h2d, h_meta), col_tile0=3, n_slabs=3, mode="flat")
    att = _attention(qk, qk_meta, vt, v_meta, gates, rel_bias, lam_q1, lam_k1, lam_q2, lam_k2,
                     subln_g)
    out = _tail(x2d, att, gates, gates_meta, conv_w[0], conv_b, w_a[0], b_a, w_x[0], b_x, lru_lambda,
                w_out[0], final_g.reshape(1, D_MODEL), seq=seq)
    return out.reshape(batch, seq, D_MODEL)
```

```python
import functools
import math

import numpy as np
import jax
import jax.numpy as jnp
from jax import lax
from jax.experimental import pallas as pl
from jax.experimental.pallas import tpu as pltpu

D_MODEL = 2048
N_META = 16
D_ATTN = 1024
D_LRU = 1024
N_HEADS = 8
HEAD_DIM = 64
V_DIM = 128
N_LRU_BLOCKS = 8
LRU_BLOCK = 128
CONV_WIDTH = 4
LRU_C = 8.0
N_BUCKETS = 32
MAX_DISTANCE = 128
NORM_EPS = 1e-6
SUBLN_EPS = 1e-5
NEG_INF = -1e30
LAMBDA_INIT = 0.8 - 0.6 * math.exp(-0.3 * 0)
LOG2E = math.log2(math.e)

BF16 = jnp.bfloat16
F32 = jnp.float32

VMEM_LIMIT_PROJ = 56 * 1024 * 1024
VMEM_LIMIT_ATTN = 48 * 1024 * 1024
VMEM_LIMIT_OUT = 56 * 1024 * 1024

PROJ_ROWS = 1024
ATTN_TQ = 512
ATTN_TK = 512
SUM_ROWS = 16
OUT_ROWS = 256
SEG_PITCH = 40


def _bucket_thresholds():
    max_exact = N_BUCKETS // 2
    d = np.arange(0, 4 * MAX_DISTANCE, dtype=np.int64)
    val = (np.log(np.maximum(d, 1).astype(np.float64) / max_exact)
           / math.log(MAX_DISTANCE / max_exact) * (N_BUCKETS - max_exact))
    large = np.minimum(max_exact + np.floor(val + 1e-9).astype(np.int64), N_BUCKETS - 1)
    bucket = np.where(d < max_exact, d, large)
    frac = np.abs(val - np.round(val))
    interior = (d > max_exact) & (d < MAX_DISTANCE)
    assert frac[interior].min() > 1e-3
    assert (np.diff(bucket) >= 0).all() and bucket[MAX_DISTANCE] == N_BUCKETS - 1
    return tuple(int(np.argmax(bucket >= j)) for j in range(1, N_BUCKETS))


BUCKET_THRESHOLDS = _bucket_thresholds()


def _rms(x, g, eps):
    y = x * lax.rsqrt(jnp.mean(x * x, axis=-1, keepdims=True) + eps)
    return y * g


def _dot(a, b):
    return jnp.dot(a, b, preferred_element_type=F32)


def _dot_nt(a, b):
    return lax.dot_general(a, b, (((1,), (1,)), ((), ())), preferred_element_type=F32)


def _proj_kernel(*refs, mode):
    if mode == "vt":
        x_ref, meta_ref, g_ref, w_ref, o_ref, om_ref, h_ref, hm_ref, wb_ref = refs
    else:
        h_ref, hm_ref, w_ref, o_ref, om_ref, wb_ref = refs
    j = pl.program_id(0)
    i = pl.program_id(1)
    out_scale = jnp.where(j == 0, HEAD_DIM ** -0.5 * LOG2E, 1.0).astype(F32) if mode == "heads" else 1.0
    head = lambda y, h: y[:, h * V_DIM:(h + 1) * V_DIM]

    @pl.when(i == 0)
    def _():
        wb_ref[...] = w_ref[...].astype(BF16)
        if mode == "vt":
            hm_ref[...] = _rms(meta_ref[...], g_ref[...], NORM_EPS).astype(BF16)
        ym = _dot(hm_ref[...], wb_ref[...]) * out_scale
        if mode == "flat":
            om_ref[0] = ym
        else:
            for h in range(N_HEADS):
                om_ref[0, h] = head(ym, h).astype(BF16)

    if mode == "vt":
        h_ref[...] = _rms(x_ref[...], g_ref[...], NORM_EPS).astype(BF16)
    y = _dot(h_ref[...], wb_ref[...]) * out_scale
    if mode == "flat":
        o_ref[0] = y
    elif mode == "heads":
        for h in range(N_HEADS):
            o_ref[0, 0, h] = head(y, h).astype(BF16)
    else:
        rows = y.shape[0]
        for h in range(N_HEADS):
            o_ref[0, h, pl.ds(0, V_DIM), :] = head(y, h).T.astype(BF16)
            o_ref[0, h, pl.ds(V_DIM, SUM_ROWS), :] = jnp.ones((SUM_ROWS, rows), BF16)


def _project(inputs, w_in, *, col_tile0, n_slabs, mode, batch, seq):
    rows = inputs[0].shape[0]
    tm = PROJ_ROWS
    n_i = rows // tm
    n_ib = seq // tm
    row_tile = pl.BlockSpec((tm, D_MODEL), lambda j, i: (i, 0))
    meta_rows = pl.BlockSpec((N_META, D_MODEL), lambda j, i: (0, 0))
    if mode == "vt":
        in_specs = [row_tile, meta_rows, pl.BlockSpec((1, D_MODEL), lambda j, i: (0, 0))]
    else:
        in_specs = [row_tile, meta_rows]
    in_specs.append(pl.BlockSpec((D_MODEL, D_ATTN), lambda j, i: (0, j + col_tile0)))
    if mode == "heads":
        out_shape = (jax.ShapeDtypeStruct((n_slabs, batch, N_HEADS, seq, V_DIM), BF16),
                     jax.ShapeDtypeStruct((n_slabs, N_HEADS, N_META, V_DIM), BF16))
        out_specs = (pl.BlockSpec((1, 1, N_HEADS, tm, V_DIM),
                                  lambda j, i: (j, i // n_ib, 0, i % n_ib, 0)),
                     pl.BlockSpec((1, N_HEADS, N_META, V_DIM), lambda j, i: (j, 0, 0, 0)))
    elif mode == "vt":
        assert n_slabs == 1
        out_shape = (jax.ShapeDtypeStruct((batch, N_HEADS, V_DIM + SUM_ROWS, seq), BF16),
                     jax.ShapeDtypeStruct((n_slabs, N_HEADS, N_META, V_DIM), BF16),
                     jax.ShapeDtypeStruct((rows, D_MODEL), BF16),
                     jax.ShapeDtypeStruct((N_META, D_MODEL), BF16))
        out_specs = (pl.BlockSpec((1, N_HEADS, V_DIM + SUM_ROWS, tm),
                                  lambda j, i: (i // n_ib, 0, 0, i % n_ib)),
                     pl.BlockSpec((1, N_HEADS, N_META, V_DIM), lambda j, i: (j, 0, 0, 0)),
                     row_tile, meta_rows)
    else:
        out_shape = (jax.ShapeDtypeStruct((n_slabs, rows, D_ATTN), F32),
                     jax.ShapeDtypeStruct((n_slabs, N_META, D_ATTN), F32))
        out_specs = (pl.BlockSpec((1, tm, D_ATTN), lambda j, i: (j, i, 0)),
                     pl.BlockSpec((1, N_META, D_ATTN), lambda j, i: (j, 0, 0)))
    return pl.pallas_call(
        functools.partial(_proj_kernel, mode=mode),
        grid=(n_slabs, n_i),
        in_specs=in_specs,
        out_specs=out_specs,
        out_shape=out_shape,
        scratch_shapes=[pltpu.VMEM((D_MODEL, D_ATTN), BF16)],
        compiler_params=pltpu.CompilerParams(
            dimension_semantics=("arbitrary", "arbitrary"),
            vmem_limit_bytes=VMEM_LIMIT_PROJ),
        name={"heads": "proj_qk", "vt": "proj_v", "flat": "proj_gates"}[mode],
    )(*inputs, w_in)


def _toeplitz_bias(dist, rb_ref, h, far):
    b = jnp.full(dist.shape, (rb_ref[0, h] - far) * LOG2E, F32)
    for j, thr in enumerate(BUCKET_THRESHOLDS, start=1):
        b = jnp.where(dist >= thr, (rb_ref[j, h] - far) * LOG2E, b)
    return b


def _fill_bias(ref, lead, n_rows, n_cols, d0, rb_ref, h, far):
    sub = lax.broadcasted_iota(jnp.int32, (8, V_DIM), 0)
    lane = lax.broadcasted_iota(jnp.int32, (8, V_DIM), 1)
    zeros = jnp.zeros((8, V_DIM), F32)
    masked = jnp.full((8, V_DIM), NEG_INF, F32)
    cache = {}
    for a8 in range(n_rows // 8):
        for b in range(n_cols // V_DIM):
            off = d0 + V_DIM * b - 8 * a8
            if off + V_DIM - 1 < 0:
                tile = masked
            elif off - 7 >= MAX_DISTANCE:
                tile = zeros
            else:
                if off not in cache:
                    d = off + lane - sub
                    cache[off] = jnp.where(d >= 0, _toeplitz_bias(d, rb_ref, h, far), NEG_INF)
                tile = cache[off]
            ref[(*lead, pl.ds(8 * a8, 8), pl.ds(V_DIM * b, V_DIM))] = tile


def _attn_kernel(rb_ref, lq1_ref, lk1_ref, lq2_ref, lk2_ref, q_ref, k_ref, vt_ref, km_ref, vm_ref,
                 g_ref, sg_ref, o_ref, vmt_ref, bd_ref, bm_ref, bn_ref, sbuf, sc_ref, mx_ref,
                 m_ref, acc_ref):
    h = pl.program_id(0)
    tq, tk = ATTN_TQ, ATTN_TK
    seq = k_ref.shape[2]
    nq = seq // tq

    far = rb_ref[N_BUCKETS - 1, h]

    @pl.when(pl.program_id(1) == 0)
    def _():
        _fill_bias(bd_ref, (), tk, tq, 0, rb_ref, h, far)
        _fill_bias(bn_ref, (), MAX_DISTANCE, MAX_DISTANCE, MAX_DISTANCE, rb_ref, h, far)
        _fill_bias(bm_ref, (1,), N_META, tq, N_META, rb_ref, h, far)
        bm_ref[0] = jnp.zeros((N_META, tq), F32)
        padded = jnp.concatenate(
            [vm_ref[0].astype(F32), jnp.zeros((V_DIM - N_META, V_DIM), F32)], axis=0)
        vmt_ref[pl.ds(0, V_DIM), :] = padded.T.astype(BF16)
        vmt_ref[pl.ds(V_DIM, SUM_ROWS), :] = jnp.ones((SUM_ROWS, V_DIM), BF16)

    lane = lax.broadcasted_iota(jnp.int32, (tq, V_DIM), 1)

    lam = (jnp.exp(jnp.sum(lq1_ref[...] * lk1_ref[...], keepdims=True))
           - jnp.exp(jnp.sum(lq2_ref[...] * lk2_ref[...], keepdims=True))
           + LAMBDA_INIT)

    def init_stats(st):
        m_ref[st] = jnp.full(m_ref.shape[1:], NEG_INF, F32)
        for a in range(2):
            acc_ref[st, a, pl.ds(0, V_DIM), :] = jnp.zeros((V_DIM, tq), F32)
            acc_ref[st, a, pl.ds(V_DIM, SUM_ROWS), :] = jnp.ones((SUM_ROWS, tq), F32)

    hk, hq = tk // 2, tq // 2

    def produce(buf, q_off, k_off, kind):
        q = q_ref[0, 0, pl.ds(pl.multiple_of(q_off, tq), tq), :]
        zero = jnp.zeros_like(q)
        k_off = pl.multiple_of(k_off, tk)
        k_t = k_ref[0, 0, pl.ds(k_off, tk), :]
        for a in range(2):
            qa = jnp.where((lane < HEAD_DIM) if a == 0 else (lane >= HEAD_DIM), q, zero)
            if kind == "diag":
                top = _dot_nt(k_t[:hk], qa) + bd_ref[pl.ds(0, hk), :]
                low = _dot_nt(k_t[hk:], qa[hq:]) + bd_ref[pl.ds(hk, hk), pl.ds(hq, hq)]
                sc = _dot_nt(km_ref[0], qa) + bm_ref[jnp.where(k_off == 0, 1, 0)]
                sbuf[buf, a, pl.ds(0, hk), :] = top
                sbuf[buf, a, pl.ds(hk, hk), pl.ds(hq, hq)] = low
                sc_ref[buf, a] = sc
                mx = jnp.maximum(jnp.max(top, axis=0, keepdims=True),
                                 jnp.max(sc, axis=0, keepdims=True))
                mx_low = jnp.max(low, axis=0, keepdims=True)
                mx_ref[buf, a] = jnp.concatenate(
                    [mx[:, :hq], jnp.maximum(mx[:, hq:], mx_low)], axis=1)
                continue
            s = _dot_nt(k_t, qa)
            if kind == "near":
                band = tk - MAX_DISTANCE
                corner = s[band:, :MAX_DISTANCE] + bn_ref[...]
                sbuf[buf, a, pl.ds(0, band), :] = s[:band]
                sbuf[buf, a, pl.ds(band, MAX_DISTANCE), pl.ds(0, MAX_DISTANCE)] = corner
                sbuf[buf, a, pl.ds(band, MAX_DISTANCE), pl.ds(MAX_DISTANCE, tq - MAX_DISTANCE)] = (
                    s[band:, MAX_DISTANCE:])
                mx = jnp.max(s[:band], axis=0, keepdims=True)
                mx_ref[buf, a] = jnp.concatenate(
                    [jnp.maximum(mx[:, :MAX_DISTANCE], jnp.max(corner, axis=0, keepdims=True)),
                     jnp.maximum(mx[:, MAX_DISTANCE:],
                                 jnp.max(s[band:, MAX_DISTANCE:], axis=0, keepdims=True))], axis=1)
                continue
            sbuf[buf, a] = s
            mx_ref[buf, a] = jnp.max(s, axis=0, keepdims=True)

    def consume(buf, st, k_off, diag=False):
        k_off = pl.multiple_of(k_off, tk)
        v_t = vt_ref[0, 0, :, pl.ds(k_off, tk)]
        for a in range(2):
            m_old = m_ref[st, a]
            m_new = jnp.maximum(m_old, mx_ref[buf, a])
            alpha = jnp.exp2(m_old - m_new)
            if diag:
                top = jnp.exp2(sbuf[buf, a, pl.ds(0, hk), :] - m_new)
                low = jnp.exp2(sbuf[buf, a, pl.ds(hk, hk), pl.ds(hq, hq)] - m_new[:, hq:])
                pc = jnp.exp2(sc_ref[buf, a] - m_new)
                pv = (_dot(v_t[:, :hk], top.astype(BF16))
                      + _dot(vmt_ref[...][:, :N_META], pc.astype(BF16)))
                pv_low = _dot(v_t[:, hk:], low.astype(BF16))
                pv = jnp.concatenate([pv[:, :hq], pv[:, hq:] + pv_low], axis=1)
            else:
                p = jnp.exp2(sbuf[buf, a] - m_new)
                pv = _dot(v_t, p.astype(BF16))
            acc_ref[st, a] = alpha * acc_ref[st, a] + pv
            m_ref[st, a] = m_new

    def finalize(st, q_off):
        q_rows = pl.ds(pl.multiple_of(q_off, tq), tq)
        heads = [acc_ref[st, a, pl.ds(0, V_DIM), :] / acc_ref[st, a, pl.ds(V_DIM, 1), :]
                 for a in range(2)]
        out_t = heads[0] - lam * heads[1]
        inv = lax.rsqrt(jnp.mean(out_t * out_t, axis=0, keepdims=True) + SUBLN_EPS)
        att = ((out_t * inv).T * sg_ref[...]) * (1.0 - LAMBDA_INIT)
        g = g_ref[0, q_rows, :]
        o_ref[q_rows, :] = (att * (g * jax.nn.sigmoid(g))).astype(o_ref.dtype)

    init_stats(0)
    init_stats(1)
    produce(0, 0, 0, "diag")

    def q_tile(i, cur, odd):
        oth = 1 - cur
        q_off = i * tq

        def far_pair(tt, c):
            t = 2 * tt
            produce(oth, q_off, (t + 1) * tk, "far")
            consume(cur, cur, t * tk)
            produce(cur, q_off, (t + 2) * tk, "far")
            consume(oth, cur, (t + 1) * tk)
            return c

        n_far = jnp.maximum(i - 2, 0)
        lax.fori_loop(0, n_far // 2, far_pair, 0)

        def near_diag():
            produce(oth, q_off, (i - 1) * tk, "near")
            consume(cur, cur, (i - 2) * tk)
            produce(cur, q_off, i * tk, "diag")
            consume(oth, cur, (i - 1) * tk)

        def last(next_kind):
            finalize(oth, jnp.maximum(i - 1, 0) * tq)
            init_stats(oth)
            produce(oth, jnp.minimum(i + 1, nq - 1) * tq, 0, next_kind)
            consume(cur, cur, i * tk, diag=True)

        if odd:
            @pl.when(i >= 3)
            def _():
                consume(cur, cur, (n_far - 1) * tk)
                produce(cur, q_off, n_far * tk, "far")
                near_diag()
                last("far")

            @pl.when(i == 1)
            def _():
                consume(cur, cur, 0)
                produce(cur, q_off, tk, "diag")
                last("far")
        else:
            @pl.when(i >= 2)
            def _():
                near_diag()
                last("far")

            @pl.when(i == 0)
            def _():
                last("near")

    def q_pair(ii, carry):
        q_tile(2 * ii, 0, False)
        q_tile(2 * ii + 1, 1, True)
        return carry

    lax.fori_loop(0, nq // 2, q_pair, 0)
    finalize((nq - 1) % 2, (nq - 1) * tq)


def _attention(qk, qk_meta, vt, v_meta, gates, rel_bias, lam_q1, lam_k1, lam_q2, lam_k2, subln_g):
    _, batch, _, seq, _ = qk.shape
    tq, tk = ATTN_TQ, ATTN_TK
    smem = pl.BlockSpec(memory_space=pltpu.SMEM)
    row64 = pl.BlockSpec((1, HEAD_DIM), lambda h, b: (0, 0))
    qk_spec = lambda which: pl.BlockSpec((1, 1, 1, seq, V_DIM), lambda h, b: (which, b, h, 0, 0))
    meta_spec = lambda which: pl.BlockSpec((1, 1, N_META, V_DIM), lambda h, b: (which, h, 0, 0))

    def kernel(rb, lq1, lk1, lq2, lk2, q_ref, k_ref, vt_ref, km_ref, vm_ref, *rest):
        _attn_kernel(rb, lq1, lk1, lq2, lk2, q_ref.at[0], k_ref.at[0], vt_ref,
                     km_ref.at[0], vm_ref.at[0], *rest)

    return pl.pallas_call(
        kernel,
        grid=(N_HEADS, batch),
        in_specs=[smem, row64, row64, row64, row64,
                  qk_spec(0), qk_spec(1),
                  pl.BlockSpec((1, 1, V_DIM + SUM_ROWS, seq), lambda h, b: (b, h, 0, 0)),
                  meta_spec(1), meta_spec(0),
                  pl.BlockSpec((1, seq, V_DIM), lambda h, b: (0, b, h)),
                  pl.BlockSpec((1, V_DIM), lambda h, b: (0, 0))],
        out_specs=pl.BlockSpec((seq, V_DIM), lambda h, b: (b, h)),
        out_shape=jax.ShapeDtypeStruct((batch * seq, D_ATTN), BF16),
        scratch_shapes=[pltpu.VMEM((V_DIM + SUM_ROWS, V_DIM), BF16),
                        pltpu.VMEM((tk, tq), F32),
                        pltpu.VMEM((2, N_META, tq), F32),
                        pltpu.VMEM((MAX_DISTANCE, MAX_DISTANCE), F32),
                        pltpu.VMEM((2, 2, tk, tq), F32),
                        pltpu.VMEM((2, 2, N_META, tq), F32),
                        pltpu.VMEM((2, 2, 1, tq), F32),
                        pltpu.VMEM((2, 2, 1, tq), F32),
                        pltpu.VMEM((2, 2, V_DIM + SUM_ROWS, tq), F32)],
        compiler_params=pltpu.CompilerParams(
            dimension_semantics=("arbitrary", "arbitrary"),
            vmem_limit_bytes=VMEM_LIMIT_ATTN),
        name="diff_attention",
    )(rel_bias, lam_q1, lam_k1, lam_q2, lam_k2, qk, qk, vt, qk_meta, v_meta, gates, subln_g)


def _scan_block(a, b):
    n = a.shape[0]
    row = lax.broadcasted_iota(jnp.int32, a.shape, 0)
    s = 1
    while s < n:
        keep = row >= s
        a_sh = jnp.where(keep, pltpu.roll(a, s, 0), 1.0)
        b_sh = jnp.where(keep, pltpu.roll(b, s, 0), 0.0)
        b = b + a * b_sh
        a = a * a_sh
        s *= 2
    return a, b


def _tail_kernel(x_ref, att_ref, u_ref, um_ref, g_ref, cw_ref, cb_ref, wa_ref, wx_ref, ba_ref, bx_ref,
                 lam_ref, w_ref, fg_ref, o_ref, wb_ref, wg_ref, ubuf, tail_ref, h_ref, u_p, h_p, rec_s,
                 rec_prev, z_ref, *, chunks_per_batch):
    s = pl.program_id(0)
    n = u_ref.shape[0]
    hist = 8
    seg = n // 8

    x = -lam_ref[...]
    softplus = jnp.maximum(x, 0.0) + jnp.log1p(jnp.exp(-jnp.abs(x)))

    @pl.when(s == 0)
    def _():
        wb_ref[...] = w_ref[...].astype(BF16)
        for blk in range(N_LRU_BLOCKS):
            wg_ref[blk, :, pl.ds(0, LRU_BLOCK)] = wa_ref[blk].astype(BF16)
            wg_ref[blk, :, pl.ds(LRU_BLOCK, LRU_BLOCK)] = wx_ref[blk].astype(BF16)
        rec_s[...] = jnp.zeros_like(rec_s)
        z_ref[...] = jnp.zeros_like(z_ref)

    def gate_inputs(rows, cols, blk):
        cw = cw_ref[:, cols]
        uc = (cw[3:4] * ubuf[pl.ds(hist, rows), cols] + cw[2:3] * ubuf[pl.ds(hist - 1, rows), cols]
              + cw[1:2] * ubuf[pl.ds(hist - 2, rows), cols] + cw[0:1] * ubuf[pl.ds(hist - 3, rows), cols]
              + cb_ref[:, cols])
        return uc, _dot(uc.astype(BF16), wg_ref[blk])

    def gates(uc, pre, cols, first):
        r = jax.nn.sigmoid(pre[:, :LRU_BLOCK] + ba_ref[:, cols])
        gi = jax.nn.sigmoid(pre[:, LRU_BLOCK:] + bx_ref[:, cols])
        log_a = -LRU_C * r * softplus[:, cols]
        a = jnp.exp(log_a)
        v = jnp.tanh(-log_a) * (a * a + 1.0)
        mult = jnp.where(v > 0.0, v * lax.rsqrt(v), 0.0)
        if first:
            row = lax.broadcasted_iota(jnp.int32, mult.shape, 0)
            mult = jnp.where(row == 0, 1.0, mult)
        return a, mult * gi * uc

    @pl.when(s % chunks_per_batch == 0)
    def _():
        ubuf[pl.ds(0, hist), :] = jnp.zeros((hist, D_LRU), F32)
        ubuf[pl.ds(hist, N_META), :] = um_ref[...]
        for blk in range(N_LRU_BLOCKS):
            cols = slice(blk * LRU_BLOCK, (blk + 1) * LRU_BLOCK)
            a, b = gates(*gate_inputs(N_META, cols, blk), cols, True)
            _, hm = _scan_block(a, b)
            h_ref[:, cols] = hm[N_META - 1:N_META, :]
        tail_ref[...] = um_ref[pl.ds(N_META - hist, hist), :]

    rec_prev[...] = rec_s[...]
    sub = lax.broadcasted_iota(jnp.int32, (8, LRU_BLOCK), 0)
    n_out = D_MODEL // N_LRU_BLOCKS
    ssq = jnp.zeros((n, LRU_BLOCK), F32)
    step = lambda j: pl.ds(j, 8, stride=SEG_PITCH)
    for blk in range(N_LRU_BLOCKS):
        cols = slice(blk * LRU_BLOCK, (blk + 1) * LRU_BLOCK)
        ocols = slice(blk * n_out, (blk + 1) * n_out)
        z = z_ref[:, ocols] + _dot(rec_prev[...], wb_ref[pl.ds(D_ATTN, D_LRU), ocols])
        o_ref[:, ocols] = z
        for part in range(n_out // LRU_BLOCK):
            zp = z[:, part * LRU_BLOCK:(part + 1) * LRU_BLOCK]
            ssq = ssq + zp * zp

        for k in range(8):
            u_p[blk, pl.ds(SEG_PITCH * k, seg), :] = u_ref[pl.ds(seg * k, seg), cols]
        us = [u_p[blk, step(j), :] for j in range(seg)]
        prev = tail_ref[:, cols]

        def before(back):
            return jnp.where(sub == 0, prev[hist - back:hist - back + 1, :],
                             pltpu.roll(us[seg - back], 1, 0))

        older = {-back: before(back) for back in range(1, CONV_WIDTH)}
        u_at = lambda j: us[j] if j >= 0 else older[j]
        cw = cw_ref[:, cols]
        cb = cb_ref[:, cols]
        uc = jnp.concatenate(
            [cw[3:4] * u_at(j) + cw[2:3] * u_at(j - 1) + cw[1:2] * u_at(j - 2)
             + cw[0:1] * u_at(j - 3) + cb for j in range(seg)], axis=0)
        a, b = gates(uc, _dot(uc.astype(BF16), wg_ref[blk]), cols, False)
        z_ref[:, ocols] = x_ref[:, ocols] + _dot(att_ref[...], wb_ref[pl.ds(0, D_ATTN), ocols])
        a_j = lambda j: a[8 * j:8 * j + 8]
        b_j = lambda j: b[8 * j:8 * j + 8]
        a_run, h_run = a_j(0), b_j(0)
        for j in range(1, seg):
            h_run = a_j(j) * h_run + b_j(j)
            a_run = a_j(j) * a_run
        a_cum, h_cum = _scan_block(a_run, h_run)
        carry = h_ref[:, cols]
        seg_end = h_cum + a_cum * carry
        h_ref[:, cols] = seg_end[7:8, :]
        h_run = jnp.where(sub == 0, carry, pltpu.roll(seg_end, 1, 0))
        for j in range(seg):
            h_run = a_j(j) * h_run + b_j(j)
            h_p[blk, step(j), :] = h_run
        g = g_ref[:, cols]
        h = jnp.concatenate([h_p[blk, pl.ds(SEG_PITCH * k, seg), :] for k in range(8)], axis=0)
        rec_s[:, cols] = (h * (g * jax.nn.sigmoid(g))).astype(BF16)
    tail_ref[...] = u_ref[pl.ds(n - hist, hist), :]
    inv = lax.rsqrt(jnp.sum(ssq, axis=1, keepdims=True) * (1.0 / D_MODEL) + NORM_EPS)
    o_ref[...] = (o_ref[...] * inv) * fg_ref[...]


def _tail(x2d, att, gates, gates_meta, conv_w, conv_b, w_a, b_a, w_x, b_x, lru_lambda, w_out,
          final_g, *, seq):
    rows = x2d.shape[0]
    tm = OUT_ROWS
    n_tiles = rows // tm
    prev = lambda s: (jnp.maximum(s - 1, 0), 0)
    cur = lambda which: (lambda s: (which, jnp.minimum(s, n_tiles - 1), 0))
    row = lambda n: pl.BlockSpec((n, D_LRU), lambda s: (0, 0))
    wspec = pl.BlockSpec((N_LRU_BLOCKS, LRU_BLOCK, LRU_BLOCK), lambda s: (0, 0, 0))

    def kernel(x_ref, att_ref, u_ref, um_ref, g_ref, *rest):
        _tail_kernel(x_ref, att_ref, u_ref.at[0], um_ref.at[0], g_ref.at[0], *rest,
                     chunks_per_batch=seq // tm)

    return pl.pallas_call(
        kernel,
        grid=(n_tiles + 1,),
        in_specs=[pl.BlockSpec((tm, D_MODEL), lambda s: (jnp.minimum(s, n_tiles - 1), 0)),
                  pl.BlockSpec((tm, D_ATTN), lambda s: (jnp.minimum(s, n_tiles - 1), 0)),
                  pl.BlockSpec((1, tm, D_LRU), cur(1)),
                  pl.BlockSpec((1, N_META, D_LRU), lambda s: (1, 0, 0)),
                  pl.BlockSpec((1, tm, D_LRU), cur(2)),
                  row(CONV_WIDTH), row(1), wspec, wspec, row(1), row(1), row(1),
                  pl.BlockSpec((D_ATTN + D_LRU, D_MODEL), lambda s: (0, 0),
                               pipeline_mode=pl.Buffered(1)),
                  pl.BlockSpec((1, D_MODEL), lambda s: (0, 0))],
        out_specs=pl.BlockSpec((tm, D_MODEL), prev),
        out_shape=jax.ShapeDtypeStruct((rows, D_MODEL), F32),
        scratch_shapes=[pltpu.VMEM((D_ATTN + D_LRU, D_MODEL), BF16),
                        pltpu.VMEM((N_LRU_BLOCKS, LRU_BLOCK, 2 * LRU_BLOCK), BF16),
                        pltpu.VMEM((8 + N_META, D_LRU), F32),
                        pltpu.VMEM((8, D_LRU), F32),
                        pltpu.VMEM((1, D_LRU), F32),
                        pltpu.VMEM((N_LRU_BLOCKS, 8 * SEG_PITCH, LRU_BLOCK), F32),
                        pltpu.VMEM((N_LRU_BLOCKS, 8 * SEG_PITCH, LRU_BLOCK), F32),
                        pltpu.VMEM((tm, D_LRU), BF16),
                        pltpu.VMEM((tm, D_LRU), BF16),
                        pltpu.VMEM((tm, D_MODEL), F32)],
        compiler_params=pltpu.CompilerParams(
            dimension_semantics=("arbitrary",),
            vmem_limit_bytes=VMEM_LIMIT_OUT),
        name="rglru_out_proj",
    )(x2d, att, gates, gates_meta, gates, conv_w, conv_b, w_a, w_x, b_a, b_x, lru_lambda, w_out,
      final_g)


def kernel(x, meta_tokens, rel_bias, norm_g, w_in, conv_w, conv_b, w_a, b_a, w_x, b_x, lru_lambda,
           lam_q1, lam_k1, lam_q2, lam_k2, subln_g, w_out, final_g):
    batch, seq, _ = x.shape
    x2d = x.reshape(batch * seq, D_MODEL)
    project = functools.partial(_project, w_in=w_in[0], batch=batch, seq=seq)
    vt, v_meta, h2d, h_meta = project((x2d, meta_tokens, norm_g), col_tile0=2, n_slabs=1, mode="vt")
    qk, qk_meta = project((h2d, h_meta), col_tile0=0, n_slabs=2, mode="heads")
    gates, gates_meta = project((h2d, h_meta), col_tile0=3, n_slabs=3, mode="flat")
    att = _attention(qk, qk_meta, vt, v_meta, gates, rel_bias, lam_q1, lam_k1, lam_q2, lam_k2,
                     subln_g)
    out = _tail(x2d, att, gates, gates_meta, conv_w[0], conv_b, w_a[0], b_a, w_x[0], b_x, lru_lambda,
                w_out[0], final_g.reshape(1, D_MODEL), seq=seq)
    return out.reshape(batch, seq, D_MODEL)
```

```python
import functools
import math

import numpy as np
import jax
import jax.numpy as jnp
from jax import lax
from jax.experimental import pallas as pl
from jax.experimental.pallas import tpu as pltpu

D_MODEL = 2048
N_META = 16
D_ATTN = 1024
D_LRU = 1024
N_HEADS = 8
HEAD_DIM = 64
V_DIM = 128
N_LRU_BLOCKS = 8
LRU_BLOCK = 128
CONV_WIDTH = 4
LRU_C = 8.0
N_BUCKETS = 32
MAX_DISTANCE = 128
NORM_EPS = 1e-6
SUBLN_EPS = 1e-5
NEG_INF = -1e30
LAMBDA_INIT = 0.8 - 0.6 * math.exp(-0.3 * 0)
LOG2E = math.log2(math.e)

BF16 = jnp.bfloat16
F32 = jnp.float32

VMEM_LIMIT_PROJ = 56 * 1024 * 1024
VMEM_LIMIT_ATTN = 48 * 1024 * 1024
VMEM_LIMIT_OUT = 56 * 1024 * 1024

PROJ_ROWS = 1024
ATTN_TQ = 512
ATTN_TK = 512
SUM_ROWS = 16
OUT_ROWS = 256
SEG_PITCH = 40


def _bucket_thresholds():
    max_exact = N_BUCKETS // 2
    d = np.arange(0, 4 * MAX_DISTANCE, dtype=np.int64)
    val = (np.log(np.maximum(d, 1).astype(np.float64) / max_exact)
           / math.log(MAX_DISTANCE / max_exact) * (N_BUCKETS - max_exact))
    large = np.minimum(max_exact + np.floor(val + 1e-9).astype(np.int64), N_BUCKETS - 1)
    bucket = np.where(d < max_exact, d, large)
    frac = np.abs(val - np.round(val))
    interior = (d > max_exact) & (d < MAX_DISTANCE)
    assert frac[interior].min() > 1e-3
    assert (np.diff(bucket) >= 0).all() and bucket[MAX_DISTANCE] == N_BUCKETS - 1
    return tuple(int(np.argmax(bucket >= j)) for j in range(1, N_BUCKETS))


BUCKET_THRESHOLDS = _bucket_thresholds()


def _rms(x, g, eps):
    y = x * lax.rsqrt(jnp.mean(x * x, axis=-1, keepdims=True) + eps)
    return y * g


def _dot(a, b):
    return jnp.dot(a, b, preferred_element_type=F32)


def _dot_nt(a, b):
    return lax.dot_general(a, b, (((1,), (1,)), ((), ())), preferred_element_type=F32)


def _proj_kernel(*refs, mode):
    if mode == "vt":
        x_ref, meta_ref, g_ref, w_ref, o_ref, om_ref, h_ref, hm_ref, wb_ref = refs
    else:
        h_ref, hm_ref, w_ref, o_ref, om_ref, wb_ref = refs
    j = pl.program_id(0)
    i = pl.program_id(1)
    out_scale = jnp.where(j == 0, HEAD_DIM ** -0.5 * LOG2E, 1.0).astype(F32) if mode == "heads" else 1.0
    head = lambda y, h: y[:, h * V_DIM:(h + 1) * V_DIM]

    @pl.when(i == 0)
    def _():
        wb_ref[...] = w_ref[...].astype(BF16)
        if mode == "vt":
            hm_ref[...] = _rms(meta_ref[...], g_ref[...], NORM_EPS).astype(BF16)
        ym = _dot(hm_ref[...], wb_ref[...]) * out_scale
        if mode == "flat":
            om_ref[0] = ym
        else:
            for h in range(N_HEADS):
                om_ref[0, h] = head(ym, h).astype(BF16)

    if mode == "vt":
        h_ref[...] = _rms(x_ref[...], g_ref[...], NORM_EPS).astype(BF16)
    y = _dot(h_ref[...], wb_ref[...]) * out_scale
    if mode == "flat":
        o_ref[0] = y
    elif mode == "heads":
        for h in range(N_HEADS):
            o_ref[0, 0, h] = head(y, h).astype(BF16)
    else:
        rows = y.shape[0]
        for h in range(N_HEADS):
            o_ref[0, h, pl.ds(0, V_DIM), :] = head(y, h).T.astype(BF16)
            o_ref[0, h, pl.ds(V_DIM, SUM_ROWS), :] = jnp.ones((SUM_ROWS, rows), BF16)


def _project(inputs, w_in, *, col_tile0, n_slabs, mode, batch, seq):
    rows = inputs[0].shape[0]
    tm = PROJ_ROWS
    n_i = rows // tm
    n_ib = seq // tm
    row_tile = pl.BlockSpec((tm, D_MODEL), lambda j, i: (i, 0))
    meta_rows = pl.BlockSpec((N_META, D_MODEL), lambda j, i: (0, 0))
    if mode == "vt":
        in_specs = [row_tile, meta_rows, pl.BlockSpec((1, D_MODEL), lambda j, i: (0, 0))]
    else:
        in_specs = [row_tile, meta_rows]
    in_specs.append(pl.BlockSpec((D_MODEL, D_ATTN), lambda j, i: (0, j + col_tile0)))
    if mode == "heads":
        out_shape = (jax.ShapeDtypeStruct((n_slabs, batch, N_HEADS, seq, V_DIM), BF16),
                     jax.ShapeDtypeStruct((n_slabs, N_HEADS, N_META, V_DIM), BF16))
        out_specs = (pl.BlockSpec((1, 1, N_HEADS, tm, V_DIM),
                                  lambda j, i: (j, i // n_ib, 0, i % n_ib, 0)),
                     pl.BlockSpec((1, N_HEADS, N_META, V_DIM), lambda j, i: (j, 0, 0, 0)))
    elif mode == "vt":
        assert n_slabs == 1
        out_shape = (jax.ShapeDtypeStruct((batch, N_HEADS, V_DIM + SUM_ROWS, seq), BF16),
                     jax.ShapeDtypeStruct((n_slabs, N_HEADS, N_META, V_DIM), BF16),
                     jax.ShapeDtypeStruct((rows, D_MODEL), BF16),
                     jax.ShapeDtypeStruct((N_META, D_MODEL), BF16))
        out_specs = (pl.BlockSpec((1, N_HEADS, V_DIM + SUM_ROWS, tm),
                                  lambda j, i: (i // n_ib, 0, 0, i % n_ib)),
                     pl.BlockSpec((1, N_HEADS, N_META, V_DIM), lambda j, i: (j, 0, 0, 0)),
                     row_tile, meta_rows)
    else:
        out_shape = (jax.ShapeDtypeStruct((n_slabs, rows, D_ATTN), F32),
                     jax.ShapeDtypeStruct((n_slabs, N_META, D_ATTN), F32))
        out_specs = (pl.BlockSpec((1, tm, D_ATTN), lambda j, i: (j, i, 0)),
                     pl.BlockSpec((1, N_META, D_ATTN), lambda j, i: (j, 0, 0)))
    return pl.pallas_call(
        functools.partial(_proj_kernel, mode=mode),
        grid=(n_slabs, n_i),
        in_specs=in_specs,
        out_specs=out_specs,
        out_shape=out_shape,
        scratch_shapes=[pltpu.VMEM((D_MODEL, D_ATTN), BF16)],
        compiler_params=pltpu.CompilerParams(
            dimension_semantics=("arbitrary", "arbitrary"),
            vmem_limit_bytes=VMEM_LIMIT_PROJ),
        name={"heads": "proj_qk", "vt": "proj_v", "flat": "proj_gates"}[mode],
    )(*inputs, w_in)


def _toeplitz_bias(dist, rb_ref, h, far):
    b = jnp.full(dist.shape, (rb_ref[0, h] - far) * LOG2E, F32)
    for j, thr in enumerate(BUCKET_THRESHOLDS, start=1):
        b = jnp.where(dist >= thr, (rb_ref[j, h] - far) * LOG2E, b)
    return b


def _fill_bias(ref, lead, n_rows, n_cols, d0, rb_ref, h, far):
    sub = lax.broadcasted_iota(jnp.int32, (8, V_DIM), 0)
    lane = lax.broadcasted_iota(jnp.int32, (8, V_DIM), 1)
    zeros = jnp.zeros((8, V_DIM), F32)
    masked = jnp.full((8, V_DIM), NEG_INF, F32)
    cache = {}
    for a8 in range(n_rows // 8):
        for b in range(n_cols // V_DIM):
            off = d0 + V_DIM * b - 8 * a8
            if off + V_DIM - 1 < 0:
                tile = masked
            elif off - 7 >= MAX_DISTANCE:
                tile = zeros
            else:
                if off not in cache:
                    d = off + lane - sub
                    cache[off] = jnp.where(d >= 0, _toeplitz_bias(d, rb_ref, h, far), NEG_INF)
                tile = cache[off]
            ref[(*lead, pl.ds(8 * a8, 8), pl.ds(V_DIM * b, V_DIM))] = tile


def _attn_kernel(rb_ref, lq1_ref, lk1_ref, lq2_ref, lk2_ref, q_ref, k_ref, vt_ref, km_ref, vm_ref,
                 g_ref, sg_ref, o_ref, vmt_ref, bd_ref, bm_ref, bn_ref, sbuf, sc_ref, mx_ref,
                 m_ref, acc_ref):
    h = pl.program_id(0)
    tq, tk = ATTN_TQ, ATTN_TK
    seq = k_ref.shape[2]
    nq = seq // tq

    far = rb_ref[N_BUCKETS - 1, h]

    @pl.when(pl.program_id(1) == 0)
    def _():
        _fill_bias(bd_ref, (), tk, tq, 0, rb_ref, h, far)
        _fill_bias(bn_ref, (), MAX_DISTANCE, MAX_DISTANCE, MAX_DISTANCE, rb_ref, h, far)
        _fill_bias(bm_ref, (1,), N_META, tq, N_META, rb_ref, h, far)
        bm_ref[0] = jnp.zeros((N_META, tq), F32)
        padded = jnp.concatenate(
            [vm_ref[0].astype(F32), jnp.zeros((V_DIM - N_META, V_DIM), F32)], axis=0)
        vmt_ref[pl.ds(0, V_DIM), :] = padded.T.astype(BF16)
        vmt_ref[pl.ds(V_DIM, SUM_ROWS), :] = jnp.ones((SUM_ROWS, V_DIM), BF16)

    lane = lax.broadcasted_iota(jnp.int32, (tq, V_DIM), 1)

    lam = (jnp.exp(jnp.sum(lq1_ref[...] * lk1_ref[...], keepdims=True))
           - jnp.exp(jnp.sum(lq2_ref[...] * lk2_ref[...], keepdims=True))
           + LAMBDA_INIT)

    def init_stats(st):
        m_ref[st] = jnp.full(m_ref.shape[1:], NEG_INF, F32)
        for a in range(2):
            acc_ref[st, a, pl.ds(0, V_DIM), :] = jnp.zeros((V_DIM, tq), F32)
            acc_ref[st, a, pl.ds(V_DIM, SUM_ROWS), :] = jnp.ones((SUM_ROWS, tq), F32)

    hk, hq = tk // 2, tq // 2

    def produce(buf, q_off, k_off, kind):
        q = q_ref[0, 0, pl.ds(pl.multiple_of(q_off, tq), tq), :]
        zero = jnp.zeros_like(q)
        k_off = pl.multiple_of(k_off, tk)
        k_t = k_ref[0, 0, pl.ds(k_off, tk), :]
        for a in range(2):
            qa = jnp.where((lane < HEAD_DIM) if a == 0 else (lane >= HEAD_DIM), q, zero)
            if kind == "diag":
                top = _dot_nt(k_t[:hk], qa) + bd_ref[pl.ds(0, hk), :]
                low = _dot_nt(k_t[hk:], qa[hq:]) + bd_ref[pl.ds(hk, hk), pl.ds(hq, hq)]
                sc = _dot_nt(km_ref[0], qa) + bm_ref[jnp.where(k_off == 0, 1, 0)]
                sbuf[buf, a, pl.ds(0, hk), :] = top
                sbuf[buf, a, pl.ds(hk, hk), pl.ds(hq, hq)] = low
                sc_ref[buf, a] = sc
                mx = jnp.maximum(jnp.max(top, axis=0, keepdims=True),
                                 jnp.max(sc, axis=0, keepdims=True))
                mx_low = jnp.max(low, axis=0, keepdims=True)
                mx_ref[buf, a] = jnp.concatenate(
                    [mx[:, :hq], jnp.maximum(mx[:, hq:], mx_low)], axis=1)
                continue
            s = _dot_nt(k_t, qa)
            if kind == "near":
                band = tk - MAX_DISTANCE
                corner = s[band:, :MAX_DISTANCE] + bn_ref[...]
                sbuf[buf, a, pl.ds(0, band), :] = s[:band]
                sbuf[buf, a, pl.ds(band, MAX_DISTANCE), pl.ds(0, MAX_DISTANCE)] = corner
                sbuf[buf, a, pl.ds(band, MAX_DISTANCE), pl.ds(MAX_DISTANCE, tq - MAX_DISTANCE)] = (
                    s[band:, MAX_DISTANCE:])
                mx = jnp.max(s[:band], axis=0, keepdims=True)
                mx_ref[buf, a] = jnp.concatenate(
                    [jnp.maximum(mx[:, :MAX_DISTANCE], jnp.max(corner, axis=0, keepdims=True)),
                     jnp.maximum(mx[:, MAX_DISTANCE:],
                                 jnp.max(s[band:, MAX_DISTANCE:], axis=0, keepdims=True))], axis=1)
                continue
            sbuf[buf, a] = s
            mx_ref[buf, a] = jnp.max(s, axis=0, keepdims=True)

    def consume(buf, st, k_off, diag=False):
        k_off = pl.multiple_of(k_off, tk)
        v_t = vt_ref[0, 0, :, pl.ds(k_off, tk)]
        for a in range(2):
            m_old = m_ref[st, a]
            m_new = jnp.maximum(m_old, mx_ref[buf, a])
            alpha = jnp.exp2(m_old - m_new)
            if diag:
                top = jnp.exp2(sbuf[buf, a, pl.ds(0, hk), :] - m_new)
                low = jnp.exp2(sbuf[buf, a, pl.ds(hk, hk), pl.ds(hq, hq)] - m_new[:, hq:])
                pc = jnp.exp2(sc_ref[buf, a] - m_new)
                pv = (_dot(v_t[:, :hk], top.astype(BF16))
                      + _dot(vmt_ref[...][:, :N_META], pc.astype(BF16)))
                pv_low = _dot(v_t[:, hk:], low.astype(BF16))
                pv = jnp.concatenate([pv[:, :hq], pv[:, hq:] + pv_low], axis=1)
            else:
                p = jnp.exp2(sbuf[buf, a] - m_new)
                pv = _dot(v_t, p.astype(BF16))
            acc_ref[st, a] = alpha * acc_ref[st, a] + pv
            m_ref[st, a] = m_new

    def finalize(st, q_off):
        q_rows = pl.ds(pl.multiple_of(q_off, tq), tq)
        heads = [acc_ref[st, a, pl.ds(0, V_DIM), :] / acc_ref[st, a, pl.ds(V_DIM, 1), :]
                 for a in range(2)]
        out_t = heads[0] - lam * heads[1]
        inv = lax.rsqrt(jnp.mean(out_t * out_t, axis=0, keepdims=True) + SUBLN_EPS)
        att = ((out_t * inv).T * sg_ref[...]) * (1.0 - LAMBDA_INIT)
        g = g_ref[0, q_rows, :]
        o_ref[q_rows, :] = (att * (g * jax.nn.sigmoid(g))).astype(o_ref.dtype)

    init_stats(0)
    init_stats(1)
    produce(0, 0, 0, "diag")

    def q_tile(i, cur, odd, final=False):
        oth = 1 - cur
        q_off = i * tq

        def far_pair(tt, c):
            t = 2 * tt
            produce(oth, q_off, (t + 1) * tk, "far")
            consume(cur, cur, t * tk)
            produce(cur, q_off, (t + 2) * tk, "far")
            consume(oth, cur, (t + 1) * tk)
            return c

        n_far = jnp.maximum(i - 2, 0)
        lax.fori_loop(0, n_far // 2, far_pair, 0)

        def near_diag():
            produce(oth, q_off, (i - 1) * tk, "near")
            consume(cur, cur, (i - 2) * tk)
            produce(cur, q_off, i * tk, "diag")
            consume(oth, cur, (i - 1) * tk)

        def last(next_kind):
            finalize(oth, jnp.maximum(i - 1, 0) * tq)
            init_stats(oth)
            if next_kind is not None:
                produce(oth, (i + 1) * tq, 0, next_kind)
            consume(cur, cur, i * tk, diag=True)

        def when(cond):
            if isinstance(cond, bool):
                return (lambda f: f()) if cond else (lambda f: None)
            return pl.when(cond)

        next_far = None if final else "far"
        if odd:
            @when(i >= 3)
            def _():
                consume(cur, cur, (n_far - 1) * tk)
                produce(cur, q_off, n_far * tk, "far")
                near_diag()
                last(next_far)

            @when(i == 1)
            def _():
                consume(cur, cur, 0)
                produce(cur, q_off, tk, "diag")
                last(next_far)
        else:
            @when(i >= 2)
            def _():
                near_diag()
                last(next_far)

            @when(i == 0)
            def _():
                last("near")

    def q_pair(ii, carry):
        q_tile(2 * ii, 0, False)
        q_tile(2 * ii + 1, 1, True)
        return carry

    lax.fori_loop(0, nq // 2 - 1, q_pair, 0)
    q_tile(nq - 2, 0, False)
    q_tile(nq - 1, 1, True, final=True)
    finalize((nq - 1) % 2, (nq - 1) * tq)


def _attention(qk, qk_meta, vt, v_meta, gates, rel_bias, lam_q1, lam_k1, lam_q2, lam_k2, subln_g):
    _, batch, _, seq, _ = qk.shape
    tq, tk = ATTN_TQ, ATTN_TK
    smem = pl.BlockSpec(memory_space=pltpu.SMEM)
    row64 = pl.BlockSpec((1, HEAD_DIM), lambda h, b: (0, 0))
    qk_spec = lambda which: pl.BlockSpec((1, 1, 1, seq, V_DIM), lambda h, b: (which, b, h, 0, 0))
    meta_spec = lambda which: pl.BlockSpec((1, 1, N_META, V_DIM), lambda h, b: (which, h, 0, 0))

    def kernel(rb, lq1, lk1, lq2, lk2, q_ref, k_ref, vt_ref, km_ref, vm_ref, *rest):
        _attn_kernel(rb, lq1, lk1, lq2, lk2, q_ref.at[0], k_ref.at[0], vt_ref,
                     km_ref.at[0], vm_ref.at[0], *rest)

    return pl.pallas_call(
        kernel,
        grid=(N_HEADS, batch),
        in_specs=[smem, row64, row64, row64, row64,
                  qk_spec(0), qk_spec(1),
                  pl.BlockSpec((1, 1, V_DIM + SUM_ROWS, seq), lambda h, b: (b, h, 0, 0)),
                  meta_spec(1), meta_spec(0),
                  pl.BlockSpec((1, seq, V_DIM), lambda h, b: (0, b, h)),
                  pl.BlockSpec((1, V_DIM), lambda h, b: (0, 0))],
        out_specs=pl.BlockSpec((seq, V_DIM), lambda h, b: (b, h)),
        out_shape=jax.ShapeDtypeStruct((batch * seq, D_ATTN), BF16),
        scratch_shapes=[pltpu.VMEM((V_DIM + SUM_ROWS, V_DIM), BF16),
                        pltpu.VMEM((tk, tq), F32),
                        pltpu.VMEM((2, N_META, tq), F32),
                        pltpu.VMEM((MAX_DISTANCE, MAX_DISTANCE), F32),
                        pltpu.VMEM((2, 2, tk, tq), F32),
                        pltpu.VMEM((2, 2, N_META, tq), F32),
                        pltpu.VMEM((2, 2, 1, tq), F32),
                        pltpu.VMEM((2, 2, 1, tq), F32),
                        pltpu.VMEM((2, 2, V_DIM + SUM_ROWS, tq), F32)],
        compiler_params=pltpu.CompilerParams(
            dimension_semantics=("arbitrary", "arbitrary"),
            vmem_limit_bytes=VMEM_LIMIT_ATTN),
        name="diff_attention",
    )(rel_bias, lam_q1, lam_k1, lam_q2, lam_k2, qk, qk, vt, qk_meta, v_meta, gates, subln_g)


def _scan_block(a, b):
    n = a.shape[0]
    row = lax.broadcasted_iota(jnp.int32, a.shape, 0)
    s = 1
    while s < n:
        keep = row >= s
        a_sh = jnp.where(keep, pltpu.roll(a, s, 0), 1.0)
        b_sh = jnp.where(keep, pltpu.roll(b, s, 0), 0.0)
        b = b + a * b_sh
        a = a * a_sh
        s *= 2
    return a, b


def _tail_kernel(x_ref, att_ref, u_ref, um_ref, g_ref, cw_ref, cb_ref, wa_ref, wx_ref, ba_ref, bx_ref,
                 lam_ref, w_ref, fg_ref, o_ref, wb_ref, wg_ref, ubuf, tail_ref, h_ref, u_p, h_p, rec_s,
                 rec_prev, *, chunks_per_batch):
    s = pl.program_id(0)
    n = u_ref.shape[0]
    hist = 8
    seg = n // 8

    x = -lam_ref[...]
    softplus = jnp.maximum(x, 0.0) + jnp.log1p(jnp.exp(-jnp.abs(x)))

    @pl.when(s == 0)
    def _():
        wb_ref[...] = w_ref[...].astype(BF16)
        for blk in range(N_LRU_BLOCKS):
            wg_ref[blk, :, pl.ds(0, LRU_BLOCK)] = wa_ref[blk].astype(BF16)
            wg_ref[blk, :, pl.ds(LRU_BLOCK, LRU_BLOCK)] = wx_ref[blk].astype(BF16)
        rec_s[...] = jnp.zeros_like(rec_s)

    def gate_inputs(rows, cols, blk):
        cw = cw_ref[:, cols]
        uc = (cw[3:4] * ubuf[pl.ds(hist, rows), cols] + cw[2:3] * ubuf[pl.ds(hist - 1, rows), cols]
              + cw[1:2] * ubuf[pl.ds(hist - 2, rows), cols] + cw[0:1] * ubuf[pl.ds(hist - 3, rows), cols]
              + cb_ref[:, cols])
        return uc, _dot(uc.astype(BF16), wg_ref[blk])

    def gates(uc, pre, cols, first):
        r = jax.nn.sigmoid(pre[:, :LRU_BLOCK] + ba_ref[:, cols])
        gi = jax.nn.sigmoid(pre[:, LRU_BLOCK:] + bx_ref[:, cols])
        log_a = -LRU_C * r * softplus[:, cols]
        a = jnp.exp(log_a)
        v = jnp.tanh(-log_a) * (a * a + 1.0)
        mult = jnp.where(v > 0.0, v * lax.rsqrt(v), 0.0)
        if first:
            row = lax.broadcasted_iota(jnp.int32, mult.shape, 0)
            mult = jnp.where(row == 0, 1.0, mult)
        return a, mult * gi * uc

    @pl.when(s % chunks_per_batch == 0)
    def _():
        ubuf[pl.ds(0, hist), :] = jnp.zeros((hist, D_LRU), F32)
        ubuf[pl.ds(hist, N_META), :] = um_ref[...]
        for blk in range(N_LRU_BLOCKS):
            cols = slice(blk * LRU_BLOCK, (blk + 1) * LRU_BLOCK)
            a, b = gates(*gate_inputs(N_META, cols, blk), cols, True)
            _, hm = _scan_block(a, b)
            h_ref[:, cols] = hm[N_META - 1:N_META, :]
        tail_ref[...] = um_ref[pl.ds(N_META - hist, hist), :]

    rec_prev[...] = rec_s[...]
    sub =lax.broadcasted_iota(jnp.int32, (8, LRU_BLOCK), 0)
    n_out = D_MODEL // N_LRU_BLOCKS
    ssq = jnp.zeros((n, LRU_BLOCK), F32)
    step = lambda j: pl.ds(j, 8, stride=SEG_PITCH)
    for blk in range(N_LRU_BLOCKS):
        cols = slice(blk * LRU_BLOCK, (blk + 1) * LRU_BLOCK)
        ocols = slice(blk * n_out, (blk + 1) * n_out)
        z = (x_ref[:, ocols] + _dot(att_ref[...], wb_ref[pl.ds(0, D_ATTN), ocols])
             + _dot(rec_prev[...], wb_ref[pl.ds(D_ATTN, D_LRU), ocols]))
        o_ref[:, ocols] = z
        for part in range(n_out // LRU_BLOCK):
            zp = z[:, part * LRU_BLOCK:(part + 1) * LRU_BLOCK]
            ssq = ssq + zp * zp

        for k in range(8):
            u_p[blk, pl.ds(SEG_PITCH * k, seg), :] = u_ref[pl.ds(seg * k, seg), cols]
        us = [u_p[blk, step(j), :] for j in range(seg)]
        prev = tail_ref[:, cols]

        def before(back):
            return jnp.where(sub == 0, prev[hist - back:hist - back + 1, :],
                             pltpu.roll(us[seg - back], 1, 0))

        older = {-back: before(back) for back in range(1, CONV_WIDTH)}
        u_at = lambda j: us[j] if j >= 0 else older[j]
        cw = cw_ref[:, cols]
        cb = cb_ref[:, cols]
        uc = jnp.concatenate(
            [cw[3:4] * u_at(j) + cw[2:3] * u_at(j - 1) + cw[1:2] * u_at(j - 2)
             + cw[0:1] * u_at(j - 3) + cb for j in range(seg)], axis=0)
        a, b = gates(uc, _dot(uc.astype(BF16), wg_ref[blk]), cols, False)
        a_j = lambda j: a[8 * j:8 * j + 8]
        b_j = lambda j: b[8 * j:8 * j + 8]
        a_run, h_run = a_j(0), b_j(0)
        for j in range(1, seg):
            h_run = a_j(j) * h_run + b_j(j)
            a_run = a_j(j) * a_run
        a_cum, h_cum = _scan_block(a_run, h_run)
        carry = h_ref[:, cols]
        seg_end = h_cum + a_cum * carry
        h_ref[:, cols] = seg_end[7:8, :]
        h_run = jnp.where(sub == 0, carry, pltpu.roll(seg_end, 1, 0))
        for j in range(seg):
            h_run = a_j(j) * h_run + b_j(j)
            h_p[blk, step(j), :] = h_run
        g = g_ref[:, cols]
        h = jnp.concatenate([h_p[blk, pl.ds(SEG_PITCH * k, seg), :] for k in range(8)], axis=0)
        rec_s[:, cols] = (h * (g * jax.nn.sigmoid(g))).astype(BF16)
    tail_ref[...] = u_ref[pl.ds(n - hist, hist), :]
    inv = lax.rsqrt(jnp.sum(ssq, axis=1, keepdims=True) * (1.0 / D_MODEL) + NORM_EPS)
    o_ref[...] = (o_ref[...] * inv) * fg_ref[...]


def _tail(x2d, att, gates, gates_meta, conv_w, conv_b, w_a, b_a, w_x, b_x, lru_lambda, w_out,
          final_g, *, seq):
    rows = x2d.shape[0]
    tm = OUT_ROWS
    n_tiles = rows // tm
    prev = lambda s: (jnp.maximum(s - 1, 0), 0)
    cur = lambda which: (lambda s: (which, jnp.minimum(s, n_tiles - 1), 0))
    row = lambda n: pl.BlockSpec((n, D_LRU), lambda s: (0, 0))
    wspec = pl.BlockSpec((N_LRU_BLOCKS, LRU_BLOCK, LRU_BLOCK), lambda s: (0, 0, 0))

    def kernel(x_ref, att_ref, u_ref, um_ref, g_ref, *rest):
        _tail_kernel(x_ref, att_ref, u_ref.at[0], um_ref.at[0], g_ref.at[0], *rest,
                     chunks_per_batch=seq // tm)

    return pl.pallas_call(
        kernel,
        grid=(n_tiles + 1,),
        in_specs=[pl.BlockSpec((tm, D_MODEL), prev),
                  pl.BlockSpec((tm, D_ATTN), prev),
                  pl.BlockSpec((1, tm, D_LRU), cur(1)),
                  pl.BlockSpec((1, N_META, D_LRU), lambda s: (1, 0, 0)),
                  pl.BlockSpec((1, tm, D_LRU), cur(2)),
                  row(CONV_WIDTH), row(1), wspec, wspec, row(1), row(1), row(1),
                  pl.BlockSpec((D_ATTN + D_LRU, D_MODEL), lambda s: (0, 0),
                               pipeline_mode=pl.Buffered(1)),
                  pl.BlockSpec((1, D_MODEL), lambda s: (0, 0))],
        out_specs=pl.BlockSpec((tm, D_MODEL), prev),
        out_shape=jax.ShapeDtypeStruct((rows, D_MODEL), F32),
        scratch_shapes=[pltpu.VMEM((D_ATTN + D_LRU, D_MODEL), BF16),
                        pltpu.VMEM((N_LRU_BLOCKS, LRU_BLOCK, 2 * LRU_BLOCK), BF16),
                        pltpu.VMEM((8 + N_META, D_LRU), F32),
                        pltpu.VMEM((8, D_LRU), F32),
                        pltpu.VMEM((1, D_LRU), F32),
                        pltpu.VMEM((N_LRU_BLOCKS, 8 * SEG_PITCH, LRU_BLOCK), F32),
                        pltpu.VMEM((N_LRU_BLOCKS, 8 * SEG_PITCH, LRU_BLOCK), F32),
                        pltpu.VMEM((tm, D_LRU), BF16),
                        pltpu.VMEM((tm, D_LRU), BF16)],
        compiler_params=pltpu.CompilerParams(
            dimension_semantics=("arbitrary",),
            vmem_limit_bytes=VMEM_LIMIT_OUT),
        name="rglru_out_proj",
    )(x2d, att, gates, gates_meta, gates, conv_w, conv_b, w_a, w_x, b_a, b_x, lru_lambda, w_out,
      final_g)


def kernel(x, meta_tokens, rel_bias, norm_g, w_in, conv_w, conv_b, w_a, b_a, w_x, b_x, lru_lambda,
           lam_q1, lam_k1, lam_q2, lam_k2, subln_g, w_out, final_g):
    batch, seq, _ = x.shape
    x2d = x.reshape(batch * seq, D_MODEL)
    project = functools.partial(_project, w_in=w_in[0], batch=batch, seq=seq)
    vt, v_meta, h2d, h_meta = project((x2d, meta_tokens, norm_g), col_tile0=2, n_slabs=1, mode="vt")
    qk, qk_meta = project((h2d, h_meta), col_tile0=0, n_slabs=2, mode="heads")
    gates, gates_meta = project((h2d, h_meta), col_tile0=3, n_slabs=3, mode="flat")
    att = _attention(qk, qk_meta, vt, v_meta, gates, rel_bias, lam_q1, lam_k1, lam_q2, lam_k2,
                     subln_g)
    out = _tail(x2d, att, gates, gates_meta, conv_w[0], conv_b, w_a[0], b_a, w_x[0], b_x, lru_lambda,
                w_out[0], final_g.reshape(1, D_MODEL), seq=seq)
    return out.reshape(batch, seq, D_MODEL)
```

```python
import functools
import math

import numpy as np
import jax
import jax.numpy as jnp
from jax import lax
from jax.experimental import pallas as pl
from jax.experimental.pallas import tpu as pltpu

D_MODEL = 2048
N_META = 16
D_ATTN = 1024
D_LRU = 1024
N_HEADS = 8
HEAD_DIM = 64
V_DIM = 128
N_LRU_BLOCKS = 8
LRU_BLOCK = 128
CONV_WIDTH = 4
LRU_C = 8.0
N_BUCKETS = 32
MAX_DISTANCE = 128
NORM_EPS = 1e-6
SUBLN_EPS = 1e-5
NEG_INF = -1e30
LAMBDA_INIT = 0.8 - 0.6 * math.exp(-0.3 * 0)
LOG2E = math.log2(math.e)

BF16 = jnp.bfloat16
F32 = jnp.float32

VMEM_LIMIT_PROJ = 56 * 1024 * 1024
VMEM_LIMIT_ATTN = 48 * 1024 * 1024
VMEM_LIMIT_OUT = 56 * 1024 * 1024

PROJ_ROWS = 1024
ATTN_TQ = 512
ATTN_TK = 512
SUM_ROWS = 16
OUT_ROWS = 256
SEG_PITCH = 40


def _bucket_thresholds():
    max_exact = N_BUCKETS // 2
    d = np.arange(0, 4 * MAX_DISTANCE, dtype=np.int64)
    val = (np.log(np.maximum(d, 1).astype(np.float64) / max_exact)
           / math.log(MAX_DISTANCE / max_exact) * (N_BUCKETS - max_exact))
    large = np.minimum(max_exact + np.floor(val + 1e-9).astype(np.int64), N_BUCKETS - 1)
    bucket = np.where(d < max_exact, d, large)
    frac = np.abs(val - np.round(val))
    interior = (d > max_exact) & (d < MAX_DISTANCE)
    assert frac[interior].min() > 1e-3
    assert (np.diff(bucket) >= 0).all() and bucket[MAX_DISTANCE] == N_BUCKETS - 1
    return tuple(int(np.argmax(bucket >= j)) for j in range(1, N_BUCKETS))


BUCKET_THRESHOLDS = _bucket_thresholds()


def _rms(x, g, eps):
    y = x * lax.rsqrt(jnp.mean(x * x, axis=-1, keepdims=True) + eps)
    return y * g


def _dot(a, b):
    return jnp.dot(a, b, preferred_element_type=F32)


def _dot_nt(a, b):
    return lax.dot_general(a, b, (((1,), (1,)), ((), ())), preferred_element_type=F32)


def _proj_kernel(*refs, mode):
    if mode == "vt":
        x_ref, meta_ref, g_ref, w_ref, o_ref, om_ref, h_ref, hm_ref, wb_ref = refs
    else:
        h_ref, hm_ref, w_ref, o_ref, om_ref, wb_ref = refs
    j = pl.program_id(0)
    i = pl.program_id(1)
    out_scale = jnp.where(j == 0, HEAD_DIM ** -0.5 * LOG2E, 1.0).astype(F32) if mode == "heads" else 1.0
    head = lambda y, h: y[:, h * V_DIM:(h + 1) * V_DIM]

    @pl.when(i == 0)
    def _():
        wb_ref[...] = w_ref[...].astype(BF16)
        if mode == "vt":
            hm_ref[...] = _rms(meta_ref[...], g_ref[...], NORM_EPS).astype(BF16)
        ym = _dot(hm_ref[...], wb_ref[...]) * out_scale
        if mode == "flat":
            om_ref[0] = ym
        else:
            for h in range(N_HEADS):
                om_ref[0, h] = head(ym, h).astype(BF16)

    if mode == "vt":
        h_ref[...] = _rms(x_ref[...], g_ref[...], NORM_EPS).astype(BF16)
    y = _dot(h_ref[...], wb_ref[...]) * out_scale
    if mode == "flat":
        o_ref[0] = y
    elif mode == "heads":
        for h in range(N_HEADS):
            o_ref[0, 0, h] = head(y, h).astype(BF16)
    else:
        rows = y.shape[0]
        for h in range(N_HEADS):
            o_ref[0, h, pl.ds(0, V_DIM), :] = head(y, h).T.astype(BF16)
            o_ref[0, h, pl.ds(V_DIM, SUM_ROWS), :] = jnp.ones((SUM_ROWS, rows), BF16)


def _project(inputs, w_in, *, col_tile0, n_slabs, mode, batch, seq):
    rows = inputs[0].shape[0]
    tm = PROJ_ROWS
    n_i = rows // tm
    n_ib = seq // tm
    row_tile = pl.BlockSpec((tm, D_MODEL), lambda j, i: (i, 0))
    meta_rows = pl.BlockSpec((N_META, D_MODEL), lambda j, i: (0, 0))
    if mode == "vt":
        in_specs = [row_tile, meta_rows, pl.BlockSpec((1, D_MODEL), lambda j, i: (0, 0))]
    else:
        in_specs = [row_tile, meta_rows]
    in_specs.append(pl.BlockSpec((D_MODEL, D_ATTN), lambda j, i: (0, j + col_tile0)))
    if mode == "heads":
        out_shape = (jax.ShapeDtypeStruct((n_slabs, batch, N_HEADS, seq, V_DIM), BF16),
                     jax.ShapeDtypeStruct((n_slabs, N_HEADS, N_META, V_DIM), BF16))
        out_specs = (pl.BlockSpec((1, 1, N_HEADS, tm, V_DIM),
                                  lambda j, i: (j, i // n_ib, 0, i % n_ib, 0)),
                     pl.BlockSpec((1, N_HEADS, N_META, V_DIM), lambda j, i: (j, 0, 0, 0)))
    elif mode == "vt":
        assert n_slabs == 1
        out_shape = (jax.ShapeDtypeStruct((batch, N_HEADS, V_DIM + SUM_ROWS, seq), BF16),
                     jax.ShapeDtypeStruct((n_slabs, N_HEADS, N_META, V_DIM), BF16),
                     jax.ShapeDtypeStruct((rows, D_MODEL), BF16),
                     jax.ShapeDtypeStruct((N_META, D_MODEL), BF16))
        out_specs = (pl.BlockSpec((1, N_HEADS, V_DIM + SUM_ROWS, tm),
                                  lambda j, i: (i // n_ib, 0, 0, i % n_ib)),
                     pl.BlockSpec((1, N_HEADS, N_META, V_DIM), lambda j, i: (j, 0, 0, 0)),
                     row_tile, meta_rows)
    else:
        out_shape = (jax.ShapeDtypeStruct((n_slabs, rows, D_ATTN), F32),
                     jax.ShapeDtypeStruct((n_slabs, N_META, D_ATTN), F32))
        out_specs = (pl.BlockSpec((1, tm, D_ATTN), lambda j, i: (j, i, 0)),
                     pl.BlockSpec((1, N_META, D_ATTN), lambda j, i: (j, 0, 0)))
    return pl.pallas_call(
        functools.partial(_proj_kernel, mode=mode),
        grid=(n_slabs, n_i),
        in_specs=in_specs,
        out_specs=out_specs,
        out_shape=out_shape,
        scratch_shapes=[pltpu.VMEM((D_MODEL, D_ATTN), BF16)],
        compiler_params=pltpu.CompilerParams(
            dimension_semantics=("arbitrary", "arbitrary"),
            vmem_limit_bytes=VMEM_LIMIT_PROJ),
        name={"heads": "proj_qk", "vt": "proj_v", "flat": "proj_gates"}[mode],
    )(*inputs, w_in)


def _toeplitz_bias(dist, rb_ref, h, far):
    b = jnp.full(dist.shape, (rb_ref[0, h] - far) * LOG2E, F32)
    for j, thr in enumerate(BUCKET_THRESHOLDS, start=1):
        b = jnp.where(dist >= thr, (rb_ref[j, h] - far) * LOG2E, b)
    return b


def _fill_bias(ref, lead, n_rows, n_cols, d0, rb_ref, h, far):
    sub = lax.broadcasted_iota(jnp.int32, (8, V_DIM), 0)
    lane = lax.broadcasted_iota(jnp.int32, (8, V_DIM), 1)
    zeros = jnp.zeros((8, V_DIM), F32)
    masked = jnp.full((8, V_DIM), NEG_INF, F32)
    cache = {}
    for a8 in range(n_rows // 8):
        for b in range(n_cols // V_DIM):
            off = d0 + V_DIM * b - 8 * a8
            if off + V_DIM - 1 < 0:
                tile = masked
            elif off - 7 >= MAX_DISTANCE:
                tile = zeros
            else:
                if off not in cache:
                    d = off + lane - sub
                    cache[off] = jnp.where(d >= 0, _toeplitz_bias(d, rb_ref, h, far), NEG_INF)
                tile = cache[off]
            ref[(*lead, pl.ds(8 * a8, 8), pl.ds(V_DIM * b, V_DIM))] = tile


def _attn_kernel(rb_ref, lq1_ref, lk1_ref, lq2_ref, lk2_ref, q_ref, k_ref, vt_ref, km_ref, vm_ref,
                 g_ref, sg_ref, o_ref, vmt_ref, bd_ref, bm_ref, bn_ref, sbuf, sc_ref, mx_ref,
                 m_ref, acc_ref):
    h = pl.program_id(0)
    tq, tk = ATTN_TQ, ATTN_TK
    seq = k_ref.shape[2]
    nq = seq // tq

    far = rb_ref[N_BUCKETS - 1, h]

    @pl.when(pl.program_id(1) == 0)
    def _():
        _fill_bias(bd_ref, (), tk, tq, 0, rb_ref, h, far)
        _fill_bias(bn_ref, (), MAX_DISTANCE, MAX_DISTANCE, MAX_DISTANCE, rb_ref, h, far)
        _fill_bias(bm_ref, (1,), N_META, tq, N_META, rb_ref, h, far)
        bm_ref[0] = jnp.zeros((N_META, tq), F32)
        padded = jnp.concatenate(
            [vm_ref[0].astype(F32), jnp.zeros((V_DIM - N_META, V_DIM), F32)], axis=0)
        vmt_ref[pl.ds(0, V_DIM), :] = padded.T.astype(BF16)
        vmt_ref[pl.ds(V_DIM, SUM_ROWS), :] = jnp.ones((SUM_ROWS, V_DIM), BF16)

    lane = lax.broadcasted_iota(jnp.int32, (tq, V_DIM), 1)

    lam = (jnp.exp(jnp.sum(lq1_ref[...] * lk1_ref[...], keepdims=True))
           - jnp.exp(jnp.sum(lq2_ref[...] * lk2_ref[...], keepdims=True))
           + LAMBDA_INIT)

    def init_stats(st):
        m_ref[st] = jnp.full(m_ref.shape[1:], NEG_INF, F32)
        for a in range(2):
            acc_ref[st, a, pl.ds(0, V_DIM), :] = jnp.zeros((V_DIM, tq), F32)
            acc_ref[st, a, pl.ds(V_DIM, SUM_ROWS), :] = jnp.ones((SUM_ROWS, tq), F32)

    hk, hq = tk // 2, tq // 2

    def produce(buf, q_off, k_off, kind):
        q = q_ref[0, 0, pl.ds(pl.multiple_of(q_off, tq), tq), :]
        zero = jnp.zeros_like(q)
        k_off = pl.multiple_of(k_off, tk)
        k_t = k_ref[0, 0, pl.ds(k_off, tk), :]
        for a in range(2):
            qa = jnp.where((lane < HEAD_DIM) if a == 0 else (lane >= HEAD_DIM), q, zero)
            if kind == "diag":
                top = _dot_nt(k_t[:hk], qa) + bd_ref[pl.ds(0, hk), :]
                low = _dot_nt(k_t[hk:], qa[hq:]) + bd_ref[pl.ds(hk, hk), pl.ds(hq, hq)]
                sc = _dot_nt(km_ref[0], qa) + bm_ref[jnp.where(k_off == 0, 1, 0)]
                sbuf[buf, a, pl.ds(0, hk), :] = top
                sbuf[buf, a, pl.ds(hk, hk), pl.ds(hq, hq)] = low
                sc_ref[buf, a] = sc
                mx = jnp.maximum(jnp.max(top, axis=0, keepdims=True),
                                 jnp.max(sc, axis=0, keepdims=True))
                mx_low = jnp.max(low, axis=0, keepdims=True)
                mx_ref[buf, a] = jnp.concatenate(
                    [mx[:, :hq], jnp.maximum(mx[:, hq:], mx_low)], axis=1)
                continue
            s = _dot_nt(k_t, qa)
            if kind == "near":
                band = tk - MAX_DISTANCE
                corner = s[band:, :MAX_DISTANCE] + bn_ref[...]
                sbuf[buf, a, pl.ds(0, band), :] = s[:band]
                sbuf[buf, a, pl.ds(band, MAX_DISTANCE), pl.ds(0, MAX_DISTANCE)] = corner
                sbuf[buf, a, pl.ds(band, MAX_DISTANCE), pl.ds(MAX_DISTANCE, tq - MAX_DISTANCE)] = (
                    s[band:, MAX_DISTANCE:])
                mx = jnp.max(s[:band], axis=0, keepdims=True)
                mx_ref[buf, a] = jnp.concatenate(
                    [jnp.maximum(mx[:, :MAX_DISTANCE], jnp.max(corner, axis=0, keepdims=True)),
                     jnp.maximum(mx[:, MAX_DISTANCE:],
                                 jnp.max(s[band:, MAX_DISTANCE:], axis=0, keepdims=True))], axis=1)
                continue
            sbuf[buf, a] = s
            mx_ref[buf, a] = jnp.max(s, axis=0, keepdims=True)

    def consume(buf, st, k_off, diag=False):
        k_off = pl.multiple_of(k_off, tk)
        v_t = vt_ref[0, 0, :, pl.ds(k_off, tk)]
        for a in range(2):
            m_old = m_ref[st, a]
            m_new = jnp.maximum(m_old, mx_ref[buf, a])
            alpha = jnp.exp2(m_old - m_new)
            if diag:
                top = jnp.exp2(sbuf[buf, a, pl.ds(0, hk), :] - m_new)
                low = jnp.exp2(sbuf[buf, a, pl.ds(hk, hk), pl.ds(hq, hq)] - m_new[:, hq:])
                pc = jnp.exp2(sc_ref[buf, a] - m_new)
                pv = (_dot(v_t[:, :hk], top.astype(BF16))
                      + _dot(vmt_ref[...][:, :N_META], pc.astype(BF16)))
                pv_low = _dot(v_t[:, hk:], low.astype(BF16))
                pv = jnp.concatenate([pv[:, :hq], pv[:, hq:] + pv_low], axis=1)
            else:
                p = jnp.exp2(sbuf[buf, a] - m_new)
                pv = _dot(v_t, p.astype(BF16))
            acc_ref[st, a] = alpha * acc_ref[st, a] + pv
            m_ref[st, a] = m_new

    def finalize(st, q_off):
        q_rows = pl.ds(pl.multiple_of(q_off, tq), tq)
        heads = [acc_ref[st, a, pl.ds(0, V_DIM), :] / acc_ref[st, a, pl.ds(V_DIM, 1), :]
                 for a in range(2)]
        out_t = heads[0] - lam * heads[1]
        inv = lax.rsqrt(jnp.mean(out_t * out_t, axis=0, keepdims=True) + SUBLN_EPS)
        att = ((out_t * inv).T * sg_ref[...]) * (1.0 - LAMBDA_INIT)
        g = g_ref[0, q_rows, :]
        o_ref[q_rows, :] = (att * (g * jax.nn.sigmoid(g))).astype(o_ref.dtype)

    init_stats(0)
    init_stats(1)
    produce(0, 0, 0, "diag")

    def q_tile(i, cur, odd):
        oth = 1 - cur
        q_off = i * tq

        def far_pair(tt, c):
            t = 2 * tt
            produce(oth, q_off, (t + 1) * tk, "far")
            consume(cur, cur, t * tk)
            produce(cur, q_off, (t + 2) * tk, "far")
            consume(oth, cur, (t + 1) * tk)
            return c

        n_far = jnp.maximum(i - 2, 0)
        lax.fori_loop(0, n_far // 2, far_pair, 0)

        def near_diag():
            produce(oth, q_off, (i - 1) * tk, "near")
            consume(cur, cur, (i - 2) * tk)
            produce(cur, q_off, i * tk, "diag")
            consume(oth, cur, (i - 1) * tk)

        def last(next_kind):
            finalize(oth, jnp.maximum(i - 1, 0) * tq)
            init_stats(oth)
            produce(oth, jnp.minimum(i + 1, nq - 1) * tq, 0, next_kind)
            consume(cur, cur, i * tk, diag=True)

        if odd:
            @pl.when(i >= 3)
            def _():
                consume(cur, cur, (n_far - 1) * tk)
                produce(cur, q_off, n_far * tk, "far")
                near_diag()
                last("far")

            @pl.when(i == 1)
            def _():
                consume(cur, cur, 0)
                produce(cur, q_off, tk, "diag")
                last("far")
        else:
            @pl.when(i >= 2)
            def _():
                near_diag()
                last("far")

            @pl.when(i == 0)
            def _():
                last("near")

    def q_pair(ii, carry):
        q_tile(2 * ii, 0, False)
        q_tile(2 * ii + 1, 1, True)
        return carry

    lax.fori_loop(0, nq // 2, q_pair, 0)
    finalize((nq - 1) % 2, (nq - 1) * tq)


def _attention(qk, qk_meta, vt, v_meta, gates, rel_bias, lam_q1, lam_k1, lam_q2, lam_k2, subln_g):
    _, batch, _, seq, _ = qk.shape
    tq, tk = ATTN_TQ, ATTN_TK
    smem = pl.BlockSpec(memory_space=pltpu.SMEM)
    row64 = pl.BlockSpec((1, HEAD_DIM), lambda h, b: (0, 0))
    qk_spec = lambda which: pl.BlockSpec((1, 1, 1, seq, V_DIM), lambda h, b: (which, b, h, 0, 0))
    meta_spec = lambda which: pl.BlockSpec((1, 1, N_META, V_DIM), lambda h, b: (which, h, 0, 0))

    def kernel(rb, lq1, lk1, lq2, lk2, q_ref, k_ref, vt_ref, km_ref, vm_ref, *rest):
        _attn_kernel(rb, lq1, lk1, lq2, lk2, q_ref.at[0], k_ref.at[0], vt_ref,
                     km_ref.at[0], vm_ref.at[0], *rest)

    return pl.pallas_call(
        kernel,
        grid=(N_HEADS, batch),
        in_specs=[smem, row64, row64, row64, row64,
                  qk_spec(0), qk_spec(1),
                  pl.BlockSpec((1, 1, V_DIM + SUM_ROWS, seq), lambda h, b: (b, h, 0, 0)),
                  meta_spec(1), meta_spec(0),
                  pl.BlockSpec((1, seq, V_DIM), lambda h, b: (0, b, h)),
                  pl.BlockSpec((1, V_DIM), lambda h, b: (0, 0))],
        out_specs=pl.BlockSpec((seq, V_DIM), lambda h, b: (b, h)),
        out_shape=jax.ShapeDtypeStruct((batch * seq, D_ATTN), BF16),
        scratch_shapes=[pltpu.VMEM((V_DIM + SUM_ROWS, V_DIM), BF16),
                        pltpu.VMEM((tk, tq), F32),
                        pltpu.VMEM((2, N_META, tq), F32),
                        pltpu.VMEM((MAX_DISTANCE, MAX_DISTANCE), F32),
                        pltpu.VMEM((2, 2, tk, tq), F32),
                        pltpu.VMEM((2, 2, N_META, tq), F32),
                        pltpu.VMEM((2, 2, 1, tq), F32),
                        pltpu.VMEM((2, 2, 1, tq), F32),
                        pltpu.VMEM((2, 2, V_DIM + SUM_ROWS, tq), F32)],
        compiler_params=pltpu.CompilerParams(
            dimension_semantics=("arbitrary", "arbitrary"),
            vmem_limit_bytes=VMEM_LIMIT_ATTN),
        name="diff_attention",
    )(rel_bias, lam_q1, lam_k1, lam_q2, lam_k2, qk, qk, vt, qk_meta, v_meta, gates, subln_g)


def _scan_block(a, b):
    n = a.shape[0]
    row = lax.broadcasted_iota(jnp.int32, a.shape, 0)
    s = 1
    while s < n:
        keep = row >= s
        a_sh = jnp.where(keep, pltpu.roll(a, s, 0), 1.0)
        b_sh = jnp.where(keep, pltpu.roll(b, s, 0), 0.0)
        b = b + a * b_sh
        a = a * a_sh
        s *= 2
    return a, b


def _tail_kernel(x_ref, att_ref, u_ref, um_ref, g_ref, cw_ref, cb_ref, wa_ref, wx_ref, ba_ref, bx_ref,
                 lam_ref, w_ref, fg_ref, o_ref, wb_ref, wg_ref, ubuf, tail_ref, h_ref, u_p, h_p, rec_s,
                 rec_prev, *, chunks_per_batch):
    s = pl.program_id(0)
    n = u_ref.shape[0]
    hist = 8
    seg = n // 8

    x = -lam_ref[...]
    softplus = jnp.maximum(x, 0.0) + jnp.log1p(jnp.exp(-jnp.abs(x)))

    @pl.when(s == 0)
    def _():
        wb_ref[...] = w_ref[...].astype(BF16)
        for blk in range(N_LRU_BLOCKS):
            wg_ref[blk, :, pl.ds(0, LRU_BLOCK)] = wa_ref[blk].astype(BF16)
            wg_ref[blk, :, pl.ds(LRU_BLOCK, LRU_BLOCK)] = wx_ref[blk].astype(BF16)
        rec_s[...] = jnp.zeros_like(rec_s)

    def gate_inputs(rows, cols, blk):
        cw = cw_ref[:, cols]
        uc = (cw[3:4] * ubuf[pl.ds(hist, rows), cols] + cw[2:3] * ubuf[pl.ds(hist - 1, rows), cols]
              + cw[1:2] * ubuf[pl.ds(hist - 2, rows), cols] + cw[0:1] * ubuf[pl.ds(hist - 3, rows), cols]
              + cb_ref[:, cols])
        return uc, _dot(uc.astype(BF16), wg_ref[blk])

    def gates(uc, pre, cols, first):
        r = jax.nn.sigmoid(pre[:, :LRU_BLOCK] + ba_ref[:, cols])
        gi = jax.nn.sigmoid(pre[:, LRU_BLOCK:] + bx_ref[:, cols])
        log_a = -LRU_C * r * softplus[:, cols]
        a = jnp.exp(log_a)
        v = jnp.tanh(-log_a) * (a * a + 1.0)
        mult = jnp.where(v > 0.0, v * lax.rsqrt(v), 0.0)
        if first:
            row = lax.broadcasted_iota(jnp.int32, mult.shape, 0)
            mult = jnp.where(row == 0, 1.0, mult)
        return a, mult * gi * uc

    @pl.when(s % chunks_per_batch == 0)
    def _():
        ubuf[pl.ds(0, hist), :] = jnp.zeros((hist, D_LRU), F32)
        ubuf[pl.ds(hist, N_META), :] = um_ref[...]
        for blk in range(N_LRU_BLOCKS):
            cols = slice(blk * LRU_BLOCK, (blk + 1) * LRU_BLOCK)
            a, b = gates(*gate_inputs(N_META, cols, blk), cols, True)
            _, hm = _scan_block(a, b)
            h_ref[:, cols] = hm[N_META - 1:N_META, :]
        tail_ref[...] = um_ref[pl.ds(N_META - hist, hist), :]

    rec_prev[...] = rec_s[...]
    sub = lax.broadcasted_iota(jnp.int32, (8, LRU_BLOCK), 0)
    n_out = D_MODEL // N_LRU_BLOCKS
    ssq = jnp.zeros((n, LRU_BLOCK), F32)
    step = lambda j: pl.ds(j, 8, stride=SEG_PITCH)
    for blk in range(N_LRU_BLOCKS):
        cols = slice(blk * LRU_BLOCK, (blk + 1) * LRU_BLOCK)
        ocols = slice(blk * n_out, (blk + 1) * n_out)
        z = (x_ref[:, ocols] + _dot(att_ref[...], wb_ref[pl.ds(0, D_ATTN), ocols])
             + _dot(rec_prev[...], wb_ref[pl.ds(D_ATTN, D_LRU), ocols]))
        o_ref[:, ocols] = z
        for part in range(n_out // LRU_BLOCK):
            zp = z[:, part * LRU_BLOCK:(part + 1) * LRU_BLOCK]
            ssq = ssq + zp * zp

        for k in range(8):
            u_p[blk, pl.ds(SEG_PITCH * k, seg), :] = u_ref[pl.ds(seg * k, seg), cols]
        us = [u_p[blk, step(j), :] for j in range(seg)]
        prev = tail_ref[:, cols]

        def before(back):
            return jnp.where(sub == 0, prev[hist - back:hist - back + 1, :],
                             pltpu.roll(us[seg - back], 1, 0))

        older = {-back: before(back) for back in range(1, CONV_WIDTH)}
        u_at = lambda j: us[j] if j >= 0 else older[j]
        cw = cw_ref[:, cols]
        cb = cb_ref[:, cols]
        uc = jnp.concatenate(
            [cw[3:4] * u_at(j) + cw[2:3] * u_at(j - 1) + cw[1:2] * u_at(j - 2)
             + cw[0:1] * u_at(j - 3) + cb for j in range(seg)], axis=0)
        a, b = gates(uc, _dot(uc.astype(BF16), wg_ref[blk]), cols, False)
        a_j = lambda j: a[8 * j:8 * j + 8]
        b_j = lambda j: b[8 * j:8 * j + 8]
        a_run, h_run = a_j(0), b_j(0)
        for j in range(1, seg):
            h_run = a_j(j) * h_run + b_j(j)
            a_run = a_j(j) * a_run
        a_cum, h_cum = _scan_block(a_run, h_run)
        carry = h_ref[:, cols]
        seg_end = h_cum + a_cum * carry
        h_ref[:, cols] = seg_end[7:8, :]
        h_run = jnp.where(sub == 0, carry, pltpu.roll(seg_end, 1, 0))
        for j in range(seg):
            h_run = a_j(j) * h_run + b_j(j)
            h_p[blk, step(j), :] = h_run
        g = g_ref[:, cols]
        h = jnp.concatenate([h_p[blk, pl.ds(SEG_PITCH * k, seg), :] for k in range(8)], axis=0)
        rec_s[:, cols] = (h * (g * jax.nn.sigmoid(g))).astype(BF16)
    tail_ref[...] = u_ref[pl.ds(n - hist, hist), :]
    inv = lax.rsqrt(jnp.sum(ssq, axis=1, keepdims=True) * (1.0 / D_MODEL) + NORM_EPS)
    o_ref[...] = (o_ref[...] * inv) * fg_ref[...]


def _tail(x2d, att, gates, gates_meta, conv_w, conv_b, w_a, b_a, w_x, b_x, lru_lambda, w_out,
          final_g, *, seq):
    rows = x2d.shape[0]
    tm = OUT_ROWS
    n_tiles = rows // tm
    prev = lambda s: (jnp.maximum(s - 1, 0), 0)
    cur = lambda which: (lambda s: (which, jnp.minimum(s, n_tiles - 1), 0))
    row = lambda n: pl.BlockSpec((n, D_LRU), lambda s: (0, 0))
    wspec = pl.BlockSpec((N_LRU_BLOCKS, LRU_BLOCK, LRU_BLOCK), lambda s: (0, 0, 0))

    def kernel(x_ref, att_ref, u_ref, um_ref, g_ref, *rest):
        _tail_kernel(x_ref, att_ref, u_ref.at[0], um_ref.at[0], g_ref.at[0], *rest,
                     chunks_per_batch=seq // tm)

    return pl.pallas_call(
        kernel,
        grid=(n_tiles + 1,),
        in_specs=[pl.BlockSpec((tm, D_MODEL), prev),
                  pl.BlockSpec((tm, D_ATTN), prev),
                  pl.BlockSpec((1, tm, D_LRU), cur(1)),
                  pl.BlockSpec((1, N_META, D_LRU), lambda s: (1, 0, 0)),
                  pl.BlockSpec((1, tm, D_LRU), cur(2)),
                  row(CONV_WIDTH), row(1), wspec, wspec, row(1), row(1), row(1),
                  pl.BlockSpec((D_ATTN + D_LRU, D_MODEL), lambda s: (0, 0),
                               pipeline_mode=pl.Buffered(1)),
                  pl.BlockSpec((1, D_MODEL), lambda s: (0, 0))],
        out_specs=pl.BlockSpec((tm, D_MODEL), prev),
        out_shape=jax.ShapeDtypeStruct((rows, D_MODEL), F32),
        scratch_shapes=[pltpu.VMEM((D_ATTN + D_LRU, D_MODEL), BF16),
                        pltpu.VMEM((N_LRU_BLOCKS, LRU_BLOCK, 2 * LRU_BLOCK), BF16),
                        pltpu.VMEM((8 + N_META, D_LRU), F32),
                        pltpu.VMEM((8, D_LRU), F32),
                        pltpu.VMEM((1, D_LRU), F32),
                        pltpu.VMEM((N_LRU_BLOCKS, 8 * SEG_PITCH, LRU_BLOCK), F32),
                        pltpu.VMEM((N_LRU_BLOCKS, 8 * SEG_PITCH, LRU_BLOCK), F32),
                        pltpu.VMEM((tm, D_LRU), BF16),
                        pltpu.VMEM((tm, D_LRU), BF16)],
        compiler_params=pltpu.CompilerParams(
            dimension_semantics=("arbitrary",),
            vmem_limit_bytes=VMEM_LIMIT_OUT),
        name="rglru_out_proj",
    )(x2d, att, gates, gates_meta, gates, conv_w, conv_b, w_a, w_x, b_a, b_x, lru_lambda, w_out,
      final_g)


def kernel(x, meta_tokens, rel_bias, norm_g, w_in, conv_w, conv_b, w_a, b_a, w_x, b_x, lru_lambda,
           lam_q1, lam_k1, lam_q2, lam_k2, subln_g, w_out, final_g):
    batch, seq, _ = x.shape
    x2d = x.reshape(batch * seq, D_MODEL)
    project = functools.partial(_project, w_in=w_in[0], batch=batch, seq=seq)
    vt, v_meta, h2d, h_meta = project((x2d, meta_tokens, norm_g), col_tile0=2, n_slabs=1, mode="vt")
    qk, qk_meta = project((h2d, h_meta), col_tile0=0, n_slabs=2, mode="heads")
    gates, gates_meta = project((h2d, h_meta), col_tile0=3, n_slabs=3, mode="flat")
    att = _attention(qk, qk_meta, vt, v_meta, gates, rel_bias, lam_q1, lam_k1, lam_q2, lam_k2,
                     subln_g)
    out = _tail(x2d, att, gates, gates_meta, conv_w[0], conv_b, w_a[0], b_a, w_x[0], b_x, lru_lambda,
                w_out[0], final_g.reshape(1, D_MODEL), seq=seq)
    return out.reshape(batch, seq, D_MODEL)
```

```python
import functools
import math

import numpy as np
import jax
import jax.numpy as jnp
from jax import lax
from jax.experimental import pallas as pl
from jax.experimental.pallas import tpu as pltpu

D_MODEL = 2048
N_META = 16
D_ATTN = 1024
D_LRU = 1024
N_HEADS = 8
HEAD_DIM = 64
V_DIM = 128
N_LRU_BLOCKS = 8
LRU_BLOCK = 128
CONV_WIDTH = 4
LRU_C = 8.0
N_BUCKETS = 32
MAX_DISTANCE = 128
NORM_EPS = 1e-6
SUBLN_EPS = 1e-5
NEG_INF = -1e30
LAMBDA_INIT = 0.8 - 0.6 * math.exp(-0.3 * 0)
LOG2E = math.log2(math.e)

BF16 = jnp.bfloat16
F32 = jnp.float32

VMEM_LIMIT_PROJ = 56 * 1024 * 1024
VMEM_LIMIT_ATTN = 48 * 1024 * 1024
VMEM_LIMIT_OUT = 56 * 1024 * 1024

PROJ_ROWS = 1024
ATTN_TQ = 512
ATTN_TK = 512
SUM_ROWS = 16
OUT_ROWS = 256
LRU_ROWS = 256
SEG_PITCH = 40


def _bucket_thresholds():
    max_exact = N_BUCKETS // 2
    d = np.arange(0, 4 * MAX_DISTANCE, dtype=np.int64)
    val = (np.log(np.maximum(d, 1).astype(np.float64) / max_exact)
           / math.log(MAX_DISTANCE / max_exact) * (N_BUCKETS - max_exact))
    large = np.minimum(max_exact + np.floor(val + 1e-9).astype(np.int64), N_BUCKETS - 1)
    bucket = np.where(d < max_exact, d, large)
    frac = np.abs(val - np.round(val))
    interior = (d > max_exact) & (d < MAX_DISTANCE)
    assert frac[interior].min() > 1e-3
    assert (np.diff(bucket) >= 0).all() and bucket[MAX_DISTANCE] == N_BUCKETS - 1
    return tuple(int(np.argmax(bucket >= j)) for j in range(1, N_BUCKETS))


BUCKET_THRESHOLDS = _bucket_thresholds()


def _rms(x, g, eps):
    y = x * lax.rsqrt(jnp.mean(x * x, axis=-1, keepdims=True) + eps)
    return y * g


def _dot(a, b):
    return jnp.dot(a, b, preferred_element_type=F32)


def _dot_nt(a, b):
    return lax.dot_general(a, b, (((1,), (1,)), ((), ())), preferred_element_type=F32)


def _proj_kernel(*refs, mode):
    if mode == "vt":
        x_ref, meta_ref, g_ref, w_ref, o_ref, om_ref, h_ref, hm_ref, wb_ref = refs
    else:
        h_ref, hm_ref, w_ref, o_ref, om_ref, wb_ref = refs
    j = pl.program_id(0)
    i = pl.program_id(1)
    out_scale = jnp.where(j == 0, HEAD_DIM ** -0.5 * LOG2E, 1.0).astype(F32) if mode == "heads" else 1.0
    head = lambda y, h: y[:, h * V_DIM:(h + 1) * V_DIM]

    @pl.when(i == 0)
    def _():
        wb_ref[...] = w_ref[...].astype(BF16)
        if mode == "vt":
            hm_ref[...] = _rms(meta_ref[...], g_ref[...], NORM_EPS).astype(BF16)
        ym = _dot(hm_ref[...], wb_ref[...]) * out_scale
        if mode == "flat":
            om_ref[0] = ym
        else:
            for h in range(N_HEADS):
                om_ref[0, h] = head(ym, h).astype(BF16)

    if mode == "vt":
        h_ref[...] = _rms(x_ref[...], g_ref[...], NORM_EPS).astype(BF16)
    y = _dot(h_ref[...], wb_ref[...]) * out_scale
    if mode == "flat":
        o_ref[0] = y
    elif mode == "heads":
        for h in range(N_HEADS):
            o_ref[0, 0, h] = head(y, h).astype(BF16)
    else:
        rows = y.shape[0]
        for h in range(N_HEADS):
            o_ref[0, h, pl.ds(0, V_DIM), :] = head(y, h).T.astype(BF16)
            o_ref[0, h, pl.ds(V_DIM, SUM_ROWS), :] = jnp.ones((SUM_ROWS, rows), BF16)


def _project(inputs, w_in, *, col_tile0, n_slabs, mode, batch, seq):
    rows = inputs[0].shape[0]
    tm = PROJ_ROWS
    n_i = rows // tm
    n_ib = seq // tm
    row_tile = pl.BlockSpec((tm, D_MODEL), lambda j, i: (i, 0))
    meta_rows = pl.BlockSpec((N_META, D_MODEL), lambda j, i: (0, 0))
    if mode == "vt":
        in_specs = [row_tile, meta_rows, pl.BlockSpec((1, D_MODEL), lambda j, i: (0, 0))]
    else:
        in_specs = [row_tile, meta_rows]
    in_specs.append(pl.BlockSpec((D_MODEL, D_ATTN), lambda j, i: (0, j + col_tile0)))
    if mode == "heads":
        out_shape = (jax.ShapeDtypeStruct((n_slabs, batch, N_HEADS, seq, V_DIM), BF16),
                     jax.ShapeDtypeStruct((n_slabs, N_HEADS, N_META, V_DIM), BF16))
        out_specs = (pl.BlockSpec((1, 1, N_HEADS, tm, V_DIM),
                                  lambda j, i: (j, i // n_ib, 0, i % n_ib, 0)),
                     pl.BlockSpec((1, N_HEADS, N_META, V_DIM), lambda j, i: (j, 0, 0, 0)))
    elif mode == "vt":
        assert n_slabs == 1
        out_shape = (jax.ShapeDtypeStruct((batch, N_HEADS, V_DIM + SUM_ROWS, seq), BF16),
                     jax.ShapeDtypeStruct((n_slabs, N_HEADS, N_META, V_DIM), BF16),
                     jax.ShapeDtypeStruct((rows, D_MODEL), BF16),
                     jax.ShapeDtypeStruct((N_META, D_MODEL), BF16))
        out_specs = (pl.BlockSpec((1, N_HEADS, V_DIM + SUM_ROWS, tm),
                                  lambda j, i: (i // n_ib, 0, 0, i % n_ib)),
                     pl.BlockSpec((1, N_HEADS, N_META, V_DIM), lambda j, i: (j, 0, 0, 0)),
                     row_tile, meta_rows)
    else:
        out_shape = (jax.ShapeDtypeStruct((n_slabs, rows, D_ATTN), F32),
                     jax.ShapeDtypeStruct((n_slabs, N_META, D_ATTN), F32))
        out_specs = (pl.BlockSpec((1, tm, D_ATTN), lambda j, i: (j, i, 0)),
                     pl.BlockSpec((1, N_META, D_ATTN), lambda j, i: (j, 0, 0)))
    return pl.pallas_call(
        functools.partial(_proj_kernel, mode=mode),
        grid=(n_slabs, n_i),
        in_specs=in_specs,
        out_specs=out_specs,
        out_shape=out_shape,
        scratch_shapes=[pltpu.VMEM((D_MODEL, D_ATTN), BF16)],
        compiler_params=pltpu.CompilerParams(
            dimension_semantics=("arbitrary", "arbitrary"),
            vmem_limit_bytes=VMEM_LIMIT_PROJ),
        name={"heads": "proj_qk", "vt": "proj_v", "flat": "proj_u"}[mode],
    )(*inputs, w_in)


def _toeplitz_bias(dist, rb_ref, h, far):
    b = jnp.full(dist.shape, (rb_ref[0, h] - far) * LOG2E, F32)
    for j, thr in enumerate(BUCKET_THRESHOLDS, start=1):
        b = jnp.where(dist >= thr, (rb_ref[j, h] - far) * LOG2E, b)
    return b


def _fill_bias(ref, lead, n_rows, n_cols, d0, rb_ref, h, far):
    sub = lax.broadcasted_iota(jnp.int32, (8, V_DIM), 0)
    lane = lax.broadcasted_iota(jnp.int32, (8, V_DIM), 1)
    zeros = jnp.zeros((8, V_DIM), F32)
    masked = jnp.full((8, V_DIM), NEG_INF, F32)
    cache = {}
    for a8 in range(n_rows // 8):
        for b in range(n_cols // V_DIM):
            off = d0 + V_DIM * b - 8 * a8
            if off + V_DIM - 1 < 0:
                tile = masked
            elif off - 7 >= MAX_DISTANCE:
                tile = zeros
            else:
                if off not in cache:
                    d = off + lane - sub
                    cache[off] = jnp.where(d >= 0, _toeplitz_bias(d, rb_ref, h, far), NEG_INF)
                tile = cache[off]
            ref[(*lead, pl.ds(8 * a8, 8), pl.ds(V_DIM * b, V_DIM))] = tile


def _attn_kernel(rb_ref, lq1_ref, lk1_ref, lq2_ref, lk2_ref, q_ref, k_ref, vt_ref, km_ref, vm_ref,
                 g_ref, sg_ref, o_ref, vmt_ref, bd_ref, bm_ref, bn_ref, sbuf, sc_ref, mx_ref,
                 m_ref, acc_ref):
    h = pl.program_id(0)
    tq, tk = ATTN_TQ, ATTN_TK
    seq = k_ref.shape[2]
    nq = seq // tq

    far = rb_ref[N_BUCKETS - 1, h]

    @pl.when(pl.program_id(1) == 0)
    def _():
        _fill_bias(bd_ref, (), tk, tq, 0, rb_ref, h, far)
        _fill_bias(bn_ref, (), MAX_DISTANCE, MAX_DISTANCE, MAX_DISTANCE, rb_ref, h, far)
        _fill_bias(bm_ref, (1,), N_META, tq, N_META, rb_ref, h, far)
        bm_ref[0] = jnp.zeros((N_META, tq), F32)
        padded = jnp.concatenate(
            [vm_ref[0].astype(F32), jnp.zeros((V_DIM - N_META, V_DIM), F32)], axis=0)
        vmt_ref[pl.ds(0, V_DIM), :] = padded.T.astype(BF16)
        vmt_ref[pl.ds(V_DIM, SUM_ROWS), :] = jnp.ones((SUM_ROWS, V_DIM), BF16)

    lane = lax.broadcasted_iota(jnp.int32, (tq, V_DIM), 1)

    lam = (jnp.exp(jnp.sum(lq1_ref[...] * lk1_ref[...], keepdims=True))
           - jnp.exp(jnp.sum(lq2_ref[...] * lk2_ref[...], keepdims=True))
           + LAMBDA_INIT)

    def init_stats(st):
        m_ref[st] = jnp.full(m_ref.shape[1:], NEG_INF, F32)
        for a in range(2):
            acc_ref[st, a, pl.ds(0, V_DIM), :] = jnp.zeros((V_DIM, tq), F32)
            acc_ref[st, a, pl.ds(V_DIM, SUM_ROWS), :] = jnp.ones((SUM_ROWS, tq), F32)

    hk, hq = tk // 2, tq // 2

    def produce(buf, q_off, k_off, kind):
        q = q_ref[0, 0, pl.ds(pl.multiple_of(q_off, tq), tq), :]
        zero = jnp.zeros_like(q)
        k_off = pl.multiple_of(k_off, tk)
        k_t = k_ref[0, 0, pl.ds(k_off, tk), :]
        for a in range(2):
            qa = jnp.where((lane < HEAD_DIM) if a == 0 else (lane >= HEAD_DIM), q, zero)
            if kind == "diag":
                top = _dot_nt(k_t[:hk], qa) + bd_ref[pl.ds(0, hk), :]
                low = _dot_nt(k_t[hk:], qa[hq:]) + bd_ref[pl.ds(hk, hk), pl.ds(hq, hq)]
                sc = _dot_nt(km_ref[0], qa) + bm_ref[jnp.where(k_off == 0, 1, 0)]
                sbuf[buf, a, pl.ds(0, hk), :] = top
                sbuf[buf, a, pl.ds(hk, hk), pl.ds(hq, hq)] = low
                sc_ref[buf, a] = sc
                mx = jnp.maximum(jnp.max(top, axis=0, keepdims=True),
                                 jnp.max(sc, axis=0, keepdims=True))
                mx_low = jnp.max(low, axis=0, keepdims=True)
                mx_ref[buf, a] = jnp.concatenate(
                    [mx[:, :hq], jnp.maximum(mx[:, hq:], mx_low)], axis=1)
                continue
            s = _dot_nt(k_t, qa)
            if kind == "near":
                band = tk - MAX_DISTANCE
                corner = s[band:, :MAX_DISTANCE] + bn_ref[...]
                sbuf[buf, a, pl.ds(0, band), :] = s[:band]
                sbuf[buf, a, pl.ds(band, MAX_DISTANCE), pl.ds(0, MAX_DISTANCE)] = corner
                sbuf[buf, a, pl.ds(band, MAX_DISTANCE), pl.ds(MAX_DISTANCE, tq - MAX_DISTANCE)] = (
                    s[band:, MAX_DISTANCE:])
                mx = jnp.max(s[:band], axis=0, keepdims=True)
                mx_ref[buf, a] = jnp.concatenate(
                    [jnp.maximum(mx[:, :MAX_DISTANCE], jnp.max(corner, axis=0, keepdims=True)),
                     jnp.maximum(mx[:, MAX_DISTANCE:],
                                 jnp.max(s[band:, MAX_DISTANCE:], axis=0, keepdims=True))], axis=1)
                continue
            sbuf[buf, a] = s
            mx_ref[buf, a] = jnp.max(s, axis=0, keepdims=True)

    def consume(buf, st, k_off, diag=False):
        k_off = pl.multiple_of(k_off, tk)
        v_t = vt_ref[0, 0, :, pl.ds(k_off, tk)]
        for a in range(2):
            m_old = m_ref[st, a]
            m_new = jnp.maximum(m_old, mx_ref[buf, a])
            alpha = jnp.exp2(m_old - m_new)
            if diag:
                top = jnp.exp2(sbuf[buf, a, pl.ds(0, hk), :] - m_new)
                low = jnp.exp2(sbuf[buf, a, pl.ds(hk, hk), pl.ds(hq, hq)] - m_new[:, hq:])
                pc = jnp.exp2(sc_ref[buf, a] - m_new)
                pv = (_dot(v_t[:, :hk], top.astype(BF16))
                      + _dot(vmt_ref[...][:, :N_META], pc.astype(BF16)))
                pv_low = _dot(v_t[:, hk:], low.astype(BF16))
                pv = jnp.concatenate([pv[:, :hq], pv[:, hq:] + pv_low], axis=1)
            else:
                p = jnp.exp2(sbuf[buf, a] - m_new)
                pv = _dot(v_t, p.astype(BF16))
            acc_ref[st, a] = alpha * acc_ref[st, a] + pv
            m_ref[st, a] = m_new

    def finalize(st, q_off):
        q_rows = pl.ds(pl.multiple_of(q_off, tq), tq)
        heads = [acc_ref[st, a, pl.ds(0, V_DIM), :] / acc_ref[st, a, pl.ds(V_DIM, 1), :]
                 for a in range(2)]
        out_t = heads[0] - lam * heads[1]
        inv = lax.rsqrt(jnp.mean(out_t * out_t, axis=0, keepdims=True) + SUBLN_EPS)
        att = ((out_t * inv).T * sg_ref[...]) * (1.0 - LAMBDA_INIT)
        g = g_ref[0, q_rows, :]
        o_ref[q_rows, :] = (att * (g * jax.nn.sigmoid(g))).astype(o_ref.dtype)

    init_stats(0)
    init_stats(1)
    produce(0, 0, 0, "diag")

    def q_tile(i, cur, odd):
        oth = 1 - cur
        q_off = i * tq

        def far_pair(tt, c):
            t = 2 * tt
            produce(oth, q_off, (t + 1) * tk, "far")
            consume(cur, cur, t * tk)
            produce(cur, q_off, (t + 2) * tk, "far")
            consume(oth, cur, (t + 1) * tk)
            return c

        n_far = jnp.maximum(i - 2, 0)
        lax.fori_loop(0, n_far // 2, far_pair, 0)

        def near_diag():
            produce(oth, q_off, (i - 1) * tk, "near")
            consume(cur, cur, (i - 2) * tk)
            produce(cur, q_off, i * tk, "diag")
            consume(oth, cur, (i - 1) * tk)

        def last(next_kind):
            finalize(oth, jnp.maximum(i - 1, 0) * tq)
            init_stats(oth)
            produce(oth, jnp.minimum(i + 1, nq - 1) * tq, 0, next_kind)
            consume(cur, cur, i * tk, diag=True)

        if odd:
            @pl.when(i >= 3)
            def _():
                consume(cur, cur, (n_far - 1) * tk)
                produce(cur, q_off, n_far * tk, "far")
                near_diag()
                last("far")

            @pl.when(i == 1)
            def _():
                consume(cur, cur, 0)
                produce(cur, q_off, tk, "diag")
                last("far")
        else:
            @pl.when(i >= 2)
            def _():
                near_diag()
                last("far")

            @pl.when(i == 0)
            def _():
                last("near")

    def q_pair(ii, carry):
        q_tile(2 * ii, 0, False)
        q_tile(2 * ii + 1, 1, True)
        return carry

    lax.fori_loop(0, nq // 2, q_pair, 0)
    finalize((nq - 1) % 2, (nq - 1) * tq)


def _attention(qk, qk_meta, vt, v_meta, gates, rel_bias, lam_q1, lam_k1, lam_q2, lam_k2, subln_g):
    _, batch, _, seq, _ = qk.shape
    tq, tk = ATTN_TQ, ATTN_TK
    smem = pl.BlockSpec(memory_space=pltpu.SMEM)
    row64 = pl.BlockSpec((1, HEAD_DIM), lambda h, b: (0, 0))
    qk_spec = lambda which: pl.BlockSpec((1, 1, 1, seq, V_DIM), lambda h, b: (which, b, h, 0, 0))
    meta_spec = lambda which: pl.BlockSpec((1, 1, N_META, V_DIM), lambda h, b: (which, h, 0, 0))

    def kernel(rb, lq1, lk1, lq2, lk2, q_ref, k_ref, vt_ref, km_ref, vm_ref, *rest):
        _attn_kernel(rb, lq1, lk1, lq2, lk2, q_ref.at[0], k_ref.at[0], vt_ref,
                     km_ref.at[0], vm_ref.at[0], *rest)

    return pl.pallas_call(
        kernel,
        grid=(N_HEADS, batch),
        in_specs=[smem, row64, row64, row64, row64,
                  qk_spec(0), qk_spec(1),
                  pl.BlockSpec((1, 1, V_DIM + SUM_ROWS, seq), lambda h, b: (b, h, 0, 0)),
                  meta_spec(1), meta_spec(0),
                  pl.BlockSpec((1, seq, V_DIM), lambda h, b: (0, b, h)),
                  pl.BlockSpec((1, V_DIM), lambda h, b: (0, 0))],
        out_specs=pl.BlockSpec((seq, V_DIM), lambda h, b: (b, h)),
        out_shape=jax.ShapeDtypeStruct((batch * seq, D_ATTN), BF16),
        scratch_shapes=[pltpu.VMEM((V_DIM + SUM_ROWS, V_DIM), BF16),
                        pltpu.VMEM((tk, tq), F32),
                        pltpu.VMEM((2, N_META, tq), F32),
                        pltpu.VMEM((MAX_DISTANCE, MAX_DISTANCE), F32),
                        pltpu.VMEM((2, 2, tk, tq), F32),
                        pltpu.VMEM((2, 2, N_META, tq), F32),
                        pltpu.VMEM((2, 2, 1, tq), F32),
                        pltpu.VMEM((2, 2, 1, tq), F32),
                        pltpu.VMEM((2, 2, V_DIM + SUM_ROWS, tq), F32)],
        compiler_params=pltpu.CompilerParams(
            dimension_semantics=("arbitrary", "arbitrary"),
            vmem_limit_bytes=VMEM_LIMIT_ATTN),
        name="diff_attention",
    )(rel_bias, lam_q1, lam_k1, lam_q2, lam_k2, qk, qk, vt, qk_meta, v_meta, gates, subln_g)


def _scan_block(a, b):
    n = a.shape[0]
    row = lax.broadcasted_iota(jnp.int32, a.shape, 0)
    s = 1
    while s < n:
        keep = row >= s
        a_sh = jnp.where(keep, pltpu.roll(a, s, 0), 1.0)
        b_sh = jnp.where(keep, pltpu.roll(b, s, 0), 0.0)
        b = b + a * b_sh
        a = a * a_sh
        s *= 2
    return a, b


def _gate_lru_kernel(h_ref, w_ref, u_ref, um_ref, cw_ref, cb_ref, wa_ref, wx_ref, ba_ref, bx_ref,
                     lam_ref, o_ref, rec_ref, wb_ref, wg_ref, ubuf, tail_ref, st_ref, u_p, h_p,
                     *, tiles_per_batch):
    i = pl.program_id(1)
    n = LRU_ROWS
    hist = 8
    seg = n // 8
    n_blk = u_ref.shape[1] // LRU_BLOCK
    n_out = D_ATTN // n_blk

    x = -lam_ref[...]
    softplus = jnp.maximum(x, 0.0) + jnp.log1p(jnp.exp(-jnp.abs(x)))

    @pl.when(i == 0)
    def _():
        wb_ref[...] = w_ref[...].astype(BF16)
        for blk in range(n_blk):
            wg_ref[blk, :, pl.ds(0, LRU_BLOCK)] = wa_ref[blk].astype(BF16)
            wg_ref[blk, :, pl.ds(LRU_BLOCK, LRU_BLOCK)] = wx_ref[blk].astype(BF16)

    def gate_inputs(rows, cols, blk):
        cw = cw_ref[:, cols]
        uc = (cw[3:4] * ubuf[pl.ds(hist, rows), cols] + cw[2:3] * ubuf[pl.ds(hist - 1, rows), cols]
              + cw[1:2] * ubuf[pl.ds(hist - 2, rows), cols] + cw[0:1] * ubuf[pl.ds(hist - 3, rows), cols]
              + cb_ref[:, cols])
        return uc, _dot(uc.astype(BF16), wg_ref[blk])

    def gates(uc, pre, cols, first):
        r = jax.nn.sigmoid(pre[:, :LRU_BLOCK] + ba_ref[:, cols])
        gi = jax.nn.sigmoid(pre[:, LRU_BLOCK:] + bx_ref[:, cols])
        log_a = -LRU_C * r * softplus[:, cols]
        a = jnp.exp(log_a)
        v = jnp.tanh(-log_a) * (a * a + 1.0)
        mult = jnp.where(v > 0.0, v * lax.rsqrt(v), 0.0)
        if first:
            row = lax.broadcasted_iota(jnp.int32, mult.shape, 0)
            mult = jnp.where(row == 0, 1.0, mult)
        return a, mult * gi * uc

    @pl.when(i % tiles_per_batch == 0)
    def _():
        ubuf[pl.ds(0, hist), :] = jnp.zeros((hist, ubuf.shape[1]), F32)
        ubuf[pl.ds(hist, N_META), :] = um_ref[...]
        for blk in range(n_blk):
            cols = slice(blk * LRU_BLOCK, (blk + 1) * LRU_BLOCK)
            a, b = gates(*gate_inputs(N_META, cols, blk), cols, True)
            _, hm = _scan_block(a, b)
            st_ref[:, cols] = hm[N_META - 1:N_META, :]
        tail_ref[...] = um_ref[pl.ds(N_META - hist, hist), :]

    sub = lax.broadcasted_iota(jnp.int32, (8, LRU_BLOCK), 0)
    step = lambda j: pl.ds(j, 8, stride=SEG_PITCH)

    def piece(t, carry):
        r0 = pl.multiple_of(t * n, n)
        for blk in range(n_blk):
            lru_block(r0, blk)
        tail_ref[...] = u_ref[pl.ds(r0 + n - hist, hist), :]
        return carry

    def lru_block(r0, blk):
        cols = slice(blk * LRU_BLOCK, (blk + 1) * LRU_BLOCK)
        ocols = slice(blk * n_out, (blk + 1) * n_out)
        o_ref[0, pl.ds(r0, n), ocols] = _dot(h_ref[pl.ds(r0, n), :], wb_ref[:, ocols])

        for k in range(8):
            u_p[blk, pl.ds(SEG_PITCH * k, seg), :] = u_ref[pl.ds(r0 + seg * k, seg), cols]
        us = [u_p[blk, step(j), :] for j in range(seg)]
        prev = tail_ref[:, cols]

        def before(back):
            return jnp.where(sub == 0, prev[hist - back:hist - back + 1, :],
                             pltpu.roll(us[seg - back], 1, 0))

        older = {-back: before(back) for back in range(1, CONV_WIDTH)}
        u_at = lambda j: us[j] if j >= 0 else older[j]
        cw = cw_ref[:, cols]
        cb = cb_ref[:, cols]
        uc = jnp.concatenate(
            [cw[3:4] * u_at(j) + cw[2:3] * u_at(j - 1) + cw[1:2] * u_at(j - 2)
             + cw[0:1] * u_at(j - 3) + cb for j in range(seg)], axis=0)
        a, b = gates(uc, _dot(uc.astype(BF16), wg_ref[blk]), cols, False)
        a_j = lambda j: a[8 * j:8 * j + 8]
        b_j = lambda j: b[8 * j:8 * j + 8]
        a_run, h_run = a_j(0), b_j(0)
        for j in range(1, seg):
            h_run = a_j(j) * h_run + b_j(j)
            a_run = a_j(j) * a_run
        a_cum, h_cum = _scan_block(a_run, h_run)
        carry = st_ref[:, cols]
        seg_end = h_cum + a_cum * carry
        st_ref[:, cols] = seg_end[7:8, :]
        h_run = jnp.where(sub == 0, carry, pltpu.roll(seg_end, 1, 0))
        for j in range(seg):
            h_run = a_j(j) * h_run + b_j(j)
            h_p[blk, step(j), :] = h_run
        for k in range(8):
            rec_ref[pl.ds(r0 + seg * k, seg), cols] = h_p[blk, pl.ds(SEG_PITCH * k, seg), :]

    lax.fori_loop(0, h_ref.shape[0] // n, piece, 0, unroll=True)


def _gate_lru(h2d, w_in, u, u_meta, conv_w, conv_b, w_a, b_a, w_x, b_x, lru_lambda, *, seq):
    rows = h2d.shape[0]
    tm = PROJ_ROWS
    n_halves = 2
    width = D_LRU // n_halves
    n_blk = N_LRU_BLOCKS // n_halves
    gate_col_tile = lambda j: 3 + 2 * j
    half = lambda n: pl.BlockSpec((n, width), lambda j, i: (0, j))
    wspec = pl.BlockSpec((n_blk, LRU_BLOCK, LRU_BLOCK), lambda j, i: (j, 0, 0))

    def kernel(h_ref, w_ref, u_ref, um_ref, *rest):
        _gate_lru_kernel(h_ref, w_ref, u_ref.at[0], um_ref.at[0], *rest, tiles_per_batch=seq // tm)

    return pl.pallas_call(
        kernel,
        grid=(n_halves, rows // tm),
        in_specs=[pl.BlockSpec((tm, D_MODEL), lambda j, i: (i, 0)),
                  pl.BlockSpec((D_MODEL, D_ATTN), lambda j, i: (0, gate_col_tile(j))),
                  pl.BlockSpec((1, tm, width), lambda j, i: (0, i, j)),
                  pl.BlockSpec((1, N_META, width), lambda j, i: (0, 0, j)),
                  half(CONV_WIDTH), half(1), wspec, wspec, half(1), half(1), half(1)],
        out_specs=(pl.BlockSpec((1, tm, D_ATTN), lambda j, i: (j, i, 0)),
                   pl.BlockSpec((tm, width), lambda j, i: (i, j))),
        out_shape=(jax.ShapeDtypeStruct((n_halves, rows, D_ATTN), F32),
                   jax.ShapeDtypeStruct((rows, D_LRU), F32)),
        scratch_shapes=[pltpu.VMEM((D_MODEL, D_ATTN), BF16),
                        pltpu.VMEM((n_blk, LRU_BLOCK, 2 * LRU_BLOCK), BF16),
                        pltpu.VMEM((8 + N_META, width), F32),
                        pltpu.VMEM((8, width), F32),
                        pltpu.VMEM((1, width), F32),
                        pltpu.VMEM((n_blk, 8 * SEG_PITCH, LRU_BLOCK), F32),
                        pltpu.VMEM((n_blk, 8 * SEG_PITCH, LRU_BLOCK), F32)],
        compiler_params=pltpu.CompilerParams(
            dimension_semantics=("arbitrary", "arbitrary"),
            vmem_limit_bytes=VMEM_LIMIT_PROJ),
        name="proj_gates_rglru",
    )(h2d, w_in, u, u_meta, conv_w, conv_b, w_a, w_x, b_a, b_x, lru_lambda)


def _out_kernel(x_ref, att_ref, rec_ref, g_ref, w_ref, fg_ref, o_ref, wb_ref, rec_s):
    @pl.when(pl.program_id(0) == 0)
    def _():
        wb_ref[...] = w_ref[...].astype(BF16)

    g = g_ref[0]
    rec_s[...] = (rec_ref[...] * (g * jax.nn.sigmoid(g))).astype(BF16)
    n_out = 2 * LRU_BLOCK
    ssq = jnp.zeros((x_ref.shape[0], LRU_BLOCK), F32)
    for blk in range(D_MODEL // n_out):
        ocols = slice(blk * n_out, (blk + 1) * n_out)
        z = (x_ref[:, ocols] + _dot(att_ref[...], wb_ref[pl.ds(0, D_ATTN), ocols])
             + _dot(rec_s[...], wb_ref[pl.ds(D_ATTN, D_LRU), ocols]))
        o_ref[:, ocols] = z
        for part in range(n_out // LRU_BLOCK):
            zp = z[:, part * LRU_BLOCK:(part + 1) * LRU_BLOCK]
            ssq = ssq + zp * zp
    inv = lax.rsqrt(jnp.sum(ssq, axis=1, keepdims=True) * (1.0 / D_MODEL) + NORM_EPS)
    o_ref[...] = (o_ref[...] * inv) * fg_ref[...]


def _out_proj(x2d, att, rec, gates, w_out, final_g):
    rows = x2d.shape[0]
    tm = OUT_ROWS
    tile = lambda width: pl.BlockSpec((tm, width), lambda s: (s, 0))
    return pl.pallas_call(
        _out_kernel,
        grid=(rows // tm,),
        in_specs=[tile(D_MODEL), tile(D_ATTN), tile(D_LRU),
                  pl.BlockSpec((1, tm, D_LRU), lambda s: (1, s, 0)),
                  pl.BlockSpec((D_ATTN + D_LRU, D_MODEL), lambda s: (0, 0),
                               pipeline_mode=pl.Buffered(1)),
                  pl.BlockSpec((1, D_MODEL), lambda s: (0, 0))],
        out_specs=tile(D_MODEL),
        out_shape=jax.ShapeDtypeStruct((rows, D_MODEL), F32),
        scratch_shapes=[pltpu.VMEM((D_ATTN + D_LRU, D_MODEL), BF16),
                        pltpu.VMEM((tm, D_LRU), BF16)],
        compiler_params=pltpu.CompilerParams(
            dimension_semantics=("arbitrary",),
            vmem_limit_bytes=VMEM_LIMIT_OUT),
        name="out_proj",
    )(x2d, att, rec, gates, w_out, final_g)


def kernel(x, meta_tokens, rel_bias, norm_g, w_in, conv_w, conv_b, w_a, b_a, w_x, b_x, lru_lambda,
           lam_q1, lam_k1, lam_q2, lam_k2, subln_g, w_out, final_g):
    batch, seq, _ = x.shape
    x2d = x.reshape(batch * seq, D_MODEL)
    project = functools.partial(_project, w_in=w_in[0], batch=batch, seq=seq)
    vt, v_meta, h2d, h_meta = project((x2d, meta_tokens, norm_g), col_tile0=2, n_slabs=1, mode="vt")
    qk, qk_meta = project((h2d, h_meta), col_tile0=0, n_slabs=2, mode="heads")
    u, u_meta = project((h2d, h_meta), col_tile0=4, n_slabs=1, mode="flat")
    gates, rec = _gate_lru(h2d, w_in[0], u, u_meta, conv_w[0], conv_b, w_a[0], b_a, w_x[0], b_x,
                           lru_lambda, seq=seq)
    att = _attention(qk, qk_meta, vt, v_meta, gates, rel_bias, lam_q1, lam_k1, lam_q2, lam_k2,
                     subln_g)
    out = _out_proj(x2d, att, rec, gates, w_out[0], final_g.reshape(1, D_MODEL))
    return out.reshape(batch, seq, D_MODEL)
```

```python
import functools
import math

import numpy as np
import jax
import jax.numpy as jnp
from jax import lax
from jax.experimental import pallas as pl
from jax.experimental.pallas import tpu as pltpu

D_MODEL = 2048
N_META = 16
D_ATTN = 1024
D_LRU = 1024
N_HEADS = 8
HEAD_DIM = 64
V_DIM = 128
N_LRU_BLOCKS = 8
LRU_BLOCK = 128
CONV_WIDTH = 4
LRU_C = 8.0
N_BUCKETS = 32
MAX_DISTANCE = 128
NORM_EPS = 1e-6
SUBLN_EPS = 1e-5
NEG_INF = -1e30
LAMBDA_INIT = 0.8 - 0.6 * math.exp(-0.3 * 0)
LOG2E = math.log2(math.e)

BF16 = jnp.bfloat16
F32 = jnp.float32

VMEM_LIMIT_PROJ = 56 * 1024 * 1024
VMEM_LIMIT_ATTN = 48 * 1024 * 1024
VMEM_LIMIT_OUT = 56 * 1024 * 1024

PROJ_ROWS = 1024
ATTN_TQ = 512
ATTN_TK = 512
SUM_ROWS = 16
OUT_ROWS = 512
OUT_PIECE = 256
LRU_ROWS = 256
SEG_PITCH = 40


def _bucket_thresholds():
    max_exact = N_BUCKETS // 2
    d = np.arange(0, 4 * MAX_DISTANCE, dtype=np.int64)
    val = (np.log(np.maximum(d, 1).astype(np.float64) / max_exact)
           / math.log(MAX_DISTANCE / max_exact) * (N_BUCKETS - max_exact))
    large = np.minimum(max_exact + np.floor(val + 1e-9).astype(np.int64), N_BUCKETS - 1)
    bucket = np.where(d < max_exact, d, large)
    frac = np.abs(val - np.round(val))
    interior = (d > max_exact) & (d < MAX_DISTANCE)
    assert frac[interior].min() > 1e-3
    assert (np.diff(bucket) >= 0).all() and bucket[MAX_DISTANCE] == N_BUCKETS - 1
    return tuple(int(np.argmax(bucket >= j)) for j in range(1, N_BUCKETS))


BUCKET_THRESHOLDS = _bucket_thresholds()


def _rms(x, g, eps):
    y = x * lax.rsqrt(jnp.mean(x * x, axis=-1, keepdims=True) + eps)
    return y * g


def _dot(a, b):
    return jnp.dot(a, b, preferred_element_type=F32)


def _dot_nt(a, b):
    return lax.dot_general(a, b, (((1,), (1,)), ((), ())), preferred_element_type=F32)


def _proj_kernel(*refs, mode):
    if mode == "vt":
        x_ref, meta_ref, g_ref, w_ref, o_ref, om_ref, h_ref, hm_ref, wb_ref = refs
    elif mode == "flat":
        h_ref, hm_ref, wo_ref, w_ref, o_ref, om_ref, wob_ref, wb_ref = refs
    else:
        h_ref, hm_ref, w_ref, o_ref, om_ref, wb_ref = refs
    j = pl.program_id(0)
    i = pl.program_id(1)
    out_scale = jnp.where(j == 0, HEAD_DIM ** -0.5 * LOG2E, 1.0).astype(F32) if mode == "heads" else 1.0
    head = lambda y, h: y[:, h * V_DIM:(h + 1) * V_DIM]

    @pl.when(i == 0)
    def _():
        wb_ref[...] = w_ref[...].astype(BF16)
        if mode == "vt":
            hm_ref[...] = _rms(meta_ref[...], g_ref[...], NORM_EPS).astype(BF16)
        ym = _dot(hm_ref[...], wb_ref[...]) * out_scale
        if mode == "flat":
            om_ref[0] = ym
        else:
            for h in range(N_HEADS):
                om_ref[0, h] = head(ym, h).astype(BF16)

    if mode == "vt":
        h_ref[...] = _rms(x_ref[...], g_ref[...], NORM_EPS).astype(BF16)
    y = _dot(h_ref[...], wb_ref[...]) * out_scale
    if mode == "flat":
        o_ref[0] = y
        wob_ref[...] = wo_ref[...].astype(BF16)
    elif mode == "heads":
        for h in range(N_HEADS):
            o_ref[0, 0, h] = head(y, h).astype(BF16)
    else:
        rows = y.shape[0]
        for h in range(N_HEADS):
            o_ref[0, h, pl.ds(0, V_DIM), :] = head(y, h).T.astype(BF16)
            o_ref[0, h, pl.ds(V_DIM, SUM_ROWS), :] = jnp.ones((SUM_ROWS, rows), BF16)


def _project(inputs, w_in, *, col_tile0, n_slabs, mode, batch, seq):
    rows = inputs[0].shape[0]
    tm = PROJ_ROWS
    n_i = rows // tm
    n_ib = seq // tm
    row_tile = pl.BlockSpec((tm, D_MODEL), lambda j, i: (i, 0))
    meta_rows = pl.BlockSpec((N_META, D_MODEL), lambda j, i: (0, 0))
    if mode == "vt":
        in_specs = [row_tile, meta_rows, pl.BlockSpec((1, D_MODEL), lambda j, i: (0, 0))]
    elif mode == "flat":
        assert n_slabs == 1
        w_out_rows = pl.BlockSpec(((D_ATTN + D_LRU) // n_i, D_MODEL), lambda j, i: (i, 0))
        in_specs = [row_tile, meta_rows, w_out_rows]
    else:
        in_specs = [row_tile, meta_rows]
    in_specs.append(pl.BlockSpec((D_MODEL, D_ATTN), lambda j, i: (0, j + col_tile0)))
    if mode == "heads":
        out_shape = (jax.ShapeDtypeStruct((n_slabs, batch, N_HEADS, seq, V_DIM), BF16),
                     jax.ShapeDtypeStruct((n_slabs, N_HEADS, N_META, V_DIM), BF16))
        out_specs = (pl.BlockSpec((1, 1, N_HEADS, tm, V_DIM),
                                  lambda j, i: (j, i // n_ib, 0, i % n_ib, 0)),
                     pl.BlockSpec((1, N_HEADS, N_META, V_DIM), lambda j, i: (j, 0, 0, 0)))
    elif mode == "vt":
        assert n_slabs == 1
        out_shape = (jax.ShapeDtypeStruct((batch, N_HEADS, V_DIM + SUM_ROWS, seq), BF16),
                     jax.ShapeDtypeStruct((n_slabs, N_HEADS, N_META, V_DIM), BF16),
                     jax.ShapeDtypeStruct((rows, D_MODEL), BF16),
                     jax.ShapeDtypeStruct((N_META, D_MODEL), BF16))
        out_specs = (pl.BlockSpec((1, N_HEADS, V_DIM + SUM_ROWS, tm),
                                  lambda j, i: (i // n_ib, 0, 0, i % n_ib)),
                     pl.BlockSpec((1, N_HEADS, N_META, V_DIM), lambda j, i: (j, 0, 0, 0)),
                     row_tile, meta_rows)
    else:
        out_shape = (jax.ShapeDtypeStruct((n_slabs, rows, D_ATTN), F32),
                     jax.ShapeDtypeStruct((n_slabs, N_META, D_ATTN), F32),
                     jax.ShapeDtypeStruct((D_ATTN + D_LRU, D_MODEL), BF16))
        out_specs = (pl.BlockSpec((1, tm, D_ATTN), lambda j, i: (j, i, 0)),
                     pl.BlockSpec((1, N_META, D_ATTN), lambda j, i: (j, 0, 0)),
                     w_out_rows)
    return pl.pallas_call(
        functools.partial(_proj_kernel, mode=mode),
        grid=(n_slabs, n_i),
        in_specs=in_specs,
        out_specs=out_specs,
        out_shape=out_shape,
        scratch_shapes=[pltpu.VMEM((D_MODEL, D_ATTN), BF16)],
        compiler_params=pltpu.CompilerParams(
            dimension_semantics=("arbitrary", "arbitrary"),
            vmem_limit_bytes=VMEM_LIMIT_PROJ),
        name={"heads": "proj_qk", "vt": "proj_v", "flat": "proj_u"}[mode],
    )(*inputs, w_in)


def _toeplitz_bias(dist, rb_ref, h, far):
    b = jnp.full(dist.shape, (rb_ref[0, h] - far) * LOG2E, F32)
    for j, thr in enumerate(BUCKET_THRESHOLDS, start=1):
        b = jnp.where(dist >= thr, (rb_ref[j, h] - far) * LOG2E, b)
    return b


def _fill_bias(ref, lead, n_rows, n_cols, d0, rb_ref, h, far):
    sub = lax.broadcasted_iota(jnp.int32, (8, V_DIM), 0)
    lane = lax.broadcasted_iota(jnp.int32, (8, V_DIM), 1)
    zeros = jnp.zeros((8, V_DIM), F32)
    masked = jnp.full((8, V_DIM), NEG_INF, F32)
    cache = {}
    for a8 in range(n_rows // 8):
        for b in range(n_cols // V_DIM):
            off = d0 + V_DIM * b - 8 * a8
            if off + V_DIM - 1 < 0:
                tile = masked
            elif off - 7 >= MAX_DISTANCE:
                tile = zeros
            else:
                if off not in cache:
                    d = off + lane - sub
                    cache[off] = jnp.where(d >= 0, _toeplitz_bias(d, rb_ref, h, far), NEG_INF)
                tile = cache[off]
            ref[(*lead, pl.ds(8 * a8, 8), pl.ds(V_DIM * b, V_DIM))] = tile


def _attn_kernel(rb_ref, lq1_ref, lk1_ref, lq2_ref, lk2_ref, q_ref, k_ref, vt_ref, km_ref, vm_ref,
                 g_ref, sg_ref, o_ref, vmt_ref, bd_ref, bm_ref, bn_ref, sbuf, sc_ref, mx_ref,
                 m_ref, acc_ref):
    h = pl.program_id(0)
    tq, tk = ATTN_TQ, ATTN_TK
    seq = k_ref.shape[2]
    nq = seq // tq

    far = rb_ref[N_BUCKETS - 1, h]

    @pl.when(pl.program_id(1) == 0)
    def _():
        _fill_bias(bd_ref, (), tk, tq, 0, rb_ref, h, far)
        _fill_bias(bn_ref, (), MAX_DISTANCE, MAX_DISTANCE, MAX_DISTANCE, rb_ref, h, far)
        _fill_bias(bm_ref, (1,), N_META, tq, N_META, rb_ref, h, far)
        bm_ref[0] = jnp.zeros((N_META, tq), F32)
        padded = jnp.concatenate(
            [vm_ref[0].astype(F32), jnp.zeros((V_DIM - N_META, V_DIM), F32)], axis=0)
        vmt_ref[pl.ds(0, V_DIM), :] = padded.T.astype(BF16)
        vmt_ref[pl.ds(V_DIM, SUM_ROWS), :] = jnp.ones((SUM_ROWS, V_DIM), BF16)

    lane = lax.broadcasted_iota(jnp.int32, (tq, V_DIM), 1)

    lam = (jnp.exp(jnp.sum(lq1_ref[...] * lk1_ref[...], keepdims=True))
           - jnp.exp(jnp.sum(lq2_ref[...] * lk2_ref[...], keepdims=True))
           + LAMBDA_INIT)

    def init_stats(st):
        m_ref[st] = jnp.full(m_ref.shape[1:], NEG_INF, F32)
        for a in range(2):
            acc_ref[st, a, pl.ds(0, V_DIM), :] = jnp.zeros((V_DIM, tq), F32)
            acc_ref[st, a, pl.ds(V_DIM, SUM_ROWS), :] = jnp.ones((SUM_ROWS, tq), F32)

    hk, hq = tk // 2, tq // 2

    def produce(buf, q_off, k_off, kind):
        q = q_ref[0, 0, pl.ds(pl.multiple_of(q_off, tq), tq), :]
        zero = jnp.zeros_like(q)
        k_off = pl.multiple_of(k_off, tk)
        k_t = k_ref[0, 0, pl.ds(k_off, tk), :]
        for a in range(2):
            qa = jnp.where((lane < HEAD_DIM) if a == 0 else (lane >= HEAD_DIM), q, zero)
            if kind == "diag":
                top = _dot_nt(k_t[:hk], qa) + bd_ref[pl.ds(0, hk), :]
                low = _dot_nt(k_t[hk:], qa[hq:]) + bd_ref[pl.ds(hk, hk), pl.ds(hq, hq)]
                sc = _dot_nt(km_ref[0], qa) + bm_ref[jnp.where(k_off == 0, 1, 0)]
                sbuf[buf, a, pl.ds(0, hk), :] = top
                sbuf[buf, a, pl.ds(hk, hk), pl.ds(hq, hq)] = low
                sc_ref[buf, a] = sc
                mx = jnp.maximum(jnp.max(top, axis=0, keepdims=True),
                                 jnp.max(sc, axis=0, keepdims=True))
                mx_low = jnp.max(low, axis=0, keepdims=True)
                mx_ref[buf, a] = jnp.concatenate(
                    [mx[:, :hq], jnp.maximum(mx[:, hq:], mx_low)], axis=1)
                continue
            s = _dot_nt(k_t, qa)
            if kind == "near":
                band = tk - MAX_DISTANCE
                corner = s[band:, :MAX_DISTANCE] + bn_ref[...]
                sbuf[buf, a, pl.ds(0, band), :] = s[:band]
                sbuf[buf, a, pl.ds(band, MAX_DISTANCE), pl.ds(0, MAX_DISTANCE)] = corner
                sbuf[buf, a, pl.ds(band, MAX_DISTANCE), pl.ds(MAX_DISTANCE, tq - MAX_DISTANCE)] = (
                    s[band:, MAX_DISTANCE:])
                mx = jnp.max(s[:band], axis=0, keepdims=True)
                mx_ref[buf, a] = jnp.concatenate(
                    [jnp.maximum(mx[:, :MAX_DISTANCE], jnp.max(corner, axis=0, keepdims=True)),
                     jnp.maximum(mx[:, MAX_DISTANCE:],
                                 jnp.max(s[band:, MAX_DISTANCE:], axis=0, keepdims=True))], axis=1)
                continue
            sbuf[buf, a] = s
            mx_ref[buf, a] = jnp.max(s, axis=0, keepdims=True)

    def consume(buf, st, k_off, diag=False):
        k_off = pl.multiple_of(k_off, tk)
        v_t = vt_ref[0, 0, :, pl.ds(k_off, tk)]
        for a in range(2):
            m_old = m_ref[st, a]
            m_new = jnp.maximum(m_old, mx_ref[buf, a])
            alpha = jnp.exp2(m_old - m_new)
            if diag:
                top = jnp.exp2(sbuf[buf, a, pl.ds(0, hk), :] - m_new)
                low = jnp.exp2(sbuf[buf, a, pl.ds(hk, hk), pl.ds(hq, hq)] - m_new[:, hq:])
                pc = jnp.exp2(sc_ref[buf, a] - m_new)
                pv = (_dot(v_t[:, :hk], top.astype(BF16))
                      + _dot(vmt_ref[...][:, :N_META], pc.astype(BF16)))
                pv_low = _dot(v_t[:, hk:], low.astype(BF16))
                pv = jnp.concatenate([pv[:, :hq], pv[:, hq:] + pv_low], axis=1)
            else:
                p = jnp.exp2(sbuf[buf, a] - m_new)
                pv = _dot(v_t, p.astype(BF16))
            acc_ref[st, a] = alpha * acc_ref[st, a] + pv
            m_ref[st, a] = m_new

    def finalize(st, q_off):
        q_rows = pl.ds(pl.multiple_of(q_off, tq), tq)
        heads = [acc_ref[st, a, pl.ds(0, V_DIM), :] / acc_ref[st, a, pl.ds(V_DIM, 1), :]
                 for a in range(2)]
        out_t = heads[0] - lam * heads[1]
        inv = lax.rsqrt(jnp.mean(out_t * out_t, axis=0, keepdims=True) + SUBLN_EPS)
        att = ((out_t * inv).T * sg_ref[...]) * (1.0 - LAMBDA_INIT)
        g = g_ref[0, q_rows, :]
        o_ref[q_rows, :] = (att * (g * jax.nn.sigmoid(g))).astype(o_ref.dtype)

    init_stats(0)
    init_stats(1)
    produce(0, 0, 0, "diag")

    def q_tile(i, cur, odd):
        oth = 1 - cur
        q_off = i * tq

        def far_pair(tt, c):
            t = 2 * tt
            produce(oth, q_off, (t + 1) * tk, "far")
            consume(cur, cur, t * tk)
            produce(cur, q_off, (t + 2) * tk, "far")
            consume(oth, cur, (t + 1) * tk)
            return c

        n_far = jnp.maximum(i - 2, 0)
        lax.fori_loop(0, n_far // 2, far_pair, 0)

        def near_diag():
            produce(oth, q_off, (i - 1) * tk, "near")
            consume(cur, cur, (i - 2) * tk)
            produce(cur, q_off, i * tk, "diag")
            consume(oth, cur, (i - 1) * tk)

        def last(next_kind):
            finalize(oth, jnp.maximum(i - 1, 0) * tq)
            init_stats(oth)
            produce(oth, jnp.minimum(i + 1, nq - 1) * tq, 0, next_kind)
            consume(cur, cur, i * tk, diag=True)

        if odd:
            @pl.when(i >= 3)
            def _():
                consume(cur, cur, (n_far - 1) * tk)
                produce(cur, q_off, n_far * tk, "far")
                near_diag()
                last("far")

            @pl.when(i == 1)
            def _():
                consume(cur, cur, 0)
                produce(cur, q_off, tk, "diag")
                last("far")
        else:
            @pl.when(i >= 2)
            def _():
                near_diag()
                last("far")

            @pl.when(i == 0)
            def _():
                last("near")

    def q_pair(ii, carry):
        q_tile(2 * ii, 0, False)
        q_tile(2 * ii + 1, 1, True)
        return carry

    lax.fori_loop(0, nq // 2, q_pair, 0)
    finalize((nq - 1) % 2, (nq - 1) * tq)


def _attention(qk, qk_meta, vt, v_meta, gates, rel_bias, lam_q1, lam_k1, lam_q2, lam_k2, subln_g):
    _, batch, _, seq, _ = qk.shape
    tq, tk = ATTN_TQ, ATTN_TK
    smem = pl.BlockSpec(memory_space=pltpu.SMEM)
    row64 = pl.BlockSpec((1, HEAD_DIM), lambda h, b: (0, 0))
    qk_spec = lambda which: pl.BlockSpec((1, 1, 1, seq, V_DIM), lambda h, b: (which, b, h, 0, 0))
    meta_spec = lambda which: pl.BlockSpec((1, 1, N_META, V_DIM), lambda h, b: (which, h, 0, 0))

    def kernel(rb, lq1, lk1, lq2, lk2, q_ref, k_ref, vt_ref, km_ref, vm_ref, *rest):
        _attn_kernel(rb, lq1, lk1, lq2, lk2, q_ref.at[0], k_ref.at[0], vt_ref,
                     km_ref.at[0], vm_ref.at[0], *rest)

    return pl.pallas_call(
        kernel,
        grid=(N_HEADS, batch),
        in_specs=[smem, row64, row64, row64, row64,
                  qk_spec(0), qk_spec(1),
                  pl.BlockSpec((1, 1, V_DIM + SUM_ROWS, seq), lambda h, b: (b, h, 0, 0)),
                  meta_spec(1), meta_spec(0),
                  pl.BlockSpec((1, seq, V_DIM), lambda h, b: (0, b, h)),
                  pl.BlockSpec((1, V_DIM), lambda h, b: (0, 0))],
        out_specs=pl.BlockSpec((seq, V_DIM), lambda h, b: (b, h)),
        out_shape=jax.ShapeDtypeStruct((batch * seq, D_ATTN), BF16),
        scratch_shapes=[pltpu.VMEM((V_DIM + SUM_ROWS, V_DIM), BF16),
                        pltpu.VMEM((tk, tq), F32),
                        pltpu.VMEM((2, N_META, tq), F32),
                        pltpu.VMEM((MAX_DISTANCE, MAX_DISTANCE), F32),
                        pltpu.VMEM((2, 2, tk, tq), F32),
                        pltpu.VMEM((2, 2, N_META, tq), F32),
                        pltpu.VMEM((2, 2, 1, tq), F32),
                        pltpu.VMEM((2, 2, 1, tq), F32),
                        pltpu.VMEM((2, 2, V_DIM + SUM_ROWS, tq), F32)],
        compiler_params=pltpu.CompilerParams(
            dimension_semantics=("arbitrary", "arbitrary"),
            vmem_limit_bytes=VMEM_LIMIT_ATTN),
        name="diff_attention",
    )(rel_bias, lam_q1, lam_k1, lam_q2, lam_k2, qk, qk, vt, qk_meta, v_meta, gates, subln_g)


def _scan_block(a, b):
    n = a.shape[0]
    row = lax.broadcasted_iota(jnp.int32, a.shape, 0)
    s = 1
    while s < n:
        keep = row >= s
        a_sh = jnp.where(keep, pltpu.roll(a, s, 0), 1.0)
        b_sh = jnp.where(keep, pltpu.roll(b, s, 0), 0.0)
        b = b + a * b_sh
        a = a * a_sh
        s *= 2
    return a, b


def _gate_lru_kernel(h_ref, w_ref, u_ref, um_ref, cw_ref, cb_ref, wa_ref, wx_ref, ba_ref, bx_ref,
                     lam_ref, o_ref, rec_ref, wb_ref, wg_ref, ubuf, tail_ref, st_ref, u_p, h_p,
                     *, tiles_per_batch):
    i = pl.program_id(1)
    n = LRU_ROWS
    hist = 8
    seg = n // 8
    n_blk = u_ref.shape[1] // LRU_BLOCK
    n_out = D_ATTN // n_blk

    x = -lam_ref[...]
    softplus = jnp.maximum(x, 0.0) + jnp.log1p(jnp.exp(-jnp.abs(x)))

    @pl.when(i == 0)
    def _():
        wb_ref[...] = w_ref[...].astype(BF16)
        for blk in range(n_blk):
            wg_ref[blk, :, pl.ds(0, LRU_BLOCK)] = wa_ref[blk].astype(BF16)
            wg_ref[blk, :, pl.ds(LRU_BLOCK, LRU_BLOCK)] = wx_ref[blk].astype(BF16)

    def gate_inputs(rows, cols, blk):
        cw = cw_ref[:, cols]
        uc = (cw[3:4] * ubuf[pl.ds(hist, rows), cols] + cw[2:3] * ubuf[pl.ds(hist - 1, rows), cols]
              + cw[1:2] * ubuf[pl.ds(hist - 2, rows), cols] + cw[0:1] * ubuf[pl.ds(hist - 3, rows), cols]
              + cb_ref[:, cols])
        return uc, _dot(uc.astype(BF16), wg_ref[blk])

    def gates(uc, pre, cols, first):
        r = jax.nn.sigmoid(pre[:, :LRU_BLOCK] + ba_ref[:, cols])
        gi = jax.nn.sigmoid(pre[:, LRU_BLOCK:] + bx_ref[:, cols])
        log_a = -LRU_C * r * softplus[:, cols]
        a = jnp.exp(log_a)
        v = jnp.tanh(-log_a) * (a * a + 1.0)
        mult = jnp.where(v > 0.0, v * lax.rsqrt(v), 0.0)
        if first:
            row = lax.broadcasted_iota(jnp.int32, mult.shape, 0)
            mult = jnp.where(row == 0, 1.0, mult)
        return a, mult * gi * uc

    @pl.when(i % tiles_per_batch == 0)
    def _():
        ubuf[pl.ds(0, hist), :] = jnp.zeros((hist, ubuf.shape[1]), F32)
        ubuf[pl.ds(hist, N_META), :] = um_ref[...]
        for blk in range(n_blk):
            cols = slice(blk * LRU_BLOCK, (blk + 1) * LRU_BLOCK)
            a, b = gates(*gate_inputs(N_META, cols, blk), cols, True)
            _, hm = _scan_block(a, b)
            st_ref[:, cols] = hm[N_META - 1:N_META, :]
        tail_ref[...] = um_ref[pl.ds(N_META - hist, hist), :]

    sub = lax.broadcasted_iota(jnp.int32, (8, LRU_BLOCK), 0)
    step = lambda j: pl.ds(j, 8, stride=SEG_PITCH)

    def piece(t, carry):
        r0 = pl.multiple_of(t * n, n)
        for blk in range(n_blk):
            lru_block(r0, blk)
        tail_ref[...] = u_ref[pl.ds(r0 + n - hist, hist), :]
        return carry

    def lru_block(r0, blk):
        cols = slice(blk * LRU_BLOCK, (blk + 1) * LRU_BLOCK)
        ocols = slice(blk * n_out, (blk + 1) * n_out)
        o_ref[0, pl.ds(r0, n), ocols] = _dot(h_ref[pl.ds(r0, n), :], wb_ref[:, ocols])

        for k in range(8):
            u_p[blk, pl.ds(SEG_PITCH * k, seg), :] = u_ref[pl.ds(r0 + seg * k, seg), cols]
        us = [u_p[blk, step(j), :] for j in range(seg)]
        prev = tail_ref[:, cols]

        def before(back):
            return jnp.where(sub == 0, prev[hist - back:hist - back + 1, :],
                             pltpu.roll(us[seg - back], 1, 0))

        older = {-back: before(back) for back in range(1, CONV_WIDTH)}
        u_at = lambda j: us[j] if j >= 0 else older[j]
        cw = cw_ref[:, cols]
        cb = cb_ref[:, cols]
        uc = jnp.concatenate(
            [cw[3:4] * u_at(j) + cw[2:3] * u_at(j - 1) + cw[1:2] * u_at(j - 2)
             + cw[0:1] * u_at(j - 3) + cb for j in range(seg)], axis=0)
        a, b = gates(uc, _dot(uc.astype(BF16), wg_ref[blk]), cols, False)
        a_j = lambda j: a[8 * j:8 * j + 8]
        b_j = lambda j: b[8 * j:8 * j + 8]
        a_run, h_run = a_j(0), b_j(0)
        for j in range(1, seg):
            h_run = a_j(j) * h_run + b_j(j)
            a_run = a_j(j) * a_run
        a_cum, h_cum = _scan_block(a_run, h_run)
        carry = st_ref[:, cols]
        seg_end = h_cum + a_cum * carry
        st_ref[:, cols] = seg_end[7:8, :]
        h_run = jnp.where(sub == 0, carry, pltpu.roll(seg_end, 1, 0))
        for j in range(seg):
            h_run = a_j(j) * h_run + b_j(j)
            h_p[blk, step(j), :] = h_run
        for k in range(8):
            rec_ref[pl.ds(r0 + seg * k, seg), cols] = h_p[blk, pl.ds(SEG_PITCH * k, seg), :]

    lax.fori_loop(0, h_ref.shape[0] // n, piece, 0, unroll=True)


def _gate_lru(h2d, w_in, u, u_meta, conv_w, conv_b, w_a, b_a, w_x, b_x, lru_lambda, *, seq):
    rows = h2d.shape[0]
    tm = PROJ_ROWS
    n_halves = 2
    width = D_LRU // n_halves
    n_blk = N_LRU_BLOCKS // n_halves
    gate_col_tile = lambda j: 3 + 2 * j
    half = lambda n: pl.BlockSpec((n, width), lambda j, i: (0, j))
    wspec = pl.BlockSpec((n_blk, LRU_BLOCK, LRU_BLOCK), lambda j, i: (j, 0, 0))

    def kernel(h_ref, w_ref, u_ref, um_ref, *rest):
        _gate_lru_kernel(h_ref, w_ref, u_ref.at[0], um_ref.at[0], *rest, tiles_per_batch=seq // tm)

    return pl.pallas_call(
        kernel,
        grid=(n_halves, rows // tm),
        in_specs=[pl.BlockSpec((tm, D_MODEL), lambda j, i: (i, 0)),
                  pl.BlockSpec((D_MODEL, D_ATTN), lambda j, i: (0, gate_col_tile(j))),
                  pl.BlockSpec((1, tm, width), lambda j, i: (0, i, j)),
                  pl.BlockSpec((1, N_META, width), lambda j, i: (0, 0, j)),
                  half(CONV_WIDTH), half(1), wspec, wspec, half(1), half(1), half(1)],
        out_specs=(pl.BlockSpec((1, tm, D_ATTN), lambda j, i: (j, i, 0)),
                   pl.BlockSpec((tm, width), lambda j, i: (i, j))),
        out_shape=(jax.ShapeDtypeStruct((n_halves, rows, D_ATTN), F32),
                   jax.ShapeDtypeStruct((rows, D_LRU), F32)),
        scratch_shapes=[pltpu.VMEM((D_MODEL, D_ATTN), BF16),
                        pltpu.VMEM((n_blk, LRU_BLOCK, 2 * LRU_BLOCK), BF16),
                        pltpu.VMEM((8 + N_META, width), F32),
                        pltpu.VMEM((8, width), F32),
                        pltpu.VMEM((1, width), F32),
                        pltpu.VMEM((n_blk, 8 * SEG_PITCH, LRU_BLOCK), F32),
                        pltpu.VMEM((n_blk, 8 * SEG_PITCH, LRU_BLOCK), F32)],
        compiler_params=pltpu.CompilerParams(
            dimension_semantics=("arbitrary", "arbitrary"),
            vmem_limit_bytes=VMEM_LIMIT_PROJ),
        name="proj_gates_rglru",
    )(h2d, w_in, u, u_meta, conv_w, conv_b, w_a, w_x, b_a, b_x, lru_lambda)


def _out_kernel(x_ref, att_ref, rec_ref, g_ref, w_ref, fg_ref, o_ref, rec_s):
    n_out = 2 * LRU_BLOCK
    for piece in range(x_ref.shape[0] // OUT_PIECE):
        rows = pl.ds(piece * OUT_PIECE, OUT_PIECE)
        g = g_ref[0, rows, :]
        rec_s[rows, :] = (rec_ref[rows, :] * (g * jax.nn.sigmoid(g))).astype(BF16)
        ssq = jnp.zeros((OUT_PIECE, LRU_BLOCK), F32)
        for blk in range(D_MODEL // n_out):
            ocols = slice(blk * n_out, (blk + 1) * n_out)
            z = (x_ref[rows, ocols] + _dot(att_ref[rows, :], w_ref[pl.ds(0, D_ATTN), ocols])
                 + _dot(rec_s[rows, :], w_ref[pl.ds(D_ATTN, D_LRU), ocols]))
            o_ref[rows, ocols] = z
            for part in range(n_out // LRU_BLOCK):
                zp = z[:, part * LRU_BLOCK:(part + 1) * LRU_BLOCK]
                ssq = ssq + zp * zp
        inv = lax.rsqrt(jnp.sum(ssq, axis=1, keepdims=True) * (1.0 / D_MODEL) + NORM_EPS)
        o_ref[rows, :] = (o_ref[rows, :] * inv) * fg_ref[...]


def _out_proj(x2d, att, rec, gates, w_out_bf16, final_g):
    rows = x2d.shape[0]
    tm = OUT_ROWS
    tile = lambda width: pl.BlockSpec((tm, width), lambda s: (s, 0))
    return pl.pallas_call(
        _out_kernel,
        grid=(rows // tm,),
        in_specs=[tile(D_MODEL), tile(D_ATTN), tile(D_LRU),
                  pl.BlockSpec((1, tm, D_LRU), lambda s: (1, s, 0)),
                  pl.BlockSpec((D_ATTN + D_LRU, D_MODEL), lambda s: (0, 0),
                               pipeline_mode=pl.Buffered(1)),
                  pl.BlockSpec((1, D_MODEL), lambda s: (0, 0))],
        out_specs=tile(D_MODEL),
        out_shape=jax.ShapeDtypeStruct((rows, D_MODEL), F32),
        scratch_shapes=[pltpu.VMEM((tm, D_LRU), BF16)],
        compiler_params=pltpu.CompilerParams(
            dimension_semantics=("arbitrary",),
            vmem_limit_bytes=VMEM_LIMIT_OUT),
        name="out_proj",
    )(x2d, att, rec, gates, w_out_bf16, final_g)


def kernel(x, meta_tokens, rel_bias, norm_g, w_in, conv_w, conv_b, w_a, b_a, w_x, b_x, lru_lambda,
           lam_q1, lam_k1, lam_q2, lam_k2, subln_g, w_out, final_g):
    batch, seq, _ = x.shape
    x2d = x.reshape(batch * seq, D_MODEL)
    project = functools.partial(_project, w_in=w_in[0], batch=batch, seq=seq)
    vt, v_meta, h2d, h_meta = project((x2d, meta_tokens, norm_g), col_tile0=2, n_slabs=1, mode="vt")
    qk, qk_meta = project((h2d, h_meta), col_tile0=0, n_slabs=2, mode="heads")
    u, u_meta, w_out_bf16 = project((h2d, h_meta, w_out[0]), col_tile0=4, n_slabs=1, mode="flat")
    gates, rec = _gate_lru(h2d, w_in[0], u, u_meta, conv_w[0], conv_b, w_a[0], b_a, w_x[0], b_x,
                           lru_lambda, seq=seq)
    att = _attention(qk, qk_meta, vt, v_meta, gates, rel_bias, lam_q1, lam_k1, lam_q2, lam_k2,
                     subln_g)
    out = _out_proj(x2d, att, rec, gates, w_out_bf16, final_g.reshape(1, D_MODEL))
    return out.reshape(batch, seq, D_MODEL)
```

```python
import functools
import math

import numpy as np
import jax
import jax.numpy as jnp
from jax import lax
from jax.experimental import pallas as pl
from jax.experimental.pallas import tpu as pltpu

D_MODEL = 2048
N_META = 16
D_ATTN = 1024
D_LRU = 1024
N_HEADS = 8
HEAD_DIM = 64
V_DIM = 128
N_LRU_BLOCKS = 8
LRU_BLOCK = 128
CONV_WIDTH = 4
LRU_C = 8.0
N_BUCKETS = 32
MAX_DISTANCE = 128
NORM_EPS = 1e-6
SUBLN_EPS = 1e-5
NEG_INF = -1e30
LAMBDA_INIT = 0.8 - 0.6 * math.exp(-0.3 * 0)
LOG2E = math.log2(math.e)

BF16 = jnp.bfloat16
F32 = jnp.float32

VMEM_LIMIT_PROJ = 56 * 1024 * 1024
VMEM_LIMIT_ATTN = 48 * 1024 * 1024
VMEM_LIMIT_OUT = 56 * 1024 * 1024

PROJ_ROWS = 1024
ATTN_TQ = 512
ATTN_TK = 512
SUM_ROWS = 16
OUT_ROWS = 512
OUT_PIECE = 256
LRU_ROWS = 256
SEG_PITCH = 40


def _bucket_thresholds():
    max_exact = N_BUCKETS // 2
    d = np.arange(0, 4 * MAX_DISTANCE, dtype=np.int64)
    val = (np.log(np.maximum(d, 1).astype(np.float64) / max_exact)
           / math.log(MAX_DISTANCE / max_exact) * (N_BUCKETS - max_exact))
    large = np.minimum(max_exact + np.floor(val + 1e-9).astype(np.int64), N_BUCKETS - 1)
    bucket = np.where(d < max_exact, d, large)
    frac = np.abs(val - np.round(val))
    interior = (d > max_exact) & (d < MAX_DISTANCE)
    assert frac[interior].min() > 1e-3
    assert (np.diff(bucket) >= 0).all() and bucket[MAX_DISTANCE] == N_BUCKETS - 1
    return tuple(int(np.argmax(bucket >= j)) for j in range(1, N_BUCKETS))


BUCKET_THRESHOLDS = _bucket_thresholds()


def _rms(x, g, eps):
    y = x * lax.rsqrt(jnp.mean(x * x, axis=-1, keepdims=True) + eps)
    return y * g


def _dot(a, b):
    return jnp.dot(a, b, preferred_element_type=F32)


def _dot_nt(a, b):
    return lax.dot_general(a, b, (((1,), (1,)), ((), ())), preferred_element_type=F32)


def _proj_kernel(*refs, mode):
    if mode == "vt":
        x_ref, meta_ref, g_ref, w_ref, o_ref, om_ref, h_ref, hm_ref, wb_ref = refs
    elif mode == "flat":
        h_ref, hm_ref, wo_ref, w_ref, o_ref, om_ref, wob_ref, wb_ref = refs
    else:
        h_ref, hm_ref, w_ref, o_ref, om_ref, wb_ref = refs
    j = pl.program_id(0)
    i = pl.program_id(1)
    out_scale = jnp.where(j == 0, HEAD_DIM ** -0.5 * LOG2E, 1.0).astype(F32) if mode == "heads" else 1.0
    head = lambda y, h: y[:, h * V_DIM:(h + 1) * V_DIM]

    @pl.when(i == 0)
    def _():
        wb_ref[...] = w_ref[...].astype(BF16)
        if mode == "vt":
            hm_ref[...] = _rms(meta_ref[...], g_ref[...], NORM_EPS).astype(BF16)
        ym = _dot(hm_ref[...], wb_ref[...]) * out_scale
        if mode == "flat":
            om_ref[0] = ym
        else:
            for h in range(N_HEADS):
                om_ref[0, h] = head(ym, h).astype(BF16)

    if mode == "vt":
        h_ref[...] = _rms(x_ref[...], g_ref[...], NORM_EPS).astype(BF16)
    y = _dot(h_ref[...], wb_ref[...]) * out_scale
    if mode == "flat":
        o_ref[0] = y
        wob_ref[...] = wo_ref[...].astype(BF16)
    elif mode == "heads":
        for h in range(N_HEADS):
            o_ref[0, 0, h] = head(y, h).astype(BF16)
    else:
        rows = y.shape[0]
        for h in range(N_HEADS):
            o_ref[0, h, pl.ds(0, V_DIM), :] = head(y, h).T.astype(BF16)
            o_ref[0, h, pl.ds(V_DIM, SUM_ROWS), :] = jnp.ones((SUM_ROWS, rows), BF16)


def _project(inputs, w_in, *, col_tile0, n_slabs, mode, batch, seq):
    rows = inputs[0].shape[0]
    tm = PROJ_ROWS
    n_i = rows // tm
    n_ib = seq // tm
    row_tile = pl.BlockSpec((tm, D_MODEL), lambda j, i: (i, 0))
    meta_rows = pl.BlockSpec((N_META, D_MODEL), lambda j, i: (0, 0))
    if mode == "vt":
        in_specs = [row_tile, meta_rows, pl.BlockSpec((1, D_MODEL), lambda j, i: (0, 0))]
    elif mode == "flat":
        assert n_slabs == 1
        w_out_rows = pl.BlockSpec(((D_ATTN + D_LRU) // n_i, D_MODEL), lambda j, i: (i, 0))
        in_specs = [row_tile, meta_rows, w_out_rows]
    else:
        in_specs = [row_tile, meta_rows]
    in_specs.append(pl.BlockSpec((D_MODEL, D_ATTN), lambda j, i: (0, j + col_tile0)))
    if mode == "heads":
        out_shape = (jax.ShapeDtypeStruct((n_slabs, batch, N_HEADS, seq, V_DIM), BF16),
                     jax.ShapeDtypeStruct((n_slabs, N_HEADS, N_META, V_DIM), BF16))
        out_specs = (pl.BlockSpec((1, 1, N_HEADS, tm, V_DIM),
                                  lambda j, i: (j, i // n_ib, 0, i % n_ib, 0)),
                     pl.BlockSpec((1, N_HEADS, N_META, V_DIM), lambda j, i: (j, 0, 0, 0)))
    elif mode == "vt":
        assert n_slabs == 1
        out_shape = (jax.ShapeDtypeStruct((batch, N_HEADS, V_DIM + SUM_ROWS, seq), BF16),
                     jax.ShapeDtypeStruct((n_slabs, N_HEADS, N_META, V_DIM), BF16),
                     jax.ShapeDtypeStruct((rows, D_MODEL), BF16),
                     jax.ShapeDtypeStruct((N_META, D_MODEL), BF16))
        out_specs = (pl.BlockSpec((1, N_HEADS, V_DIM + SUM_ROWS, tm),
                                  lambda j, i: (i // n_ib, 0, 0, i % n_ib)),
                     pl.BlockSpec((1, N_HEADS, N_META, V_DIM), lambda j, i: (j, 0, 0, 0)),
                     row_tile, meta_rows)
    else:
        out_shape = (jax.ShapeDtypeStruct((n_slabs, rows, D_ATTN), F32),
                     jax.ShapeDtypeStruct((n_slabs, N_META, D_ATTN), F32),
                     jax.ShapeDtypeStruct((D_ATTN + D_LRU, D_MODEL), BF16))
        out_specs = (pl.BlockSpec((1, tm, D_ATTN), lambda j, i: (j, i, 0)),
                     pl.BlockSpec((1, N_META, D_ATTN), lambda j, i: (j, 0, 0)),
                     w_out_rows)
    return pl.pallas_call(
        functools.partial(_proj_kernel, mode=mode),
        grid=(n_slabs, n_i),
        in_specs=in_specs,
        out_specs=out_specs,
        out_shape=out_shape,
        scratch_shapes=[pltpu.VMEM((D_MODEL, D_ATTN), BF16)],
        compiler_params=pltpu.CompilerParams(
            dimension_semantics=("arbitrary", "arbitrary"),
            vmem_limit_bytes=VMEM_LIMIT_PROJ),
        name={"heads": "proj_qk", "vt": "proj_v", "flat": "proj_u"}[mode],
    )(*inputs, w_in)


def _toeplitz_bias(dist, bucket_bias):
    b = jnp.full(dist.shape, bucket_bias[0], F32)
    for j, thr in enumerate(BUCKET_THRESHOLDS, start=1):
        b = jnp.where(dist >= thr, bucket_bias[j], b)
    return b


def _fill_bias(ref, lead, n_rows, n_cols, d0, bucket_bias):
    sub = lax.broadcasted_iota(jnp.int32, (8, V_DIM), 0)
    lane = lax.broadcasted_iota(jnp.int32, (8, V_DIM), 1)
    zeros = jnp.zeros((8, V_DIM), F32)
    masked = jnp.full((8, V_DIM), NEG_INF, F32)
    cache = {}
    for a8 in range(n_rows // 8):
        for b in range(n_cols // V_DIM):
            off = d0 + V_DIM * b - 8 * a8
            if off + V_DIM - 1 < 0:
                tile = masked
            elif off - 7 >= MAX_DISTANCE:
                tile = zeros
            else:
                if off not in cache:
                    d = off + lane - sub
                    cache[off] = jnp.where(d >= 0, _toeplitz_bias(d, bucket_bias), NEG_INF)
                tile = cache[off]
            ref[(*lead, pl.ds(8 * a8, 8), pl.ds(V_DIM * b, V_DIM))] = tile


def _attn_kernel(rb_ref, lq1_ref, lk1_ref, lq2_ref, lk2_ref, q_ref, k_ref, vt_ref, km_ref, vm_ref,
                 g_ref, sg_ref, o_ref, vmt_ref, bd_ref, bm_ref, bn_ref, sbuf, sc_ref, mx_ref,
                 m_ref, acc_ref):
    h = pl.program_id(0)
    tq, tk = ATTN_TQ, ATTN_TK
    seq = k_ref.shape[2]
    nq = seq // tq

    far = rb_ref[N_BUCKETS - 1, h]

    @pl.when(pl.program_id(1) == 0)
    def _():
        bucket_bias = [(rb_ref[j, h] - far) * LOG2E for j in range(N_BUCKETS)]
        _fill_bias(bd_ref, (), tk, tq, 0, bucket_bias)
        _fill_bias(bn_ref, (), MAX_DISTANCE, MAX_DISTANCE, MAX_DISTANCE, bucket_bias)
        _fill_bias(bm_ref, (1,), N_META, tq, N_META, bucket_bias)
        bm_ref[0] = jnp.zeros((N_META, tq), F32)
        padded = jnp.concatenate(
            [vm_ref[0].astype(F32), jnp.zeros((V_DIM - N_META, V_DIM), F32)], axis=0)
        vmt_ref[pl.ds(0, V_DIM), :] = padded.T.astype(BF16)
        vmt_ref[pl.ds(V_DIM, SUM_ROWS), :] = jnp.ones((SUM_ROWS, V_DIM), BF16)

    lane = lax.broadcasted_iota(jnp.int32, (tq, V_DIM), 1)

    lam = (jnp.exp(jnp.sum(lq1_ref[...] * lk1_ref[...], keepdims=True))
           - jnp.exp(jnp.sum(lq2_ref[...] * lk2_ref[...], keepdims=True))
           + LAMBDA_INIT)

    def init_stats(st):
        m_ref[st] = jnp.full(m_ref.shape[1:], NEG_INF, F32)
        for a in range(2):
            acc_ref[st, a, pl.ds(0, V_DIM), :] = jnp.zeros((V_DIM, tq), F32)
            acc_ref[st, a, pl.ds(V_DIM, SUM_ROWS), :] = jnp.ones((SUM_ROWS, tq), F32)

    hk, hq = tk // 2, tq // 2

    def produce(buf, q_off, k_off, kind):
        q = q_ref[0, 0, pl.ds(pl.multiple_of(q_off, tq), tq), :]
        zero = jnp.zeros_like(q)
        k_off = pl.multiple_of(k_off, tk)
        k_t = k_ref[0, 0, pl.ds(k_off, tk), :]
        for a in range(2):
            qa = jnp.where((lane < HEAD_DIM) if a == 0 else (lane >= HEAD_DIM), q, zero)
            if kind == "diag":
                top = _dot_nt(k_t[:hk], qa) + bd_ref[pl.ds(0, hk), :]
                low = _dot_nt(k_t[hk:], qa[hq:]) + bd_ref[pl.ds(hk, hk), pl.ds(hq, hq)]
                sc = _dot_nt(km_ref[0], qa) + bm_ref[jnp.where(k_off == 0, 1, 0)]
                sbuf[buf, a, pl.ds(0, hk), :] = top
                sbuf[buf, a, pl.ds(hk, hk), pl.ds(hq, hq)] = low
                sc_ref[buf, a] = sc
                mx = jnp.maximum(jnp.max(top, axis=0, keepdims=True),
                                 jnp.max(sc, axis=0, keepdims=True))
                mx_low = jnp.max(low, axis=0, keepdims=True)
                mx_ref[buf, a] = jnp.concatenate(
                    [mx[:, :hq], jnp.maximum(mx[:, hq:], mx_low)], axis=1)
                continue
            s = _dot_nt(k_t, qa)
            if kind == "near":
                band = tk - MAX_DISTANCE
                corner = s[band:, :MAX_DISTANCE] + bn_ref[...]
                sbuf[buf, a, pl.ds(0, band), :] = s[:band]
                sbuf[buf, a, pl.ds(band, MAX_DISTANCE), pl.ds(0, MAX_DISTANCE)] = corner
                sbuf[buf, a, pl.ds(band, MAX_DISTANCE), pl.ds(MAX_DISTANCE, tq - MAX_DISTANCE)] = (
                    s[band:, MAX_DISTANCE:])
                mx = jnp.max(s[:band], axis=0, keepdims=True)
                mx_ref[buf, a] = jnp.concatenate(
                    [jnp.maximum(mx[:, :MAX_DISTANCE], jnp.max(corner, axis=0, keepdims=True)),
                     jnp.maximum(mx[:, MAX_DISTANCE:],
                                 jnp.max(s[band:, MAX_DISTANCE:], axis=0, keepdims=True))], axis=1)
                continue
            sbuf[buf, a] = s
            mx_ref[buf, a] = jnp.max(s, axis=0, keepdims=True)

    def consume(buf, st, k_off, diag=False):
        k_off = pl.multiple_of(k_off, tk)
        v_t = vt_ref[0, 0, :, pl.ds(k_off, tk)]
        for a in range(2):
            m_old = m_ref[st, a]
            m_new = jnp.maximum(m_old, mx_ref[buf, a])
            alpha = jnp.exp2(m_old - m_new)
            if diag:
                top = jnp.exp2(sbuf[buf, a, pl.ds(0, hk), :] - m_new)
                low = jnp.exp2(sbuf[buf, a, pl.ds(hk, hk), pl.ds(hq, hq)] - m_new[:, hq:])
                pc = jnp.exp2(sc_ref[buf, a] - m_new)
                pv = (_dot(v_t[:, :hk], top.astype(BF16))
                      + _dot(vmt_ref[...][:, :N_META], pc.astype(BF16)))
                pv_low = _dot(v_t[:, hk:], low.astype(BF16))
                pv = jnp.concatenate([pv[:, :hq], pv[:, hq:] + pv_low], axis=1)
            else:
                p = jnp.exp2(sbuf[buf, a] - m_new)
                pv = _dot(v_t, p.astype(BF16))
            acc_ref[st, a] = alpha * acc_ref[st, a] + pv
            m_ref[st, a] = m_new

    def finalize(st, q_off):
        q_rows = pl.ds(pl.multiple_of(q_off, tq), tq)
        heads = [acc_ref[st, a, pl.ds(0, V_DIM), :] / acc_ref[st, a, pl.ds(V_DIM, 1), :]
                 for a in range(2)]
        out_t = heads[0] - lam * heads[1]
        inv = lax.rsqrt(jnp.mean(out_t * out_t, axis=0, keepdims=True) + SUBLN_EPS)
        att = ((out_t * inv).T * sg_ref[...]) * (1.0 - LAMBDA_INIT)
        g = g_ref[0, q_rows, :]
        o_ref[q_rows, :] = (att * (g * jax.nn.sigmoid(g))).astype(o_ref.dtype)

    init_stats(0)
    init_stats(1)
    produce(0, 0, 0, "diag")

    def q_tile(i, cur, odd):
        oth = 1 - cur
        q_off = i * tq

        def far_pair(tt, c):
            t = 2 * tt
            produce(oth, q_off, (t + 1) * tk, "far")
            consume(cur, cur, t * tk)
            produce(cur, q_off, (t + 2) * tk, "far")
            consume(oth, cur, (t + 1) * tk)
            return c

        n_far = jnp.maximum(i - 2, 0)
        lax.fori_loop(0, n_far // 2, far_pair, 0)

        def near_diag():
            produce(oth, q_off, (i - 1) * tk, "near")
            consume(cur, cur, (i - 2) * tk)
            produce(cur, q_off, i * tk, "diag")
            consume(oth, cur, (i - 1) * tk)

        def last(next_kind):
            finalize(oth, jnp.maximum(i - 1, 0) * tq)
            init_stats(oth)
            produce(oth, jnp.minimum(i + 1, nq - 1) * tq, 0, next_kind)
            consume(cur, cur, i * tk, diag=True)

        if odd:
            @pl.when(i >= 3)
            def _():
                consume(cur, cur, (n_far - 1) * tk)
                produce(cur, q_off, n_far * tk, "far")
                near_diag()
                last("far")

            @pl.when(i == 1)
            def _():
                consume(cur, cur, 0)
                produce(cur, q_off, tk, "diag")
                last("far")
        else:
            @pl.when(i >= 2)
            def _():
                near_diag()
                last("far")

            @pl.when(i == 0)
            def _():
                last("near")

    def q_pair(ii, carry):
        q_tile(2 * ii, 0, False)
        q_tile(2 * ii + 1, 1, True)
        return carry

    lax.fori_loop(0, nq // 2, q_pair, 0)
    finalize((nq - 1) % 2, (nq - 1) * tq)


def _attention(qk, qk_meta, vt, v_meta, gates, rel_bias, lam_q1, lam_k1, lam_q2, lam_k2, subln_g):
    _, batch, _, seq, _ = qk.shape
    tq, tk = ATTN_TQ, ATTN_TK
    smem = pl.BlockSpec(memory_space=pltpu.SMEM)
    row64 = pl.BlockSpec((1, HEAD_DIM), lambda h, b: (0, 0))
    qk_spec = lambda which: pl.BlockSpec((1, 1, 1, seq, V_DIM), lambda h, b: (which, b, h, 0, 0))
    meta_spec = lambda which: pl.BlockSpec((1, 1, N_META, V_DIM), lambda h, b: (which, h, 0, 0))

    def kernel(rb, lq1, lk1, lq2, lk2, q_ref, k_ref, vt_ref, km_ref, vm_ref, *rest):
        _attn_kernel(rb, lq1, lk1, lq2, lk2, q_ref.at[0], k_ref.at[0], vt_ref,
                     km_ref.at[0], vm_ref.at[0], *rest)

    return pl.pallas_call(
        kernel,
        grid=(N_HEADS, batch),
        in_specs=[smem, row64, row64, row64, row64,
                  qk_spec(0), qk_spec(1),
                  pl.BlockSpec((1, 1, V_DIM + SUM_ROWS, seq), lambda h, b: (b, h, 0, 0)),
                  meta_spec(1), meta_spec(0),
                  pl.BlockSpec((1, seq, V_DIM), lambda h, b: (0, b, h)),
                  pl.BlockSpec((1, V_DIM), lambda h, b: (0, 0))],
        out_specs=pl.BlockSpec((seq, V_DIM), lambda h, b: (b, h)),
        out_shape=jax.ShapeDtypeStruct((batch * seq, D_ATTN), BF16),
        scratch_shapes=[pltpu.VMEM((V_DIM + SUM_ROWS, V_DIM), BF16),
                        pltpu.VMEM((tk, tq), F32),
                        pltpu.VMEM((2, N_META, tq), F32),
                        pltpu.VMEM((MAX_DISTANCE, MAX_DISTANCE), F32),
                        pltpu.VMEM((2, 2, tk, tq), F32),
                        pltpu.VMEM((2, 2, N_META, tq), F32),
                        pltpu.VMEM((2, 2, 1, tq), F32),
                        pltpu.VMEM((2, 2, 1, tq), F32),
                        pltpu.VMEM((2, 2, V_DIM + SUM_ROWS, tq), F32)],
        compiler_params=pltpu.CompilerParams(
            dimension_semantics=("arbitrary", "arbitrary"),
            vmem_limit_bytes=VMEM_LIMIT_ATTN),
        name="diff_attention",
    )(rel_bias, lam_q1, lam_k1, lam_q2, lam_k2, qk, qk, vt, qk_meta, v_meta, gates, subln_g)


def _scan_block(a, b):
    n = a.shape[0]
    row = lax.broadcasted_iota(jnp.int32, a.shape, 0)
    s = 1
    while s < n:
        keep = row >= s
        a_sh = jnp.where(keep, pltpu.roll(a, s, 0), 1.0)
        b_sh = jnp.where(keep, pltpu.roll(b, s, 0), 0.0)
        b = b + a * b_sh
        a = a * a_sh
        s *= 2
    return a, b


def _gate_lru_kernel(h_ref, w_ref, u_ref, um_ref, cw_ref, cb_ref, wa_ref, wx_ref, ba_ref, bx_ref,
                     lam_ref, o_ref, rec_ref, wb_ref, wg_ref, ubuf, tail_ref, st_ref, u_p, h_p,
                     *, tiles_per_batch):
    i = pl.program_id(1)
    n = LRU_ROWS
    hist = 8
    seg = n // 8
    n_blk = u_ref.shape[1] // LRU_BLOCK
    n_out = D_ATTN // n_blk

    x = -lam_ref[...]
    softplus = jnp.maximum(x, 0.0) + jnp.log1p(jnp.exp(-jnp.abs(x)))

    @pl.when(i == 0)
    def _():
        wb_ref[...] = w_ref[...].astype(BF16)
        for blk in range(n_blk):
            wg_ref[blk, :, pl.ds(0, LRU_BLOCK)] = wa_ref[blk].astype(BF16)
            wg_ref[blk, :, pl.ds(LRU_BLOCK, LRU_BLOCK)] = wx_ref[blk].astype(BF16)

    def gate_inputs(rows, cols, blk):
        cw = cw_ref[:, cols]
        uc = (cw[3:4] * ubuf[pl.ds(hist, rows), cols] + cw[2:3] * ubuf[pl.ds(hist - 1, rows), cols]
              + cw[1:2] * ubuf[pl.ds(hist - 2, rows), cols] + cw[0:1] * ubuf[pl.ds(hist - 3, rows), cols]
              + cb_ref[:, cols])
        return uc, _dot(uc.astype(BF16), wg_ref[blk])

    def gates(uc, pre, cols, first):
        r = jax.nn.sigmoid(pre[:, :LRU_BLOCK] + ba_ref[:, cols])
        gi = jax.nn.sigmoid(pre[:, LRU_BLOCK:] + bx_ref[:, cols])
        log_a = -LRU_C * r * softplus[:, cols]
        a = jnp.exp(log_a)
        v = jnp.tanh(-log_a) * (a * a + 1.0)
        mult = jnp.where(v > 0.0, v * lax.rsqrt(v), 0.0)
        if first:
            row = lax.broadcasted_iota(jnp.int32, mult.shape, 0)
            mult = jnp.where(row == 0, 1.0, mult)
        return a, mult * gi * uc

    @pl.when(i % tiles_per_batch == 0)
    def _():
        ubuf[pl.ds(0, hist), :] = jnp.zeros((hist, ubuf.shape[1]), F32)
        ubuf[pl.ds(hist, N_META), :] = um_ref[...]
        for blk in range(n_blk):
            cols = slice(blk * LRU_BLOCK, (blk + 1) * LRU_BLOCK)
            a, b = gates(*gate_inputs(N_META, cols, blk), cols, True)
            _, hm = _scan_block(a, b)
            st_ref[:, cols] = hm[N_META - 1:N_META, :]
        tail_ref[...] = um_ref[pl.ds(N_META - hist, hist), :]

    sub = lax.broadcasted_iota(jnp.int32, (8, LRU_BLOCK), 0)
    step = lambda j: pl.ds(j, 8, stride=SEG_PITCH)

    def piece(t, carry):
        r0 = pl.multiple_of(t * n, n)
        for blk in range(n_blk):
            lru_block(r0, blk)
        tail_ref[...] = u_ref[pl.ds(r0 + n - hist, hist), :]
        return carry

    def lru_block(r0, blk):
        cols = slice(blk * LRU_BLOCK, (blk + 1) * LRU_BLOCK)
        ocols = slice(blk * n_out, (blk + 1) * n_out)
        o_ref[0, pl.ds(r0, n), ocols] = _dot(h_ref[pl.ds(r0, n), :], wb_ref[:, ocols])

        for k in range(8):
            u_p[blk, pl.ds(SEG_PITCH * k, seg), :] = u_ref[pl.ds(r0 + seg * k, seg), cols]
        us = [u_p[blk, step(j), :] for j in range(seg)]
        prev = tail_ref[:, cols]

        def before(back):
            return jnp.where(sub == 0, prev[hist - back:hist - back + 1, :],
                             pltpu.roll(us[seg - back], 1, 0))

        older = {-back: before(back) for back in range(1, CONV_WIDTH)}
        u_at = lambda j: us[j] if j >= 0 else older[j]
        cw = cw_ref[:, cols]
        cb = cb_ref[:, cols]
        uc = jnp.concatenate(
            [cw[3:4] * u_at(j) + cw[2:3] * u_at(j - 1) + cw[1:2] * u_at(j - 2)
             + cw[0:1] * u_at(j - 3) + cb for j in range(seg)], axis=0)
        a, b = gates(uc, _dot(uc.astype(BF16), wg_ref[blk]), cols, False)
        a_j = lambda j: a[8 * j:8 * j + 8]
        b_j = lambda j: b[8 * j:8 * j + 8]
        a_run, h_run = a_j(0), b_j(0)
        for j in range(1, seg):
            h_run = a_j(j) * h_run + b_j(j)
            a_run = a_j(j) * a_run
        a_cum, h_cum = _scan_block(a_run, h_run)
        carry = st_ref[:, cols]
        seg_end = h_cum + a_cum * carry
        st_ref[:, cols] = seg_end[7:8, :]
        h_run = jnp.where(sub == 0, carry, pltpu.roll(seg_end, 1, 0))
        for j in range(seg):
            h_run = a_j(j) * h_run + b_j(j)
            h_p[blk, step(j), :] = h_run
        for k in range(8):
            rec_ref[pl.ds(r0 + seg * k, seg), cols] = h_p[blk, pl.ds(SEG_PITCH * k, seg), :]

    lax.fori_loop(0, h_ref.shape[0] // n, piece, 0, unroll=True)


def _gate_lru(h2d, w_in, u, u_meta, conv_w, conv_b, w_a, b_a, w_x, b_x, lru_lambda, *, seq):
    rows = h2d.shape[0]
    tm = PROJ_ROWS
    n_halves = 2
    width = D_LRU // n_halves
    n_blk = N_LRU_BLOCKS // n_halves
    gate_col_tile = lambda j: 3 + 2 * j
    half = lambda n: pl.BlockSpec((n, width), lambda j, i: (0, j))
    wspec = pl.BlockSpec((n_blk, LRU_BLOCK, LRU_BLOCK), lambda j, i: (j, 0, 0))

    def kernel(h_ref, w_ref, u_ref, um_ref, *rest):
        _gate_lru_kernel(h_ref, w_ref, u_ref.at[0], um_ref.at[0], *rest, tiles_per_batch=seq // tm)

    return pl.pallas_call(
        kernel,
        grid=(n_halves, rows // tm),
        in_specs=[pl.BlockSpec((tm, D_MODEL), lambda j, i: (i, 0)),
                  pl.BlockSpec((D_MODEL, D_ATTN), lambda j, i: (0, gate_col_tile(j))),
                  pl.BlockSpec((1, tm, width), lambda j, i: (0, i, j)),
                  pl.BlockSpec((1, N_META, width), lambda j, i: (0, 0, j)),
                  half(CONV_WIDTH), half(1), wspec, wspec, half(1), half(1), half(1)],
        out_specs=(pl.BlockSpec((1, tm, D_ATTN), lambda j, i: (j, i, 0)),
                   pl.BlockSpec((tm, width), lambda j, i: (i, j))),
        out_shape=(jax.ShapeDtypeStruct((n_halves, rows, D_ATTN), F32),
                   jax.ShapeDtypeStruct((rows, D_LRU), F32)),
        scratch_shapes=[pltpu.VMEM((D_MODEL, D_ATTN), BF16),
                        pltpu.VMEM((n_blk, LRU_BLOCK, 2 * LRU_BLOCK), BF16),
                        pltpu.VMEM((8 + N_META, width), F32),
                        pltpu.VMEM((8, width), F32),
                        pltpu.VMEM((1, width), F32),
                        pltpu.VMEM((n_blk, 8 * SEG_PITCH, LRU_BLOCK), F32),
                        pltpu.VMEM((n_blk, 8 * SEG_PITCH, LRU_BLOCK), F32)],
        compiler_params=pltpu.CompilerParams(
            dimension_semantics=("arbitrary", "arbitrary"),
            vmem_limit_bytes=VMEM_LIMIT_PROJ),
        name="proj_gates_rglru",
    )(h2d, w_in, u, u_meta, conv_w, conv_b, w_a, w_x, b_a, b_x, lru_lambda)


def _out_kernel(x_ref, att_ref, rec_ref, g_ref, w_ref, fg_ref, o_ref, rec_s):
    n_out = 2 * LRU_BLOCK
    for piece in range(x_ref.shape[0] // OUT_PIECE):
        rows = pl.ds(piece * OUT_PIECE, OUT_PIECE)
        g = g_ref[0, rows, :]
        rec_s[rows, :] = (rec_ref[rows, :] * (g * jax.nn.sigmoid(g))).astype(BF16)
        ssq = jnp.zeros((OUT_PIECE, LRU_BLOCK), F32)
        for blk in range(D_MODEL // n_out):
            ocols = slice(blk * n_out, (blk + 1) * n_out)
            z = (x_ref[rows, ocols] + _dot(att_ref[rows, :], w_ref[pl.ds(0, D_ATTN), ocols])
                 + _dot(rec_s[rows, :], w_ref[pl.ds(D_ATTN, D_LRU), ocols]))
            o_ref[rows, ocols] = z
            for part in range(n_out // LRU_BLOCK):
                zp = z[:, part * LRU_BLOCK:(part + 1) * LRU_BLOCK]
                ssq = ssq + zp * zp
        inv = lax.rsqrt(jnp.sum(ssq, axis=1, keepdims=True) * (1.0 / D_MODEL) + NORM_EPS)
        o_ref[rows, :] = (o_ref[rows, :] * inv) * fg_ref[...]


def _out_proj(x2d, att, rec, gates, w_out_bf16, final_g):
    rows = x2d.shape[0]
    tm = OUT_ROWS
    tile = lambda width: pl.BlockSpec((tm, width), lambda s: (s, 0))
    return pl.pallas_call(
        _out_kernel,
        grid=(rows // tm,),
        in_specs=[tile(D_MODEL), tile(D_ATTN), tile(D_LRU),
                  pl.BlockSpec((1, tm, D_LRU), lambda s: (1, s, 0)),
                  pl.BlockSpec((D_ATTN + D_LRU, D_MODEL), lambda s: (0, 0),
                               pipeline_mode=pl.Buffered(1)),
                  pl.BlockSpec((1, D_MODEL), lambda s: (0, 0))],
        out_specs=tile(D_MODEL),
        out_shape=jax.ShapeDtypeStruct((rows, D_MODEL), F32),
        scratch_shapes=[pltpu.VMEM((tm, D_LRU), BF16)],
        compiler_params=pltpu.CompilerParams(
            dimension_semantics=("arbitrary",),
            vmem_limit_bytes=VMEM_LIMIT_OUT),
        name="out_proj",
    )(x2d, att, rec, gates, w_out_bf16, final_g)


def kernel(x, meta_tokens, rel_bias, norm_g, w_in, conv_w, conv_b, w_a, b_a, w_x, b_x, lru_lambda,
           lam_q1, lam_k1, lam_q2, lam_k2, subln_g, w_out, final_g):
    batch, seq, _ = x.shape
    x2d = x.reshape(batch * seq, D_MODEL)
    project = functools.partial(_project, w_in=w_in[0], batch=batch, seq=seq)
    vt, v_meta, h2d, h_meta = project((x2d, meta_tokens, norm_g), col_tile0=2, n_slabs=1, mode="vt")
    qk, qk_meta = project((h2d, h_meta), col_tile0=0, n_slabs=2, mode="heads")
    u, u_meta, w_out_bf16 = project((h2d, h_meta, w_out[0]), col_tile0=4, n_slabs=1, mode="flat")
    gates, rec = _gate_lru(h2d, w_in[0], u, u_meta, conv_w[0], conv_b, w_a[0], b_a, w_x[0], b_x,
                           lru_lambda, seq=seq)
    att = _attention(qk, qk_meta, vt, v_meta, gates, rel_bias, lam_q1, lam_k1, lam_q2, lam_k2,
                     subln_g)
    out = _out_proj(x2d, att, rec, gates, w_out_bf16, final_g.reshape(1, D_MODEL))
    return out.reshape(batch, seq, D_MODEL)
```

```python
import functools
import math

import numpy as np
import jax
import jax.numpy as jnp
from jax import lax
from jax.experimental import pallas as pl
from jax.experimental.pallas import tpu as pltpu

D_MODEL = 2048
N_META = 16
D_ATTN = 1024
D_LRU = 1024
N_HEADS = 8
HEAD_DIM = 64
V_DIM = 128
N_LRU_BLOCKS = 8
LRU_BLOCK = 128
CONV_WIDTH = 4
LRU_C = 8.0
N_BUCKETS = 32
MAX_DISTANCE = 128
NORM_EPS = 1e-6
SUBLN_EPS = 1e-5
NEG_INF = -1e30
LAMBDA_INIT = 0.8 - 0.6 * math.exp(-0.3 * 0)
LOG2E = math.log2(math.e)

BF16 = jnp.bfloat16
F32 = jnp.float32

VMEM_LIMIT_PROJ = 56 * 1024 * 1024
VMEM_LIMIT_ATTN = 48 * 1024 * 1024
VMEM_LIMIT_OUT = 56 * 1024 * 1024

PROJ_ROWS = 1024
ATTN_TQ = 512
ATTN_TK = 512
SUM_ROWS = 16
OUT_ROWS = 512
OUT_PIECE = 256
LRU_ROWS = 256
SEG_PITCH = 40


def _bucket_thresholds():
    max_exact = N_BUCKETS // 2
    d = np.arange(0, 4 * MAX_DISTANCE, dtype=np.int64)
    val = (np.log(np.maximum(d, 1).astype(np.float64) / max_exact)
           / math.log(MAX_DISTANCE / max_exact) * (N_BUCKETS - max_exact))
    large = np.minimum(max_exact + np.floor(val + 1e-9).astype(np.int64), N_BUCKETS - 1)
    bucket = np.where(d < max_exact, d, large)
    frac = np.abs(val - np.round(val))
    interior = (d > max_exact) & (d < MAX_DISTANCE)
    assert frac[interior].min() > 1e-3
    assert (np.diff(bucket) >= 0).all() and bucket[MAX_DISTANCE] == N_BUCKETS - 1
    return tuple(int(np.argmax(bucket >= j)) for j in range(1, N_BUCKETS))


BUCKET_THRESHOLDS = _bucket_thresholds()


def _rms(x, g, eps):
    y = x * lax.rsqrt(jnp.mean(x * x, axis=-1, keepdims=True) + eps)
    return y * g


def _dot(a, b):
    return jnp.dot(a, b, preferred_element_type=F32)


def _dot_nt(a, b):
    return lax.dot_general(a, b, (((1,), (1,)), ((), ())), preferred_element_type=F32)


def _proj_kernel(*refs, mode):
    if mode == "vt":
        x_ref, meta_ref, g_ref, w_ref, o_ref, om_ref, h_ref, hm_ref, wb_ref = refs
    elif mode == "flat":
        h_ref, hm_ref, wo_ref, w_ref, o_ref, om_ref, wob_ref, wb_ref = refs
    else:
        h_ref, hm_ref, w_ref, o_ref, om_ref, wb_ref = refs
    j = pl.program_id(0)
    i = pl.program_id(1)
    out_scale = jnp.where(j == 0, HEAD_DIM ** -0.5 * LOG2E, 1.0).astype(F32) if mode == "heads" else 1.0
    head = lambda y, h: y[:, h * V_DIM:(h + 1) * V_DIM]

    @pl.when(i == 0)
    def _():
        wb_ref[...] = w_ref[...].astype(BF16)
        if mode == "vt":
            hm_ref[...] = _rms(meta_ref[...], g_ref[...], NORM_EPS).astype(BF16)
        ym = _dot(hm_ref[...], wb_ref[...]) * out_scale
        if mode == "flat":
            om_ref[0] = ym
        else:
            for h in range(N_HEADS):
                om_ref[0, h] = head(ym, h).astype(BF16)

    if mode == "vt":
        h_ref[...] = _rms(x_ref[...], g_ref[...], NORM_EPS).astype(BF16)
    y = _dot(h_ref[...], wb_ref[...]) * out_scale
    if mode == "flat":
        o_ref[0] = y
        wob_ref[...] = wo_ref[...].astype(BF16)
    elif mode == "heads":
        for h in range(N_HEADS):
            o_ref[0, 0, h] = head(y, h).astype(BF16)
    else:
        rows = y.shape[0]
        for h in range(N_HEADS):
            o_ref[0, h, pl.ds(0, V_DIM), :] = head(y, h).T.astype(BF16)
            o_ref[0, h, pl.ds(V_DIM, SUM_ROWS), :] = jnp.ones((SUM_ROWS, rows), BF16)


def _project(inputs, w_in, *, col_tile0, n_slabs, mode, batch, seq):
    rows = inputs[0].shape[0]
    tm = PROJ_ROWS
    n_i = rows // tm
    n_ib = seq // tm
    row_tile = pl.BlockSpec((tm, D_MODEL), lambda j, i: (i, 0))
    meta_rows = pl.BlockSpec((N_META, D_MODEL), lambda j, i: (0, 0))
    if mode == "vt":
        in_specs = [row_tile, meta_rows, pl.BlockSpec((1, D_MODEL), lambda j, i: (0, 0))]
    elif mode == "flat":
        assert n_slabs == 1
        w_out_rows = pl.BlockSpec(((D_ATTN + D_LRU) // n_i, D_MODEL), lambda j, i: (i, 0))
        in_specs = [row_tile, meta_rows, w_out_rows]
    else:
        in_specs = [row_tile, meta_rows]
    in_specs.append(pl.BlockSpec((D_MODEL, D_ATTN), lambda j, i: (0, j + col_tile0)))
    if mode == "heads":
        out_shape = (jax.ShapeDtypeStruct((n_slabs, batch, N_HEADS, seq, V_DIM), BF16),
                     jax.ShapeDtypeStruct((n_slabs, N_HEADS, N_META, V_DIM), BF16))
        out_specs = (pl.BlockSpec((1, 1, N_HEADS, tm, V_DIM),
                                  lambda j, i: (j, i // n_ib, 0, i % n_ib, 0)),
                     pl.BlockSpec((1, N_HEADS, N_META, V_DIM), lambda j, i: (j, 0, 0, 0)))
    elif mode == "vt":
        assert n_slabs == 1
        out_shape = (jax.ShapeDtypeStruct((batch, N_HEADS, V_DIM + SUM_ROWS, seq), BF16),
                     jax.ShapeDtypeStruct((n_slabs, N_HEADS, N_META, V_DIM), BF16),
                     jax.ShapeDtypeStruct((rows, D_MODEL), BF16),
                     jax.ShapeDtypeStruct((N_META, D_MODEL), BF16))
        out_specs = (pl.BlockSpec((1, N_HEADS, V_DIM + SUM_ROWS, tm),
                                  lambda j, i: (i // n_ib, 0, 0, i % n_ib)),
                     pl.BlockSpec((1, N_HEADS, N_META, V_DIM), lambda j, i: (j, 0, 0, 0)),
                     row_tile, meta_rows)
    else:
        out_shape = (jax.ShapeDtypeStruct((n_slabs, rows, D_ATTN), F32),
                     jax.ShapeDtypeStruct((n_slabs, N_META, D_ATTN), F32),
                     jax.ShapeDtypeStruct((D_ATTN + D_LRU, D_MODEL), BF16))
        out_specs = (pl.BlockSpec((1, tm, D_ATTN), lambda j, i: (j, i, 0)),
                     pl.BlockSpec((1, N_META, D_ATTN), lambda j, i: (j, 0, 0)),
                     w_out_rows)
    return pl.pallas_call(
        functools.partial(_proj_kernel, mode=mode),
        grid=(n_slabs, n_i),
        in_specs=in_specs,
        out_specs=out_specs,
        out_shape=out_shape,
        scratch_shapes=[pltpu.VMEM((D_MODEL, D_ATTN), BF16)],
        compiler_params=pltpu.CompilerParams(
            dimension_semantics=("arbitrary", "arbitrary"),
            vmem_limit_bytes=VMEM_LIMIT_PROJ),
        name={"heads": "proj_qk", "vt": "proj_v", "flat": "proj_u"}[mode],
    )(*inputs, w_in)


def _toeplitz_bias(dist, bucket_bias):
    b = jnp.full(dist.shape, bucket_bias[0], F32)
    for j, thr in enumerate(BUCKET_THRESHOLDS, start=1):
        b = jnp.where(dist >= thr, bucket_bias[j], b)
    return b


def _fill_bias(ref, lead, n_rows, n_cols, d0, bucket_bias):
    sub = lax.broadcasted_iota(jnp.int32, (8, V_DIM), 0)
    lane = lax.broadcasted_iota(jnp.int32, (8, V_DIM), 1)
    zeros = jnp.zeros((8, V_DIM), F32)
    masked = jnp.full((8, V_DIM), NEG_INF, F32)
    cache = {}
    for a8 in range(n_rows // 8):
        for b in range(n_cols // V_DIM):
            off = d0 + V_DIM * b - 8 * a8
            if off + V_DIM - 1 < 0:
                tile = masked
            elif off - 7 >= MAX_DISTANCE:
                tile = zeros
            else:
                if off not in cache:
                    d = off + lane - sub
                    cache[off] = jnp.where(d >= 0, _toeplitz_bias(d, bucket_bias), NEG_INF)
                tile = cache[off]
            ref[(*lead, pl.ds(8 * a8, 8), pl.ds(V_DIM * b, V_DIM))] = tile


def _attn_kernel(rb_ref, lq1_ref, lk1_ref, lq2_ref, lk2_ref, q_ref, k_ref, vt_ref, km_ref, vm_ref,
                 g_ref, sg_ref, o_ref, vmt_ref, bd_ref, bm_ref, bn_ref, sbuf, sc_ref, mx_ref,
                 m_ref, acc_ref):
    h = pl.program_id(0)
    tq, tk = ATTN_TQ, ATTN_TK
    seq = k_ref.shape[2]
    nq = seq // tq
    assert nq % 2 == 0 and nq >= 4

    far = rb_ref[N_BUCKETS - 1, h]

    @pl.when(pl.program_id(1) == 0)
    def _():
        bucket_bias = [(rb_ref[j, h] - far) * LOG2E for j in range(N_BUCKETS)]
        _fill_bias(bd_ref, (), tk, tq, 0, bucket_bias)
        _fill_bias(bn_ref, (), MAX_DISTANCE, MAX_DISTANCE, MAX_DISTANCE, bucket_bias)
        _fill_bias(bm_ref, (1,), N_META, tq, N_META, bucket_bias)
        bm_ref[0] = jnp.zeros((N_META, tq), F32)
        padded = jnp.concatenate(
            [vm_ref[0].astype(F32), jnp.zeros((V_DIM - N_META, V_DIM), F32)], axis=0)
        vmt_ref[pl.ds(0, V_DIM), :] = padded.T.astype(BF16)
        vmt_ref[pl.ds(V_DIM, SUM_ROWS), :] = jnp.ones((SUM_ROWS, V_DIM), BF16)

    lane = lax.broadcasted_iota(jnp.int32, (tq, V_DIM), 1)

    lam = (jnp.exp(jnp.sum(lq1_ref[...] * lk1_ref[...], keepdims=True))
           - jnp.exp(jnp.sum(lq2_ref[...] * lk2_ref[...], keepdims=True))
           + LAMBDA_INIT)

    def init_stats(st):
        m_ref[st] = jnp.full(m_ref.shape[1:], NEG_INF, F32)
        for a in range(2):
            acc_ref[st, a, pl.ds(0, V_DIM), :] = jnp.zeros((V_DIM, tq), F32)
            acc_ref[st, a, pl.ds(V_DIM, SUM_ROWS), :] = jnp.ones((SUM_ROWS, tq), F32)

    hk, hq = tk // 2, tq // 2

    def produce(buf, q_off, k_off, kind):
        q = q_ref[0, 0, pl.ds(pl.multiple_of(q_off, tq), tq), :]
        zero = jnp.zeros_like(q)
        k_off = pl.multiple_of(k_off, tk)
        k_t = k_ref[0, 0, pl.ds(k_off, tk), :]
        for a in range(2):
            qa = jnp.where((lane < HEAD_DIM) if a == 0 else (lane >= HEAD_DIM), q, zero)
            if kind == "diag":
                top = _dot_nt(k_t[:hk], qa) + bd_ref[pl.ds(0, hk), :]
                low = _dot_nt(k_t[hk:], qa[hq:]) + bd_ref[pl.ds(hk, hk), pl.ds(hq, hq)]
                sc = _dot_nt(km_ref[0], qa) + bm_ref[jnp.where(k_off == 0, 1, 0)]
                sbuf[buf, a, pl.ds(0, hk), :] = top
                sbuf[buf, a, pl.ds(hk, hk), pl.ds(hq, hq)] = low
                sc_ref[buf, a] = sc
                mx = jnp.maximum(jnp.max(top, axis=0, keepdims=True),
                                 jnp.max(sc, axis=0, keepdims=True))
                mx_low = jnp.max(low, axis=0, keepdims=True)
                mx_ref[buf, a] = jnp.concatenate(
                    [mx[:, :hq], jnp.maximum(mx[:, hq:], mx_low)], axis=1)
                continue
            s = _dot_nt(k_t, qa)
            if kind == "near":
                band = tk - MAX_DISTANCE
                corner = s[band:, :MAX_DISTANCE] + bn_ref[...]
                sbuf[buf, a, pl.ds(0, band), :] = s[:band]
                sbuf[buf, a, pl.ds(band, MAX_DISTANCE), pl.ds(0, MAX_DISTANCE)] = corner
                sbuf[buf, a, pl.ds(band, MAX_DISTANCE), pl.ds(MAX_DISTANCE, tq - MAX_DISTANCE)] = (
                    s[band:, MAX_DISTANCE:])
                mx = jnp.max(s[:band], axis=0, keepdims=True)
                mx_ref[buf, a] = jnp.concatenate(
                    [jnp.maximum(mx[:, :MAX_DISTANCE], jnp.max(corner, axis=0, keepdims=True)),
                     jnp.maximum(mx[:, MAX_DISTANCE:],
                                 jnp.max(s[band:, MAX_DISTANCE:], axis=0, keepdims=True))], axis=1)
                continue
            sbuf[buf, a] = s
            mx_ref[buf, a] = jnp.max(s, axis=0, keepdims=True)

    def consume(buf, st, k_off, diag=False):
        k_off = pl.multiple_of(k_off, tk)
        v_t = vt_ref[0, 0, :, pl.ds(k_off, tk)]
        for a in range(2):
            m_old = m_ref[st, a]
            m_new = jnp.maximum(m_old, mx_ref[buf, a])
            alpha = jnp.exp2(m_old - m_new)
            if diag:
                top = jnp.exp2(sbuf[buf, a, pl.ds(0, hk), :] - m_new)
                low = jnp.exp2(sbuf[buf, a, pl.ds(hk, hk), pl.ds(hq, hq)] - m_new[:, hq:])
                pc = jnp.exp2(sc_ref[buf, a] - m_new)
                pv = (_dot(v_t[:, :hk], top.astype(BF16))
                      + _dot(vmt_ref[...][:, :N_META], pc.astype(BF16)))
                pv_low = _dot(v_t[:, hk:], low.astype(BF16))
                pv = jnp.concatenate([pv[:, :hq], pv[:, hq:] + pv_low], axis=1)
            else:
                p = jnp.exp2(sbuf[buf, a] - m_new)
                pv = _dot(v_t, p.astype(BF16))
            acc_ref[st, a] = alpha * acc_ref[st, a] + pv
            m_ref[st, a] = m_new

    def finalize(st, q_off):
        q_rows = pl.ds(pl.multiple_of(q_off, tq), tq)
        heads = [acc_ref[st, a, pl.ds(0, V_DIM), :] / acc_ref[st, a, pl.ds(V_DIM, 1), :]
                 for a in range(2)]
        out_t = heads[0] - lam * heads[1]
        inv = lax.rsqrt(jnp.mean(out_t * out_t, axis=0, keepdims=True) + SUBLN_EPS)
        att = ((out_t * inv).T * sg_ref[...]) * (1.0 - LAMBDA_INIT)
        g = g_ref[0, q_rows, :]
        o_ref[q_rows, :] = (att * (g * jax.nn.sigmoid(g))).astype(o_ref.dtype)

    init_stats(0)
    produce(0, 0, 0, "diag")

    def q_tile(i, cur, odd):
        oth = 1 - cur
        q_off = i * tq

        def far_pair(tt, c):
            t = 2 * tt
            produce(oth, q_off, (t + 1) * tk, "far")
            consume(cur, cur, t * tk)
            produce(cur, q_off, (t + 2) * tk, "far")
            consume(oth, cur, (t + 1) * tk)
            return c

        n_far = jnp.maximum(i - 2, 0)
        lax.fori_loop(0, n_far // 2, far_pair, 0)

        def near_diag():
            produce(oth, q_off, (i - 1) * tk, "near")
            consume(cur, cur, (i - 2) * tk)
            produce(cur, q_off, i * tk, "diag")
            consume(oth, cur, (i - 1) * tk)

        def last(next_kind, has_prev=True):
            if has_prev:
                finalize(oth, (i - 1) * tq)
            if next_kind is not None:
                init_stats(oth)
                produce(oth, (i + 1) * tq, 0, next_kind)
            consume(cur, cur, i * tk, diag=True)

        if odd:
            def leftover_far_then(next_kind):
                consume(cur, cur, (n_far - 1) * tk)
                produce(cur, q_off, n_far * tk, "far")
                near_diag()
                last(next_kind)

            @pl.when((i >= 3) & (i < nq - 1))
            def _():
                leftover_far_then("far")

            @pl.when(i == nq - 1)
            def _():
                leftover_far_then(None)

            @pl.when(i == 1)
            def _():
                consume(cur, cur, 0)
                produce(cur, q_off, tk, "diag")
                last("far")
        else:
            @pl.when(i >= 2)
            def _():
                near_diag()
                last("far")

            @pl.when(i == 0)
            def _():
                last("near", has_prev=False)

    def q_pair(ii, carry):
        q_tile(2 * ii, 0, False)
        q_tile(2 * ii + 1, 1, True)
        return carry

    lax.fori_loop(0, nq // 2, q_pair, 0)
    finalize((nq - 1) % 2, (nq - 1) * tq)


def _attention(qk, qk_meta, vt, v_meta, gates, rel_bias, lam_q1, lam_k1, lam_q2, lam_k2, subln_g):
    _, batch, _, seq, _ = qk.shape
    tq, tk = ATTN_TQ, ATTN_TK
    smem = pl.BlockSpec(memory_space=pltpu.SMEM)
    row64 = pl.BlockSpec((1, HEAD_DIM), lambda h, b: (0, 0))
    qk_spec = lambda which: pl.BlockSpec((1, 1, 1, seq, V_DIM), lambda h, b: (which, b, h, 0, 0))
    meta_spec = lambda which: pl.BlockSpec((1, 1, N_META, V_DIM), lambda h, b: (which, h, 0, 0))

    def kernel(rb, lq1, lk1, lq2, lk2, q_ref, k_ref, vt_ref, km_ref, vm_ref, *rest):
        _attn_kernel(rb, lq1, lk1, lq2, lk2, q_ref.at[0], k_ref.at[0], vt_ref,
                     km_ref.at[0], vm_ref.at[0], *rest)

    return pl.pallas_call(
        kernel,
        grid=(N_HEADS, batch),
        in_specs=[smem, row64, row64, row64, row64,
                  qk_spec(0), qk_spec(1),
                  pl.BlockSpec((1, 1, V_DIM + SUM_ROWS, seq), lambda h, b: (b, h, 0, 0)),
                  meta_spec(1), meta_spec(0),
                  pl.BlockSpec((1, seq, V_DIM), lambda h, b: (0, b, h)),
                  pl.BlockSpec((1, V_DIM), lambda h, b: (0, 0))],
        out_specs=pl.BlockSpec((seq, V_DIM), lambda h, b: (b, h)),
        out_shape=jax.ShapeDtypeStruct((batch * seq, D_ATTN), BF16),
        scratch_shapes=[pltpu.VMEM((V_DIM + SUM_ROWS, V_DIM), BF16),
                        pltpu.VMEM((tk, tq), F32),
                        pltpu.VMEM((2, N_META, tq), F32),
                        pltpu.VMEM((MAX_DISTANCE, MAX_DISTANCE), F32),
                        pltpu.VMEM((2, 2, tk, tq), F32),
                        pltpu.VMEM((2, 2, N_META, tq), F32),
                        pltpu.VMEM((2, 2, 1, tq), F32),
                        pltpu.VMEM((2, 2, 1, tq), F32),
                        pltpu.VMEM((2, 2, V_DIM + SUM_ROWS, tq), F32)],
        compiler_params=pltpu.CompilerParams(
            dimension_semantics=("arbitrary", "arbitrary"),
            vmem_limit_bytes=VMEM_LIMIT_ATTN),
        name="diff_attention",
    )(rel_bias, lam_q1, lam_k1, lam_q2, lam_k2, qk, qk, vt, qk_meta, v_meta, gates, subln_g)


def _scan_block(a, b):
    n = a.shape[0]
    row = lax.broadcasted_iota(jnp.int32, a.shape, 0)
    s = 1
    while s < n:
        keep = row >= s
        a_sh = jnp.where(keep, pltpu.roll(a, s, 0), 1.0)
        b_sh = jnp.where(keep, pltpu.roll(b, s, 0), 0.0)
        b = b + a * b_sh
        a = a * a_sh
        s *= 2
    return a, b


def _gate_lru_kernel(h_ref, w_ref, u_ref, um_ref, cw_ref, cb_ref, wa_ref, wx_ref, ba_ref, bx_ref,
                     lam_ref, o_ref, rec_ref, wb_ref, wg_ref, ubuf, tail_ref, st_ref, u_p, h_p,
                     *, tiles_per_batch):
    i = pl.program_id(1)
    n = LRU_ROWS
    hist = 8
    seg = n // 8
    n_blk = u_ref.shape[1] // LRU_BLOCK
    n_out = D_ATTN // n_blk

    x = -lam_ref[...]
    softplus = jnp.maximum(x, 0.0) + jnp.log1p(jnp.exp(-jnp.abs(x)))

    @pl.when(i == 0)
    def _():
        wb_ref[...] = w_ref[...].astype(BF16)
        for blk in range(n_blk):
            wg_ref[blk, :, pl.ds(0, LRU_BLOCK)] = wa_ref[blk].astype(BF16)
            wg_ref[blk, :, pl.ds(LRU_BLOCK, LRU_BLOCK)] = wx_ref[blk].astype(BF16)

    def gate_inputs(rows, cols, blk):
        cw = cw_ref[:, cols]
        uc = (cw[3:4] * ubuf[pl.ds(hist, rows), cols] + cw[2:3] * ubuf[pl.ds(hist - 1, rows), cols]
              + cw[1:2] * ubuf[pl.ds(hist - 2, rows), cols] + cw[0:1] * ubuf[pl.ds(hist - 3, rows), cols]
              + cb_ref[:, cols])
        return uc, _dot(uc.astype(BF16), wg_ref[blk])

    def gates(uc, pre, cols, first):
        r = jax.nn.sigmoid(pre[:, :LRU_BLOCK] + ba_ref[:, cols])
        gi = jax.nn.sigmoid(pre[:, LRU_BLOCK:] + bx_ref[:, cols])
        log_a = -LRU_C * r * softplus[:, cols]
        a = jnp.exp(log_a)
        v = jnp.tanh(-log_a) * (a * a + 1.0)
        mult = jnp.where(v > 0.0, v * lax.rsqrt(v), 0.0)
        if first:
            row = lax.broadcasted_iota(jnp.int32, mult.shape, 0)
            mult = jnp.where(row == 0, 1.0, mult)
        return a, mult * gi * uc

    @pl.when(i % tiles_per_batch == 0)
    def _():
        ubuf[pl.ds(0, hist), :] = jnp.zeros((hist, ubuf.shape[1]), F32)
        ubuf[pl.ds(hist, N_META), :] = um_ref[...]
        for blk in range(n_blk):
            cols = slice(blk * LRU_BLOCK, (blk + 1) * LRU_BLOCK)
            a, b = gates(*gate_inputs(N_META, cols, blk), cols, True)
            _, hm = _scan_block(a, b)
            st_ref[:, cols] = hm[N_META - 1:N_META, :]
        tail_ref[...] = um_ref[pl.ds(N_META - hist, hist), :]

    sub = lax.broadcasted_iota(jnp.int32, (8, LRU_BLOCK), 0)
    step = lambda j: pl.ds(j, 8, stride=SEG_PITCH)

    def piece(t, carry):
        r0 = pl.multiple_of(t * n, n)
        for blk in range(n_blk):
            lru_block(r0, blk)
        tail_ref[...] = u_ref[pl.ds(r0 + n - hist, hist), :]
        return carry

    def lru_block(r0, blk):
        cols = slice(blk * LRU_BLOCK, (blk + 1) * LRU_BLOCK)
        ocols = slice(blk * n_out, (blk + 1) * n_out)
        o_ref[0, pl.ds(r0, n), ocols] = _dot(h_ref[pl.ds(r0, n), :], wb_ref[:, ocols])

        for k in range(8):
            u_p[blk, pl.ds(SEG_PITCH * k, seg), :] = u_ref[pl.ds(r0 + seg * k, seg), cols]
        us = [u_p[blk, step(j), :] for j in range(seg)]
        prev = tail_ref[:, cols]

        def before(back):
            return jnp.where(sub == 0, prev[hist - back:hist - back + 1, :],
                             pltpu.roll(us[seg - back], 1, 0))

        older = {-back: before(back) for back in range(1, CONV_WIDTH)}
        u_at = lambda j: us[j] if j >= 0 else older[j]
        cw = cw_ref[:, cols]
        cb = cb_ref[:, cols]
        uc = jnp.concatenate(
            [cw[3:4] * u_at(j) + cw[2:3] * u_at(j - 1) + cw[1:2] * u_at(j - 2)
             + cw[0:1] * u_at(j - 3) + cb for j in range(seg)], axis=0)
        a, b = gates(uc, _dot(uc.astype(BF16), wg_ref[blk]), cols, False)
        a_j = lambda j: a[8 * j:8 * j + 8]
        b_j = lambda j: b[8 * j:8 * j + 8]
        a_run, h_run = a_j(0), b_j(0)
        for j in range(1, seg):
            h_run = a_j(j) * h_run + b_j(j)
            a_run = a_j(j) * a_run
        a_cum, h_cum = _scan_block(a_run, h_run)
        carry = st_ref[:, cols]
        seg_end = h_cum + a_cum * carry
        st_ref[:, cols] = seg_end[7:8, :]
        h_run = jnp.where(sub == 0, carry, pltpu.roll(seg_end, 1, 0))
        for j in range(seg):
            h_run = a_j(j) * h_run + b_j(j)
            h_p[blk, step(j), :] = h_run
        for k in range(8):
            rec_ref[pl.ds(r0 + seg * k, seg), cols] = h_p[blk, pl.ds(SEG_PITCH * k, seg), :]

    lax.fori_loop(0, h_ref.shape[0] // n, piece, 0, unroll=True)


def _gate_lru(h2d, w_in, u, u_meta, conv_w, conv_b, w_a, b_a, w_x, b_x, lru_lambda, *, seq):
    rows = h2d.shape[0]
    tm = PROJ_ROWS
    n_halves = 2
    width = D_LRU // n_halves
    n_blk = N_LRU_BLOCKS // n_halves
    gate_col_tile = lambda j: 3 + 2 * j
    half = lambda n: pl.BlockSpec((n, width), lambda j, i: (0, j))
    wspec = pl.BlockSpec((n_blk, LRU_BLOCK, LRU_BLOCK), lambda j, i: (j, 0, 0))

    def kernel(h_ref, w_ref, u_ref, um_ref, *rest):
        _gate_lru_kernel(h_ref, w_ref, u_ref.at[0], um_ref.at[0], *rest, tiles_per_batch=seq // tm)

    return pl.pallas_call(
        kernel,
        grid=(n_halves, rows // tm),
        in_specs=[pl.BlockSpec((tm, D_MODEL), lambda j, i: (i, 0)),
                  pl.BlockSpec((D_MODEL, D_ATTN), lambda j, i: (0, gate_col_tile(j))),
                  pl.BlockSpec((1, tm, width), lambda j, i: (0, i, j)),
                  pl.BlockSpec((1, N_META, width), lambda j, i: (0, 0, j)),
                  half(CONV_WIDTH), half(1), wspec, wspec, half(1), half(1), half(1)],
        out_specs=(pl.BlockSpec((1, tm, D_ATTN), lambda j, i: (j, i, 0)),
                   pl.BlockSpec((tm, width), lambda j, i: (i, j))),
        out_shape=(jax.ShapeDtypeStruct((n_halves, rows, D_ATTN), F32),
                   jax.ShapeDtypeStruct((rows, D_LRU), F32)),
        scratch_shapes=[pltpu.VMEM((D_MODEL, D_ATTN), BF16),
                        pltpu.VMEM((n_blk, LRU_BLOCK, 2 * LRU_BLOCK), BF16),
                        pltpu.VMEM((8 + N_META, width), F32),
                        pltpu.VMEM((8, width), F32),
                        pltpu.VMEM((1, width), F32),
                        pltpu.VMEM((n_blk, 8 * SEG_PITCH, LRU_BLOCK), F32),
                        pltpu.VMEM((n_blk, 8 * SEG_PITCH, LRU_BLOCK), F32)],
        compiler_params=pltpu.CompilerParams(
            dimension_semantics=("arbitrary", "arbitrary"),
            vmem_limit_bytes=VMEM_LIMIT_PROJ),
        name="proj_gates_rglru",
    )(h2d, w_in, u, u_meta, conv_w, conv_b, w_a, w_x, b_a, b_x, lru_lambda)


def _out_kernel(x_ref, att_ref, rec_ref, g_ref, w_ref, fg_ref, o_ref, rec_s):
    n_out = 2 * LRU_BLOCK
    for piece in range(x_ref.shape[0] // OUT_PIECE):
        rows = pl.ds(piece * OUT_PIECE, OUT_PIECE)
        g = g_ref[0, rows, :]
        rec_s[rows, :] = (rec_ref[rows, :] * (g * jax.nn.sigmoid(g))).astype(BF16)
        ssq = jnp.zeros((OUT_PIECE, LRU_BLOCK), F32)
        for blk in range(D_MODEL // n_out):
            ocols = slice(blk * n_out, (blk + 1) * n_out)
            z = (x_ref[rows, ocols] + _dot(att_ref[rows, :], w_ref[pl.ds(0, D_ATTN), ocols])
                 + _dot(rec_s[rows, :], w_ref[pl.ds(D_ATTN, D_LRU), ocols]))
            o_ref[rows, ocols] = z
            for part in range(n_out // LRU_BLOCK):
                zp = z[:, part * LRU_BLOCK:(part + 1) * LRU_BLOCK]
                ssq = ssq + zp * zp
        inv = lax.rsqrt(jnp.sum(ssq, axis=1, keepdims=True) * (1.0 / D_MODEL) + NORM_EPS)
        o_ref[rows, :] = (o_ref[rows, :] * inv) * fg_ref[...]


def _out_proj(x2d, att, rec, gates, w_out_bf16, final_g):
    rows = x2d.shape[0]
    tm = OUT_ROWS
    tile = lambda width: pl.BlockSpec((tm, width), lambda s: (s, 0))
    return pl.pallas_call(
        _out_kernel,
        grid=(rows // tm,),
        in_specs=[tile(D_MODEL), tile(D_ATTN), tile(D_LRU),
                  pl.BlockSpec((1, tm, D_LRU), lambda s: (1, s, 0)),
                  pl.BlockSpec((D_ATTN + D_LRU, D_MODEL), lambda s: (0, 0),
                               pipeline_mode=pl.Buffered(1)),
                  pl.BlockSpec((1, D_MODEL), lambda s: (0, 0))],
        out_specs=tile(D_MODEL),
        out_shape=jax.ShapeDtypeStruct((rows, D_MODEL), F32),
        scratch_shapes=[pltpu.VMEM((tm, D_LRU), BF16)],
        compiler_params=pltpu.CompilerParams(
            dimension_semantics=("arbitrary",),
            vmem_limit_bytes=VMEM_LIMIT_OUT),
        name="out_proj",
    )(x2d, att, rec, gates, w_out_bf16, final_g)


def kernel(x, meta_tokens, rel_bias, norm_g, w_in, conv_w, conv_b, w_a, b_a, w_x, b_x, lru_lambda,
           lam_q1, lam_k1, lam_q2, lam_k2, subln_g, w_out, final_g):
    batch, seq, _ = x.shape
    x2d = x.reshape(batch * seq, D_MODEL)
    project = functools.partial(_project, w_in=w_in[0], batch=batch, seq=seq)
    vt, v_meta, h2d, h_meta = project((x2d, meta_tokens, norm_g), col_tile0=2, n_slabs=1, mode="vt")
    qk, qk_meta = project((h2d, h_meta), col_tile0=0, n_slabs=2, mode="heads")
    u, u_meta, w_out_bf16 = project((h2d, h_meta, w_out[0]), col_tile0=4, n_slabs=1, mode="flat")
    gates, rec = _gate_lru(h2d, w_in[0], u, u_meta, conv_w[0], conv_b, w_a[0], b_a, w_x[0], b_x,
                           lru_lambda, seq=seq)
    att = _attention(qk, qk_meta, vt, v_meta, gates, rel_bias, lam_q1, lam_k1, lam_q2, lam_k2,
                     subln_g)
    out = _out_proj(x2d, att, rec, gates, w_out_bf16, final_g.reshape(1, D_MODEL))
    return out.reshape(batch, seq, D_MODEL)
```

```python
import functools
import math

import numpy as np
import jax
import jax.numpy as jnp
from jax import lax
from jax.experimental import pallas as pl
from jax.experimental.pallas import tpu as pltpu

D_MODEL = 2048
N_META = 16
D_ATTN = 1024
D_LRU = 1024
N_HEADS = 8
HEAD_DIM = 64
V_DIM = 128
N_LRU_BLOCKS = 8
LRU_BLOCK = 128
CONV_WIDTH = 4
LRU_C = 8.0
N_BUCKETS = 32
MAX_DISTANCE = 128
NORM_EPS = 1e-6
SUBLN_EPS = 1e-5
NEG_INF = -1e30
LAMBDA_INIT = 0.8 - 0.6 * math.exp(-0.3 * 0)
LOG2E = math.log2(math.e)

BF16 = jnp.bfloat16
F32 = jnp.float32

VMEM_LIMIT_PROJ = 56 * 1024 * 1024
VMEM_LIMIT_ATTN = 48 * 1024 * 1024
VMEM_LIMIT_OUT = 56 * 1024 * 1024

PROJ_ROWS = 1024
NORM_PIECE = 256
ATTN_TQ = 512
ATTN_TK = 512
SUM_ROWS = 16
OUT_ROWS = 512
OUT_PIECE = 256
LRU_ROWS = 256
SEG_PITCH = 40


def _bucket_thresholds():
    max_exact = N_BUCKETS // 2
    d = np.arange(0, 4 * MAX_DISTANCE, dtype=np.int64)
    val = (np.log(np.maximum(d, 1).astype(np.float64) / max_exact)
           / math.log(MAX_DISTANCE / max_exact) * (N_BUCKETS - max_exact))
    large = np.minimum(max_exact + np.floor(val + 1e-9).astype(np.int64), N_BUCKETS - 1)
    bucket = np.where(d < max_exact, d, large)
    frac = np.abs(val - np.round(val))
    interior = (d > max_exact) & (d < MAX_DISTANCE)
    assert frac[interior].min() > 1e-3
    assert (np.diff(bucket) >= 0).all() and bucket[MAX_DISTANCE] == N_BUCKETS - 1
    return tuple(int(np.argmax(bucket >= j)) for j in range(1, N_BUCKETS))


BUCKET_THRESHOLDS = _bucket_thresholds()


def _rms(x, g, eps):
    y = x * lax.rsqrt(jnp.mean(x * x, axis=-1, keepdims=True) + eps)
    return y * g


def _dot(a, b):
    return jnp.dot(a, b, preferred_element_type=F32)


def _dot_nt(a, b):
    return lax.dot_general(a, b, (((1,), (1,)), ((), ())), preferred_element_type=F32)


def _proj_kernel(*refs, mode):
    if mode == "vt":
        x_ref, meta_ref, g_ref, w_ref, o_ref, om_ref, h_ref, hm_ref, wb_ref = refs
    elif mode == "flat":
        h_ref, hm_ref, wo_ref, w_ref, o_ref, om_ref, wob_ref, wb_ref = refs
    else:
        h_ref, hm_ref, w_ref, o_ref, om_ref, wb_ref = refs
    j = pl.program_id(0)
    i = pl.program_id(1)
    out_scale = jnp.where(j == 0, HEAD_DIM ** -0.5 * LOG2E, 1.0).astype(F32) if mode == "heads" else 1.0
    head = lambda y, h: y[:, h * V_DIM:(h + 1) * V_DIM]

    @pl.when(i == 0)
    def _():
        wb_ref[...] = w_ref[...].astype(BF16)
        if mode == "vt":
            hm_ref[...] = _rms(meta_ref[...], g_ref[...], NORM_EPS).astype(BF16)
        ym = _dot(hm_ref[...], wb_ref[...]) * out_scale
        if mode == "flat":
            om_ref[0] = ym
        else:
            for h in range(N_HEADS):
                om_ref[0, h] = head(ym, h).astype(BF16)

    if mode == "vt":
        for piece in range(x_ref.shape[0] // NORM_PIECE):
            rows = pl.ds(piece * NORM_PIECE, NORM_PIECE)
            h_ref[rows, :] = _rms(x_ref[rows, :], g_ref[...], NORM_EPS).astype(BF16)
            y = _dot(h_ref[rows, :], wb_ref[...])
            for h in range(N_HEADS):
                o_ref[0, h, pl.ds(0, V_DIM), rows] = head(y, h).T.astype(BF16)
                o_ref[0, h, pl.ds(V_DIM, SUM_ROWS), rows] = jnp.ones((SUM_ROWS, NORM_PIECE), BF16)
        return
    y = _dot(h_ref[...], wb_ref[...]) * out_scale
    if mode == "flat":
        o_ref[0] = y
        wob_ref[...] = wo_ref[...].astype(BF16)
    else:
        for h in range(N_HEADS):
            o_ref[0, 0, h] = head(y, h).astype(BF16)


def _project(inputs, w_in, *, col_tile0, n_slabs, mode, batch, seq):
    rows = inputs[0].shape[0]
    tm = PROJ_ROWS
    n_i = rows // tm
    n_ib = seq // tm
    row_tile = pl.BlockSpec((tm, D_MODEL), lambda j, i: (i, 0))
    meta_rows = pl.BlockSpec((N_META, D_MODEL), lambda j, i: (0, 0))
    if mode == "vt":
        in_specs = [row_tile, meta_rows, pl.BlockSpec((1, D_MODEL), lambda j, i: (0, 0))]
    elif mode == "flat":
        assert n_slabs == 1
        w_out_rows = pl.BlockSpec(((D_ATTN + D_LRU) // n_i, D_MODEL), lambda j, i: (i, 0))
        in_specs = [row_tile, meta_rows, w_out_rows]
    else:
        in_specs = [row_tile, meta_rows]
    in_specs.append(pl.BlockSpec((D_MODEL, D_ATTN), lambda j, i: (0, j + col_tile0)))
    if mode == "heads":
        out_shape = (jax.ShapeDtypeStruct((n_slabs, batch, N_HEADS, seq, V_DIM), BF16),
                     jax.ShapeDtypeStruct((n_slabs, N_HEADS, N_META, V_DIM), BF16))
        out_specs = (pl.BlockSpec((1, 1, N_HEADS, tm, V_DIM),
                                  lambda j, i: (j, i // n_ib, 0, i % n_ib, 0)),
                     pl.BlockSpec((1, N_HEADS, N_META, V_DIM), lambda j, i: (j, 0, 0, 0)))
    elif mode == "vt":
        assert n_slabs == 1
        out_shape = (jax.ShapeDtypeStruct((batch, N_HEADS, V_DIM + SUM_ROWS, seq), BF16),
                     jax.ShapeDtypeStruct((n_slabs, N_HEADS, N_META, V_DIM), BF16),
                     jax.ShapeDtypeStruct((rows, D_MODEL), BF16),
                     jax.ShapeDtypeStruct((N_META, D_MODEL), BF16))
        out_specs = (pl.BlockSpec((1, N_HEADS, V_DIM + SUM_ROWS, tm),
                                  lambda j, i: (i // n_ib, 0, 0, i % n_ib)),
                     pl.BlockSpec((1, N_HEADS, N_META, V_DIM), lambda j, i: (j, 0, 0, 0)),
                     row_tile, meta_rows)
    else:
        out_shape = (jax.ShapeDtypeStruct((n_slabs, rows, D_ATTN), F32),
                     jax.ShapeDtypeStruct((n_slabs, N_META, D_ATTN), F32),
                     jax.ShapeDtypeStruct((D_ATTN + D_LRU, D_MODEL), BF16))
        out_specs = (pl.BlockSpec((1, tm, D_ATTN), lambda j, i: (j, i, 0)),
                     pl.BlockSpec((1, N_META, D_ATTN), lambda j, i: (j, 0, 0)),
                     w_out_rows)
    return pl.pallas_call(
        functools.partial(_proj_kernel, mode=mode),
        grid=(n_slabs, n_i),
        in_specs=in_specs,
        out_specs=out_specs,
        out_shape=out_shape,
        scratch_shapes=[pltpu.VMEM((D_MODEL, D_ATTN), BF16)],
        compiler_params=pltpu.CompilerParams(
            dimension_semantics=("arbitrary", "arbitrary"),
            vmem_limit_bytes=VMEM_LIMIT_PROJ),
        name={"heads": "proj_qk", "vt": "proj_v", "flat": "proj_u"}[mode],
    )(*inputs, w_in)


def _toeplitz_bias(dist, bucket_bias):
    b = jnp.full(dist.shape, bucket_bias[0], F32)
    for j, thr in enumerate(BUCKET_THRESHOLDS, start=1):
        b = jnp.where(dist >= thr, bucket_bias[j], b)
    return b


def _fill_bias(ref, lead, n_rows, n_cols, d0, bucket_bias):
    sub = lax.broadcasted_iota(jnp.int32, (8, V_DIM), 0)
    lane = lax.broadcasted_iota(jnp.int32, (8, V_DIM), 1)
    zeros = jnp.zeros((8, V_DIM), F32)
    masked = jnp.full((8, V_DIM), NEG_INF, F32)
    cache = {}
    for a8 in range(n_rows // 8):
        for b in range(n_cols // V_DIM):
            off = d0 + V_DIM * b - 8 * a8
            if off + V_DIM - 1 < 0:
                tile = masked
            elif off - 7 >= MAX_DISTANCE:
                tile = zeros
            else:
                if off not in cache:
                    d = off + lane - sub
                    cache[off] = jnp.where(d >= 0, _toeplitz_bias(d, bucket_bias), NEG_INF)
                tile = cache[off]
            ref[(*lead, pl.ds(8 * a8, 8), pl.ds(V_DIM * b, V_DIM))] = tile


def _attn_kernel(rb_ref, lq1_ref, lk1_ref, lq2_ref, lk2_ref, q_ref, k_ref, vt_ref, km_ref, vm_ref,
                 g_ref, sg_ref, o_ref, vmt_ref, bd_ref, bm_ref, bn_ref, sbuf, sc_ref, mx_ref,
                 m_ref, acc_ref):
    h = pl.program_id(0)
    tq, tk = ATTN_TQ, ATTN_TK
    seq = k_ref.shape[2]
    nq = seq // tq
    assert nq % 2 == 0 and nq >= 4

    far = rb_ref[N_BUCKETS - 1, h]

    @pl.when(pl.program_id(1) == 0)
    def _():
        bucket_bias = [(rb_ref[j, h] - far) * LOG2E for j in range(N_BUCKETS)]
        _fill_bias(bd_ref, (), tk, tq, 0, bucket_bias)
        _fill_bias(bn_ref, (), MAX_DISTANCE, MAX_DISTANCE, MAX_DISTANCE, bucket_bias)
        _fill_bias(bm_ref, (1,), N_META, tq, N_META, bucket_bias)
        bm_ref[0] = jnp.zeros((N_META, tq), F32)
        padded = jnp.concatenate(
            [vm_ref[0].astype(F32), jnp.zeros((V_DIM - N_META, V_DIM), F32)], axis=0)
        vmt_ref[pl.ds(0, V_DIM), :] = padded.T.astype(BF16)
        vmt_ref[pl.ds(V_DIM, SUM_ROWS), :] = jnp.ones((SUM_ROWS, V_DIM), BF16)

    lane = lax.broadcasted_iota(jnp.int32, (tq, V_DIM), 1)

    lam = (jnp.exp(jnp.sum(lq1_ref[...] * lk1_ref[...], keepdims=True))
           - jnp.exp(jnp.sum(lq2_ref[...] * lk2_ref[...], keepdims=True))
           + LAMBDA_INIT)

    def init_stats(st):
        m_ref[st] = jnp.full(m_ref.shape[1:], NEG_INF, F32)
        for a in range(2):
            acc_ref[st, a, pl.ds(0, V_DIM), :] = jnp.zeros((V_DIM, tq), F32)
            acc_ref[st, a, pl.ds(V_DIM, SUM_ROWS), :] = jnp.ones((SUM_ROWS, tq), F32)

    hk, hq = tk // 2, tq // 2

    def produce(buf, q_off, k_off, kind):
        q = q_ref[0, 0, pl.ds(pl.multiple_of(q_off, tq), tq), :]
        zero = jnp.zeros_like(q)
        k_off = pl.multiple_of(k_off, tk)
        k_t = k_ref[0, 0, pl.ds(k_off, tk), :]
        for a in range(2):
            qa = jnp.where((lane < HEAD_DIM) if a == 0 else (lane >= HEAD_DIM), q, zero)
            if kind == "diag":
                top = _dot_nt(k_t[:hk], qa) + bd_ref[pl.ds(0, hk), :]
                low = _dot_nt(k_t[hk:], qa[hq:]) + bd_ref[pl.ds(hk, hk), pl.ds(hq, hq)]
                sc = _dot_nt(km_ref[0], qa) + bm_ref[jnp.where(k_off == 0, 1, 0)]
                sbuf[buf, a, pl.ds(0, hk), :] = top
                sbuf[buf, a, pl.ds(hk, hk), pl.ds(hq, hq)] = low
                sc_ref[buf, a] = sc
                mx = jnp.maximum(jnp.max(top, axis=0, keepdims=True),
                                 jnp.max(sc, axis=0, keepdims=True))
                mx_low = jnp.max(low, axis=0, keepdims=True)
                mx_ref[buf, a] = jnp.concatenate(
                    [mx[:, :hq], jnp.maximum(mx[:, hq:], mx_low)], axis=1)
                continue
            s = _dot_nt(k_t, qa)
            if kind == "near":
                band = tk - MAX_DISTANCE
                corner = s[band:, :MAX_DISTANCE] + bn_ref[...]
                sbuf[buf, a, pl.ds(0, band), :] = s[:band]
                sbuf[buf, a, pl.ds(band, MAX_DISTANCE), pl.ds(0, MAX_DISTANCE)] = corner
                sbuf[buf, a, pl.ds(band, MAX_DISTANCE), pl.ds(MAX_DISTANCE, tq - MAX_DISTANCE)] = (
                    s[band:, MAX_DISTANCE:])
                mx = jnp.max(s[:band], axis=0, keepdims=True)
                mx_ref[buf, a] = jnp.concatenate(
                    [jnp.maximum(mx[:, :MAX_DISTANCE], jnp.max(corner, axis=0, keepdims=True)),
                     jnp.maximum(mx[:, MAX_DISTANCE:],
                                 jnp.max(s[band:, MAX_DISTANCE:], axis=0, keepdims=True))], axis=1)
                continue
            sbuf[buf, a] = s
            mx_ref[buf, a] = jnp.max(s, axis=0, keepdims=True)

    def consume(buf, st, k_off, diag=False):
        k_off = pl.multiple_of(k_off, tk)
        v_t = vt_ref[0, 0, :, pl.ds(k_off, tk)]
        for a in range(2):
            m_old = m_ref[st, a]
            m_new = jnp.maximum(m_old, mx_ref[buf, a])
            alpha = jnp.exp2(m_old - m_new)
            if diag:
                top = jnp.exp2(sbuf[buf, a, pl.ds(0, hk), :] - m_new)
                low = jnp.exp2(sbuf[buf, a, pl.ds(hk, hk), pl.ds(hq, hq)] - m_new[:, hq:])
                pc = jnp.exp2(sc_ref[buf, a] - m_new)
                pv = (_dot(v_t[:, :hk], top.astype(BF16))
                      + _dot(vmt_ref[...][:, :N_META], pc.astype(BF16)))
                pv_low = _dot(v_t[:, hk:], low.astype(BF16))
                pv = jnp.concatenate([pv[:, :hq], pv[:, hq:] + pv_low], axis=1)
            else:
                p = jnp.exp2(sbuf[buf, a] - m_new)
                pv = _dot(v_t, p.astype(BF16))
            acc_ref[st, a] = alpha * acc_ref[st, a] + pv
            m_ref[st, a] = m_new

    def finalize(st, q_off):
        q_rows = pl.ds(pl.multiple_of(q_off, tq), tq)
        heads = [acc_ref[st, a, pl.ds(0, V_DIM), :] / acc_ref[st, a, pl.ds(V_DIM, 1), :]
                 for a in range(2)]
        out_t = heads[0] - lam * heads[1]
        inv = lax.rsqrt(jnp.mean(out_t * out_t, axis=0, keepdims=True) + SUBLN_EPS)
        att = ((out_t * inv).T * sg_ref[...]) * (1.0 - LAMBDA_INIT)
        g = g_ref[0, q_rows, :]
        o_ref[q_rows, :] = (att * (g * jax.nn.sigmoid(g))).astype(o_ref.dtype)

    init_stats(0)
    produce(0, 0, 0, "diag")

    def q_tile(i, cur, odd):
        oth = 1 - cur
        q_off = i * tq

        def far_pair(tt, c):
            t = 2 * tt
            produce(oth, q_off, (t + 1) * tk, "far")
            consume(cur, cur, t * tk)
            produce(cur, q_off, (t + 2) * tk, "far")
            consume(oth, cur, (t + 1) * tk)
            return c

        n_far = jnp.maximum(i - 2, 0)
        lax.fori_loop(0, n_far // 2, far_pair, 0)

        def near_diag():
            produce(oth, q_off, (i - 1) * tk, "near")
            consume(cur, cur, (i - 2) * tk)
            produce(cur, q_off, i * tk, "diag")
            consume(oth, cur, (i - 1) * tk)

        def last(next_kind, has_prev=True):
            if has_prev:
                finalize(oth, (i - 1) * tq)
            if next_kind is not None:
                init_stats(oth)
                produce(oth, (i + 1) * tq, 0, next_kind)
            consume(cur, cur, i * tk, diag=True)

        if odd:
            def leftover_far_then(next_kind):
                consume(cur, cur, (n_far - 1) * tk)
                produce(cur, q_off, n_far * tk, "far")
                near_diag()
                last(next_kind)

            @pl.when((i >= 3) & (i < nq - 1))
            def _():
                leftover_far_then("far")

            @pl.when(i == nq - 1)
            def _():
                leftover_far_then(None)

            @pl.when(i == 1)
            def _():
                consume(cur, cur, 0)
                produce(cur, q_off, tk, "diag")
                last("far")
        else:
            @pl.when(i >= 2)
            def _():
                near_diag()
                last("far")

            @pl.when(i == 0)
            def _():
                last("near", has_prev=False)

    def q_pair(ii, carry):
        q_tile(2 * ii, 0, False)
        q_tile(2 * ii + 1, 1, True)
        return carry

    lax.fori_loop(0, nq // 2, q_pair, 0)
    finalize((nq - 1) % 2, (nq - 1) * tq)


def _attention(qk, qk_meta, vt, v_meta, gates, rel_bias, lam_q1, lam_k1, lam_q2, lam_k2, subln_g):
    _, batch, _, seq, _ = qk.shape
    tq, tk = ATTN_TQ, ATTN_TK
    smem = pl.BlockSpec(memory_space=pltpu.SMEM)
    row64 = pl.BlockSpec((1, HEAD_DIM), lambda h, b: (0, 0))
    qk_spec = lambda which: pl.BlockSpec((1, 1, 1, seq, V_DIM), lambda h, b: (which, b, h, 0, 0))
    meta_spec = lambda which: pl.BlockSpec((1, 1, N_META, V_DIM), lambda h, b: (which, h, 0, 0))

    def kernel(rb, lq1, lk1, lq2, lk2, q_ref, k_ref, vt_ref, km_ref, vm_ref, *rest):
        _attn_kernel(rb, lq1, lk1, lq2, lk2, q_ref.at[0], k_ref.at[0], vt_ref,
                     km_ref.at[0], vm_ref.at[0], *rest)

    return pl.pallas_call(
        kernel,
        grid=(N_HEADS, batch),
        in_specs=[smem, row64, row64, row64, row64,
                  qk_spec(0), qk_spec(1),
                  pl.BlockSpec((1, 1, V_DIM + SUM_ROWS, seq), lambda h, b: (b, h, 0, 0)),
                  meta_spec(1), meta_spec(0),
                  pl.BlockSpec((1, seq, V_DIM), lambda h, b: (0, b, h)),
                  pl.BlockSpec((1, V_DIM), lambda h, b: (0, 0))],
        out_specs=pl.BlockSpec((seq, V_DIM), lambda h, b: (b, h)),
        out_shape=jax.ShapeDtypeStruct((batch * seq, D_ATTN), BF16),
        scratch_shapes=[pltpu.VMEM((V_DIM + SUM_ROWS, V_DIM), BF16),
                        pltpu.VMEM((tk, tq), F32),
                        pltpu.VMEM((2, N_META, tq), F32),
                        pltpu.VMEM((MAX_DISTANCE, MAX_DISTANCE), F32),
                        pltpu.VMEM((2, 2, tk, tq), F32),
                        pltpu.VMEM((2, 2, N_META, tq), F32),
                        pltpu.VMEM((2, 2, 1, tq), F32),
                        pltpu.VMEM((2, 2, 1, tq), F32),
                        pltpu.VMEM((2, 2, V_DIM + SUM_ROWS, tq), F32)],
        compiler_params=pltpu.CompilerParams(
            dimension_semantics=("arbitrary", "arbitrary"),
            vmem_limit_bytes=VMEM_LIMIT_ATTN),
        name="diff_attention",
    )(rel_bias, lam_q1, lam_k1, lam_q2, lam_k2, qk, qk, vt, qk_meta, v_meta, gates, subln_g)


def _scan_block(a, b):
    n = a.shape[0]
    row = lax.broadcasted_iota(jnp.int32, a.shape, 0)
    s = 1
    while s < n:
        keep = row >= s
        a_sh = jnp.where(keep, pltpu.roll(a, s, 0), 1.0)
        b_sh = jnp.where(keep, pltpu.roll(b, s, 0), 0.0)
        b = b + a * b_sh
        a = a * a_sh
        s *= 2
    return a, b


def _gate_lru_kernel(h_ref, w_ref, u_ref, um_ref, cw_ref, cb_ref, wa_ref, wx_ref, ba_ref, bx_ref,
                     lam_ref, o_ref, rec_ref, wb_ref, wg_ref, ubuf, tail_ref, st_ref, u_p, h_p,
                     *, tiles_per_batch):
    i = pl.program_id(1)
    n = LRU_ROWS
    hist = 8
    seg = n // 8
    n_blk = u_ref.shape[1] // LRU_BLOCK
    n_out = D_ATTN // n_blk

    x = -lam_ref[...]
    softplus = jnp.maximum(x, 0.0) + jnp.log1p(jnp.exp(-jnp.abs(x)))

    @pl.when(i == 0)
    def _():
        wb_ref[...] = w_ref[...].astype(BF16)
        for blk in range(n_blk):
            wg_ref[blk, :, pl.ds(0, LRU_BLOCK)] = wa_ref[blk].astype(BF16)
            wg_ref[blk, :, pl.ds(LRU_BLOCK, LRU_BLOCK)] = wx_ref[blk].astype(BF16)

    def gate_inputs(rows, cols, blk):
        cw = cw_ref[:, cols]
        uc = (cw[3:4] * ubuf[pl.ds(hist, rows), cols] + cw[2:3] * ubuf[pl.ds(hist - 1, rows), cols]
              + cw[1:2] * ubuf[pl.ds(hist - 2, rows), cols] + cw[0:1] * ubuf[pl.ds(hist - 3, rows), cols]
              + cb_ref[:, cols])
        return uc, _dot(uc.astype(BF16), wg_ref[blk])

    def gates(uc, pre, cols, first):
        r = jax.nn.sigmoid(pre[:, :LRU_BLOCK] + ba_ref[:, cols])
        gi = jax.nn.sigmoid(pre[:, LRU_BLOCK:] + bx_ref[:, cols])
        log_a = -LRU_C * r * softplus[:, cols]
        a = jnp.exp(log_a)
        v = jnp.tanh(-log_a) * (a * a + 1.0)
        mult = jnp.where(v > 0.0, v * lax.rsqrt(v), 0.0)
        if first:
            row = lax.broadcasted_iota(jnp.int32, mult.shape, 0)
            mult = jnp.where(row == 0, 1.0, mult)
        return a, mult * gi * uc

    @pl.when(i % tiles_per_batch == 0)
    def _():
        ubuf[pl.ds(0, hist), :] = jnp.zeros((hist, ubuf.shape[1]), F32)
        ubuf[pl.ds(hist, N_META), :] = um_ref[...]
        for blk in range(n_blk):
            cols = slice(blk * LRU_BLOCK, (blk + 1) * LRU_BLOCK)
            a, b = gates(*gate_inputs(N_META, cols, blk), cols, True)
            _, hm = _scan_block(a, b)
            st_ref[:, cols] = hm[N_META - 1:N_META, :]
        tail_ref[...] = um_ref[pl.ds(N_META - hist, hist), :]

    sub = lax.broadcasted_iota(jnp.int32, (8, LRU_BLOCK), 0)
    step = lambda j: pl.ds(j, 8, stride=SEG_PITCH)

    def piece(t, carry):
        r0 = pl.multiple_of(t * n, n)
        for blk in range(n_blk):
            lru_block(r0, blk)
        tail_ref[...] = u_ref[pl.ds(r0 + n - hist, hist), :]
        return carry

    def lru_block(r0, blk):
        cols = slice(blk * LRU_BLOCK, (blk + 1) * LRU_BLOCK)
        ocols = slice(blk * n_out, (blk + 1) * n_out)
        o_ref[0, pl.ds(r0, n), ocols] = _dot(h_ref[pl.ds(r0, n), :], wb_ref[:, ocols])

        for k in range(8):
            u_p[blk, pl.ds(SEG_PITCH * k, seg), :] = u_ref[pl.ds(r0 + seg * k, seg), cols]
        us = [u_p[blk, step(j), :] for j in range(seg)]
        prev = tail_ref[:, cols]

        def before(back):
            return jnp.where(sub == 0, prev[hist - back:hist - back + 1, :],
                             pltpu.roll(us[seg - back], 1, 0))

        older = {-back: before(back) for back in range(1, CONV_WIDTH)}
        u_at = lambda j: us[j] if j >= 0 else older[j]
        cw = cw_ref[:, cols]
        cb = cb_ref[:, cols]
        uc = jnp.concatenate(
            [cw[3:4] * u_at(j) + cw[2:3] * u_at(j - 1) + cw[1:2] * u_at(j - 2)
             + cw[0:1] * u_at(j - 3) + cb for j in range(seg)], axis=0)
        a, b = gates(uc, _dot(uc.astype(BF16), wg_ref[blk]), cols, False)
        a_j = lambda j: a[8 * j:8 * j + 8]
        b_j = lambda j: b[8 * j:8 * j + 8]
        a_run, h_run = a_j(0), b_j(0)
        for j in range(1, seg):
            h_run = a_j(j) * h_run + b_j(j)
            a_run = a_j(j) * a_run
        a_cum, h_cum = _scan_block(a_run, h_run)
        carry = st_ref[:, cols]
        seg_end = h_cum + a_cum * carry
        st_ref[:, cols] = seg_end[7:8, :]
        h_run = jnp.where(sub == 0, carry, pltpu.roll(seg_end, 1, 0))
        for j in range(seg):
            h_run = a_j(j) * h_run + b_j(j)
            h_p[blk, step(j), :] = h_run
        for k in range(8):
            rec_ref[pl.ds(r0 + seg * k, seg), cols] = h_p[blk, pl.ds(SEG_PITCH * k, seg), :]

    lax.fori_loop(0, h_ref.shape[0] // n, piece, 0, unroll=True)


def _gate_lru(h2d, w_in, u, u_meta, conv_w, conv_b, w_a, b_a, w_x, b_x, lru_lambda, *, seq):
    rows = h2d.shape[0]
    tm = PROJ_ROWS
    n_halves = 2
    width = D_LRU // n_halves
    n_blk = N_LRU_BLOCKS // n_halves
    gate_col_tile = lambda j: 3 + 2 * j
    half = lambda n: pl.BlockSpec((n, width), lambda j, i: (0, j))
    wspec = pl.BlockSpec((n_blk, LRU_BLOCK, LRU_BLOCK), lambda j, i: (j, 0, 0))

    def kernel(h_ref, w_ref, u_ref, um_ref, *rest):
        _gate_lru_kernel(h_ref, w_ref, u_ref.at[0], um_ref.at[0], *rest, tiles_per_batch=seq // tm)

    return pl.pallas_call(
        kernel,
        grid=(n_halves, rows // tm),
        in_specs=[pl.BlockSpec((tm, D_MODEL), lambda j, i: (i, 0)),
                  pl.BlockSpec((D_MODEL, D_ATTN), lambda j, i: (0, gate_col_tile(j))),
                  pl.BlockSpec((1, tm, width), lambda j, i: (0, i, j)),
                  pl.BlockSpec((1, N_META, width), lambda j, i: (0, 0, j)),
                  half(CONV_WIDTH), half(1), wspec, wspec, half(1), half(1), half(1)],
        out_specs=(pl.BlockSpec((1, tm, D_ATTN), lambda j, i: (j, i, 0)),
                   pl.BlockSpec((tm, width), lambda j, i: (i, j))),
        out_shape=(jax.ShapeDtypeStruct((n_halves, rows, D_ATTN), F32),
                   jax.ShapeDtypeStruct((rows, D_LRU), F32)),
        scratch_shapes=[pltpu.VMEM((D_MODEL, D_ATTN), BF16),
                        pltpu.VMEM((n_blk, LRU_BLOCK, 2 * LRU_BLOCK), BF16),
                        pltpu.VMEM((8 + N_META, width), F32),
                        pltpu.VMEM((8, width), F32),
                        pltpu.VMEM((1, width), F32),
                        pltpu.VMEM((n_blk, 8 * SEG_PITCH, LRU_BLOCK), F32),
                        pltpu.VMEM((n_blk, 8 * SEG_PITCH, LRU_BLOCK), F32)],
        compiler_params=pltpu.CompilerParams(
            dimension_semantics=("arbitrary", "arbitrary"),
            vmem_limit_bytes=VMEM_LIMIT_PROJ),
        name="proj_gates_rglru",
    )(h2d, w_in, u, u_meta, conv_w, conv_b, w_a, w_x, b_a, b_x, lru_lambda)


def _out_kernel(x_ref, att_ref, rec_ref, g_ref, w_ref, fg_ref, o_ref, rec_s):
    n_out = 2 * LRU_BLOCK
    for piece in range(x_ref.shape[0] // OUT_PIECE):
        rows = pl.ds(piece * OUT_PIECE, OUT_PIECE)
        g = g_ref[0, rows, :]
        rec_s[rows, :] = (rec_ref[rows, :] * (g * jax.nn.sigmoid(g))).astype(BF16)
        ssq = jnp.zeros((OUT_PIECE, LRU_BLOCK), F32)
        for blk in range(D_MODEL // n_out):
            ocols = slice(blk * n_out, (blk + 1) * n_out)
            z = (x_ref[rows, ocols] + _dot(att_ref[rows, :], w_ref[pl.ds(0, D_ATTN), ocols])
                 + _dot(rec_s[rows, :], w_ref[pl.ds(D_ATTN, D_LRU), ocols]))
            o_ref[rows, ocols] = z
            for part in range(n_out // LRU_BLOCK):
                zp = z[:, part * LRU_BLOCK:(part + 1) * LRU_BLOCK]
                ssq = ssq + zp * zp
        inv = lax.rsqrt(jnp.sum(ssq, axis=1, keepdims=True) * (1.0 / D_MODEL) + NORM_EPS)
        o_ref[rows, :] = (o_ref[rows, :] * inv) * fg_ref[...]


def _out_proj(x2d, att, rec, gates, w_out_bf16, final_g):
    rows = x2d.shape[0]
    tm = OUT_ROWS
    tile = lambda width: pl.BlockSpec((tm, width), lambda s: (s, 0))
    return pl.pallas_call(
        _out_kernel,
        grid=(rows // tm,),
        in_specs=[tile(D_MODEL), tile(D_ATTN), tile(D_LRU),
                  pl.BlockSpec((1, tm, D_LRU), lambda s: (1, s, 0)),
                  pl.BlockSpec((D_ATTN + D_LRU, D_MODEL), lambda s: (0, 0),
                               pipeline_mode=pl.Buffered(1)),
                  pl.BlockSpec((1, D_MODEL), lambda s: (0, 0))],
        out_specs=tile(D_MODEL),
        out_shape=jax.ShapeDtypeStruct((rows, D_MODEL), F32),
        scratch_shapes=[pltpu.VMEM((tm, D_LRU), BF16)],
        compiler_params=pltpu.CompilerParams(
            dimension_semantics=("arbitrary",),
            vmem_limit_bytes=VMEM_LIMIT_OUT),
        name="out_proj",
    )(x2d, att, rec, gates, w_out_bf16, final_g)


def kernel(x, meta_tokens, rel_bias, norm_g, w_in, conv_w, conv_b, w_a, b_a, w_x, b_x, lru_lambda,
           lam_q1, lam_k1, lam_q2, lam_k2, subln_g, w_out, final_g):
    batch, seq, _ = x.shape
    x2d = x.reshape(batch * seq, D_MODEL)
    project = functools.partial(_project, w_in=w_in[0], batch=batch, seq=seq)
    vt, v_meta, h2d, h_meta = project((x2d, meta_tokens, norm_g), col_tile0=2, n_slabs=1, mode="vt")
    qk, qk_meta = project((h2d, h_meta), col_tile0=0, n_slabs=2, mode="heads")
    u, u_meta, w_out_bf16 = project((h2d, h_meta, w_out[0]), col_tile0=4, n_slabs=1, mode="flat")
    gates, rec = _gate_lru(h2d, w_in[0], u, u_meta, conv_w[0], conv_b, w_a[0], b_a, w_x[0], b_x,
                           lru_lambda, seq=seq)
    att = _attention(qk, qk_meta, vt, v_meta, gates, rel_bias, lam_q1, lam_k1, lam_q2, lam_k2,
                     subln_g)
    out = _out_proj(x2d, att, rec, gates, w_out_bf16, final_g.reshape(1, D_MODEL))
    return out.reshape(batch, seq, D_MODEL)
```

```python
import functools
import math

import numpy as np
import jax
import jax.numpy as jnp
from jax import lax
from jax.experimental import pallas as pl
from jax.experimental.pallas import tpu as pltpu

D_MODEL = 2048
N_META = 16
D_ATTN = 1024
D_LRU = 1024
N_HEADS = 8
HEAD_DIM = 64
V_DIM = 128
N_LRU_BLOCKS = 8
LRU_BLOCK = 128
CONV_WIDTH = 4
LRU_C = 8.0
N_BUCKETS = 32
MAX_DISTANCE = 128
NORM_EPS = 1e-6
SUBLN_EPS = 1e-5
NEG_INF = -1e30
LAMBDA_INIT = 0.8 - 0.6 * math.exp(-0.3 * 0)
LOG2E = math.log2(math.e)

BF16 = jnp.bfloat16
F32 = jnp.float32

VMEM_LIMIT_PROJ = 56 * 1024 * 1024
VMEM_LIMIT_ATTN = 48 * 1024 * 1024
VMEM_LIMIT_OUT = 56 * 1024 * 1024

PROJ_ROWS = 1024
ATTN_TQ = 512
ATTN_TK = 512
SUM_ROWS = 16
OUT_ROWS = 512
OUT_PIECE = 256
LRU_ROWS = 256
SEG_PITCH = 40


def _bucket_thresholds():
    max_exact = N_BUCKETS // 2
    d = np.arange(0, 4 * MAX_DISTANCE, dtype=np.int64)
    val = (np.log(np.maximum(d, 1).astype(np.float64) / max_exact)
           / math.log(MAX_DISTANCE / max_exact) * (N_BUCKETS - max_exact))
    large = np.minimum(max_exact + np.floor(val + 1e-9).astype(np.int64), N_BUCKETS - 1)
    bucket = np.where(d < max_exact, d, large)
    frac = np.abs(val - np.round(val))
    interior = (d > max_exact) & (d < MAX_DISTANCE)
    assert frac[interior].min() > 1e-3
    assert (np.diff(bucket) >= 0).all() and bucket[MAX_DISTANCE] == N_BUCKETS - 1
    return tuple(int(np.argmax(bucket >= j)) for j in range(1, N_BUCKETS))


BUCKET_THRESHOLDS = _bucket_thresholds()


def _rms(x, g, eps):
    y = x * lax.rsqrt(jnp.mean(x * x, axis=-1, keepdims=True) + eps)
    return y * g


def _dot(a, b):
    return jnp.dot(a, b, preferred_element_type=F32)


def _dot_nt(a, b):
    return lax.dot_general(a, b, (((1,), (1,)), ((), ())), preferred_element_type=F32)


def _proj_kernel(*refs, mode):
    if mode == "vt":
        x_ref, meta_ref, g_ref, w_ref, o_ref, om_ref, h_ref, hm_ref, wb_ref = refs
    elif mode == "flat":
        h_ref, hm_ref, wo_ref, w_ref, o_ref, om_ref, wob_ref, wb_ref = refs
    else:
        h_ref, hm_ref, w_ref, o_ref, om_ref, wb_ref = refs
    j = pl.program_id(0)
    i = pl.program_id(1)
    out_scale = jnp.where(j == 0, HEAD_DIM ** -0.5 * LOG2E, 1.0).astype(F32) if mode == "heads" else 1.0
    head = lambda y, h: y[:, h * V_DIM:(h + 1) * V_DIM]

    @pl.when(i == 0)
    def _():
        wb_ref[...] = w_ref[...].astype(BF16)
        if mode == "vt":
            hm_ref[...] = _rms(meta_ref[...], g_ref[...], NORM_EPS).astype(BF16)
        ym = _dot(hm_ref[...], wb_ref[...]) * out_scale
        if mode == "flat":
            om_ref[0] = ym
        else:
            for h in range(N_HEADS):
                om_ref[0, h] = head(ym, h).astype(BF16)

    if mode == "vt":
        h_ref[...] = _rms(x_ref[...], g_ref[...], NORM_EPS).astype(BF16)
    y = _dot(h_ref[...], wb_ref[...]) * out_scale
    if mode == "flat":
        o_ref[0] = y
        wob_ref[...] = wo_ref[...].astype(BF16)
    elif mode == "heads":
        for h in range(N_HEADS):
            o_ref[0, 0, h] = head(y, h).astype(BF16)
    else:
        rows = y.shape[0]
        for h in range(N_HEADS):
            o_ref[0, h, pl.ds(0, V_DIM), :] = head(y, h).T.astype(BF16)
            o_ref[0, h, pl.ds(V_DIM, SUM_ROWS), :] = jnp.ones((SUM_ROWS, rows), BF16)


def _project(inputs, w_in, *, col_tile0, n_slabs, mode, batch, seq):
    rows = inputs[0].shape[0]
    tm = PROJ_ROWS
    n_i = rows // tm
    n_ib = seq // tm
    row_tile = pl.BlockSpec((tm, D_MODEL), lambda j, i: (i, 0))
    meta_rows = pl.BlockSpec((N_META, D_MODEL), lambda j, i: (0, 0))
    if mode == "vt":
        in_specs = [row_tile, meta_rows, pl.BlockSpec((1, D_MODEL), lambda j, i: (0, 0))]
    elif mode == "flat":
        assert n_slabs == 1
        w_out_rows = pl.BlockSpec(((D_ATTN + D_LRU) // n_i, D_MODEL), lambda j, i: (i, 0))
        in_specs = [row_tile, meta_rows, w_out_rows]
    else:
        in_specs = [row_tile, meta_rows]
    in_specs.append(pl.BlockSpec((D_MODEL, D_ATTN), lambda j, i: (0, j + col_tile0)))
    if mode == "heads":
        out_shape = (jax.ShapeDtypeStruct((n_slabs, batch, N_HEADS, seq, V_DIM), BF16),
                     jax.ShapeDtypeStruct((n_slabs, N_HEADS, N_META, V_DIM), BF16))
        out_specs = (pl.BlockSpec((1, 1, N_HEADS, tm, V_DIM),
                                  lambda j, i: (j, i // n_ib, 0, i % n_ib, 0)),
                     pl.BlockSpec((1, N_HEADS, N_META, V_DIM), lambda j, i: (j, 0, 0, 0)))
    elif mode == "vt":
        assert n_slabs == 1
        out_shape = (jax.ShapeDtypeStruct((batch, N_HEADS, V_DIM + SUM_ROWS, seq), BF16),
                     jax.ShapeDtypeStruct((n_slabs, N_HEADS, N_META, V_DIM), BF16),
                     jax.ShapeDtypeStruct((rows, D_MODEL), BF16),
                     jax.ShapeDtypeStruct((N_META, D_MODEL), BF16))
        out_specs = (pl.BlockSpec((1, N_HEADS, V_DIM + SUM_ROWS, tm),
                                  lambda j, i: (i // n_ib, 0, 0, i % n_ib)),
                     pl.BlockSpec((1, N_HEADS, N_META, V_DIM), lambda j, i: (j, 0, 0, 0)),
                     row_tile, meta_rows)
    else:
        out_shape = (jax.ShapeDtypeStruct((n_slabs, rows, D_ATTN), F32),
                     jax.ShapeDtypeStruct((n_slabs, N_META, D_ATTN), F32),
                     jax.ShapeDtypeStruct((D_ATTN + D_LRU, D_MODEL), BF16))
        out_specs = (pl.BlockSpec((1, tm, D_ATTN), lambda j, i: (j, i, 0)),
                     pl.BlockSpec((1, N_META, D_ATTN), lambda j, i: (j, 0, 0)),
                     w_out_rows)
    return pl.pallas_call(
        functools.partial(_proj_kernel, mode=mode),
        grid=(n_slabs, n_i),
        in_specs=in_specs,
        out_specs=out_specs,
        out_shape=out_shape,
        scratch_shapes=[pltpu.VMEM((D_MODEL, D_ATTN), BF16)],
        compiler_params=pltpu.CompilerParams(
            dimension_semantics=("arbitrary", "arbitrary"),
            vmem_limit_bytes=VMEM_LIMIT_PROJ),
        name={"heads": "proj_qk", "vt": "proj_v", "flat": "proj_u"}[mode],
    )(*inputs, w_in)


def _toeplitz_bias(dist, bucket_bias):
    b = jnp.full(dist.shape, bucket_bias[0], F32)
    for j, thr in enumerate(BUCKET_THRESHOLDS, start=1):
        b = jnp.where(dist >= thr, bucket_bias[j], b)
    return b


def _fill_bias(ref, lead, n_rows, n_cols, d0, bucket_bias):
    sub = lax.broadcasted_iota(jnp.int32, (8, V_DIM), 0)
    lane = lax.broadcasted_iota(jnp.int32, (8, V_DIM), 1)
    zeros = jnp.zeros((8, V_DIM), F32)
    masked = jnp.full((8, V_DIM), NEG_INF, F32)
    cache = {}
    for a8 in range(n_rows // 8):
        for b in range(n_cols // V_DIM):
            off = d0 + V_DIM * b - 8 * a8
            if off + V_DIM - 1 < 0:
                tile = masked
            elif off - 7 >= MAX_DISTANCE:
                tile = zeros
            else:
                if off not in cache:
                    d = off + lane - sub
                    cache[off] = jnp.where(d >= 0, _toeplitz_bias(d, bucket_bias), NEG_INF)
                tile = cache[off]
            ref[(*lead, pl.ds(8 * a8, 8), pl.ds(V_DIM * b, V_DIM))] = tile


def _attn_kernel(rb_ref, lq1_ref, lk1_ref, lq2_ref, lk2_ref, q_ref, k_ref, vt_ref, km_ref, vm_ref,
                 g_ref, sg_ref, o_ref, vmt_ref, bd_ref, bm_ref, bn_ref, sbuf, sc_ref, mx_ref,
                 m_ref, acc_ref):
    h = pl.program_id(0)
    tq, tk = ATTN_TQ, ATTN_TK
    seq = k_ref.shape[2]
    nq = seq // tq
    assert nq % 2 == 0 and nq >= 4

    far = rb_ref[N_BUCKETS - 1, h]

    @pl.when(pl.program_id(1) == 0)
    def _():
        bucket_bias = [(rb_ref[j, h] - far) * LOG2E for j in range(N_BUCKETS)]
        _fill_bias(bd_ref, (), tk, tq, 0, bucket_bias)
        _fill_bias(bn_ref, (), MAX_DISTANCE, MAX_DISTANCE, MAX_DISTANCE, bucket_bias)
        _fill_bias(bm_ref, (1,), N_META, tq, N_META, bucket_bias)
        bm_ref[0] = jnp.zeros((N_META, tq), F32)
        padded = jnp.concatenate(
            [vm_ref[0].astype(F32), jnp.zeros((V_DIM - N_META, V_DIM), F32)], axis=0)
        vmt_ref[pl.ds(0, V_DIM), :] = padded.T.astype(BF16)
        vmt_ref[pl.ds(V_DIM, SUM_ROWS), :] = jnp.ones((SUM_ROWS, V_DIM), BF16)

    lane = lax.broadcasted_iota(jnp.int32, (tq, V_DIM), 1)

    lam = (jnp.exp(jnp.sum(lq1_ref[...] * lk1_ref[...], keepdims=True))
           - jnp.exp(jnp.sum(lq2_ref[...] * lk2_ref[...], keepdims=True))
           + LAMBDA_INIT)

    def init_stats(st):
        m_ref[st] = jnp.full(m_ref.shape[1:], NEG_INF, F32)
        acc_ref[st] = jnp.zeros(acc_ref.shape[1:], F32)

    hk, hq = tk // 2, tq // 2

    def produce(buf, q_off, k_off, kind):
        q = q_ref[0, 0, pl.ds(pl.multiple_of(q_off, tq), tq), :]
        zero = jnp.zeros_like(q)
        k_off = pl.multiple_of(k_off, tk)
        k_t = k_ref[0, 0, pl.ds(k_off, tk), :]
        for a in range(2):
            qa = jnp.where((lane < HEAD_DIM) if a == 0 else (lane >= HEAD_DIM), q, zero)
            if kind == "diag":
                top = _dot_nt(k_t[:hk], qa) + bd_ref[pl.ds(0, hk), :]
                low = _dot_nt(k_t[hk:], qa[hq:]) + bd_ref[pl.ds(hk, hk), pl.ds(hq, hq)]
                sc = _dot_nt(km_ref[0], qa) + bm_ref[jnp.where(k_off == 0, 1, 0)]
                sbuf[buf, a, pl.ds(0, hk), :] = top
                sbuf[buf, a, pl.ds(hk, hk), pl.ds(hq, hq)] = low
                sc_ref[buf, a] = sc
                mx = jnp.maximum(jnp.max(top, axis=0, keepdims=True),
                                 jnp.max(sc, axis=0, keepdims=True))
                mx_low = jnp.max(low, axis=0, keepdims=True)
                mx_ref[buf, a] = jnp.concatenate(
                    [mx[:, :hq], jnp.maximum(mx[:, hq:], mx_low)], axis=1)
                continue
            s = _dot_nt(k_t, qa)
            if kind == "near":
                band = tk - MAX_DISTANCE
                corner = s[band:, :MAX_DISTANCE] + bn_ref[...]
                sbuf[buf, a, pl.ds(0, band), :] = s[:band]
                sbuf[buf, a, pl.ds(band, MAX_DISTANCE), pl.ds(0, MAX_DISTANCE)] = corner
                sbuf[buf, a, pl.ds(band, MAX_DISTANCE), pl.ds(MAX_DISTANCE, tq - MAX_DISTANCE)] = (
                    s[band:, MAX_DISTANCE:])
                mx = jnp.max(s[:band], axis=0, keepdims=True)
                mx_ref[buf, a] = jnp.concatenate(
                    [jnp.maximum(mx[:, :MAX_DISTANCE], jnp.max(corner, axis=0, keepdims=True)),
                     jnp.maximum(mx[:, MAX_DISTANCE:],
                                 jnp.max(s[band:, MAX_DISTANCE:], axis=0, keepdims=True))], axis=1)
                continue
            sbuf[buf, a] = s
            mx_ref[buf, a] = jnp.max(s, axis=0, keepdims=True)

    def consume(buf, st, k_off, diag=False):
        k_off = pl.multiple_of(k_off, tk)
        v_t = vt_ref[0, 0, :, pl.ds(k_off, tk)]
        for a in range(2):
            m_old = m_ref[st, a]
            m_new = jnp.maximum(m_old, mx_ref[buf, a])
            alpha = jnp.exp2(m_old - m_new)
            if diag:
                top = jnp.exp2(sbuf[buf, a, pl.ds(0, hk), :] - m_new)
                low = jnp.exp2(sbuf[buf, a, pl.ds(hk, hk), pl.ds(hq, hq)] - m_new[:, hq:])
                pc = jnp.exp2(sc_ref[buf, a] - m_new)
                pv = (_dot(v_t[:, :hk], top.astype(BF16))
                      + _dot(vmt_ref[...][:, :N_META], pc.astype(BF16)))
                pv_low = _dot(v_t[:, hk:], low.astype(BF16))
                pv = jnp.concatenate([pv[:, :hq], pv[:, hq:] + pv_low], axis=1)
            else:
                p = jnp.exp2(sbuf[buf, a] - m_new)
                pv = _dot(v_t, p.astype(BF16))
            acc_ref[st, a] = alpha * acc_ref[st, a] + pv
            m_ref[st, a] = m_new

    def finalize(st, q_off):
        q_rows = pl.ds(pl.multiple_of(q_off, tq), tq)
        heads = [acc_ref[st, a, pl.ds(0, V_DIM), :] / acc_ref[st, a, pl.ds(V_DIM, 1), :]
                 for a in range(2)]
        out_t = heads[0] - lam * heads[1]
        inv = lax.rsqrt(jnp.mean(out_t * out_t, axis=0, keepdims=True) + SUBLN_EPS)
        att = ((out_t * inv).T * sg_ref[...]) * (1.0 - LAMBDA_INIT)
        g = g_ref[0, q_rows, :]
        o_ref[q_rows, :] = (att * (g * jax.nn.sigmoid(g))).astype(o_ref.dtype)

    init_stats(0)
    produce(0, 0, 0, "diag")

    def q_tile(i, cur, odd):
        oth = 1 - cur
        q_off = i * tq

        def far_pair(tt, c):
            t = 2 * tt
            produce(oth, q_off, (t + 1) * tk, "far")
            consume(cur, cur, t * tk)
            produce(cur, q_off, (t + 2) * tk, "far")
            consume(oth, cur, (t + 1) * tk)
            return c

        n_far = jnp.maximum(i - 2, 0)
        lax.fori_loop(0, n_far // 2, far_pair, 0)

        def near_diag():
            produce(oth, q_off, (i - 1) * tk, "near")
            consume(cur, cur, (i - 2) * tk)
            produce(cur, q_off, i * tk, "diag")
            consume(oth, cur, (i - 1) * tk)

        def last(next_kind, has_prev=True):
            if has_prev:
                finalize(oth, (i - 1) * tq)
            if next_kind is not None:
                init_stats(oth)
                produce(oth, (i + 1) * tq, 0, next_kind)
            consume(cur, cur, i * tk, diag=True)

        if odd:
            def leftover_far_then(next_kind):
                consume(cur, cur, (n_far - 1) * tk)
                produce(cur, q_off, n_far * tk, "far")
                near_diag()
                last(next_kind)

            @pl.when((i >= 3) & (i < nq - 1))
            def _():
                leftover_far_then("far")

            @pl.when(i == nq - 1)
            def _():
                leftover_far_then(None)

            @pl.when(i == 1)
            def _():
                consume(cur, cur, 0)
                produce(cur, q_off, tk, "diag")
                last("far")
        else:
            @pl.when(i >= 2)
            def _():
                near_diag()
                last("far")

            @pl.when(i == 0)
            def _():
                last("near", has_prev=False)

    def q_pair(ii, carry):
        q_tile(2 * ii, 0, False)
        q_tile(2 * ii + 1, 1, True)
        return carry

    lax.fori_loop(0, nq // 2, q_pair, 0)
    finalize((nq - 1) % 2, (nq - 1) * tq)


def _attention(qk, qk_meta, vt, v_meta, gates, rel_bias, lam_q1, lam_k1, lam_q2, lam_k2, subln_g):
    _, batch, _, seq, _ = qk.shape
    tq, tk = ATTN_TQ, ATTN_TK
    smem = pl.BlockSpec(memory_space=pltpu.SMEM)
    row64 = pl.BlockSpec((1, HEAD_DIM), lambda h, b: (0, 0))
    qk_spec = lambda which: pl.BlockSpec((1, 1, 1, seq, V_DIM), lambda h, b: (which, b, h, 0, 0))
    meta_spec = lambda which: pl.BlockSpec((1, 1, N_META, V_DIM), lambda h, b: (which, h, 0, 0))

    def kernel(rb, lq1, lk1, lq2, lk2, q_ref, k_ref, vt_ref, km_ref, vm_ref, *rest):
        _attn_kernel(rb, lq1, lk1, lq2, lk2, q_ref.at[0], k_ref.at[0], vt_ref,
                     km_ref.at[0], vm_ref.at[0], *rest)

    return pl.pallas_call(
        kernel,
        grid=(N_HEADS, batch),
        in_specs=[smem, row64, row64, row64, row64,
                  qk_spec(0), qk_spec(1),
                  pl.BlockSpec((1, 1, V_DIM + SUM_ROWS, seq), lambda h, b: (b, h, 0, 0)),
                  meta_spec(1), meta_spec(0),
                  pl.BlockSpec((1, seq, V_DIM), lambda h, b: (0, b, h)),
                  pl.BlockSpec((1, V_DIM), lambda h, b: (0, 0))],
        out_specs=pl.BlockSpec((seq, V_DIM), lambda h, b: (b, h)),
        out_shape=jax.ShapeDtypeStruct((batch * seq, D_ATTN), BF16),
        scratch_shapes=[pltpu.VMEM((V_DIM + SUM_ROWS, V_DIM), BF16),
                        pltpu.VMEM((tk, tq), F32),
                        pltpu.VMEM((2, N_META, tq), F32),
                        pltpu.VMEM((MAX_DISTANCE, MAX_DISTANCE), F32),
                        pltpu.VMEM((2, 2, tk, tq), F32),
                        pltpu.VMEM((2, 2, N_META, tq), F32),
                        pltpu.VMEM((2, 2, 1, tq), F32),
                        pltpu.VMEM((2, 2, 1, tq), F32),
                        pltpu.VMEM((2, 2, V_DIM + SUM_ROWS, tq), F32)],
        compiler_params=pltpu.CompilerParams(
            dimension_semantics=("arbitrary", "arbitrary"),
            vmem_limit_bytes=VMEM_LIMIT_ATTN),
        name="diff_attention",
    )(rel_bias, lam_q1, lam_k1, lam_q2, lam_k2, qk, qk, vt, qk_meta, v_meta, gates, subln_g)


def _scan_block(a, b):
    n = a.shape[0]
    row = lax.broadcasted_iota(jnp.int32, a.shape, 0)
    s = 1
    while s < n:
        keep = row >= s
        a_sh = jnp.where(keep, pltpu.roll(a, s, 0), 1.0)
        b_sh = jnp.where(keep, pltpu.roll(b, s, 0), 0.0)
        b = b + a * b_sh
        a = a * a_sh
        s *= 2
    return a, b


def _gate_lru_kernel(h_ref, w_ref, u_ref, um_ref, cw_ref, cb_ref, wa_ref, wx_ref, ba_ref, bx_ref,
                     lam_ref, o_ref, rec_ref, wb_ref, wg_ref, ubuf, tail_ref, st_ref, u_p, h_p,
                     *, tiles_per_batch):
    i = pl.program_id(1)
    n = LRU_ROWS
    hist = 8
    seg = n // 8
    n_blk = u_ref.shape[1] // LRU_BLOCK
    n_out = D_ATTN // n_blk

    x = -lam_ref[...]
    softplus = jnp.maximum(x, 0.0) + jnp.log1p(jnp.exp(-jnp.abs(x)))

    @pl.when(i == 0)
    def _():
        wb_ref[...] = w_ref[...].astype(BF16)
        for blk in range(n_blk):
            wg_ref[blk, :, pl.ds(0, LRU_BLOCK)] = wa_ref[blk].astype(BF16)
            wg_ref[blk, :, pl.ds(LRU_BLOCK, LRU_BLOCK)] = wx_ref[blk].astype(BF16)

    def gate_inputs(rows, cols, blk):
        cw = cw_ref[:, cols]
        uc = (cw[3:4] * ubuf[pl.ds(hist, rows), cols] + cw[2:3] * ubuf[pl.ds(hist - 1, rows), cols]
              + cw[1:2] * ubuf[pl.ds(hist - 2, rows), cols] + cw[0:1] * ubuf[pl.ds(hist - 3, rows), cols]
              + cb_ref[:, cols])
        return uc, _dot(uc.astype(BF16), wg_ref[blk])

    def gates(uc, pre, cols, first):
        r = jax.nn.sigmoid(pre[:, :LRU_BLOCK] + ba_ref[:, cols])
        gi = jax.nn.sigmoid(pre[:, LRU_BLOCK:] + bx_ref[:, cols])
        log_a = -LRU_C * r * softplus[:, cols]
        a = jnp.exp(log_a)
        v = jnp.tanh(-log_a) * (a * a + 1.0)
        mult = jnp.where(v > 0.0, v * lax.rsqrt(v), 0.0)
        if first:
            row = lax.broadcasted_iota(jnp.int32, mult.shape, 0)
            mult = jnp.where(row == 0, 1.0, mult)
        return a, mult * gi * uc

    @pl.when(i % tiles_per_batch == 0)
    def _():
        ubuf[pl.ds(0, hist), :] = jnp.zeros((hist, ubuf.shape[1]), F32)
        ubuf[pl.ds(hist, N_META), :] = um_ref[...]
        for blk in range(n_blk):
            cols = slice(blk * LRU_BLOCK, (blk + 1) * LRU_BLOCK)
            a, b = gates(*gate_inputs(N_META, cols, blk), cols, True)
            _, hm = _scan_block(a, b)
            st_ref[:, cols] = hm[N_META - 1:N_META, :]
        tail_ref[...] = um_ref[pl.ds(N_META - hist, hist), :]

    sub = lax.broadcasted_iota(jnp.int32, (8, LRU_BLOCK), 0)
    step = lambda j: pl.ds(j, 8, stride=SEG_PITCH)

    def piece(t, carry):
        r0 = pl.multiple_of(t * n, n)
        for blk in range(n_blk):
            lru_block(r0, blk)
        tail_ref[...] = u_ref[pl.ds(r0 + n - hist, hist), :]
        return carry

    def lru_block(r0, blk):
        cols = slice(blk * LRU_BLOCK, (blk + 1) * LRU_BLOCK)
        ocols = slice(blk * n_out, (blk + 1) * n_out)
        o_ref[0, pl.ds(r0, n), ocols] = _dot(h_ref[pl.ds(r0, n), :], wb_ref[:, ocols])

        for k in range(8):
            u_p[blk, pl.ds(SEG_PITCH * k, seg), :] = u_ref[pl.ds(r0 + seg * k, seg), cols]
        us = [u_p[blk, step(j), :] for j in range(seg)]
        prev = tail_ref[:, cols]

        def before(back):
            return jnp.where(sub == 0, prev[hist - back:hist - back + 1, :],
                             pltpu.roll(us[seg - back], 1, 0))

        older = {-back: before(back) for back in range(1, CONV_WIDTH)}
        u_at = lambda j: us[j] if j >= 0 else older[j]
        cw = cw_ref[:, cols]
        cb = cb_ref[:, cols]
        uc = jnp.concatenate(
            [cw[3:4] * u_at(j) + cw[2:3] * u_at(j - 1) + cw[1:2] * u_at(j - 2)
             + cw[0:1] * u_at(j - 3) + cb for j in range(seg)], axis=0)
        a, b = gates(uc, _dot(uc.astype(BF16), wg_ref[blk]), cols, False)
        a_j = lambda j: a[8 * j:8 * j + 8]
        b_j = lambda j: b[8 * j:8 * j + 8]
        a_run, h_run = a_j(0), b_j(0)
        for j in range(1, seg):
            h_run = a_j(j) * h_run + b_j(j)
            a_run = a_j(j) * a_run
        a_cum, h_cum = _scan_block(a_run, h_run)
        carry = st_ref[:, cols]
        seg_end = h_cum + a_cum * carry
        st_ref[:, cols] = seg_end[7:8, :]
        h_run = jnp.where(sub == 0, carry, pltpu.roll(seg_end, 1, 0))
        for j in range(seg):
            h_run = a_j(j) * h_run + b_j(j)
            h_p[blk, step(j), :] = h_run
        for k in range(8):
            rec_ref[pl.ds(r0 + seg * k, seg), cols] = h_p[blk, pl.ds(SEG_PITCH * k, seg), :]

    lax.fori_loop(0, h_ref.shape[0] // n, piece, 0, unroll=True)


def _gate_lru(h2d, w_in, u, u_meta, conv_w, conv_b, w_a, b_a, w_x, b_x, lru_lambda, *, seq):
    rows = h2d.shape[0]
    tm = PROJ_ROWS
    n_halves = 2
    width = D_LRU // n_halves
    n_blk = N_LRU_BLOCKS // n_halves
    gate_col_tile = lambda j: 3 + 2 * j
    half = lambda n: pl.BlockSpec((n, width), lambda j, i: (0, j))
    wspec = pl.BlockSpec((n_blk, LRU_BLOCK, LRU_BLOCK), lambda j, i: (j, 0, 0))

    def kernel(h_ref, w_ref, u_ref, um_ref, *rest):
        _gate_lru_kernel(h_ref, w_ref, u_ref.at[0], um_ref.at[0], *rest, tiles_per_batch=seq // tm)

    return pl.pallas_call(
        kernel,
        grid=(n_halves, rows // tm),
        in_specs=[pl.BlockSpec((tm, D_MODEL), lambda j, i: (i, 0)),
                  pl.BlockSpec((D_MODEL, D_ATTN), lambda j, i: (0, gate_col_tile(j))),
                  pl.BlockSpec((1, tm, width), lambda j, i: (0, i, j)),
                  pl.BlockSpec((1, N_META, width), lambda j, i: (0, 0, j)),
                  half(CONV_WIDTH), half(1), wspec, wspec, half(1), half(1), half(1)],
        out_specs=(pl.BlockSpec((1, tm, D_ATTN), lambda j, i: (j, i, 0)),
                   pl.BlockSpec((tm, width), lambda j, i: (i, j))),
        out_shape=(jax.ShapeDtypeStruct((n_halves, rows, D_ATTN), F32),
                   jax.ShapeDtypeStruct((rows, D_LRU), F32)),
        scratch_shapes=[pltpu.VMEM((D_MODEL, D_ATTN), BF16),
                        pltpu.VMEM((n_blk, LRU_BLOCK, 2 * LRU_BLOCK), BF16),
                        pltpu.VMEM((8 + N_META, width), F32),
                        pltpu.VMEM((8, width), F32),
                        pltpu.VMEM((1, width), F32),
                        pltpu.VMEM((n_blk, 8 * SEG_PITCH, LRU_BLOCK), F32),
                        pltpu.VMEM((n_blk, 8 * SEG_PITCH, LRU_BLOCK), F32)],
        compiler_params=pltpu.CompilerParams(
            dimension_semantics=("arbitrary", "arbitrary"),
            vmem_limit_bytes=VMEM_LIMIT_PROJ),
        name="proj_gates_rglru",
    )(h2d, w_in, u, u_meta, conv_w, conv_b, w_a, w_x, b_a, b_x, lru_lambda)


def _out_kernel(x_ref, att_ref, rec_ref, g_ref, w_ref, fg_ref, o_ref, rec_s):
    n_out = 2 * LRU_BLOCK
    for piece in range(x_ref.shape[0] // OUT_PIECE):
        rows = pl.ds(piece * OUT_PIECE, OUT_PIECE)
        g = g_ref[0, rows, :]
        rec_s[rows, :] = (rec_ref[rows, :] * (g * jax.nn.sigmoid(g))).astype(BF16)
        ssq = jnp.zeros((OUT_PIECE, LRU_BLOCK), F32)
        for blk in range(D_MODEL // n_out):
            ocols = slice(blk * n_out, (blk + 1) * n_out)
            z = (x_ref[rows, ocols] + _dot(att_ref[rows, :], w_ref[pl.ds(0, D_ATTN), ocols])
                 + _dot(rec_s[rows, :], w_ref[pl.ds(D_ATTN, D_LRU), ocols]))
            o_ref[rows, ocols] = z
            for part in range(n_out // LRU_BLOCK):
                zp = z[:, part * LRU_BLOCK:(part + 1) * LRU_BLOCK]
                ssq = ssq + zp * zp
        inv = lax.rsqrt(jnp.sum(ssq, axis=1, keepdims=True) * (1.0 / D_MODEL) + NORM_EPS)
        o_ref[rows, :] = (o_ref[rows, :] * inv) * fg_ref[...]


def _out_proj(x2d, att, rec, gates, w_out_bf16, final_g):
    rows = x2d.shape[0]
    tm = OUT_ROWS
    tile = lambda width: pl.BlockSpec((tm, width), lambda s: (s, 0))
    return pl.pallas_call(
        _out_kernel,
        grid=(rows // tm,),
        in_specs=[tile(D_MODEL), tile(D_ATTN), tile(D_LRU),
                  pl.BlockSpec((1, tm, D_LRU), lambda s: (1, s, 0)),
                  pl.BlockSpec((D_ATTN + D_LRU, D_MODEL), lambda s: (0, 0),
                               pipeline_mode=pl.Buffered(1)),
                  pl.BlockSpec((1, D_MODEL), lambda s: (0, 0))],
        out_specs=tile(D_MODEL),
        out_shape=jax.ShapeDtypeStruct((rows, D_MODEL), F32),
        scratch_shapes=[pltpu.VMEM((tm, D_LRU), BF16)],
        compiler_params=pltpu.CompilerParams(
            dimension_semantics=("arbitrary",),
            vmem_limit_bytes=VMEM_LIMIT_OUT),
        name="out_proj",
    )(x2d, att, rec, gates, w_out_bf16, final_g)


def kernel(x, meta_tokens, rel_bias, norm_g, w_in, conv_w, conv_b, w_a, b_a, w_x, b_x, lru_lambda,
           lam_q1, lam_k1, lam_q2, lam_k2, subln_g, w_out, final_g):
    batch, seq, _ = x.shape
    x2d = x.reshape(batch * seq, D_MODEL)
    project = functools.partial(_project, w_in=w_in[0], batch=batch, seq=seq)
    vt, v_meta, h2d, h_meta = project((x2d, meta_tokens, norm_g), col_tile0=2, n_slabs=1, mode="vt")
    qk, qk_meta = project((h2d, h_meta), col_tile0=0, n_slabs=2, mode="heads")
    u, u_meta, w_out_bf16 = project((h2d, h_meta, w_out[0]), col_tile0=4, n_slabs=1, mode="flat")
    gates, rec = _gate_lru(h2d, w_in[0], u, u_meta, conv_w[0], conv_b, w_a[0], b_a, w_x[0], b_x,
                           lru_lambda, seq=seq)
    att = _attention(qk, qk_meta, vt, v_meta, gates, rel_bias, lam_q1, lam_k1, lam_q2, lam_k2,
                     subln_g)
    out = _out_proj(x2d, att, rec, gates, w_out_bf16, final_g.reshape(1, D_MODEL))
    return out.reshape(batch, seq, D_MODEL)
```

```python
import functools
import math

import numpy as np
import jax
import jax.numpy as jnp
from jax import lax
from jax.experimental import pallas as pl
from jax.experimental.pallas import tpu as pltpu

D_MODEL = 2048
N_META = 16
D_ATTN = 1024
D_LRU = 1024
N_HEADS = 8
HEAD_DIM = 64
V_DIM = 128
N_LRU_BLOCKS = 8
LRU_BLOCK = 128
CONV_WIDTH = 4
LRU_C = 8.0
N_BUCKETS = 32
MAX_DISTANCE = 128
NORM_EPS = 1e-6
SUBLN_EPS = 1e-5
NEG_INF = -1e30
LAMBDA_INIT = 0.8 - 0.6 * math.exp(-0.3 * 0)
LOG2E = math.log2(math.e)

BF16 = jnp.bfloat16
F32 = jnp.float32

VMEM_LIMIT_PROJ = 56 * 1024 * 1024
VMEM_LIMIT_ATTN = 48 * 1024 * 1024
VMEM_LIMIT_OUT = 56 * 1024 * 1024

PROJ_ROWS = 1024
ATTN_TQ = 512
ATTN_TK = 512
SUM_ROWS = 16
OUT_ROWS = 512
OUT_PIECE = 256
LRU_ROWS = 256
SEG_PITCH = 40


def _bucket_thresholds():
    max_exact = N_BUCKETS // 2
    d = np.arange(0, 4 * MAX_DISTANCE, dtype=np.int64)
    val = (np.log(np.maximum(d, 1).astype(np.float64) / max_exact)
           / math.log(MAX_DISTANCE / max_exact) * (N_BUCKETS - max_exact))
    large = np.minimum(max_exact + np.floor(val + 1e-9).astype(np.int64), N_BUCKETS - 1)
    bucket = np.where(d < max_exact, d, large)
    frac = np.abs(val - np.round(val))
    interior = (d > max_exact) & (d < MAX_DISTANCE)
    assert frac[interior].min() > 1e-3
    assert (np.diff(bucket) >= 0).all() and bucket[MAX_DISTANCE] == N_BUCKETS - 1
    return tuple(int(np.argmax(bucket >= j)) for j in range(1, N_BUCKETS))


BUCKET_THRESHOLDS = _bucket_thresholds()


def _rms(x, g, eps):
    y = x * lax.rsqrt(jnp.mean(x * x, axis=-1, keepdims=True) + eps)
    return y * g


def _dot(a, b):
    return jnp.dot(a, b, preferred_element_type=F32)


def _dot_nt(a, b):
    return lax.dot_general(a, b, (((1,), (1,)), ((), ())), preferred_element_type=F32)


def _proj_kernel(*refs, mode):
    if mode == "vt":
        x_ref, meta_ref, g_ref, w_ref, o_ref, om_ref, h_ref, hm_ref, wb_ref = refs
    elif mode == "flat":
        h_ref, hm_ref, wo_ref, wb_ref, o_ref, om_ref, wob_ref = refs
    else:
        h_ref, hm_ref, wi_ref, w_ref, o_ref, om_ref, wib_ref, wb_ref = refs
    j = pl.program_id(0)
    i = pl.program_id(1)
    out_scale = jnp.where(j == 0, HEAD_DIM ** -0.5 * LOG2E, 1.0).astype(F32) if mode == "heads" else 1.0
    head = lambda y, h: y[:, h * V_DIM:(h + 1) * V_DIM]

    @pl.when(i == 0)
    def _():
        if mode != "flat":
            wb_ref[...] = w_ref[...].astype(BF16)
        if mode == "vt":
            hm_ref[...] = _rms(meta_ref[...], g_ref[...], NORM_EPS).astype(BF16)
        ym = _dot(hm_ref[...], wb_ref[...]) * out_scale
        if mode == "flat":
            om_ref[0] = ym
        else:
            for h in range(N_HEADS):
                om_ref[0, h] = head(ym, h).astype(BF16)

    if mode == "vt":
        h_ref[...] = _rms(x_ref[...], g_ref[...], NORM_EPS).astype(BF16)
    y = _dot(h_ref[...], wb_ref[...]) * out_scale
    if mode == "flat":
        o_ref[0] = y
        wob_ref[...] = wo_ref[...].astype(BF16)
    elif mode == "heads":
        for h in range(N_HEADS):
            o_ref[0, 0, h] = head(y, h).astype(BF16)
        wib_ref[...] = wi_ref[...].astype(BF16)
    else:
        rows = y.shape[0]
        for h in range(N_HEADS):
            o_ref[0, h, pl.ds(0, V_DIM), :] = head(y, h).T.astype(BF16)
            o_ref[0, h, pl.ds(V_DIM, SUM_ROWS), :] = jnp.ones((SUM_ROWS, rows), BF16)


def _project(inputs, w_in, *, col_tile0, n_slabs, mode, batch, seq):
    rows = inputs[0].shape[0]
    tm = PROJ_ROWS
    n_i = rows // tm
    n_ib = seq // tm
    row_tile = pl.BlockSpec((tm, D_MODEL), lambda j, i: (i, 0))
    meta_rows = pl.BlockSpec((N_META, D_MODEL), lambda j, i: (0, 0))
    if mode == "vt":
        in_specs = [row_tile, meta_rows, pl.BlockSpec((1, D_MODEL), lambda j, i: (0, 0))]
    elif mode == "flat":
        assert n_slabs == 1
        w_out_rows = pl.BlockSpec(((D_ATTN + D_LRU) // n_i, D_MODEL), lambda j, i: (i, 0))
        in_specs = [row_tile, meta_rows, w_out_rows]
    else:
        w_gate_rows = pl.BlockSpec((D_MODEL // (n_slabs * n_i), 3 * D_ATTN), lambda j, i: (j * n_i + i, 1))
        in_specs = [row_tile, meta_rows, w_gate_rows]
    in_specs.append(pl.BlockSpec((D_MODEL, D_ATTN), lambda j, i: (0, j + col_tile0)))
    if mode == "heads":
        out_shape = (jax.ShapeDtypeStruct((n_slabs, batch, N_HEADS, seq, V_DIM), BF16),
                     jax.ShapeDtypeStruct((n_slabs, N_HEADS, N_META, V_DIM), BF16),
                     jax.ShapeDtypeStruct((D_MODEL, 3 * D_ATTN), BF16))
        out_specs = (pl.BlockSpec((1, 1, N_HEADS, tm, V_DIM),
                                  lambda j, i: (j, i // n_ib, 0, i % n_ib, 0)),
                     pl.BlockSpec((1, N_HEADS, N_META, V_DIM), lambda j, i: (j, 0, 0, 0)),
                     pl.BlockSpec(w_gate_rows.block_shape, lambda j, i: (j * n_i + i, 0)))
    elif mode == "vt":
        assert n_slabs == 1
        out_shape = (jax.ShapeDtypeStruct((batch, N_HEADS, V_DIM + SUM_ROWS, seq), BF16),
                     jax.ShapeDtypeStruct((n_slabs, N_HEADS, N_META, V_DIM), BF16),
                     jax.ShapeDtypeStruct((rows, D_MODEL), BF16),
                     jax.ShapeDtypeStruct((N_META, D_MODEL), BF16))
        out_specs = (pl.BlockSpec((1, N_HEADS, V_DIM + SUM_ROWS, tm),
                                  lambda j, i: (i // n_ib, 0, 0, i % n_ib)),
                     pl.BlockSpec((1, N_HEADS, N_META, V_DIM), lambda j, i: (j, 0, 0, 0)),
                     row_tile, meta_rows)
    else:
        out_shape = (jax.ShapeDtypeStruct((n_slabs, rows, D_ATTN), F32),
                     jax.ShapeDtypeStruct((n_slabs, N_META, D_ATTN), F32),
                     jax.ShapeDtypeStruct((D_ATTN + D_LRU, D_MODEL), BF16))
        out_specs = (pl.BlockSpec((1, tm, D_ATTN), lambda j, i: (j, i, 0)),
                     pl.BlockSpec((1, N_META, D_ATTN), lambda j, i: (j, 0, 0)),
                     w_out_rows)
    return pl.pallas_call(
        functools.partial(_proj_kernel, mode=mode),
        grid=(n_slabs, n_i),
        in_specs=in_specs,
        out_specs=out_specs,
        out_shape=out_shape,
        scratch_shapes=[] if mode == "flat" else [pltpu.VMEM((D_MODEL, D_ATTN), BF16)],
        compiler_params=pltpu.CompilerParams(
            dimension_semantics=("arbitrary", "arbitrary"),
            vmem_limit_bytes=VMEM_LIMIT_PROJ),
        name={"heads": "proj_qk", "vt": "proj_v", "flat": "proj_u"}[mode],
    )(*inputs, w_in)


def _toeplitz_bias(dist, bucket_bias):
    b = jnp.full(dist.shape, bucket_bias[0], F32)
    for j, thr in enumerate(BUCKET_THRESHOLDS, start=1):
        b = jnp.where(dist >= thr, bucket_bias[j], b)
    return b


def _fill_bias(ref, lead, n_rows, n_cols, d0, bucket_bias):
    sub = lax.broadcasted_iota(jnp.int32, (8, V_DIM), 0)
    lane = lax.broadcasted_iota(jnp.int32, (8, V_DIM), 1)
    zeros = jnp.zeros((8, V_DIM), F32)
    masked = jnp.full((8, V_DIM), NEG_INF, F32)
    cache = {}
    for a8 in range(n_rows // 8):
        for b in range(n_cols // V_DIM):
            off = d0 + V_DIM * b - 8 * a8
            if off + V_DIM - 1 < 0:
                tile = masked
            elif off - 7 >= MAX_DISTANCE:
                tile = zeros
            else:
                if off not in cache:
                    d = off + lane - sub
                    cache[off] = jnp.where(d >= 0, _toeplitz_bias(d, bucket_bias), NEG_INF)
                tile = cache[off]
            ref[(*lead, pl.ds(8 * a8, 8), pl.ds(V_DIM * b, V_DIM))] = tile


def _attn_kernel(rb_ref, lq1_ref, lk1_ref, lq2_ref, lk2_ref, q_ref, k_ref, vt_ref, km_ref, vm_ref,
                 g_ref, sg_ref, o_ref, vmt_ref, bd_ref, bm_ref, bn_ref, sbuf, sc_ref, mx_ref,
                 m_ref, acc_ref):
    h = pl.program_id(0)
    tq, tk = ATTN_TQ, ATTN_TK
    seq = k_ref.shape[2]
    nq = seq // tq
    assert nq % 2 == 0 and nq >= 4

    far = rb_ref[N_BUCKETS - 1, h]

    @pl.when(pl.program_id(1) == 0)
    def _():
        bucket_bias = [(rb_ref[j, h] - far) * LOG2E for j in range(N_BUCKETS)]
        _fill_bias(bd_ref, (), tk, tq, 0, bucket_bias)
        _fill_bias(bn_ref, (), MAX_DISTANCE, MAX_DISTANCE, MAX_DISTANCE, bucket_bias)
        _fill_bias(bm_ref, (1,), N_META, tq, N_META, bucket_bias)
        bm_ref[0] = jnp.zeros((N_META, tq), F32)
        padded = jnp.concatenate(
            [vm_ref[0].astype(F32), jnp.zeros((V_DIM - N_META, V_DIM), F32)], axis=0)
        vmt_ref[pl.ds(0, V_DIM), :] = padded.T.astype(BF16)
        vmt_ref[pl.ds(V_DIM, SUM_ROWS), :] = jnp.ones((SUM_ROWS, V_DIM), BF16)

    lane = lax.broadcasted_iota(jnp.int32, (tq, V_DIM), 1)

    lam = (jnp.exp(jnp.sum(lq1_ref[...] * lk1_ref[...], keepdims=True))
           - jnp.exp(jnp.sum(lq2_ref[...] * lk2_ref[...], keepdims=True))
           + LAMBDA_INIT)

    def init_stats(st):
        m_ref[st] = jnp.full(m_ref.shape[1:], NEG_INF, F32)
        acc_ref[st] = jnp.zeros(acc_ref.shape[1:], F32)

    hk, hq = tk // 2, tq // 2

    def produce(buf, q_off, k_off, kind):
        q = q_ref[0, 0, pl.ds(pl.multiple_of(q_off, tq), tq), :]
        zero = jnp.zeros_like(q)
        k_off = pl.multiple_of(k_off, tk)
        k_t = k_ref[0, 0, pl.ds(k_off, tk), :]
        for a in range(2):
            qa = jnp.where((lane < HEAD_DIM) if a == 0 else (lane >= HEAD_DIM), q, zero)
            if kind == "diag":
                top = _dot_nt(k_t[:hk], qa) + bd_ref[pl.ds(0, hk), :]
                low = _dot_nt(k_t[hk:], qa[hq:]) + bd_ref[pl.ds(hk, hk), pl.ds(hq, hq)]
                sc = _dot_nt(km_ref[0], qa) + bm_ref[jnp.where(k_off == 0, 1, 0)]
                sbuf[buf, a, pl.ds(0, hk), :] = top
                sbuf[buf, a, pl.ds(hk, hk), pl.ds(hq, hq)] = low
                sc_ref[buf, a] = sc
                mx = jnp.maximum(jnp.max(top, axis=0, keepdims=True),
                                 jnp.max(sc, axis=0, keepdims=True))
                mx_low = jnp.max(low, axis=0, keepdims=True)
                mx_ref[buf, a] = jnp.concatenate(
                    [mx[:, :hq], jnp.maximum(mx[:, hq:], mx_low)], axis=1)
                continue
            s = _dot_nt(k_t, qa)
            if kind == "near":
                band = tk - MAX_DISTANCE
                corner = s[band:, :MAX_DISTANCE] + bn_ref[...]
                sbuf[buf, a, pl.ds(0, band), :] = s[:band]
                sbuf[buf, a, pl.ds(band, MAX_DISTANCE), pl.ds(0, MAX_DISTANCE)] = corner
                sbuf[buf, a, pl.ds(band, MAX_DISTANCE), pl.ds(MAX_DISTANCE, tq - MAX_DISTANCE)] = (
                    s[band:, MAX_DISTANCE:])
                mx = jnp.max(s[:band], axis=0, keepdims=True)
                mx_ref[buf, a] = jnp.concatenate(
                    [jnp.maximum(mx[:, :MAX_DISTANCE], jnp.max(corner, axis=0, keepdims=True)),
                     jnp.maximum(mx[:, MAX_DISTANCE:],
                                 jnp.max(s[band:, MAX_DISTANCE:], axis=0, keepdims=True))], axis=1)
                continue
            sbuf[buf, a] = s
            mx_ref[buf, a] = jnp.max(s, axis=0, keepdims=True)

    def consume(buf, st, k_off, diag=False):
        k_off = pl.multiple_of(k_off, tk)
        v_t = vt_ref[0, 0, :, pl.ds(k_off, tk)]
        for a in range(2):
            m_old = m_ref[st, a]
            m_new = jnp.maximum(m_old, mx_ref[buf, a])
            alpha = jnp.exp2(m_old - m_new)
            if diag:
                top = jnp.exp2(sbuf[buf, a, pl.ds(0, hk), :] - m_new)
                low = jnp.exp2(sbuf[buf, a, pl.ds(hk, hk), pl.ds(hq, hq)] - m_new[:, hq:])
                pc = jnp.exp2(sc_ref[buf, a] - m_new)
                pv = (_dot(v_t[:, :hk], top.astype(BF16))
                      + _dot(vmt_ref[...][:, :N_META], pc.astype(BF16)))
                pv_low = _dot(v_t[:, hk:], low.astype(BF16))
                pv = jnp.concatenate([pv[:, :hq], pv[:, hq:] + pv_low], axis=1)
            else:
                p = jnp.exp2(sbuf[buf, a] - m_new)
                pv = _dot(v_t, p.astype(BF16))
            acc_ref[st, a] = alpha * acc_ref[st, a] + pv
            m_ref[st, a] = m_new

    def finalize(st, q_off):
        q_rows = pl.ds(pl.multiple_of(q_off, tq), tq)
        heads = [acc_ref[st, a, pl.ds(0, V_DIM), :] / acc_ref[st, a, pl.ds(V_DIM, 1), :]
                 for a in range(2)]
        out_t = heads[0] - lam * heads[1]
        inv = lax.rsqrt(jnp.mean(out_t * out_t, axis=0, keepdims=True) + SUBLN_EPS)
        att = ((out_t * inv).T * sg_ref[...]) * (1.0 - LAMBDA_INIT)
        g = g_ref[0, q_rows, :]
        o_ref[q_rows, :] = (att * (g * jax.nn.sigmoid(g))).astype(o_ref.dtype)

    init_stats(0)
    produce(0, 0, 0, "diag")

    def q_tile(i, cur, odd):
        oth = 1 - cur
        q_off = i * tq

        def far_pair(tt, c):
            t = 2 * tt
            produce(oth, q_off, (t + 1) * tk, "far")
            consume(cur, cur, t * tk)
            produce(cur, q_off, (t + 2) * tk, "far")
            consume(oth, cur, (t + 1) * tk)
            return c

        n_far = jnp.maximum(i - 2, 0)
        lax.fori_loop(0, n_far // 2, far_pair, 0)

        def near_diag():
            produce(oth, q_off, (i - 1) * tk, "near")
            consume(cur, cur, (i - 2) * tk)
            produce(cur, q_off, i * tk, "diag")
            consume(oth, cur, (i - 1) * tk)

        def last(next_kind, has_prev=True):
            if has_prev:
                finalize(oth, (i - 1) * tq)
            if next_kind is not None:
                init_stats(oth)
                produce(oth, (i + 1) * tq, 0, next_kind)
            consume(cur, cur, i * tk, diag=True)

        if odd:
            def leftover_far_then(next_kind):
                consume(cur, cur, (n_far - 1) * tk)
                produce(cur, q_off, n_far * tk, "far")
                near_diag()
                last(next_kind)

            @pl.when((i >= 3) & (i < nq - 1))
            def _():
                leftover_far_then("far")

            @pl.when(i == nq - 1)
            def _():
                leftover_far_then(None)

            @pl.when(i == 1)
            def _():
                consume(cur, cur, 0)
                produce(cur, q_off, tk, "diag")
                last("far")
        else:
            @pl.when(i >= 2)
            def _():
                near_diag()
                last("far")

            @pl.when(i == 0)
            def _():
                last("near", has_prev=False)

    def q_pair(ii, carry):
        q_tile(2 * ii, 0, False)
        q_tile(2 * ii + 1, 1, True)
        return carry

    lax.fori_loop(0, nq // 2, q_pair, 0)
    finalize((nq - 1) % 2, (nq - 1) * tq)


def _attention(qk, qk_meta, vt, v_meta, gates, rel_bias, lam_q1, lam_k1, lam_q2, lam_k2, subln_g):
    _, batch, _, seq, _ = qk.shape
    tq, tk = ATTN_TQ, ATTN_TK
    smem = pl.BlockSpec(memory_space=pltpu.SMEM)
    row64 = pl.BlockSpec((1, HEAD_DIM), lambda h, b: (0, 0))
    qk_spec = lambda which: pl.BlockSpec((1, 1, 1, seq, V_DIM), lambda h, b: (which, b, h, 0, 0))
    meta_spec = lambda which: pl.BlockSpec((1, 1, N_META, V_DIM), lambda h, b: (which, h, 0, 0))

    def kernel(rb, lq1, lk1, lq2, lk2, q_ref, k_ref, vt_ref, km_ref, vm_ref, *rest):
        _attn_kernel(rb, lq1, lk1, lq2, lk2, q_ref.at[0], k_ref.at[0], vt_ref,
                     km_ref.at[0], vm_ref.at[0], *rest)

    return pl.pallas_call(
        kernel,
        grid=(N_HEADS, batch),
        in_specs=[smem, row64, row64, row64, row64,
                  qk_spec(0), qk_spec(1),
                  pl.BlockSpec((1, 1, V_DIM + SUM_ROWS, seq), lambda h, b: (b, h, 0, 0)),
                  meta_spec(1), meta_spec(0),
                  pl.BlockSpec((1, seq, V_DIM), lambda h, b: (0, b, h)),
                  pl.BlockSpec((1, V_DIM), lambda h, b: (0, 0))],
        out_specs=pl.BlockSpec((seq, V_DIM), lambda h, b: (b, h)),
        out_shape=jax.ShapeDtypeStruct((batch * seq, D_ATTN), BF16),
        scratch_shapes=[pltpu.VMEM((V_DIM + SUM_ROWS, V_DIM), BF16),
                        pltpu.VMEM((tk, tq), F32),
                        pltpu.VMEM((2, N_META, tq), F32),
                        pltpu.VMEM((MAX_DISTANCE, MAX_DISTANCE), F32),
                        pltpu.VMEM((2, 2, tk, tq), F32),
                        pltpu.VMEM((2, 2, N_META, tq), F32),
                        pltpu.VMEM((2, 2, 1, tq), F32),
                        pltpu.VMEM((2, 2, 1, tq), F32),
                        pltpu.VMEM((2, 2, V_DIM + SUM_ROWS, tq), F32)],
        compiler_params=pltpu.CompilerParams(
            dimension_semantics=("arbitrary", "arbitrary"),
            vmem_limit_bytes=VMEM_LIMIT_ATTN),
        name="diff_attention",
    )(rel_bias, lam_q1, lam_k1, lam_q2, lam_k2, qk, qk, vt, qk_meta, v_meta, gates, subln_g)


def _scan_block(a, b):
    n = a.shape[0]
    row = lax.broadcasted_iota(jnp.int32, a.shape, 0)
    s = 1
    while s < n:
        keep = row >= s
        a_sh = jnp.where(keep, pltpu.roll(a, s, 0), 1.0)
        b_sh = jnp.where(keep, pltpu.roll(b, s, 0), 0.0)
        b = b + a * b_sh
        a = a * a_sh
        s *= 2
    return a, b


def _gate_lru_kernel(h_ref, wb_ref, u_ref, um_ref, cw_ref, cb_ref, wa_ref, wx_ref, ba_ref, bx_ref,
                     lam_ref, o_ref, rec_ref, wg_ref, ubuf, tail_ref, st_ref, u_p, h_p,
                     *, tiles_per_batch):
    i = pl.program_id(1)
    n = LRU_ROWS
    hist = 8
    seg = n // 8
    n_blk = u_ref.shape[1] // LRU_BLOCK
    n_out = D_ATTN // n_blk

    x = -lam_ref[...]
    softplus = jnp.maximum(x, 0.0) + jnp.log1p(jnp.exp(-jnp.abs(x)))

    @pl.when(i == 0)
    def _():
        for blk in range(n_blk):
            wg_ref[blk, :, pl.ds(0, LRU_BLOCK)] = wa_ref[blk].astype(BF16)
            wg_ref[blk, :, pl.ds(LRU_BLOCK, LRU_BLOCK)] = wx_ref[blk].astype(BF16)

    def gate_inputs(rows, cols, blk):
        cw = cw_ref[:, cols]
        uc = (cw[3:4] * ubuf[pl.ds(hist, rows), cols] + cw[2:3] * ubuf[pl.ds(hist - 1, rows), cols]
              + cw[1:2] * ubuf[pl.ds(hist - 2, rows), cols] + cw[0:1] * ubuf[pl.ds(hist - 3, rows), cols]
              + cb_ref[:, cols])
        return uc, _dot(uc.astype(BF16), wg_ref[blk])

    def gates(uc, pre, cols, first):
        r = jax.nn.sigmoid(pre[:, :LRU_BLOCK] + ba_ref[:, cols])
        gi = jax.nn.sigmoid(pre[:, LRU_BLOCK:] + bx_ref[:, cols])
        log_a = -LRU_C * r * softplus[:, cols]
        a = jnp.exp(log_a)
        v = jnp.tanh(-log_a) * (a * a + 1.0)
        mult = jnp.where(v > 0.0, v * lax.rsqrt(v), 0.0)
        if first:
            row = lax.broadcasted_iota(jnp.int32, mult.shape, 0)
            mult = jnp.where(row == 0, 1.0, mult)
        return a, mult * gi * uc

    @pl.when(i % tiles_per_batch == 0)
    def _():
        ubuf[pl.ds(0, hist), :] = jnp.zeros((hist, ubuf.shape[1]), F32)
        ubuf[pl.ds(hist, N_META), :] = um_ref[...]
        for blk in range(n_blk):
            cols = slice(blk * LRU_BLOCK, (blk + 1) * LRU_BLOCK)
            a, b = gates(*gate_inputs(N_META, cols, blk), cols, True)
            _, hm = _scan_block(a, b)
            st_ref[:, cols] = hm[N_META - 1:N_META, :]
        tail_ref[...] = um_ref[pl.ds(N_META - hist, hist), :]

    sub = lax.broadcasted_iota(jnp.int32, (8, LRU_BLOCK), 0)
    step = lambda j: pl.ds(j, 8, stride=SEG_PITCH)

    def piece(t, carry):
        r0 = pl.multiple_of(t * n, n)
        for blk in range(n_blk):
            lru_block(r0, blk)
        tail_ref[...] = u_ref[pl.ds(r0 + n - hist, hist), :]
        return carry

    def lru_block(r0, blk):
        cols = slice(blk * LRU_BLOCK, (blk + 1) * LRU_BLOCK)
        ocols = slice(blk * n_out, (blk + 1) * n_out)
        o_ref[0, pl.ds(r0, n), ocols] = _dot(h_ref[pl.ds(r0, n), :], wb_ref[:, ocols])

        for k in range(8):
            u_p[blk, pl.ds(SEG_PITCH * k, seg), :] = u_ref[pl.ds(r0 + seg * k, seg), cols]
        us = [u_p[blk, step(j), :] for j in range(seg)]
        prev = tail_ref[:, cols]

        def before(back):
            return jnp.where(sub == 0, prev[hist - back:hist - back + 1, :],
                             pltpu.roll(us[seg - back], 1, 0))

        older = {-back: before(back) for back in range(1, CONV_WIDTH)}
        u_at = lambda j: us[j] if j >= 0 else older[j]
        cw = cw_ref[:, cols]
        cb = cb_ref[:, cols]
        uc = jnp.concatenate(
            [cw[3:4] * u_at(j) + cw[2:3] * u_at(j - 1) + cw[1:2] * u_at(j - 2)
             + cw[0:1] * u_at(j - 3) + cb for j in range(seg)], axis=0)
        a, b = gates(uc, _dot(uc.astype(BF16), wg_ref[blk]), cols, False)
        a_j = lambda j: a[8 * j:8 * j + 8]
        b_j = lambda j: b[8 * j:8 * j + 8]
        a_run, h_run = a_j(0), b_j(0)
        for j in range(1, seg):
            h_run = a_j(j) * h_run + b_j(j)
            a_run = a_j(j) * a_run
        a_cum, h_cum = _scan_block(a_run, h_run)
        carry = st_ref[:, cols]
        seg_end = h_cum + a_cum * carry
        st_ref[:, cols] = seg_end[7:8, :]
        h_run = jnp.where(sub == 0, carry, pltpu.roll(seg_end, 1, 0))
        for j in range(seg):
            h_run = a_j(j) * h_run + b_j(j)
            h_p[blk, step(j), :] = h_run
        for k in range(8):
            rec_ref[pl.ds(r0 + seg * k, seg), cols] = h_p[blk, pl.ds(SEG_PITCH * k, seg), :]

    lax.fori_loop(0, h_ref.shape[0] // n, piece, 0, unroll=True)


def _gate_lru(h2d, w_gate, u, u_meta, conv_w, conv_b, w_a, b_a, w_x, b_x, lru_lambda, *, seq):
    rows = h2d.shape[0]
    tm = PROJ_ROWS
    n_halves = 2
    width = D_LRU // n_halves
    n_blk = N_LRU_BLOCKS // n_halves
    gate_col_tile = lambda j: 2 * j
    half = lambda n: pl.BlockSpec((n, width), lambda j, i: (0, j))
    wspec = pl.BlockSpec((n_blk, LRU_BLOCK, LRU_BLOCK), lambda j, i: (j, 0, 0))

    def kernel(h_ref, w_ref, u_ref, um_ref, *rest):
        _gate_lru_kernel(h_ref, w_ref, u_ref.at[0], um_ref.at[0], *rest, tiles_per_batch=seq // tm)

    return pl.pallas_call(
        kernel,
        grid=(n_halves, rows // tm),
        in_specs=[pl.BlockSpec((tm, D_MODEL), lambda j, i: (i, 0)),
                  pl.BlockSpec((D_MODEL, D_ATTN), lambda j, i: (0, gate_col_tile(j))),
                  pl.BlockSpec((1, tm, width), lambda j, i: (0, i, j)),
                  pl.BlockSpec((1, N_META, width), lambda j, i: (0, 0, j)),
                  half(CONV_WIDTH), half(1), wspec, wspec, half(1), half(1), half(1)],
        out_specs=(pl.BlockSpec((1, tm, D_ATTN), lambda j, i: (j, i, 0)),
                   pl.BlockSpec((tm, width), lambda j, i: (i, j))),
        out_shape=(jax.ShapeDtypeStruct((n_halves, rows, D_ATTN), F32),
                   jax.ShapeDtypeStruct((rows, D_LRU), F32)),
        scratch_shapes=[pltpu.VMEM((n_blk, LRU_BLOCK, 2 * LRU_BLOCK), BF16),
                        pltpu.VMEM((8 + N_META, width), F32),
                        pltpu.VMEM((8, width), F32),
                        pltpu.VMEM((1, width), F32),
                        pltpu.VMEM((n_blk, 8 * SEG_PITCH, LRU_BLOCK), F32),
                        pltpu.VMEM((n_blk, 8 * SEG_PITCH, LRU_BLOCK), F32)],
        compiler_params=pltpu.CompilerParams(
            dimension_semantics=("arbitrary", "arbitrary"),
            vmem_limit_bytes=VMEM_LIMIT_PROJ),
        name="proj_gates_rglru",
    )(h2d, w_gate, u, u_meta, conv_w, conv_b, w_a, w_x, b_a, b_x, lru_lambda)


def _out_kernel(x_ref, att_ref, rec_ref, g_ref, w_ref, fg_ref, o_ref, rec_s):
    n_out = 2 * LRU_BLOCK
    for piece in range(x_ref.shape[0] // OUT_PIECE):
        rows = pl.ds(piece * OUT_PIECE, OUT_PIECE)
        g = g_ref[0, rows, :]
        rec_s[rows, :] = (rec_ref[rows, :] * (g * jax.nn.sigmoid(g))).astype(BF16)
        ssq = jnp.zeros((OUT_PIECE, LRU_BLOCK), F32)
        for blk in range(D_MODEL // n_out):
            ocols = slice(blk * n_out, (blk + 1) * n_out)
            z = (x_ref[rows, ocols] + _dot(att_ref[rows, :], w_ref[pl.ds(0, D_ATTN), ocols])
                 + _dot(rec_s[rows, :], w_ref[pl.ds(D_ATTN, D_LRU), ocols]))
            o_ref[rows, ocols] = z
            for part in range(n_out // LRU_BLOCK):
                zp = z[:, part * LRU_BLOCK:(part + 1) * LRU_BLOCK]
                ssq = ssq + zp * zp
        inv = lax.rsqrt(jnp.sum(ssq, axis=1, keepdims=True) * (1.0 / D_MODEL) + NORM_EPS)
        o_ref[rows, :] = (o_ref[rows, :] * inv) * fg_ref[...]


def _out_proj(x2d, att, rec, gates, w_out_bf16, final_g):
    rows = x2d.shape[0]
    tm = OUT_ROWS
    tile = lambda width: pl.BlockSpec((tm, width), lambda s: (s, 0))
    return pl.pallas_call(
        _out_kernel,
        grid=(rows // tm,),
        in_specs=[tile(D_MODEL), tile(D_ATTN), tile(D_LRU),
                  pl.BlockSpec((1, tm, D_LRU), lambda s: (1, s, 0)),
                  pl.BlockSpec((D_ATTN + D_LRU, D_MODEL), lambda s: (0, 0),
                               pipeline_mode=pl.Buffered(1)),
                  pl.BlockSpec((1, D_MODEL), lambda s: (0, 0))],
        out_specs=tile(D_MODEL),
        out_shape=jax.ShapeDtypeStruct((rows, D_MODEL), F32),
        scratch_shapes=[pltpu.VMEM((tm, D_LRU), BF16)],
        compiler_params=pltpu.CompilerParams(
            dimension_semantics=("arbitrary",),
            vmem_limit_bytes=VMEM_LIMIT_OUT),
        name="out_proj",
    )(x2d, att, rec, gates, w_out_bf16, final_g)


def kernel(x, meta_tokens, rel_bias, norm_g, w_in, conv_w, conv_b, w_a, b_a, w_x, b_x, lru_lambda,
           lam_q1, lam_k1, lam_q2, lam_k2, subln_g, w_out, final_g):
    batch, seq, _ = x.shape
    x2d = x.reshape(batch * seq, D_MODEL)
    project = functools.partial(_project, batch=batch, seq=seq)
    vt, v_meta, h2d, h_meta = project((x2d, meta_tokens, norm_g), w_in[0], col_tile0=2, n_slabs=1,
                                      mode="vt")
    qk, qk_meta, w_gate = project((h2d, h_meta, w_in[0]), w_in[0], col_tile0=0, n_slabs=2,
                                  mode="heads")
    u, u_meta, w_out_bf16 = project((h2d, h_meta, w_out[0]), w_gate, col_tile0=1, n_slabs=1,
                                    mode="flat")
    gates, rec = _gate_lru(h2d, w_gate, u, u_meta, conv_w[0], conv_b, w_a[0], b_a, w_x[0], b_x,
                           lru_lambda, seq=seq)
    att = _attention(qk, qk_meta, vt, v_meta, gates, rel_bias, lam_q1, lam_k1, lam_q2, lam_k2,
                     subln_g)
    out = _out_proj(x2d, att, rec, gates, w_out_bf16, final_g.reshape(1, D_MODEL))
    return out.reshape(batch, seq, D_MODEL)
```
